```python
import jax
import jax.numpy as jnp
from jax import lax
import numpy as np

D_MODEL = 1024
BATCH = 8
SEQ = 2048
DEPTH = 4

GRID_W = 64
CTX_LEN = 256
N_MIXERS = 4
N_LAYERS_NA = (DEPTH + 3) // 4
N_LAYERS_ML = (DEPTH + 2) // 4
N_LAYERS_SW = (DEPTH + 1) // 4
N_LAYERS_GL = DEPTH // 4
RMS_EPS = 1e-6
NEG_INF = -1e30
ROPE_BASE = 10000.0
F32 = jnp.float32

NA_HEADS = 8
NA_HEAD_DIM = D_MODEL // NA_HEADS
NA_WIN_R = 8
NA_WIN_C = 16
ML_HEADS = 8
ML_DK = D_MODEL // ML_HEADS // 2
ML_DV = D_MODEL // ML_HEADS
ML_CHUNK = 64
ML_GATE_CAP = 15.0
ML_FGATE_BIAS = 3.0
ML_IN = ML_HEADS * (2 * ML_DK + 2 * ML_DV) + 4 * ML_HEADS
SW_HEADS = 16
SW_KV_HEADS = 4
SW_HEAD_DIM = D_MODEL // SW_HEADS
SW_WINDOW = 128
SW_BLOCK = 128
SW_IN = (SW_HEADS + 2 * SW_KV_HEADS) * SW_HEAD_DIM
GL_HEADS = 4
GL_DK = D_MODEL // 2 // GL_HEADS
GL_DV = D_MODEL // GL_HEADS
GL_GATE_RANK = 16
GL_GATE_TAU = 16.0
GL_CHUNK = 64
GL_IN = GL_HEADS * (2 * GL_DK + 2 * GL_DV) + 2 * GL_GATE_RANK
MOE_GROUPS = 4
MOE_PER_GROUP = 8
MOE_EXPERTS = MOE_GROUPS * MOE_PER_GROUP
MOE_TOPK = 2
MOE_FF = 512
MOE_BLOCK = 128

kernel_name = 'hybrid_nat_mlstm_swa_gla_hmoe_dit'


def rmsnorm(x, g):
    xf = x.astype(F32)
    y = xf * lax.rsqrt(jnp.mean(xf * xf, axis=-1, keepdims=True) + RMS_EPS)
    return (y * g.astype(F32)).astype(x.dtype)


def modulate(h, shift, scale):
    return h * (1.0 + scale) + shift


def flip_time(t, direction):
    return jnp.flip(t, axis=2) if direction == 1 else t


def rope_2d(x, rows, cols):
    hd = x.shape[-1]
    half = hd // 2
    nf = half // 2
    inv = ROPE_BASE ** (-jnp.arange(nf, dtype=F32) / nf)
    bshape = (x.shape[1],) + (1,) * (x.ndim - 3) + (nf,)
    xf = x.astype(F32)

    def rot(xa, p):
        ang = (p.astype(F32)[:, None] * inv).reshape(bshape)
        cs, sn = jnp.cos(ang), jnp.sin(ang)
        x1, x2 = xa[..., :nf], xa[..., nf:]
        return jnp.concatenate([x1 * cs - x2 * sn, x1 * sn + x2 * cs], axis=-1)

    return jnp.concatenate([rot(xf[..., :half], rows), rot(xf[..., half:], cols)], axis=-1).astype(x.dtype)


def ctx_self_attention(q, k, v, sink):
    s = jnp.einsum('bqhgd,bkhd->bhgqk', q, k, preferred_element_type=F32) * (q.shape[-1] ** -0.5)
    if sink is None:
        p = jax.nn.softmax(s, axis=-1)
    else:
        sk = jnp.broadcast_to(sink[None, :, :, None, None], s.shape[:-1] + (1,))
        p = jax.nn.softmax(jnp.concatenate([sk, s], axis=-1), axis=-1)[..., 1:]
    return jnp.einsum('bhgqk,bkhd->bqhgd', p.astype(v.dtype), v)


def neighbourhood_attention(hc, hl, w_qkv, qk_g, rpb, w_o, need_ctx_out):
    bsz, seq, _ = hl.shape
    lc = hc.shape[1]
    H, hd = NA_HEADS, NA_HEAD_DIM
    n_rows = seq // GRID_W
    wr = min(NA_WIN_R, n_rows)
    scale = hd ** -0.5

    def proj(h):
        qkv = (h @ w_qkv).reshape(h.shape[0], h.shape[1], 3, H, hd)
        return rmsnorm(qkv[:, :, 0], qk_g[0]), rmsnorm(qkv[:, :, 1], qk_g[1]), qkv[:, :, 2]

    qc, kc, vc = proj(hc)
    ql, kl, vl = proj(hl)
    kg = kl.reshape(bsz, n_rows, GRID_W, H, hd)
    vg = vl.reshape(bsz, n_rows, GRID_W, H, hd)
    cq = jnp.arange(GRID_W)[:, None]
    ck = jnp.arange(GRID_W)[None, :]
    c0 = jnp.clip(cq - NA_WIN_C // 2, 0, GRID_W - NA_WIN_C)
    col_ok = (ck >= c0) & (ck < c0 + NA_WIN_C)
    dc = jnp.clip(ck - cq + NA_WIN_C - 1, 0, 2 * NA_WIN_C - 2)
    rpb_c = rpb[:, :, dc].astype(F32)

    def row_block(args):
        q_r, r = args
        r0 = jnp.clip(r - wr // 2, 0, n_rows - wr)
        k_b = lax.dynamic_slice_in_dim(kg, r0, wr, axis=1)
        v_b = lax.dynamic_slice_in_dim(vg, r0, wr, axis=1)
        bias = jnp.transpose(rpb_c[:, r0 + jnp.arange(wr) - r + NA_WIN_R - 1], (0, 2, 1, 3))
        s_nb = jnp.einsum('bqhd,brkhd->bhqrk', q_r, k_b, preferred_element_type=F32) * scale + bias
        s_nb = jnp.where(col_ok[:, None, :], s_nb, NEG_INF)
        s_cx = jnp.einsum('bqhd,bkhd->bhqk', q_r, kc, preferred_element_type=F32) * scale
        logits = jnp.concatenate([s_nb.reshape(bsz, H, GRID_W, wr * GRID_W), s_cx], axis=-1)
        p = jax.nn.softmax(logits, axis=-1).astype(v_b.dtype)
        p_nb = p[..., :wr * GRID_W].reshape(bsz, H, GRID_W, wr, GRID_W)
        return (jnp.einsum('bhqrk,brkhd->bqhd', p_nb, v_b)
                + jnp.einsum('bhqk,bkhd->bqhd', p[..., wr * GRID_W:], vc))

    q_rows = jnp.moveaxis(ql.reshape(bsz, n_rows, GRID_W, H, hd), 1, 0)
    o_rows = lax.map(row_block, (q_rows, jnp.arange(n_rows)))
    ol = jnp.moveaxis(o_rows, 0, 1).reshape(bsz, seq, H * hd) @ w_o
    oc = None
    if need_ctx_out:
        oc = ctx_self_attention(qc[:, :, :, None], kc, vc, None).reshape(bsz, lc, H * hd) @ w_o
    return oc, ol


def mlstm_chunkwise(q, k, v, log_i, log_f, state, need_out):
    bsz, H, L, dk = q.shape
    dv = v.shape[-1]
    nc, lc = L // ML_CHUNK, ML_CHUNK
    q = q.reshape(bsz, H, nc, lc, dk) * (dk ** -0.5)
    k = k.reshape(bsz, H, nc, lc, dk)
    v = v.reshape(bsz, H, nc, lc, dv)
    li = log_i.reshape(bsz, H, nc, lc)
    bf = jnp.cumsum(log_f.reshape(bsz, H, nc, lc), axis=-1)
    g = bf[..., -1]
    a = g[..., None] - bf + li
    m_loc = jnp.max(a, axis=-1)
    w = jnp.exp(a - m_loc[..., None])
    c_loc = jnp.einsum('bhcsv,bhcsk->bhcvk', v * w[..., None], k)
    n_loc = jnp.einsum('bhcs,bhcsk->bhck', w, k)

    def step(carry, xs):
        c_st, n_st, m_st = carry
        g_c, m_c, c_c, n_c = xs
        m_new = jnp.maximum(g_c + m_st, m_c)
        dec = jnp.exp(g_c + m_st - m_new)
        inc = jnp.exp(m_c - m_new)
        new = (dec[..., None, None] * c_st + inc[..., None, None] * c_c,
               dec[..., None] * n_st + inc[..., None] * n_c, m_new)
        return new, (c_st, n_st, m_st)

    xs = tuple(jnp.moveaxis(t, 2, 0) for t in (g, m_loc, c_loc, n_loc))
    final, (c_s, n_s, m_s) = lax.scan(step, state, xs)
    if not need_out:
        return None, final
    c_s, n_s, m_s = (jnp.moveaxis(t, 0, 2) for t in (c_s, n_s, m_s))
    tri = jnp.tril(jnp.ones((lc, lc), bool))
    dmat = jnp.where(tri, bf[..., :, None] - bf[..., None, :] + li[..., None, :], NEG_INF)
    inter = bf + m_s[..., None]
    m_t = jnp.maximum(inter, jnp.max(dmat, axis=-1))
    sc = jnp.einsum('bhctd,bhcsd->bhcts', q, k) * jnp.exp(dmat - m_t[..., None])
    e_inter = jnp.exp(inter - m_t)
    num = (jnp.einsum('bhcts,bhcsv->bhctv', sc, v)
           + e_inter[..., None] * jnp.einsum('bhctk,bhcvk->bhctv', q, c_s))
    den = jnp.sum(sc, axis=-1) + e_inter * jnp.einsum('bhctk,bhck->bhct', q, n_s)
    h = num / jnp.maximum(jnp.abs(den), jnp.exp(-m_t))[..., None]
    return h.reshape(bsz, H, L, dv), final


def mlstm_mixer(hc, hl, w_in, b_gates, norm_g, w_o, need_ctx_out):
    H, dk, dv = ML_HEADS, ML_DK, ML_DV
    nq, nv = H * dk, H * dv

    def proj(h):
        b_, l_ = h.shape[:2]
        u = h @ w_in
        uf = u[..., :2 * nq + nv].astype(F32)
        q = uf[..., :nq].reshape(b_, l_, H, dk).transpose(0, 2, 1, 3)
        k = uf[..., nq:2 * nq].reshape(b_, l_, H, dk).transpose(0, 2, 1, 3)
        v = uf[..., 2 * nq:].reshape(b_, l_, H, dv).transpose(0, 2, 1, 3)
        o_gate = jax.nn.sigmoid(u[..., 2 * nq + nv:2 * nq + 2 * nv])
        pre = (u[..., 2 * nq + 2 * nv:] + b_gates).astype(F32)
        pre = ML_GATE_CAP * jnp.tanh(pre / ML_GATE_CAP)
        pre = pre.reshape(b_, l_, 2, 2, H).transpose(2, 3, 0, 4, 1)
        return q, k, v, pre[:, 0], jax.nn.log_sigmoid(pre[:, 1]), o_gate

    cq, ck, cv, cli, clf, cog = proj(hc)
    lq, lk, lv, lli, llf, log_ = proj(hl)
    bsz = hc.shape[0]
    zero = (jnp.zeros((bsz, H, dv, dk), F32), jnp.zeros((bsz, H, dk), F32), jnp.zeros((bsz, H), F32))
    h_c, h_l = 0.0, 0.0
    for d in range(2):
        oc, st = mlstm_chunkwise(*[flip_time(t, d) for t in (cq, ck, cv, cli[d], clf[d])], zero, need_ctx_out)
        ol, _ = mlstm_chunkwise(*[flip_time(t, d) for t in (lq, lk, lv, lli[d], llf[d])], st, True)
        h_l = h_l + flip_time(ol, d)
        if need_ctx_out:
            h_c = h_c + flip_time(oc, d)

    def readout(hs, og):
        b_, _, l_, _ = hs.shape
        y = rmsnorm(hs.transpose(0, 2, 1, 3), norm_g.reshape(H, dv)).reshape(b_, l_, nv)
        return (y.astype(og.dtype) * og) @ w_o

    oc_out = readout(h_c, cog) if need_ctx_out else None
    return oc_out, readout(h_l, log_)


def window_gqa_attention(hc, hl, w_qkv, qk_g, sink, w_o, rows, cols, need_ctx_out):
    bsz, seq, _ = hl.shape
    lc = hc.shape[1]
    Hq, Hk, hd = SW_HEADS, SW_KV_HEADS, SW_HEAD_DIM
    G = Hq // Hk
    scale = hd ** -0.5

    def proj(h):
        b_, l_ = h.shape[:2]
        qkv = h @ w_qkv
        q = rmsnorm(qkv[..., :Hq * hd].reshape(b_, l_, Hk, G, hd), qk_g[0])
        k = rmsnorm(qkv[..., Hq * hd:(Hq + Hk) * hd].reshape(b_, l_, Hk, hd), qk_g[1])
        v = qkv[..., (Hq + Hk) * hd:].reshape(b_, l_, Hk, hd)
        return q, k, v

    qc, kc, vc = proj(hc)
    ql, kl, vl = proj(hl)
    ql = rope_2d(ql, rows, cols)
    kl = rope_2d(kl, rows, cols)
    sink_hg = sink.reshape(Hk, G).astype(F32)
    pad = ((0, 0), (SW_WINDOW, SW_WINDOW), (0, 0), (0, 0))
    kp = jnp.pad(kl, pad)
    vp = jnp.pad(vl, pad)
    n_keys = SW_BLOCK + 2 * SW_WINDOW
    n_blk = seq // SW_BLOCK

    def block(args):
        q_b, jb = args
        k_b = lax.dynamic_slice_in_dim(kp, jb * SW_BLOCK, n_keys, axis=1)
        v_b = lax.dynamic_slice_in_dim(vp, jb * SW_BLOCK, n_keys, axis=1)
        qpos = jb * SW_BLOCK + jnp.arange(SW_BLOCK)
        kpos = jb * SW_BLOCK - SW_WINDOW + jnp.arange(n_keys)
        ok = (jnp.abs(qpos[:, None] - kpos[None, :]) <= SW_WINDOW) & (kpos >= 0) & (kpos < seq)
        s_w = jnp.einsum('bqhgd,bkhd->bhgqk', q_b, k_b, preferred_element_type=F32) * scale
        s_w = jnp.where(ok, s_w, NEG_INF)
        s_c = jnp.einsum('bqhgd,bkhd->bhgqk', q_b, kc, preferred_element_type=F32) * scale
        sk = jnp.broadcast_to(sink_hg[None, :, :, None, None], s_w.shape[:-1] + (1,))
        p = jax.nn.softmax(jnp.concatenate([sk, s_w, s_c], axis=-1), axis=-1).astype(v_b.dtype)
        return (jnp.einsum('bhgqk,bkhd->bqhgd', p[..., 1:1 + n_keys], v_b)
                + jnp.einsum('bhgqk,bkhd->bqhgd', p[..., 1 + n_keys:], vc))

    q_blocks = jnp.moveaxis(ql.reshape(bsz, n_blk, SW_BLOCK, Hk, G, hd), 1, 0)
    o = lax.map(block, (q_blocks, jnp.arange(n_blk)))
    ol = jnp.moveaxis(o, 0, 1).reshape(bsz, seq, Hq * hd) @ w_o
    oc = None
    if need_ctx_out:
        oc = ctx_self_attention(qc, kc, vc, sink_hg).reshape(bsz, lc, Hq * hd) @ w_o
    return oc, ol


def gla_chunkwise(q, k, v, log_a, state, need_out):
    bsz, H, L, dk = q.shape
    dv = v.shape[-1]
    nc, lc = L // GL_CHUNK, GL_CHUNK
    q = q.reshape(bsz, H, nc, lc, dk) * (dk ** -0.5)
    k = k.reshape(bsz, H, nc, lc, dk)
    v = v.reshape(bsz, H, nc, lc, dv)
    bc = jnp.cumsum(log_a.reshape(bsz, H, nc, lc, dk), axis=3)
    g = bc[:, :, :, -1]
    s_loc = jnp.einsum('bhcsk,bhcsv->bhckv', k * jnp.exp(g[:, :, :, None, :] - bc), v)

    def step(s_st, xs):
        g_c, s_c = xs
        return jnp.exp(g_c)[..., None] * s_st + s_c, s_st

    final, s_s = lax.scan(step, state, (jnp.moveaxis(g, 2, 0), jnp.moveaxis(s_loc, 2, 0)))
    if not need_out:
        return None, final
    s_s = jnp.moveaxis(s_s, 0, 2)
    q_t = q * jnp.exp(bc)
    k_t = k * jnp.exp(-bc)
    tri = jnp.tril(jnp.ones((lc, lc), bool))
    att = jnp.where(tri, jnp.einsum('bhctk,bhcsk->bhcts', q_t, k_t), 0.0)
    o = jnp.einsum('bhcts,bhcsv->bhctv', att, v) + jnp.einsum('bhctk,bhckv->bhctv', q_t, s_s)
    return o.reshape(bsz, H, L, dv), final


def gla_mixer(hc, hl, w_in, w_a2, b_a, norm_g, w_o, need_ctx_out):
    H, dk, dv, rk = GL_HEADS, GL_DK, GL_DV, GL_GATE_RANK
    nq, nv = H * dk, H * dv

    def proj(h):
        b_, l_ = h.shape[:2]
        u = h @ w_in
        uf = u[..., :2 * nq + nv].astype(F32)
        q = uf[..., :nq].reshape(b_, l_, H, dk).transpose(0, 2, 1, 3)
        k = uf[..., nq:2 * nq].reshape(b_, l_, H, dk).transpose(0, 2, 1, 3)
        v = uf[..., 2 * nq:].reshape(b_, l_, H, dv).transpose(0, 2, 1, 3)
        gate = jax.nn.silu(u[..., 2 * nq + nv:2 * nq + 2 * nv])
        z = u[..., 2 * nq + 2 * nv:].reshape(b_, l_, 2, rk)
        z = jnp.einsum('blxr,xrk->xblk', z, w_a2) + b_a[:, None, None, :]
        log_a = (jax.nn.log_sigmoid(z.astype(F32)) / GL_GATE_TAU).reshape(2, b_, l_, H, dk).transpose(0, 1, 3, 2, 4)
        return q, k, v, log_a, gate

    cq, ck, cv, cla, cgate = proj(hc)
    lq, lk, lv, lla, lgate = proj(hl)
    bsz = hc.shape[0]
    zero = jnp.zeros((bsz, H, dk, dv), F32)
    h_c, h_l = 0.0, 0.0
    for d in range(2):
        oc, st = gla_chunkwise(*[flip_time(t, d) for t in (cq, ck, cv, cla[d])], zero, need_ctx_out)
        ol, _ = gla_chunkwise(*[flip_time(t, d) for t in (lq, lk, lv, lla[d])], st, True)
        h_l = h_l + flip_time(ol, d)
        if need_ctx_out:
            h_c = h_c + flip_time(oc, d)

    def readout(hs, gate):
        b_, _, l_, _ = hs.shape
        y = rmsnorm(hs.transpose(0, 2, 1, 3), norm_g.reshape(H, dv)).reshape(b_, l_, nv)
        return (y.astype(gate.dtype) * gate) @ w_o

    oc_out = readout(h_c, cgate) if need_ctx_out else None
    return oc_out, readout(h_l, lgate)


def hierarchical_moe(h, w_grp, b_grp, w_exp, b_exp, w_gate, w_up, w_down):
    T, D = h.shape
    lg = (h @ w_grp).astype(F32) + b_grp.astype(F32)
    g_idx = jnp.argmax(lg, axis=-1)
    g_w = jnp.take_along_axis(jax.nn.softmax(lg, axis=-1), g_idx[:, None], axis=-1)
    le = ((h @ w_exp).astype(F32) + b_exp.astype(F32)).reshape(T, MOE_GROUPS, MOE_PER_GROUP)
    le = jnp.take_along_axis(le, g_idx[:, None, None], axis=1)[:, 0]
    top_v, top_i = lax.top_k(le, MOE_TOPK)
    wts = (jax.nn.softmax(top_v, axis=-1) * g_w).reshape(-1)
    eid = (g_idx[:, None] * MOE_PER_GROUP + top_i).reshape(-1)
    tok = jnp.repeat(jnp.arange(T, dtype=jnp.int32), MOE_TOPK)
    n = T * MOE_TOPK
    order = jnp.argsort(eid)
    e_sorted = eid[order]
    counts = jnp.bincount(eid, length=MOE_EXPERTS)
    padded = (counts + MOE_BLOCK - 1) // MOE_BLOCK * MOE_BLOCK
    p_end = jnp.cumsum(padded)
    p_start = p_end - padded
    start = jnp.cumsum(counts) - counts
    dest = p_start[e_sorted] + jnp.arange(n) - start[e_sorted]
    n_slots = -(-(n + MOE_EXPERTS * (MOE_BLOCK - 1)) // MOE_BLOCK) * MOE_BLOCK
    slot_tok = jnp.full((n_slots,), T, jnp.int32).at[dest].set(tok[order])
    slot_w = jnp.zeros((n_slots,), F32).at[dest].set(wts[order])
    n_blk = n_slots // MOE_BLOCK
    blk_e = jnp.minimum(jnp.searchsorted(p_end, jnp.arange(n_blk) * MOE_BLOCK, side='right'), MOE_EXPERTS - 1)
    xb = jnp.take(h, slot_tok, axis=0, mode='clip').reshape(n_blk, MOE_BLOCK, D)

    def expert_block(args):
        xe, e = args
        return (jax.nn.silu(xe @ w_gate[e]) * (xe @ w_up[e])) @ w_down[e]

    y = lax.map(expert_block, (xb, blk_e)).reshape(n_slots, D)
    return jnp.zeros_like(h).at[slot_tok].add(y * slot_w[:, None].astype(y.dtype), mode='drop')


def setup_inputs(seed: int = 0) -> dict:
    key = jax.random.key(seed)
    keys = iter(list(jax.random.split(key, 40)))

    def nrm(shape, scale):
        return jax.random.normal(next(keys), shape, F32) * scale

    def gain(shape):
        return 1.0 + nrm(shape, 0.02)

    D = D_MODEL
    din = D ** -0.5
    ml_b = nrm((N_LAYERS_ML, 2, 2, ML_HEADS), 0.1) + jnp.array([0.0, ML_FGATE_BIAS], F32)[:, None]
    return {
        'x': nrm((BATCH, SEQ, D), 1.0),
        'c': nrm((BATCH, D), 1.0),
        'ctx': nrm((BATCH, CTX_LEN, D), 1.0),
        'c_ctx': nrm((D,), 1.0),
        'ada_w': nrm((DEPTH, D, 6 * D), 0.5 * din),
        'ada_b': nrm((DEPTH, 6 * D), 0.02),
        'norm_mix_g': gain((DEPTH, D)),
        'norm_ffn_g': gain((DEPTH, D)),
        'na_w_qkv': nrm((N_LAYERS_NA, D, 3 * NA_HEADS * NA_HEAD_DIM), din),
        'na_qk_g': gain((N_LAYERS_NA, 2, NA_HEAD_DIM)),
        'na_rpb': nrm((N_LAYERS_NA, NA_HEADS, 2 * NA_WIN_R - 1, 2 * NA_WIN_C - 1), 0.2),
        'na_w_o': nrm((N_LAYERS_NA, NA_HEADS * NA_HEAD_DIM, D), din),
        'ml_w_in': nrm((N_LAYERS_ML, D, ML_IN), din),
        'ml_b_gates': ml_b.reshape(N_LAYERS_ML, 4 * ML_HEADS),
        'ml_norm_g': gain((N_LAYERS_ML, ML_HEADS * ML_DV)),
        'ml_w_o': nrm((N_LAYERS_ML, ML_HEADS * ML_DV, D), (ML_HEADS * ML_DV) ** -0.5),
        'sw_w_qkv': nrm((N_LAYERS_SW, D, SW_IN), din),
        'sw_qk_g': gain((N_LAYERS_SW, 2, SW_HEAD_DIM)),
        'sw_sink': nrm((N_LAYERS_SW, SW_HEADS), 0.5),
        'sw_w_o': nrm((N_LAYERS_SW, SW_HEADS * SW_HEAD_DIM, D), din),
        'gl_w_in': nrm((N_LAYERS_GL, D, GL_IN), din),
        'gl_w_a2': nrm((N_LAYERS_GL, 2, GL_GATE_RANK, GL_HEADS * GL_DK), GL_GATE_RANK ** -0.5),
        'gl_b_a': nrm((N_LAYERS_GL, 2, GL_HEADS * GL_DK), 0.1),
        'gl_norm_g': gain((N_LAYERS_GL, GL_HEADS * GL_DV)),
        'gl_w_o': nrm((N_LAYERS_GL, GL_HEADS * GL_DV, D), (GL_HEADS * GL_DV) ** -0.5),
        'moe_w_grp': nrm((DEPTH, D, MOE_GROUPS), din),
        'moe_b_grp': nrm((DEPTH, MOE_GROUPS), 0.01),
        'moe_w_exp': nrm((DEPTH, D, MOE_EXPERTS), din),
        'moe_b_exp': nrm((DEPTH, MOE_EXPERTS), 0.01),
        'moe_w_gate': nrm((DEPTH, MOE_EXPERTS, D, MOE_FF), din),
        'moe_w_up': nrm((DEPTH, MOE_EXPERTS, D, MOE_FF), din),
        'moe_w_down': nrm((DEPTH, MOE_EXPERTS, MOE_FF, D), MOE_FF ** -0.5),
    }


def reference(x, c, ctx, c_ctx, ada_w, ada_b, norm_mix_g, norm_ffn_g,
              na_w_qkv, na_qk_g, na_rpb, na_w_o,
              ml_w_in, ml_b_gates, ml_norm_g, ml_w_o,
              sw_w_qkv, sw_qk_g, sw_sink, sw_w_o,
              gl_w_in, gl_w_a2, gl_b_a, gl_norm_g, gl_w_o,
              moe_w_grp, moe_b_grp, moe_w_exp, moe_b_exp, moe_w_gate, moe_w_up, moe_w_down):
    bsz, seq, D = x.shape
    pos = jnp.arange(seq, dtype=jnp.int32)
    rows, cols = pos // GRID_W, pos % GRID_W
    cond_l = jax.nn.silu(c)
    cond_c = jax.nn.silu(c_ctx)[None]
    xl, xc = x, ctx
    for i in range(DEPTH):
        last = i == DEPTH - 1
        j = i // N_MIXERS
        kind = i % N_MIXERS
        mod_l = jnp.split((cond_l @ ada_w[i] + ada_b[i])[:, None, :], 6, axis=-1)
        mod_c = jnp.split((cond_c @ ada_w[i] + ada_b[i])[:, None, :], 6, axis=-1)
        hl = modulate(rmsnorm(xl, norm_mix_g[i]), mod_l[0], mod_l[1])
        hc = modulate(rmsnorm(xc, norm_mix_g[i]), mod_c[0], mod_c[1])
        if kind == 0:
            oc, ol = neighbourhood_attention(hc, hl, na_w_qkv[j], na_qk_g[j], na_rpb[j], na_w_o[j], not last)
        elif kind == 1:
            oc, ol = mlstm_mixer(hc, hl, ml_w_in[j], ml_b_gates[j], ml_norm_g[j], ml_w_o[j], not last)
        elif kind == 2:
            oc, ol = window_gqa_attention(hc, hl, sw_w_qkv[j], sw_qk_g[j], sw_sink[j], sw_w_o[j], rows, cols, not last)
        else:
            oc, ol = gla_mixer(hc, hl, gl_w_in[j], gl_w_a2[j], gl_b_a[j], gl_norm_g[j], gl_w_o[j], not last)
        xl = xl + mod_l[2] * ol
        fl = modulate(rmsnorm(xl, norm_ffn_g[i]), mod_l[3], mod_l[4])
        moe_args = (moe_w_grp[i], moe_b_grp[i], moe_w_exp[i], moe_b_exp[i], moe_w_gate[i], moe_w_up[i], moe_w_down[i])
        if last:
            yl = hierarchical_moe(fl.reshape(-1, D), *moe_args).reshape(bsz, seq, D)
        else:
            xc = xc + mod_c[2] * oc
            fc = modulate(rmsnorm(xc, norm_ffn_g[i]), mod_c[3], mod_c[4])
            n_ctx_tok = fc.shape[0] * fc.shape[1]
            y = hierarchical_moe(jnp.concatenate([fc.reshape(-1, D), fl.reshape(-1, D)], axis=0), *moe_args)
            xc = xc + mod_c[5] * y[:n_ctx_tok].reshape(fc.shape)
            yl = y[n_ctx_tok:].reshape(bsz, seq, D)
        xl = xl + mod_l[5] * yl
    return xl
```

```python
import functools

import jax
import jax.numpy as jnp
from jax import lax
from jax.experimental import pallas as pl
from jax.experimental.pallas import tpu as pltpu

F32 = jnp.float32
BF16 = jnp.bfloat16
HIGHEST = lax.Precision.HIGHEST

D_MODEL = 1024
CTX_LEN = 256
GRID_W = 64
RMS_EPS = 1e-6
NEG_INF = -1e30
ROPE_BASE = 10000.0

NA_HEADS, NA_HD, NA_WIN_R, NA_WIN_C = 8, 128, 8, 16
NA_QROWS = 4
NA_KROWS = NA_QROWS + NA_WIN_R - 1
ML_HEADS, ML_DK, ML_DV, ML_CHUNK, ML_CAP = 8, 64, 128, 64, 15.0
SW_HEADS, SW_KV, SW_HD, SW_WINDOW = 16, 4, 64, 128
GL_HEADS, GL_DK, GL_DV, GL_RANK, GL_TAU, GL_CHUNK = 4, 128, 256, 16, 16.0, 64
MOE_GROUPS, MOE_PER_GROUP, MOE_EXPERTS, MOE_FF, MOE_BLOCK = 4, 8, 32, 512, 128

LANES = 128
SUBLANES = 8
TM = 256
VMEM_LIMIT = 48 * 1024 * 1024


def _cparams(*sem):
    return pltpu.CompilerParams(dimension_semantics=sem, vmem_limit_bytes=VMEM_LIMIT)


def _norm_mod(x, g, mod_ref, k):
    y = x * lax.rsqrt(jnp.mean(x * x, axis=-1, keepdims=True) + RMS_EPS) * g
    return y * (1.0 + mod_ref[k + 1:k + 2, :]) + mod_ref[k:k + 1, :]


def _log_sigmoid(x):
    return jnp.minimum(x, 0.0) - jnp.log(1.0 + jnp.exp(-jnp.abs(x)))


def _silu(x):
    return x * jax.nn.sigmoid(x)


def _mod_kernel(c_ref, w_ref, b_ref, o_ref):
    s = _silu(c_ref[...])
    o_ref[...] = jnp.dot(s.astype(BF16), w_ref[...].astype(BF16), preferred_element_type=F32) + b_ref[...]


def _ada_mods(cond, ada_w, ada_b):
    depth, d, _ = ada_w.shape
    rows = cond.shape[0]
    out = pl.pallas_call(
        _mod_kernel,
        grid=(depth, 6),
        in_specs=[pl.BlockSpec((rows, d), lambda i, n: (0, 0)),
                  pl.BlockSpec((None, d, d), lambda i, n: (i, 0, n)),
                  pl.BlockSpec((None, 1, d), lambda i, n: (i, 0, n))],
        out_specs=pl.BlockSpec((None, rows, d), lambda i, n: (i, 0, n)),
        out_shape=jax.ShapeDtypeStruct((depth, rows, 6 * d), F32),
        compiler_params=_cparams("arbitrary", "arbitrary"),
        name="ada_mods",
    )(cond, ada_w, ada_b.reshape(depth, 1, 6 * d))
    return out.reshape(depth, rows, 6, d)


def _mod_spec(nb):
    return pl.BlockSpec((None, 6, D_MODEL), lambda b, t: (jnp.where(t == 0, nb, b), 0, 0))


def _proj_call(kernel, x, mods, g, w, extras, out_cols, out_dtypes, name):
    nb, l, d = x.shape
    nt = l // TM
    n = w.shape[1]
    in_specs = [pl.BlockSpec((None, TM, d), lambda b, t: (b, t, 0)),
                _mod_spec(nb),
                pl.BlockSpec((1, d), lambda b, t: (0, 0)),
                pl.BlockSpec((d, n), lambda b, t: (0, 0))]
    args = [x, mods, g.reshape(1, d), w]
    for e, per_tile in extras:
        if per_tile:
            in_specs.append(pl.BlockSpec((TM, e.shape[1]), lambda b, t: (t, 0)))
        else:
            in_specs.append(pl.BlockSpec(e.shape, lambda b, t: (0, 0)))
        args.append(e)
    return pl.pallas_call(
        kernel,
        grid=(nb, nt),
        in_specs=in_specs,
        out_specs=[pl.BlockSpec((None, TM, c), lambda b, t: (b, t, 0)) for c in out_cols],
        out_shape=[jax.ShapeDtypeStruct((nb, l, c), dt) for c, dt in zip(out_cols, out_dtypes)],
        compiler_params=_cparams("arbitrary", "arbitrary"),
        name=name,
    )(*args)


def _proj_na_kernel(x_ref, mod_ref, g_ref, w_ref, qkg_ref, o_ref):
    h = _norm_mod(x_ref[...], g_ref[...], mod_ref, 0)
    u = jnp.dot(h.astype(BF16), w_ref[...], preferred_element_type=F32)
    nq = NA_HEADS * NA_HD
    for part in range(2):
        gain = qkg_ref[part:part + 1, :]
        for hh in range(NA_HEADS):
            lo = part * nq + hh * NA_HD
            z = u[:, lo:lo + NA_HD]
            z = z * lax.rsqrt(jnp.mean(z * z, axis=-1, keepdims=True) + RMS_EPS) * gain
            o_ref[:, lo:lo + NA_HD] = z.astype(BF16)
    o_ref[:, 2 * nq:] = u[:, 2 * nq:].astype(BF16)


def _head64_rms(z, gain):
    lane = lax.broadcasted_iota(jnp.int32, z.shape, 1)
    lo = lane < SW_HD
    zz = z * z
    s_lo = jnp.sum(jnp.where(lo, zz, 0.0), axis=-1, keepdims=True)
    s_hi = jnp.sum(jnp.where(lo, 0.0, zz), axis=-1, keepdims=True)
    ms = jnp.where(lo, s_lo, s_hi) * (1.0 / SW_HD)
    return z * lax.rsqrt(ms + RMS_EPS) * gain


def _rope_slab(z, cs, sn):
    lane = lax.broadcasted_iota(jnp.int32, z.shape, 1)
    first = (lane % 32) < 16
    partner = jnp.where(first, pltpu.roll(z, LANES - 16, 1), pltpu.roll(z, 16, 1))
    return z * cs + partner * sn


def _proj_sw_kernel(x_ref, mod_ref, g_ref, w_ref, qkg_ref, cs_ref, sn_ref, q_ref, kv_ref):
    h = _norm_mod(x_ref[...], g_ref[...], mod_ref, 0)
    u = jnp.dot(h.astype(BF16), w_ref[...], preferred_element_type=F32)
    cs, sn = cs_ref[...], sn_ref[...]
    nq, nk = SW_HEADS * SW_HD, SW_KV * SW_HD
    scale = SW_HD ** -0.5
    for s in range(nq // LANES):
        z = _head64_rms(u[:, s * LANES:(s + 1) * LANES], qkg_ref[0:1, :])
        q_ref[:, s * LANES:(s + 1) * LANES] = (_rope_slab(z, cs, sn) * scale).astype(BF16)
    for s in range(nk // LANES):
        z = _head64_rms(u[:, nq + s * LANES:nq + (s + 1) * LANES], qkg_ref[1:2, :])
        kv_ref[:, s * LANES:(s + 1) * LANES] = _rope_slab(z, cs, sn).astype(BF16)
    kv_ref[:, nk:] = u[:, nq + nk:].astype(BF16)


def _proj_ml_kernel(x_ref, mod_ref, g_ref, w_ref, bg_ref, qkv_ref, og_ref, gt_ref):
    h = _norm_mod(x_ref[...], g_ref[...], mod_ref, 0)
    u = jnp.dot(h.astype(BF16), w_ref[...], preferred_element_type=F32)
    nq, nv = ML_HEADS * ML_DK, ML_HEADS * ML_DV
    qkv_ref[:, :nq] = (u[:, :nq] * (ML_DK ** -0.5)).astype(BF16)
    qkv_ref[:, nq:] = u[:, nq:2 * nq + nv].astype(BF16)
    og_ref[...] = jax.nn.sigmoid(u[:, 2 * nq + nv:2 * nq + 2 * nv]).astype(BF16)
    pre = u[:, 2 * nq + 2 * nv:] + bg_ref[...]
    pre = ML_CAP * jnp.tanh(pre / ML_CAP)
    lane = lax.broadcasted_iota(jnp.int32, pre.shape, 1)
    gt_ref[...] = jnp.where((lane % 16) >= 8, _log_sigmoid(pre), pre)


def _proj_gl_kernel(x_ref, mod_ref, g_ref, w_ref, wa_ref, ba_ref, qkv_ref, gate_ref, la_ref):
    h = _norm_mod(x_ref[...], g_ref[...], mod_ref, 0)
    u = jnp.dot(h.astype(BF16), w_ref[...], preferred_element_type=F32)
    nq, nv = GL_HEADS * GL_DK, GL_HEADS * GL_DV
    qkv_ref[...] = u[:, :2 * nq + nv].astype(BF16)
    gate_ref[...] = _silu(u[:, 2 * nq + nv:2 * nq + 2 * nv]).astype(BF16)
    z = u[:, 2 * nq + 2 * nv:]
    a = jnp.dot(z.astype(BF16), wa_ref[...], preferred_element_type=F32) + ba_ref[...]
    la_ref[...] = _log_sigmoid(a) * (1.0 / GL_TAU)


def _na_bias_table(rpb, n_rows):
    nblk = n_rows // NA_QROWS
    jj = jnp.arange(nblk)[:, None, None, None, None]
    ri = jnp.arange(NA_QROWS)[None, :, None, None, None]
    cq = jnp.arange(GRID_W)[None, None, :, None, None]
    kr = jnp.arange(NA_KROWS)[None, None, None, :, None]
    ck = jnp.arange(GRID_W)[None, None, None, None, :]
    r = NA_QROWS * jj + ri
    r0 = jnp.clip(r - NA_WIN_R // 2, 0, n_rows - NA_WIN_R)
    ws = jnp.clip(NA_QROWS * jj - NA_WIN_R // 2, 0, n_rows - NA_KROWS)
    krow = ws + kr
    row_ok = (krow >= r0) & (krow < r0 + NA_WIN_R)
    dr = jnp.clip(krow - r + NA_WIN_R - 1, 0, 2 * NA_WIN_R - 2)
    c0 = jnp.clip(cq - NA_WIN_C // 2, 0, GRID_W - NA_WIN_C)
    col_ok = (ck >= c0) & (ck < c0 + NA_WIN_C)
    dc = jnp.clip(ck - cq + NA_WIN_C - 1, 0, 2 * NA_WIN_C - 2)
    shape = (nblk, NA_QROWS, GRID_W, NA_KROWS, GRID_W)
    vals = rpb[:, jnp.broadcast_to(dr, shape), jnp.broadcast_to(dc, shape)].astype(F32)
    tab = jnp.where(jnp.broadcast_to(row_ok & col_ok, shape), vals, NEG_INF)
    return tab.reshape(rpb.shape[0], nblk, NA_QROWS * GRID_W, NA_KROWS * GRID_W)


def _na_kernel(q_ref, k_ref, v_ref, bias_ref, o_ref, *, n_rows):
    j = pl.program_id(1)
    nb = q_ref.shape[0]
    scale = NA_HD ** -0.5
    nt = (((1,), (1,)), ((), ()))
    nkw = NA_KROWS * GRID_W

    @pl.when(j == 0)
    def _ctx():
        def body(b, c):
            q = q_ref[b]
            s = lax.dot_general(q, k_ref[b, 0:CTX_LEN, :], nt, preferred_element_type=F32) * scale
            p = jnp.exp(s - jnp.max(s, axis=-1, keepdims=True))
            l = jnp.sum(p, axis=-1, keepdims=True)
            o = jnp.dot(p.astype(BF16), v_ref[b, 0:CTX_LEN, :], preferred_element_type=F32)
            o_ref[b] = (o / l).astype(BF16)
            return c
        lax.fori_loop(0, nb, body, 0)

    @pl.when(j > 0)
    def _lat():
        ws = jnp.clip(NA_QROWS * (j - 1) - NA_WIN_R // 2, 0, n_rows - NA_KROWS)
        start = pl.multiple_of(CTX_LEN + ws * GRID_W, GRID_W)
        bias = bias_ref[...]

        def body(b, c):
            q = q_ref[b]
            s_n = lax.dot_general(q, k_ref[b, pl.ds(start, nkw), :], nt, preferred_element_type=F32) * scale + bias
            s_c = lax.dot_general(q, k_ref[b, 0:CTX_LEN, :], nt, preferred_element_type=F32) * scale
            m = jnp.maximum(jnp.max(s_n, axis=-1, keepdims=True), jnp.max(s_c, axis=-1, keepdims=True))
            p_n = jnp.exp(s_n - m)
            p_c = jnp.exp(s_c - m)
            l = jnp.sum(p_n, axis=-1, keepdims=True) + jnp.sum(p_c, axis=-1, keepdims=True)
            o = (jnp.dot(p_n.astype(BF16), v_ref[b, pl.ds(start, nkw), :], preferred_element_type=F32)
                 + jnp.dot(p_c.astype(BF16), v_ref[b, 0:CTX_LEN, :], preferred_element_type=F32))
            o_ref[b] = (o / l).astype(BF16)
            return c
        lax.fori_loop(0, nb, body, 0)


def _na_attention(qkv, bias_tab):
    nb, l, _ = qkv.shape
    n_rows = (l - CTX_LEN) // GRID_W
    nt = l // TM
    hq = NA_HEADS
    return pl.pallas_call(
        functools.partial(_na_kernel, n_rows=n_rows),
        grid=(hq, nt),
        in_specs=[pl.BlockSpec((nb, TM, NA_HD), lambda h, j: (0, j, h)),
                  pl.BlockSpec((nb, l, NA_HD), lambda h, j: (0, 0, hq + h)),
                  pl.BlockSpec((nb, l, NA_HD), lambda h, j: (0, 0, 2 * hq + h)),
                  pl.BlockSpec((None, None, TM, NA_KROWS * GRID_W),
                               lambda h, j: (h, jnp.maximum(j - 1, 0), 0, 0))],
        out_specs=pl.BlockSpec((nb, TM, NA_HD), lambda h, j: (0, j, h)),
        out_shape=jax.ShapeDtypeStruct((nb, l, hq * NA_HD), BF16),
        compiler_params=_cparams("arbitrary", "arbitrary"),
        name="na_attention",
    )(qkv, qkv, qkv, bias_tab)


def _sw_kernel(sink_ref, q_ref, kv_ref, o_ref, *, seq):
    j = pl.program_id(1)
    nt = (((1,), (1,)), ((), ()))
    nkv = SW_KV * SW_HD
    grp = SW_HEADS // SW_KV
    kwin = TM + 2 * SW_WINDOW

    def head_cols(h):
        return slice(h * SW_HD, (h + 1) * SW_HD)

    @pl.when(j == 0)
    def _ctx():
        for h in range(SW_HEADS):
            hk = h // grp
            sink = sink_ref[h]
            q = q_ref[:, head_cols(h)]
            s = lax.dot_general(q, kv_ref[0:CTX_LEN, head_cols(hk)], nt, preferred_element_type=F32)
            m = jnp.maximum(jnp.max(s, axis=-1, keepdims=True), sink)
            p = jnp.exp(s - m)
            l = jnp.sum(p, axis=-1, keepdims=True) + jnp.exp(sink - m)
            o = jnp.dot(p.astype(BF16), kv_ref[0:CTX_LEN, nkv + hk * SW_HD:nkv + (hk + 1) * SW_HD],
                        preferred_element_type=F32)
            o_ref[:, head_cols(h)] = (o / l).astype(BF16)

    @pl.when(j > 0)
    def _lat():
        q0 = (j - 1) * TM
        ws = jnp.clip(q0 - SW_WINDOW, 0, seq - kwin)
        start = pl.multiple_of(CTX_LEN + ws, SW_WINDOW)
        qpos = q0 + lax.broadcasted_iota(jnp.int32, (TM, kwin), 0)
        kpos = ws + lax.broadcasted_iota(jnp.int32, (TM, kwin), 1)
        ok = jnp.abs(qpos - kpos) <= SW_WINDOW
        for h in range(SW_HEADS):
            hk = h // grp
            sink = sink_ref[h]
            q = q_ref[:, head_cols(h)]
            vcols = slice(nkv + hk * SW_HD, nkv + (hk + 1) * SW_HD)
            s_w = lax.dot_general(q, kv_ref[pl.ds(start, kwin), head_cols(hk)], nt, preferred_element_type=F32)
            s_w = jnp.where(ok, s_w, NEG_INF)
            s_c = lax.dot_general(q, kv_ref[0:CTX_LEN, head_cols(hk)], nt, preferred_element_type=F32)
            m = jnp.maximum(jnp.maximum(jnp.max(s_w, axis=-1, keepdims=True),
                                        jnp.max(s_c, axis=-1, keepdims=True)), sink)
            p_w = jnp.exp(s_w - m)
            p_c = jnp.exp(s_c - m)
            l = (jnp.sum(p_w, axis=-1, keepdims=True) + jnp.sum(p_c, axis=-1, keepdims=True)
                 + jnp.exp(sink - m))
            o = (jnp.dot(p_w.astype(BF16), kv_ref[pl.ds(start, kwin), vcols], preferred_element_type=F32)
                 + jnp.dot(p_c.astype(BF16), kv_ref[0:CTX_LEN, vcols], preferred_element_type=F32))
            o_ref[:, head_cols(h)] = (o / l).astype(BF16)


def _sw_attention(q, kv, sink):
    nb, l, nq = q.shape
    nt = l // TM
    return pl.pallas_call(
        functools.partial(_sw_kernel, seq=l - CTX_LEN),
        grid=(nb, nt),
        in_specs=[pl.BlockSpec(memory_space=pltpu.SMEM),
                  pl.BlockSpec((None, TM, nq), lambda b, j: (b, j, 0)),
                  pl.BlockSpec((None, l, kv.shape[2]), lambda b, j: (b, 0, 0))],
        out_specs=pl.BlockSpec((None, TM, nq), lambda b, j: (b, j, 0)),
        out_shape=jax.ShapeDtypeStruct((nb, l, nq), BF16),
        compiler_params=_cparams("arbitrary", "arbitrary"),
        name="sw_attention",
    )(sink, q, kv)


def _chunk_index(step, rev, n_ctx, n_all):
    if not rev:
        return step
    return jnp.where(step < n_ctx, n_ctx - 1 - step, n_all + n_ctx - 1 - step)


def _tri(n, rev):
    r = lax.broadcasted_iota(jnp.int32, (n, n), 0)
    c = lax.broadcasted_iota(jnp.int32, (n, n), 1)
    return (c >= r) if rev else (c <= r)


def _row_cumsum(x, rev):
    lane = lax.broadcasted_iota(jnp.int32, x.shape, 1)
    s = 1
    while s < ML_CHUNK:
        if rev:
            x = x + jnp.where(lane + s < ML_CHUNK, pltpu.roll(x, LANES - s, 1), 0.0)
        else:
            x = x + jnp.where(lane >= s, pltpu.roll(x, s, 1), 0.0)
        s *= 2
    return x


def _ml_kernel(q_ref, k_ref, v_ref, og_ref, gr_ref, ng_ref, y_ref, acc_ref, *, n_ctx, n_all):
    lc = ML_CHUNK
    hps = q_ref.shape[1] // ML_DK
    eye = _tri(lc, False) & _tri(lc, True)
    t0 = (((0,), (0,)), ((), ()))
    nt = (((1,), (1,)), ((), ()))

    def col(row):
        return jnp.sum(jnp.where(eye, jnp.broadcast_to(row, (lc, lc)), 0.0), axis=1, keepdims=True)

    def chain(step, hl, rev, state):
        ct, n, m = state
        c = _chunk_index(step, rev, n_ctx, n_all)
        r0 = pl.multiple_of(c * lc, lc)
        g8 = gr_ref[c]
        cum = _row_cumsum(g8, rev)
        base = hl * 4 + (2 if rev else 0)
        li_r = g8[base:base + 1, :lc]
        bf_r = cum[base + 1:base + 2, :lc]
        g = jnp.sum(g8[base + 1:base + 2, :lc], axis=1, keepdims=True)
        bf_c, li_c = col(bf_r), col(li_r)
        q = q_ref[pl.ds(r0, lc), hl * ML_DK:(hl + 1) * ML_DK]
        k = k_ref[pl.ds(r0, lc), hl * ML_DK:(hl + 1) * ML_DK]
        v = v_ref[pl.ds(r0, lc), hl * ML_DV:(hl + 1) * ML_DV]
        kf, vf = k.astype(F32), v.astype(F32)
        dm = jnp.where(_tri(lc, rev), bf_c - bf_r + li_r, NEG_INF)
        inter = bf_c + m
        m_t = jnp.maximum(inter, jnp.max(dm, axis=1, keepdims=True))
        sc = lax.dot_general(q, k, nt, preferred_element_type=F32) * jnp.exp(dm - m_t)
        e_int = jnp.exp(inter - m_t)
        num = (jnp.dot(sc.astype(BF16), v, preferred_element_type=F32)
               + e_int * jnp.dot(q, ct.astype(BF16), preferred_element_type=F32))
        den = jnp.sum(sc, axis=1, keepdims=True) + e_int * jnp.sum(q.astype(F32) * n, axis=1, keepdims=True)
        hout = num / jnp.maximum(jnp.abs(den), jnp.exp(-m_t))
        acc_ref[pl.ds(r0, lc), hl * ML_DV:(hl + 1) * ML_DV] += hout
        a = g - bf_c + li_c
        m_loc = jnp.max(a, axis=0, keepdims=True)
        w = jnp.exp(a - m_loc)
        c_loc = lax.dot_general(k, (vf * w).astype(BF16), t0, preferred_element_type=F32)
        n_loc = jnp.sum(kf * w, axis=0, keepdims=True)
        m_new = jnp.maximum(g + m, m_loc)
        dec = jnp.exp(g + m - m_new)
        inc = jnp.exp(m_loc - m_new)
        return dec * ct + inc * c_loc, dec * n + inc * n_loc, m_new

    acc_ref[...] = jnp.zeros_like(acc_ref)
    chains = [(hl, rev) for hl in range(hps) for rev in (False, True)]
    init = tuple((jnp.zeros((ML_DK, ML_DV), F32), jnp.zeros((1, ML_DK), F32), jnp.zeros((1, 1), F32))
                 for _ in chains)

    def step_fn(step, states):
        return tuple(chain(step, hl, rev, st) for (hl, rev), st in zip(chains, states))

    lax.fori_loop(0, n_all, step_fn, init)

    for hl in range(hps):
        cols = slice(hl * ML_DV, (hl + 1) * ML_DV)
        z = acc_ref[:, cols]
        z = z * lax.rsqrt(jnp.mean(z * z, axis=-1, keepdims=True) + RMS_EPS) * ng_ref[:, cols]
        y_ref[:, cols] = (z.astype(BF16) * og_ref[:, cols])


def _ml_mixer(qkv, og, gates, norm_g):
    nb, l, _ = qkv.shape
    lc = ML_CHUNK
    n_all, n_ctx = l // lc, CTX_LEN // lc
    hps = 2
    ngrp = ML_HEADS // hps
    nq = ML_HEADS * ML_DK
    g = gates[:, :, :4 * ML_HEADS].reshape(nb, n_all, lc, 2, 2, ngrp, hps)
    g = g.transpose(0, 5, 1, 6, 3, 4, 2).reshape(nb, ngrp, n_all, hps * 4, lc)
    g = jnp.pad(g, ((0, 0),) * 4 + ((0, LANES - lc),))
    qb, vb = hps * ML_DK // LANES, hps * ML_DV
    return pl.pallas_call(
        functools.partial(_ml_kernel, n_ctx=n_ctx, n_all=n_all),
        grid=(nb, ngrp),
        in_specs=[pl.BlockSpec((None, l, hps * ML_DK), lambda b, h: (b, 0, h)),
                  pl.BlockSpec((None, l, hps * ML_DK), lambda b, h: (b, 0, nq // (hps * ML_DK) + h)),
                  pl.BlockSpec((None, l, vb), lambda b, h: (b, 0, 2 * nq // vb + h)),
                  pl.BlockSpec((None, l, vb), lambda b, h: (b, 0, h)),
                  pl.BlockSpec((None, None, n_all, hps * 4, LANES), lambda b, h: (b, h, 0, 0, 0)),
                  pl.BlockSpec((1, vb), lambda b, h: (0, h))],
        out_specs=pl.BlockSpec((None, l, vb), lambda b, h: (b, 0, h)),
        out_shape=jax.ShapeDtypeStruct((nb, l, ML_HEADS * ML_DV), BF16),
        scratch_shapes=[pltpu.VMEM((l, vb), F32)],
        compiler_params=_cparams("arbitrary", "arbitrary"),
        name="mlstm_mixer",
    )(qkv, qkv, qkv, og, g, norm_g.reshape(1, -1))


def _gl_kernel(q_ref, k_ref, v_ref, gate_ref, la0_ref, la1_ref, ng_ref, y_ref, acc_ref, st_ref, *, n_ctx, n_all):
    lc = GL_CHUNK
    t0 = (((0,), (0,)), ((), ()))
    nt = (((1,), (1,)), ((), ()))
    scale = GL_DK ** -0.5

    def chain(step, rev):
        la_ref = la1_ref if rev else la0_ref
        sidx = 1 if rev else 0
        c = _chunk_index(step, rev, n_ctx, n_all)
        r0 = pl.multiple_of(c * lc, lc)
        tri = _tri(lc, rev)
        la = la_ref[pl.ds(r0, lc), :]
        bc = jnp.dot(tri.astype(F32), la, precision=HIGHEST, preferred_element_type=F32)
        g = jnp.sum(la, axis=0, keepdims=True)
        q = q_ref[pl.ds(r0, lc), :].astype(F32) * scale
        k = k_ref[pl.ds(r0, lc), :].astype(F32)
        v = v_ref[pl.ds(r0, lc), :]
        st = st_ref[sidx]
        q_t = (q * jnp.exp(bc)).astype(BF16)
        k_t = (k * jnp.exp(-bc)).astype(BF16)
        att = jnp.where(tri, lax.dot_general(q_t, k_t, nt, preferred_element_type=F32), 0.0)
        o = (jnp.dot(att.astype(BF16), v, preferred_element_type=F32)
             + lax.dot_general(q_t, st.astype(BF16), nt, preferred_element_type=F32))
        acc_ref[pl.ds(r0, lc), :] += o
        kd = (k * jnp.exp(g - bc)).astype(BF16)
        s_loc = lax.dot_general(v, kd, t0, preferred_element_type=F32)
        st_ref[sidx] = st * jnp.exp(g) + s_loc

    acc_ref[...] = jnp.zeros_like(acc_ref)
    st_ref[...] = jnp.zeros_like(st_ref)

    def step_fn(step, carry):
        chain(step, False)
        chain(step, True)
        return carry

    lax.fori_loop(0, n_all, step_fn, 0)
    z = acc_ref[...]
    z = z * lax.rsqrt(jnp.mean(z * z, axis=-1, keepdims=True) + RMS_EPS) * ng_ref[...]
    y_ref[...] = z.astype(BF16) * gate_ref[...]


def _gl_mixer(qkv, gate, la, norm_g):
    nb, l, _ = qkv.shape
    lc = GL_CHUNK
    n_all, n_ctx = l // lc, CTX_LEN // lc
    hq = GL_HEADS
    return pl.pallas_call(
        functools.partial(_gl_kernel, n_ctx=n_ctx, n_all=n_all),
        grid=(nb, hq),
        in_specs=[pl.BlockSpec((None, l, GL_DK), lambda b, h: (b, 0, h)),
                  pl.BlockSpec((None, l, GL_DK), lambda b, h: (b, 0, hq + h)),
                  pl.BlockSpec((None, l, GL_DV), lambda b, h: (b, 0, hq + h)),
                  pl.BlockSpec((None, l, GL_DV), lambda b, h: (b, 0, h)),
                  pl.BlockSpec((None, l, GL_DK), lambda b, h: (b, 0, h)),
                  pl.BlockSpec((None, l, GL_DK), lambda b, h: (b, 0, hq + h)),
                  pl.BlockSpec((1, GL_DV), lambda b, h: (0, h))],
        out_specs=pl.BlockSpec((None, l, GL_DV), lambda b, h: (b, 0, h)),
        out_shape=jax.ShapeDtypeStruct((nb, l, hq * GL_DV), BF16),
        scratch_shapes=[pltpu.VMEM((l, GL_DV), F32), pltpu.VMEM((2, GL_DV, GL_DK), F32)],
        compiler_params=_cparams("arbitrary", "arbitrary"),
        name="gla_mixer",
    )(qkv, qkv, qkv, gate, la, la, norm_g.reshape(1, -1))


def _store_token_tiles(ref, val):
    rows = val.shape[0]
    for s in range(val.shape[1] // LANES):
        ref[pl.ds(s, rows, stride=SUBLANES), :] = val[:, s * LANES:(s + 1) * LANES]


def _load_token_tiles(ref, rows):
    return jnp.concatenate([ref[pl.ds(s, rows, stride=SUBLANES), :] for s in range(SUBLANES)], axis=1)


def _post_kernel(y_ref, x_ref, mod_ref, wo_ref, g_ref, wr_ref, br_ref, xo_ref, f_ref, rt_ref, cnt_ref, run_ref):
    first = (pl.program_id(0) == 0) & (pl.program_id(1) == 0)

    @pl.when(first)
    def _():
        run_ref[...] = jnp.zeros_like(run_ref)

    o = jnp.dot(y_ref[...], wo_ref[...], preferred_element_type=F32)
    xn = x_ref[...] + mod_ref[2:3, :] * o
    xo_ref[...] = xn
    f = _norm_mod(xn, g_ref[...], mod_ref, 3)
    _store_token_tiles(f_ref, f)
    lg = jnp.dot(f, wr_ref[...], precision=HIGHEST, preferred_element_type=F32) + br_ref[...]
    lane = lax.broadcasted_iota(jnp.int32, lg.shape, 1).astype(F32)
    big = float(LANES)

    def masked_max(mask):
        return jnp.max(jnp.where(mask, lg, -jnp.inf), axis=-1, keepdims=True)

    def first_at(mask, val):
        return jnp.min(jnp.where(mask & (lg == val), lane, big), axis=-1, keepdims=True)

    gm = lane < MOE_GROUPS
    mg = masked_max(gm)
    g_idx = first_at(gm, mg)
    g_w = 1.0 / jnp.sum(jnp.where(gm, jnp.exp(lg - mg), 0.0), axis=-1, keepdims=True)
    lo = MOE_GROUPS + MOE_PER_GROUP * g_idx
    em = (lane >= lo) & (lane < lo + MOE_PER_GROUP)
    v0 = masked_max(em)
    i0 = first_at(em, v0)
    em1 = em & (lane != i0)
    v1 = masked_max(em1)
    i1 = first_at(em1, v1)
    e0, e1 = i0 - MOE_GROUPS, i1 - MOE_GROUPS
    t = jnp.exp(v1 - v0)
    w0 = g_w / (1.0 + t)
    w1 = g_w * t / (1.0 + t)
    oh = ((lane == e0) | (lane == e1)).astype(F32)
    rows = lg.shape[0]
    below = lax.broadcasted_iota(jnp.int32, (rows, rows), 0) > lax.broadcasted_iota(jnp.int32, (rows, rows), 1)
    tot = run_ref[...] + jnp.dot(below.astype(BF16), oh.astype(BF16), preferred_element_type=F32)
    r0 = jnp.sum(jnp.where(lane == e0, tot, 0.0), axis=-1, keepdims=True)
    r1 = jnp.sum(jnp.where(lane == e1, tot, 0.0), axis=-1, keepdims=True)
    run_ref[...] = run_ref[...] + jnp.sum(oh, axis=0, keepdims=True)
    cnt_ref[...] = jnp.broadcast_to(run_ref[...], cnt_ref.shape)
    out = jnp.zeros_like(lg)
    for i, val in enumerate((e0, e1, w0, w1, r0, r1)):
        out = jnp.where(lane == i, val, out)
    rt_ref[...] = out


def _post_call(y, x, mods, w_o, g_ffn, w_r, b_r):
    nb, l, d = x.shape
    nt = l // TM
    tile = lambda b, t: (b * nt + t, 0)
    return pl.pallas_call(
        _post_kernel,
        grid=(nb, nt),
        in_specs=[pl.BlockSpec((None, TM, d), lambda b, t: (b, t, 0)),
                  pl.BlockSpec((None, TM, d), lambda b, t: (b, t, 0)),
                  _mod_spec(nb),
                  pl.BlockSpec((d, d), lambda b, t: (0, 0)),
                  pl.BlockSpec((1, d), lambda b, t: (0, 0)),
                  pl.BlockSpec((d, LANES), lambda b, t: (0, 0)),
                  pl.BlockSpec((1, LANES), lambda b, t: (0, 0))],
        out_specs=[pl.BlockSpec((None, TM, d), lambda b, t: (b, t, 0)),
                   pl.BlockSpec((TM * SUBLANES, LANES), tile),
                   pl.BlockSpec((TM, LANES), tile),
                   pl.BlockSpec((SUBLANES, LANES), lambda b, t: (0, 0))],
        out_shape=[jax.ShapeDtypeStruct((nb, l, d), F32),
                   jax.ShapeDtypeStruct((nb * l * SUBLANES, LANES), F32),
                   jax.ShapeDtypeStruct((nb * l, LANES), F32),
                   jax.ShapeDtypeStruct((SUBLANES, LANES), F32)],
        scratch_shapes=[pltpu.VMEM((1, LANES), F32)],
        compiler_params=_cparams("arbitrary", "arbitrary"),
        name="outproj_router",
    )(y, x, mods, w_o, g_ffn.reshape(1, d), w_r, b_r)


def _dispatch_kernel(dest_ref, f_hbm, xs_in, xs_out, sem):
    del xs_in
    base = pl.program_id(0) * TM

    def copy(tok, k):
        d = dest_ref[2 * tok + k]
        return pltpu.make_async_copy(f_hbm.at[pl.ds(tok * SUBLANES, SUBLANES), :],
                                     xs_out.at[pl.ds(d * SUBLANES, SUBLANES), :], sem)

    def issue(t, c):
        copy(base + t, 0).start()
        copy(base + t, 1).start()
        return c

    def drain(t, c):
        copy(base + t, 0).wait()
        copy(base + t, 1).wait()
        return c

    lax.fori_loop(0, TM, issue, 0)
    lax.fori_loop(0, TM, drain, 0)


def _dispatch(dest, f_tiles, n_slots):
    n_tok = f_tiles.shape[0] // SUBLANES
    xs0 = jnp.zeros((n_slots * SUBLANES, LANES), F32)
    return pl.pallas_call(
        _dispatch_kernel,
        grid_spec=pltpu.PrefetchScalarGridSpec(
            num_scalar_prefetch=1,
            grid=(n_tok // TM,),
            in_specs=[pl.BlockSpec(memory_space=pl.ANY), pl.BlockSpec(memory_space=pl.ANY)],
            out_specs=pl.BlockSpec(memory_space=pl.ANY),
            scratch_shapes=[pltpu.SemaphoreType.DMA(())]),
        out_shape=jax.ShapeDtypeStruct(xs0.shape, F32),
        input_output_aliases={2: 0},
        compiler_params=_cparams("arbitrary"),
        name="moe_dispatch",
    )(dest, f_tiles, xs0)


def _expert_kernel(blk_e_ref, nact_ref, xs_ref, wg_ref, wu_ref, wd_ref, ys_ref, wg_s, wu_s, wd_s):
    i = pl.program_id(0)
    active = i < nact_ref[0]
    changed = (i == 0) | (blk_e_ref[i] != blk_e_ref[jnp.maximum(i - 1, 0)])

    @pl.when(active & changed)
    def _():
        wg_s[...] = wg_ref[...].astype(BF16)
        wu_s[...] = wu_ref[...].astype(BF16)
        wd_s[...] = wd_ref[...].astype(BF16)

    @pl.when(active)
    def _():
        x = _load_token_tiles(xs_ref, MOE_BLOCK).astype(BF16)
        a = jnp.dot(x, wg_s[...], preferred_element_type=F32)
        u = jnp.dot(x, wu_s[...], preferred_element_type=F32)
        y = jnp.dot((_silu(a) * u).astype(BF16), wd_s[...], preferred_element_type=F32)
        _store_token_tiles(ys_ref, y)

    @pl.when(jnp.logical_not(active))
    def _():
        ys_ref[...] = jnp.zeros_like(ys_ref)


def _experts(blk_e, nact, xs, w_gate, w_up, w_down):
    n_blk = blk_e.shape[0]
    _, d, ff = w_gate.shape
    rows = MOE_BLOCK * SUBLANES
    blk = lambda i, be, na: (jnp.minimum(i, na[0] - 1), 0)
    wsel = lambda i, be, na: (be[jnp.minimum(i, na[0] - 1)], 0, 0)
    return pl.pallas_call(
        _expert_kernel,
        grid_spec=pltpu.PrefetchScalarGridSpec(
            num_scalar_prefetch=2,
            grid=(n_blk,),
            in_specs=[pl.BlockSpec((rows, LANES), blk),
                      pl.BlockSpec((None, d, ff), wsel),
                      pl.BlockSpec((None, d, ff), wsel),
                      pl.BlockSpec((None, ff, d), wsel)],
            out_specs=pl.BlockSpec((rows, LANES), lambda i, be, na: (i, 0)),
            scratch_shapes=[pltpu.VMEM((d, ff), BF16), pltpu.VMEM((d, ff), BF16), pltpu.VMEM((ff, d), BF16)]),
        out_shape=jax.ShapeDtypeStruct(xs.shape, F32),
        compiler_params=_cparams("arbitrary"),
        name="moe_experts",
    )(blk_e, nact, xs, w_gate, w_up, w_down)


def _combine_kernel(dest_ref, ys_hbm, x_ref, rt_ref, mod_ref, o_ref, ybuf, sem, *, nt):
    base = (pl.program_id(0) * nt + pl.program_id(1)) * TM

    def copy(t, k):
        d = dest_ref[2 * (base + t) + k]
        return pltpu.make_async_copy(ys_hbm.at[pl.ds(d * SUBLANES, SUBLANES), :],
                                     ybuf.at[k, pl.ds(t * SUBLANES, SUBLANES), :], sem)

    def issue(t, c):
        copy(t, 0).start()
        copy(t, 1).start()
        return c

    def drain(t, c):
        copy(t, 0).wait()
        copy(t, 1).wait()
        return c

    lax.fori_loop(0, TM, issue, 0)
    lax.fori_loop(0, TM, drain, 0)
    y0 = _load_token_tiles(ybuf.at[0], TM)
    y1 = _load_token_tiles(ybuf.at[1], TM)
    rt = rt_ref[...]
    y = rt[:, 2:3] * y0 + rt[:, 3:4] * y1
    o_ref[...] = x_ref[...] + mod_ref[5:6, :] * y


def _combine(dest, ys, x, rt, mods):
    nb, l, d = x.shape
    nt = l // TM
    return pl.pallas_call(
        functools.partial(_combine_kernel, nt=nt),
        grid_spec=pltpu.PrefetchScalarGridSpec(
            num_scalar_prefetch=1,
            grid=(nb, nt),
            in_specs=[pl.BlockSpec(memory_space=pl.ANY),
                      pl.BlockSpec((None, TM, d), lambda b, t, dr: (b, t, 0)),
                      pl.BlockSpec((TM, LANES), lambda b, t, dr: (b * nt + t, 0)),
                      pl.BlockSpec((None, 6, d), lambda b, t, dr: (jnp.where(t == 0, nb, b), 0, 0))],
            out_specs=pl.BlockSpec((None, TM, d), lambda b, t, dr: (b, t, 0)),
            scratch_shapes=[pltpu.VMEM((2, TM * SUBLANES, LANES), F32), pltpu.SemaphoreType.DMA(())]),
        out_shape=jax.ShapeDtypeStruct(x.shape, F32),
        compiler_params=_cparams("arbitrary", "arbitrary"),
        name="moe_combine",
    )(dest, ys, x, rt, mods)


def _moe(y_mix, x, mods, w_o, g_ffn, w_grp, b_grp, w_exp, b_exp, w_gate, w_up, w_down):
    nb, l, d = x.shape
    n_tok = nb * l
    nr = MOE_GROUPS + MOE_EXPERTS
    w_r = jnp.pad(jnp.concatenate([w_grp, w_exp], axis=1), ((0, 0), (0, LANES - nr)))
    b_r = jnp.pad(jnp.concatenate([b_grp, b_exp]), (0, LANES - nr)).reshape(1, LANES)
    x_new, f_tiles, rt, cnt = _post_call(y_mix, x, mods, w_o, g_ffn, w_r, b_r)
    counts = cnt[0, :MOE_EXPERTS].astype(jnp.int32)
    padded = (counts + MOE_BLOCK - 1) // MOE_BLOCK * MOE_BLOCK
    p_end = jnp.cumsum(padded)
    p_start = p_end - padded
    n_pairs = 2 * n_tok
    n_blk = -(-(n_pairs + MOE_EXPERTS * (MOE_BLOCK - 1)) // MOE_BLOCK)
    eid = rt[:, 0:2].astype(jnp.int32)
    dest = (p_start[eid] + rt[:, 4:6].astype(jnp.int32)).reshape(-1)
    blk_e = jnp.minimum(jnp.searchsorted(p_end, jnp.arange(n_blk) * MOE_BLOCK, side='right'),
                        MOE_EXPERTS - 1).astype(jnp.int32)
    nact = (p_end[-1:] // MOE_BLOCK).astype(jnp.int32)
    xs = _dispatch(dest, f_tiles, n_blk * MOE_BLOCK)
    ys = _experts(blk_e, nact, xs, w_gate, w_up, w_down)
    return _combine(dest, ys, x_new, rt, mods)


def _rope_tables(seq):
    nf = SW_HD // 4
    inv = ROPE_BASE ** (-jnp.arange(nf, dtype=F32) / nf)
    pos = jnp.arange(seq, dtype=jnp.int32)
    rows, cols = (pos // GRID_W).astype(F32), (pos % GRID_W).astype(F32)
    lane = jnp.arange(LANES)
    p = jnp.where((lane % SW_HD < SW_HD // 2)[None, :], rows[:, None], cols[:, None])
    ang = p * inv[lane % nf][None, :]
    sign = jnp.where((lane % (2 * nf)) < nf, -1.0, 1.0)[None, :]
    cs = jnp.concatenate([jnp.ones((CTX_LEN, LANES), F32), jnp.cos(ang)], axis=0)
    sn = jnp.concatenate([jnp.zeros((CTX_LEN, LANES), F32), jnp.sin(ang) * sign], axis=0)
    return cs, sn


def _pad_cols(w, n):
    return jnp.pad(w, ((0, 0), (0, n - w.shape[1])))


def kernel(x, c, ctx, c_ctx, ada_w, ada_b, norm_mix_g, norm_ffn_g, na_w_qkv, na_qk_g, na_rpb, na_w_o, ml_w_in, ml_b_gates, ml_norm_g, ml_w_o, sw_w_qkv, sw_qk_g, sw_sink, sw_w_o, gl_w_in, gl_w_a2, gl_b_a, gl_norm_g, gl_w_o, moe_w_grp, moe_b_grp, moe_w_exp, moe_b_exp, moe_w_gate, moe_w_up, moe_w_down):
    nb, seq, d = x.shape
    depth = ada_w.shape[0]
    assert d == D_MODEL and ctx.shape[1] == CTX_LEN == TM and seq % TM == 0
    rows = -(-(nb + 1) // SUBLANES) * SUBLANES
    cond = jnp.pad(jnp.concatenate([c, c_ctx[None]], axis=0), ((0, rows - nb - 1), (0, 0)))
    mods_all = _ada_mods(cond, ada_w, ada_b)
    xa = jnp.concatenate([ctx, x], axis=1)
    for i in range(depth):
        j, kind = divmod(i, 4)
        mods = mods_all[i]
        g_mix = norm_mix_g[i]
        if kind == 0:
            qkg = na_qk_g[j]
            (qkv,) = _proj_call(_proj_na_kernel, xa, mods, g_mix, na_w_qkv[j].astype(BF16), [(qkg, False)],
                                [3 * NA_HEADS * NA_HD], [BF16], "proj_na")
            y = _na_attention(qkv, _na_bias_table(na_rpb[j], seq // GRID_W))
            w_o = na_w_o[j]
        elif kind == 1:
            n_in = -(-ml_w_in.shape[2] // LANES) * LANES
            bg = jnp.pad(ml_b_gates[j], (0, LANES - ml_b_gates.shape[1])).reshape(1, LANES)
            qkv, og, gates = _proj_call(_proj_ml_kernel, xa, mods, g_mix, _pad_cols(ml_w_in[j], n_in).astype(BF16),
                                        [(bg, False)], [2 * ML_HEADS * ML_DK + ML_HEADS * ML_DV, ML_HEADS * ML_DV, LANES],
                                        [BF16, BF16, F32], "proj_ml")
            y = _ml_mixer(qkv, og, gates, ml_norm_g[j])
            w_o = ml_w_o[j]
        elif kind == 2:
            qkg = jnp.concatenate([sw_qk_g[j], sw_qk_g[j]], axis=1)
            cs, sn = _rope_tables(seq)
            q, kv = _proj_call(_proj_sw_kernel, xa, mods, g_mix, sw_w_qkv[j].astype(BF16),
                               [(qkg, False), (cs, True), (sn, True)],
                               [SW_HEADS * SW_HD, 2 * SW_KV * SW_HD], [BF16, BF16], "proj_sw")
            y = _sw_attention(q, kv, sw_sink[j])
            w_o = sw_w_o[j]
        else:
            n_in = -(-gl_w_in.shape[2] // LANES) * LANES
            nk = GL_HEADS * GL_DK
            wa = jnp.zeros((LANES, 2 * nk), F32)
            wa = wa.at[:GL_RANK, :nk].set(gl_w_a2[j, 0]).at[GL_RANK:2 * GL_RANK, nk:].set(gl_w_a2[j, 1])
            ba = gl_b_a[j].reshape(1, 2 * nk)
            qkv, gate, la = _proj_call(_proj_gl_kernel, xa, mods, g_mix, _pad_cols(gl_w_in[j], n_in).astype(BF16),
                                       [(wa.astype(BF16), False), (ba, False)],
                                       [2 * nk + GL_HEADS * GL_DV, GL_HEADS * GL_DV, 2 * nk], [BF16, BF16, F32], "proj_gl")
            y = _gl_mixer(qkv, gate, la, gl_norm_g[j])
            w_o = gl_w_o[j]
        xa = _moe(y, xa, mods, w_o.astype(BF16), norm_ffn_g[i], moe_w_grp[i], moe_b_grp[i], moe_w_exp[i], moe_b_exp[i],
                  moe_w_gate[i], moe_w_up[i], moe_w_down[i])
    return xa[:, CTX_LEN:]
```

```python
import functools

import jax
import jax.numpy as jnp
import numpy as np
from jax import lax
from jax.experimental import pallas as pl
from jax.experimental.pallas import tpu as pltpu

F32 = jnp.float32
BF16 = jnp.bfloat16
HIGHEST = lax.Precision.HIGHEST

D_MODEL = 1024
CTX_LEN = 256
GRID_W = 64
RMS_EPS = 1e-6
NEG_INF = -1e30
ROPE_BASE = 10000.0

NA_HEADS, NA_HD, NA_WIN_R, NA_WIN_C = 8, 128, 8, 16
NA_QROWS = 4
NA_KROWS = NA_QROWS + NA_WIN_R - 1
ML_HEADS, ML_DK, ML_DV, ML_CHUNK, ML_CAP = 8, 64, 128, 64, 15.0
SW_HEADS, SW_KV, SW_HD, SW_WINDOW = 16, 4, 64, 128
GL_HEADS, GL_DK, GL_DV, GL_RANK, GL_TAU, GL_CHUNK = 4, 128, 256, 16, 16.0, 64
MOE_GROUPS, MOE_PER_GROUP, MOE_EXPERTS, MOE_FF, MOE_BLOCK = 4, 8, 32, 512, 128

LANES = 128
SUBLANES = 8
TM = 256
VMEM_LIMIT = 48 * 1024 * 1024


def _cparams(*sem):
    return pltpu.CompilerParams(dimension_semantics=sem, vmem_limit_bytes=VMEM_LIMIT)


def _norm_mod(x, g, mod_ref, k):
    y = x * lax.rsqrt(jnp.mean(x * x, axis=-1, keepdims=True) + RMS_EPS) * g
    return y * (1.0 + mod_ref[k + 1:k + 2, :]) + mod_ref[k:k + 1, :]


def _log_sigmoid(x):
    return jnp.minimum(x, 0.0) - jnp.log(1.0 + jnp.exp(-jnp.abs(x)))


def _silu(x):
    return x * jax.nn.sigmoid(x)


def _mod_kernel(c_ref, w_ref, b_ref, o_ref):
    s = _silu(c_ref[...])
    o_ref[...] = jnp.dot(s.astype(BF16), w_ref[...].astype(BF16), preferred_element_type=F32) + b_ref[...]


def _ada_mods(cond, ada_w, ada_b):
    depth, d, _ = ada_w.shape
    rows = cond.shape[0]
    out = pl.pallas_call(
        _mod_kernel,
        grid=(depth, 6),
        in_specs=[pl.BlockSpec((rows, d), lambda i, n: (0, 0)),
                  pl.BlockSpec((None, d, d), lambda i, n: (i, 0, n)),
                  pl.BlockSpec((None, 1, d), lambda i, n: (i, 0, n))],
        out_specs=pl.BlockSpec((None, rows, d), lambda i, n: (i, 0, n)),
        out_shape=jax.ShapeDtypeStruct((depth, rows, 6 * d), F32),
        compiler_params=_cparams("arbitrary", "arbitrary"),
        name="ada_mods",
    )(cond, ada_w, ada_b.reshape(depth, 1, 6 * d))
    return out.reshape(depth, rows, 6, d)


def _mod_spec(nb):
    return pl.BlockSpec((None, 6, D_MODEL), lambda b, t: (jnp.where(t == 0, nb, b), 0, 0))


def _proj_call(kernel, x, mods, g, w, extras, out_cols, out_dtypes, name):
    nb, l, d = x.shape
    nt = l // TM
    n = w.shape[1]
    in_specs = [pl.BlockSpec((None, TM, d), lambda b, t: (b, t, 0)),
                _mod_spec(nb),
                pl.BlockSpec((1, d), lambda b, t: (0, 0)),
                pl.BlockSpec((d, n), lambda b, t: (0, 0))]
    args = [x, mods, g.reshape(1, d), w]
    for e, per_tile in extras:
        if per_tile:
            in_specs.append(pl.BlockSpec((TM, e.shape[1]), lambda b, t: (t, 0)))
        else:
            in_specs.append(pl.BlockSpec(e.shape, lambda b, t: (0, 0)))
        args.append(e)
    return pl.pallas_call(
        kernel,
        grid=(nb, nt),
        in_specs=in_specs,
        out_specs=[pl.BlockSpec((None, TM, c), lambda b, t: (b, t, 0)) for c in out_cols],
        out_shape=[jax.ShapeDtypeStruct((nb, l, c), dt) for c, dt in zip(out_cols, out_dtypes)],
        compiler_params=_cparams("arbitrary", "arbitrary"),
        name=name,
    )(*args)


def _proj_na_kernel(x_ref, mod_ref, g_ref, w_ref, qkg_ref, o_ref):
    h = _norm_mod(x_ref[...], g_ref[...], mod_ref, 0)
    u = jnp.dot(h.astype(BF16), w_ref[...], preferred_element_type=F32)
    nq = NA_HEADS * NA_HD
    for part in range(2):
        gain = qkg_ref[part:part + 1, :]
        for hh in range(NA_HEADS):
            lo = part * nq + hh * NA_HD
            z = u[:, lo:lo + NA_HD]
            z = z * lax.rsqrt(jnp.mean(z * z, axis=-1, keepdims=True) + RMS_EPS) * gain
            o_ref[:, lo:lo + NA_HD] = z.astype(BF16)
    o_ref[:, 2 * nq:] = u[:, 2 * nq:].astype(BF16)


def _head64_rms(z, gain):
    lane = lax.broadcasted_iota(jnp.int32, z.shape, 1)
    lo = lane < SW_HD
    zz = z * z
    s_lo = jnp.sum(jnp.where(lo, zz, 0.0), axis=-1, keepdims=True)
    s_hi = jnp.sum(jnp.where(lo, 0.0, zz), axis=-1, keepdims=True)
    ms = jnp.where(lo, s_lo, s_hi) * (1.0 / SW_HD)
    return z * lax.rsqrt(ms + RMS_EPS) * gain


def _rope_slab(z, cs, sn):
    lane = lax.broadcasted_iota(jnp.int32, z.shape, 1)
    first = (lane % 32) < 16
    partner = jnp.where(first, pltpu.roll(z, LANES - 16, 1), pltpu.roll(z, 16, 1))
    return z * cs + partner * sn


def _proj_sw_kernel(x_ref, mod_ref, g_ref, w_ref, qkg_ref, cs_ref, sn_ref, q_ref, kv_ref):
    h = _norm_mod(x_ref[...], g_ref[...], mod_ref, 0)
    u = jnp.dot(h.astype(BF16), w_ref[...], preferred_element_type=F32)
    cs, sn = cs_ref[...], sn_ref[...]
    nq, nk = SW_HEADS * SW_HD, SW_KV * SW_HD
    scale = SW_HD ** -0.5
    for s in range(nq // LANES):
        z = _head64_rms(u[:, s * LANES:(s + 1) * LANES], qkg_ref[0:1, :])
        q_ref[:, s * LANES:(s + 1) * LANES] = (_rope_slab(z, cs, sn) * scale).astype(BF16)
    for s in range(nk // LANES):
        z = _head64_rms(u[:, nq + s * LANES:nq + (s + 1) * LANES], qkg_ref[1:2, :])
        kv_ref[:, s * LANES:(s + 1) * LANES] = _rope_slab(z, cs, sn).astype(BF16)
    kv_ref[:, nk:] = u[:, nq + nk:].astype(BF16)


def _proj_ml_kernel(x_ref, mod_ref, g_ref, w_ref, bg_ref, qkv_ref, og_ref, gt_ref):
    h = _norm_mod(x_ref[...], g_ref[...], mod_ref, 0)
    u = jnp.dot(h.astype(BF16), w_ref[...], preferred_element_type=F32)
    nq, nv = ML_HEADS * ML_DK, ML_HEADS * ML_DV
    qkv_ref[:, :nq] = (u[:, :nq] * (ML_DK ** -0.5)).astype(BF16)
    qkv_ref[:, nq:] = u[:, nq:2 * nq + nv].astype(BF16)
    og_ref[...] = jax.nn.sigmoid(u[:, 2 * nq + nv:2 * nq + 2 * nv]).astype(BF16)
    pre = u[:, 2 * nq + 2 * nv:] + bg_ref[...]
    pre = ML_CAP * jnp.tanh(pre / ML_CAP)
    lane = lax.broadcasted_iota(jnp.int32, pre.shape, 1)
    gt_ref[...] = jnp.where((lane % 16) >= 8, _log_sigmoid(pre), pre)


def _proj_gl_kernel(x_ref, mod_ref, g_ref, w_ref, wa_ref, ba_ref, qkv_ref, gate_ref, la_ref):
    h = _norm_mod(x_ref[...], g_ref[...], mod_ref, 0)
    u = jnp.dot(h.astype(BF16), w_ref[...], preferred_element_type=F32)
    nq, nv = GL_HEADS * GL_DK, GL_HEADS * GL_DV
    qkv_ref[...] = u[:, :2 * nq + nv].astype(BF16)
    gate_ref[...] = _silu(u[:, 2 * nq + nv:2 * nq + 2 * nv]).astype(BF16)
    z = u[:, 2 * nq + 2 * nv:]
    a = jnp.dot(z.astype(BF16), wa_ref[...], preferred_element_type=F32) + ba_ref[...]
    la_ref[...] = _log_sigmoid(a) * (1.0 / GL_TAU)


def _na_bias_table(rpb, n_rows):
    cq = np.arange(GRID_W)[:, None]
    ck = np.arange(GRID_W)[None, :]
    c0 = np.clip(cq - NA_WIN_C // 2, 0, GRID_W - NA_WIN_C)
    col_ok = (ck >= c0) & (ck < c0 + NA_WIN_C)
    dc = np.clip(ck - cq + NA_WIN_C - 1, 0, 2 * NA_WIN_C - 2)
    pick = ((dc[None] == np.arange(2 * NA_WIN_C - 1)[:, None, None]) & col_ok[None]).astype(np.float32)
    m = jnp.einsum('hrd,dqk->hqrk', rpb.astype(F32), jnp.asarray(pick), precision=HIGHEST)
    m = jnp.where(jnp.asarray(col_ok)[None, :, None, :], m, NEG_INF)
    blocks = []
    for jj in range(n_rows // NA_QROWS):
        ws = min(max(NA_QROWS * jj - NA_WIN_R // 2, 0), n_rows - NA_KROWS)
        per_row = []
        for ri in range(NA_QROWS):
            r = NA_QROWS * jj + ri
            r0 = min(max(r - NA_WIN_R // 2, 0), n_rows - NA_WIN_R)
            lead = r0 - ws
            d0 = r0 - r + NA_WIN_R - 1
            piece = jnp.pad(m[:, :, d0:d0 + NA_WIN_R, :],
                            ((0, 0), (0, 0), (lead, NA_KROWS - NA_WIN_R - lead), (0, 0)), constant_values=NEG_INF)
            per_row.append(piece.reshape(rpb.shape[0], GRID_W, NA_KROWS * GRID_W))
        blocks.append(jnp.concatenate(per_row, axis=1))
    return jnp.stack(blocks, axis=1)


def _na_kernel(q_ref, k_ref, v_ref, bias_ref, o_ref, *, n_rows):
    j = pl.program_id(1)
    nb = q_ref.shape[0]
    scale = NA_HD ** -0.5
    nt = (((1,), (1,)), ((), ()))
    nkw = NA_KROWS * GRID_W

    @pl.when(j == 0)
    def _ctx():
        def body(b, c):
            q = q_ref[b]
            s = lax.dot_general(q, k_ref[b, 0:CTX_LEN, :], nt, preferred_element_type=F32) * scale
            p = jnp.exp(s - jnp.max(s, axis=-1, keepdims=True))
            l = jnp.sum(p, axis=-1, keepdims=True)
            o = jnp.dot(p.astype(BF16), v_ref[b, 0:CTX_LEN, :], preferred_element_type=F32)
            o_ref[b] = (o / l).astype(BF16)
            return c
        lax.fori_loop(0, nb, body, 0)

    @pl.when(j > 0)
    def _lat():
        ws = jnp.clip(NA_QROWS * (j - 1) - NA_WIN_R // 2, 0, n_rows - NA_KROWS)
        start = pl.multiple_of(CTX_LEN + ws * GRID_W, GRID_W)
        bias = bias_ref[...]

        def body(b, c):
            q = q_ref[b]
            s_n = lax.dot_general(q, k_ref[b, pl.ds(start, nkw), :], nt, preferred_element_type=F32) * scale + bias
            s_c = lax.dot_general(q, k_ref[b, 0:CTX_LEN, :], nt, preferred_element_type=F32) * scale
            m = jnp.maximum(jnp.max(s_n, axis=-1, keepdims=True), jnp.max(s_c, axis=-1, keepdims=True))
            p_n = jnp.exp(s_n - m)
            p_c = jnp.exp(s_c - m)
            l = jnp.sum(p_n, axis=-1, keepdims=True) + jnp.sum(p_c, axis=-1, keepdims=True)
            o = (jnp.dot(p_n.astype(BF16), v_ref[b, pl.ds(start, nkw), :], preferred_element_type=F32)
                 + jnp.dot(p_c.astype(BF16), v_ref[b, 0:CTX_LEN, :], preferred_element_type=F32))
            o_ref[b] = (o / l).astype(BF16)
            return c
        lax.fori_loop(0, nb, body, 0)


def _na_attention(qkv, bias_tab):
    nb, l, _ = qkv.shape
    n_rows = (l - CTX_LEN) // GRID_W
    nt = l // TM
    hq = NA_HEADS
    return pl.pallas_call(
        functools.partial(_na_kernel, n_rows=n_rows),
        grid=(hq, nt),
        in_specs=[pl.BlockSpec((nb, TM, NA_HD), lambda h, j: (0, j, h)),
                  pl.BlockSpec((nb, l, NA_HD), lambda h, j: (0, 0, hq + h)),
                  pl.BlockSpec((nb, l, NA_HD), lambda h, j: (0, 0, 2 * hq + h)),
                  pl.BlockSpec((None, None, TM, NA_KROWS * GRID_W),
                               lambda h, j: (h, jnp.maximum(j - 1, 0), 0, 0))],
        out_specs=pl.BlockSpec((nb, TM, NA_HD), lambda h, j: (0, j, h)),
        out_shape=jax.ShapeDtypeStruct((nb, l, hq * NA_HD), BF16),
        compiler_params=_cparams("arbitrary", "arbitrary"),
        name="na_attention",
    )(qkv, qkv, qkv, bias_tab)


def _sw_kernel(sink_ref, q_ref, kv_ref, o_ref, *, seq):
    j = pl.program_id(1)
    nt = (((1,), (1,)), ((), ()))
    nkv = SW_KV * SW_HD
    grp = SW_HEADS // SW_KV
    kwin = TM + 2 * SW_WINDOW

    def head_cols(h):
        return slice(h * SW_HD, (h + 1) * SW_HD)

    @pl.when(j == 0)
    def _ctx():
        for h in range(SW_HEADS):
            hk = h // grp
            sink = sink_ref[h]
            q = q_ref[:, head_cols(h)]
            s = lax.dot_general(q, kv_ref[0:CTX_LEN, head_cols(hk)], nt, preferred_element_type=F32)
            m = jnp.maximum(jnp.max(s, axis=-1, keepdims=True), sink)
            p = jnp.exp(s - m)
            l = jnp.sum(p, axis=-1, keepdims=True) + jnp.exp(sink - m)
            o = jnp.dot(p.astype(BF16), kv_ref[0:CTX_LEN, nkv + hk * SW_HD:nkv + (hk + 1) * SW_HD],
                        preferred_element_type=F32)
            o_ref[:, head_cols(h)] = (o / l).astype(BF16)

    @pl.when(j > 0)
    def _lat():
        q0 = (j - 1) * TM
        ws = jnp.clip(q0 - SW_WINDOW, 0, seq - kwin)
        start = pl.multiple_of(CTX_LEN + ws, SW_WINDOW)
        qpos = q0 + lax.broadcasted_iota(jnp.int32, (TM, kwin), 0)
        kpos = ws + lax.broadcasted_iota(jnp.int32, (TM, kwin), 1)
        ok = jnp.abs(qpos - kpos) <= SW_WINDOW
        for h in range(SW_HEADS):
            hk = h // grp
            sink = sink_ref[h]
            q = q_ref[:, head_cols(h)]
            vcols = slice(nkv + hk * SW_HD, nkv + (hk + 1) * SW_HD)
            s_w = lax.dot_general(q, kv_ref[pl.ds(start, kwin), head_cols(hk)], nt, preferred_element_type=F32)
            s_w = jnp.where(ok, s_w, NEG_INF)
            s_c = lax.dot_general(q, kv_ref[0:CTX_LEN, head_cols(hk)], nt, preferred_element_type=F32)
            m = jnp.maximum(jnp.maximum(jnp.max(s_w, axis=-1, keepdims=True),
                                        jnp.max(s_c, axis=-1, keepdims=True)), sink)
            p_w = jnp.exp(s_w - m)
            p_c = jnp.exp(s_c - m)
            l = (jnp.sum(p_w, axis=-1, keepdims=True) + jnp.sum(p_c, axis=-1, keepdims=True)
                 + jnp.exp(sink - m))
            o = (jnp.dot(p_w.astype(BF16), kv_ref[pl.ds(start, kwin), vcols], preferred_element_type=F32)
                 + jnp.dot(p_c.astype(BF16), kv_ref[0:CTX_LEN, vcols], preferred_element_type=F32))
            o_ref[:, head_cols(h)] = (o / l).astype(BF16)


def _sw_attention(q, kv, sink):
    nb, l, nq = q.shape
    nt = l // TM
    return pl.pallas_call(
        functools.partial(_sw_kernel, seq=l - CTX_LEN),
        grid=(nb, nt),
        in_specs=[pl.BlockSpec(memory_space=pltpu.SMEM),
                  pl.BlockSpec((None, TM, nq), lambda b, j: (b, j, 0)),
                  pl.BlockSpec((None, l, kv.shape[2]), lambda b, j: (b, 0, 0))],
        out_specs=pl.BlockSpec((None, TM, nq), lambda b, j: (b, j, 0)),
        out_shape=jax.ShapeDtypeStruct((nb, l, nq), BF16),
        compiler_params=_cparams("arbitrary", "arbitrary"),
        name="sw_attention",
    )(sink, q, kv)


def _chunk_index(step, rev, n_ctx, n_all):
    if not rev:
        return step
    return jnp.where(step < n_ctx, n_ctx - 1 - step, n_all + n_ctx - 1 - step)


def _tri(n, rev):
    r = lax.broadcasted_iota(jnp.int32, (n, n), 0)
    c = lax.broadcasted_iota(jnp.int32, (n, n), 1)
    return (c >= r) if rev else (c <= r)


def _row_cumsum(x, rev):
    lane = lax.broadcasted_iota(jnp.int32, x.shape, 1)
    s = 1
    while s < ML_CHUNK:
        if rev:
            x = x + jnp.where(lane + s < ML_CHUNK, pltpu.roll(x, LANES - s, 1), 0.0)
        else:
            x = x + jnp.where(lane >= s, pltpu.roll(x, s, 1), 0.0)
        s *= 2
    return x


def _ml_kernel(q_ref, k_ref, v_ref, og_ref, gr_ref, ng_ref, y_ref, acc_ref, *, n_ctx, n_all):
    lc = ML_CHUNK
    hps = q_ref.shape[1] // ML_DK
    eye = _tri(lc, False) & _tri(lc, True)
    t0 = (((0,), (0,)), ((), ()))
    nt = (((1,), (1,)), ((), ()))

    def col(row):
        return jnp.sum(jnp.where(eye, jnp.broadcast_to(row, (lc, lc)), 0.0), axis=1, keepdims=True)

    def chain(step, hl, rev, state):
        ct, n, m = state
        c = _chunk_index(step, rev, n_ctx, n_all)
        r0 = pl.multiple_of(c * lc, lc)
        g8 = gr_ref[c]
        cum = _row_cumsum(g8, rev)
        base = hl * 4 + (2 if rev else 0)
        li_r = g8[base:base + 1, :lc]
        bf_r = cum[base + 1:base + 2, :lc]
        g = jnp.sum(g8[base + 1:base + 2, :lc], axis=1, keepdims=True)
        bf_c, li_c = col(bf_r), col(li_r)
        q = q_ref[pl.ds(r0, lc), hl * ML_DK:(hl + 1) * ML_DK]
        k = k_ref[pl.ds(r0, lc), hl * ML_DK:(hl + 1) * ML_DK]
        v = v_ref[pl.ds(r0, lc), hl * ML_DV:(hl + 1) * ML_DV]
        kf, vf = k.astype(F32), v.astype(F32)
        dm = jnp.where(_tri(lc, rev), bf_c - bf_r + li_r, NEG_INF)
        inter = bf_c + m
        m_t = jnp.maximum(inter, jnp.max(dm, axis=1, keepdims=True))
        sc = lax.dot_general(q, k, nt, preferred_element_type=F32) * jnp.exp(dm - m_t)
        e_int = jnp.exp(inter - m_t)
        num = (jnp.dot(sc.astype(BF16), v, preferred_element_type=F32)
               + e_int * jnp.dot(q, ct.astype(BF16), preferred_element_type=F32))
        den = jnp.sum(sc, axis=1, keepdims=True) + e_int * jnp.sum(q.astype(F32) * n, axis=1, keepdims=True)
        hout = num / jnp.maximum(jnp.abs(den), jnp.exp(-m_t))
        acc_ref[pl.ds(r0, lc), hl * ML_DV:(hl + 1) * ML_DV] += hout
        a = g - bf_c + li_c
        m_loc = jnp.max(a, axis=0, keepdims=True)
        w = jnp.exp(a - m_loc)
        c_loc = lax.dot_general(k, (vf * w).astype(BF16), t0, preferred_element_type=F32)
        n_loc = jnp.sum(kf * w, axis=0, keepdims=True)
        m_new = jnp.maximum(g + m, m_loc)
        dec = jnp.exp(g + m - m_new)
        inc = jnp.exp(m_loc - m_new)
        return dec * ct + inc * c_loc, dec * n + inc * n_loc, m_new

    acc_ref[...] = jnp.zeros_like(acc_ref)
    chains = [(hl, rev) for hl in range(hps) for rev in (False, True)]
    init = tuple((jnp.zeros((ML_DK, ML_DV), F32), jnp.zeros((1, ML_DK), F32), jnp.zeros((1, 1), F32))
                 for _ in chains)

    def step_fn(step, states):
        return tuple(chain(step, hl, rev, st) for (hl, rev), st in zip(chains, states))

    lax.fori_loop(0, n_all, step_fn, init)

    for hl in range(hps):
        cols = slice(hl * ML_DV, (hl + 1) * ML_DV)
        z = acc_ref[:, cols]
        z = z * lax.rsqrt(jnp.mean(z * z, axis=-1, keepdims=True) + RMS_EPS) * ng_ref[:, cols]
        y_ref[:, cols] = (z.astype(BF16) * og_ref[:, cols])


def _ml_mixer(qkv, og, gates, norm_g):
    nb, l, _ = qkv.shape
    lc = ML_CHUNK
    n_all, n_ctx = l // lc, CTX_LEN // lc
    hps = 2
    ngrp = ML_HEADS // hps
    nq = ML_HEADS * ML_DK
    g = gates[:, :, :4 * ML_HEADS].reshape(nb, n_all, lc, 2, 2, ngrp, hps)
    g = g.transpose(0, 5, 1, 6, 3, 4, 2).reshape(nb, ngrp, n_all, hps * 4, lc)
    g = jnp.pad(g, ((0, 0),) * 4 + ((0, LANES - lc),))
    qb, vb = hps * ML_DK // LANES, hps * ML_DV
    return pl.pallas_call(
        functools.partial(_ml_kernel, n_ctx=n_ctx, n_all=n_all),
        grid=(nb, ngrp),
        in_specs=[pl.BlockSpec((None, l, hps * ML_DK), lambda b, h: (b, 0, h)),
                  pl.BlockSpec((None, l, hps * ML_DK), lambda b, h: (b, 0, nq // (hps * ML_DK) + h)),
                  pl.BlockSpec((None, l, vb), lambda b, h: (b, 0, 2 * nq // vb + h)),
                  pl.BlockSpec((None, l, vb), lambda b, h: (b, 0, h)),
                  pl.BlockSpec((None, None, n_all, hps * 4, LANES), lambda b, h: (b, h, 0, 0, 0)),
                  pl.BlockSpec((1, vb), lambda b, h: (0, h))],
        out_specs=pl.BlockSpec((None, l, vb), lambda b, h: (b, 0, h)),
        out_shape=jax.ShapeDtypeStruct((nb, l, ML_HEADS * ML_DV), BF16),
        scratch_shapes=[pltpu.VMEM((l, vb), F32)],
        compiler_params=_cparams("arbitrary", "arbitrary"),
        name="mlstm_mixer",
    )(qkv, qkv, qkv, og, g, norm_g.reshape(1, -1))


def _gl_kernel(q_ref, k_ref, v_ref, gate_ref, la0_ref, la1_ref, ng_ref, y_ref, acc_ref, st_ref, *, n_ctx, n_all):
    lc = GL_CHUNK
    t0 = (((0,), (0,)), ((), ()))
    nt = (((1,), (1,)), ((), ()))
    scale = GL_DK ** -0.5

    def chain(step, rev):
        la_ref = la1_ref if rev else la0_ref
        sidx = 1 if rev else 0
        c = _chunk_index(step, rev, n_ctx, n_all)
        r0 = pl.multiple_of(c * lc, lc)
        tri = _tri(lc, rev)
        la = la_ref[pl.ds(r0, lc), :]
        bc = jnp.dot(tri.astype(F32), la, precision=HIGHEST, preferred_element_type=F32)
        g = jnp.sum(la, axis=0, keepdims=True)
        q = q_ref[pl.ds(r0, lc), :].astype(F32) * scale
        k = k_ref[pl.ds(r0, lc), :].astype(F32)
        v = v_ref[pl.ds(r0, lc), :]
        st = st_ref[sidx]
        q_t = (q * jnp.exp(bc)).astype(BF16)
        k_t = (k * jnp.exp(-bc)).astype(BF16)
        att = jnp.where(tri, lax.dot_general(q_t, k_t, nt, preferred_element_type=F32), 0.0)
        o = (jnp.dot(att.astype(BF16), v, preferred_element_type=F32)
             + lax.dot_general(q_t, st.astype(BF16), nt, preferred_element_type=F32))
        acc_ref[pl.ds(r0, lc), :] += o
        kd = (k * jnp.exp(g - bc)).astype(BF16)
        s_loc = lax.dot_general(v, kd, t0, preferred_element_type=F32)
        st_ref[sidx] = st * jnp.exp(g) + s_loc

    acc_ref[...] = jnp.zeros_like(acc_ref)
    st_ref[...] = jnp.zeros_like(st_ref)

    def step_fn(step, carry):
        chain(step, False)
        chain(step, True)
        return carry

    lax.fori_loop(0, n_all, step_fn, 0)
    z = acc_ref[...]
    z = z * lax.rsqrt(jnp.mean(z * z, axis=-1, keepdims=True) + RMS_EPS) * ng_ref[...]
    y_ref[...] = z.astype(BF16) * gate_ref[...]


def _gl_mixer(qkv, gate, la, norm_g):
    nb, l, _ = qkv.shape
    lc = GL_CHUNK
    n_all, n_ctx = l // lc, CTX_LEN // lc
    hq = GL_HEADS
    return pl.pallas_call(
        functools.partial(_gl_kernel, n_ctx=n_ctx, n_all=n_all),
        grid=(nb, hq),
        in_specs=[pl.BlockSpec((None, l, GL_DK), lambda b, h: (b, 0, h)),
                  pl.BlockSpec((None, l, GL_DK), lambda b, h: (b, 0, hq + h)),
                  pl.BlockSpec((None, l, GL_DV), lambda b, h: (b, 0, hq + h)),
                  pl.BlockSpec((None, l, GL_DV), lambda b, h: (b, 0, h)),
                  pl.BlockSpec((None, l, GL_DK), lambda b, h: (b, 0, h)),
                  pl.BlockSpec((None, l, GL_DK), lambda b, h: (b, 0, hq + h)),
                  pl.BlockSpec((1, GL_DV), lambda b, h: (0, h))],
        out_specs=pl.BlockSpec((None, l, GL_DV), lambda b, h: (b, 0, h)),
        out_shape=jax.ShapeDtypeStruct((nb, l, hq * GL_DV), BF16),
        scratch_shapes=[pltpu.VMEM((l, GL_DV), F32), pltpu.VMEM((2, GL_DV, GL_DK), F32)],
        compiler_params=_cparams("arbitrary", "arbitrary"),
        name="gla_mixer",
    )(qkv, qkv, qkv, gate, la, la, norm_g.reshape(1, -1))


def _store_token_tiles(ref, val):
    rows = val.shape[0]
    for s in range(val.shape[1] // LANES):
        ref[pl.ds(s, rows, stride=SUBLANES), :] = val[:, s * LANES:(s + 1) * LANES]


def _load_token_tiles(ref, rows):
    return jnp.concatenate([ref[pl.ds(s, rows, stride=SUBLANES), :] for s in range(SUBLANES)], axis=1)


def _post_kernel(y_ref, x_ref, mod_ref, wo_ref, g_ref, wr_ref, br_ref, xo_ref, f_ref, rt_ref, cnt_ref, run_ref):
    first = (pl.program_id(0) == 0) & (pl.program_id(1) == 0)

    @pl.when(first)
    def _():
        run_ref[...] = jnp.zeros_like(run_ref)

    o = jnp.dot(y_ref[...], wo_ref[...], preferred_element_type=F32)
    xn = x_ref[...] + mod_ref[2:3, :] * o
    xo_ref[...] = xn
    f = _norm_mod(xn, g_ref[...], mod_ref, 3)
    _store_token_tiles(f_ref, f)
    lg = jnp.dot(f, wr_ref[...], precision=HIGHEST, preferred_element_type=F32) + br_ref[...]
    lane = lax.broadcasted_iota(jnp.int32, lg.shape, 1).astype(F32)
    big = float(LANES)

    def masked_max(mask):
        return jnp.max(jnp.where(mask, lg, -jnp.inf), axis=-1, keepdims=True)

    def first_at(mask, val):
        return jnp.min(jnp.where(mask & (lg == val), lane, big), axis=-1, keepdims=True)

    gm = lane < MOE_GROUPS
    mg = masked_max(gm)
    g_idx = first_at(gm, mg)
    g_w = 1.0 / jnp.sum(jnp.where(gm, jnp.exp(lg - mg), 0.0), axis=-1, keepdims=True)
    lo = MOE_GROUPS + MOE_PER_GROUP * g_idx
    em = (lane >= lo) & (lane < lo + MOE_PER_GROUP)
    v0 = masked_max(em)
    i0 = first_at(em, v0)
    em1 = em & (lane != i0)
    v1 = masked_max(em1)
    i1 = first_at(em1, v1)
    e0, e1 = i0 - MOE_GROUPS, i1 - MOE_GROUPS
    t = jnp.exp(v1 - v0)
    w0 = g_w / (1.0 + t)
    w1 = g_w * t / (1.0 + t)
    oh = ((lane == e0) | (lane == e1)).astype(F32)
    rows = lg.shape[0]
    below = lax.broadcasted_iota(jnp.int32, (rows, rows), 0) > lax.broadcasted_iota(jnp.int32, (rows, rows), 1)
    tot = run_ref[...] + jnp.dot(below.astype(BF16), oh.astype(BF16), preferred_element_type=F32)
    r0 = jnp.sum(jnp.where(lane == e0, tot, 0.0), axis=-1, keepdims=True)
    r1 = jnp.sum(jnp.where(lane == e1, tot, 0.0), axis=-1, keepdims=True)
    run_ref[...] = run_ref[...] + jnp.sum(oh, axis=0, keepdims=True)
    cnt_ref[...] = jnp.broadcast_to(run_ref[...], cnt_ref.shape)
    out = jnp.zeros_like(lg)
    for i, val in enumerate((e0, e1, w0, w1, r0, r1)):
        out = jnp.where(lane == i, val, out)
    rt_ref[...] = out


def _post_call(y, x, mods, w_o, g_ffn, w_r, b_r):
    nb, l, d = x.shape
    nt = l // TM
    tile = lambda b, t: (b * nt + t, 0)
    return pl.pallas_call(
        _post_kernel,
        grid=(nb, nt),
        in_specs=[pl.BlockSpec((None, TM, d), lambda b, t: (b, t, 0)),
                  pl.BlockSpec((None, TM, d), lambda b, t: (b, t, 0)),
                  _mod_spec(nb),
                  pl.BlockSpec((d, d), lambda b, t: (0, 0)),
                  pl.BlockSpec((1, d), lambda b, t: (0, 0)),
                  pl.BlockSpec((d, LANES), lambda b, t: (0, 0)),
                  pl.BlockSpec((1, LANES), lambda b, t: (0, 0))],
        out_specs=[pl.BlockSpec((None, TM, d), lambda b, t: (b, t, 0)),
                   pl.BlockSpec((TM * SUBLANES, LANES), tile),
                   pl.BlockSpec((TM, LANES), tile),
                   pl.BlockSpec((SUBLANES, LANES), lambda b, t: (0, 0))],
        out_shape=[jax.ShapeDtypeStruct((nb, l, d), F32),
                   jax.ShapeDtypeStruct((nb * l * SUBLANES, LANES), F32),
                   jax.ShapeDtypeStruct((nb * l, LANES), F32),
                   jax.ShapeDtypeStruct((SUBLANES, LANES), F32)],
        scratch_shapes=[pltpu.VMEM((1, LANES), F32)],
        compiler_params=_cparams("arbitrary", "arbitrary"),
        name="outproj_router",
    )(y, x, mods, w_o, g_ffn.reshape(1, d), w_r, b_r)


def _dispatch_kernel(dest_ref, f_ref, xs_in, xs_out, sem):
    del xs_in
    base = pl.program_id(0) * TM

    def copy(t, k):
        d = dest_ref[2 * (base + t) + k]
        return pltpu.make_async_copy(f_ref.at[pl.ds(t * SUBLANES, SUBLANES), :],
                                     xs_out.at[pl.ds(d * SUBLANES, SUBLANES), :], sem)

    def issue(t, c):
        copy(t, 0).start()
        copy(t, 1).start()
        return c

    def drain(t, c):
        copy(t, 0).wait()
        copy(t, 1).wait()
        return c

    lax.fori_loop(0, TM, issue, 0)
    lax.fori_loop(0, TM, drain, 0)


def _dispatch(dest, f_tiles, n_slots):
    n_tok = f_tiles.shape[0] // SUBLANES
    xs0 = jnp.zeros((n_slots * SUBLANES, LANES), F32)
    return pl.pallas_call(
        _dispatch_kernel,
        grid_spec=pltpu.PrefetchScalarGridSpec(
            num_scalar_prefetch=1,
            grid=(n_tok // TM,),
            in_specs=[pl.BlockSpec((TM * SUBLANES, LANES), lambda i, dr: (i, 0)),
                      pl.BlockSpec(memory_space=pl.ANY)],
            out_specs=pl.BlockSpec(memory_space=pl.ANY),
            scratch_shapes=[pltpu.SemaphoreType.DMA(())]),
        out_shape=jax.ShapeDtypeStruct(xs0.shape, F32),
        input_output_aliases={2: 0},
        compiler_params=_cparams("arbitrary"),
        name="moe_dispatch",
    )(dest, f_tiles, xs0)


def _expert_kernel(blk_e_ref, nact_ref, xs_ref, wg_ref, wu_ref, wd_ref, ys_ref, wg_s, wu_s, wd_s):
    i = pl.program_id(0)
    active = i < nact_ref[0]
    changed = (i == 0) | (blk_e_ref[i] != blk_e_ref[jnp.maximum(i - 1, 0)])

    @pl.when(active & changed)
    def _():
        wg_s[...] = wg_ref[...].astype(BF16)
        wu_s[...] = wu_ref[...].astype(BF16)
        wd_s[...] = wd_ref[...].astype(BF16)

    @pl.when(active)
    def _():
        x = _load_token_tiles(xs_ref, MOE_BLOCK).astype(BF16)
        a = jnp.dot(x, wg_s[...], preferred_element_type=F32)
        u = jnp.dot(x, wu_s[...], preferred_element_type=F32)
        y = jnp.dot((_silu(a) * u).astype(BF16), wd_s[...], preferred_element_type=F32)
        _store_token_tiles(ys_ref, y)

    @pl.when(jnp.logical_not(active))
    def _():
        ys_ref[...] = jnp.zeros_like(ys_ref)


def _experts(blk_e, nact, xs, w_gate, w_up, w_down):
    n_blk = blk_e.shape[0]
    _, d, ff = w_gate.shape
    rows = MOE_BLOCK * SUBLANES
    blk = lambda i, be, na: (jnp.minimum(i, na[0] - 1), 0)
    wsel = lambda i, be, na: (be[jnp.minimum(i, na[0] - 1)], 0, 0)
    return pl.pallas_call(
        _expert_kernel,
        grid_spec=pltpu.PrefetchScalarGridSpec(
            num_scalar_prefetch=2,
            grid=(n_blk,),
            in_specs=[pl.BlockSpec((rows, LANES), blk),
                      pl.BlockSpec((None, d, ff), wsel),
                      pl.BlockSpec((None, d, ff), wsel),
                      pl.BlockSpec((None, ff, d), wsel)],
            out_specs=pl.BlockSpec((rows, LANES), lambda i, be, na: (i, 0)),
            scratch_shapes=[pltpu.VMEM((d, ff), BF16), pltpu.VMEM((d, ff), BF16), pltpu.VMEM((ff, d), BF16)]),
        out_shape=jax.ShapeDtypeStruct(xs.shape, F32),
        compiler_params=_cparams("arbitrary"),
        name="moe_experts",
    )(blk_e, nact, xs, w_gate, w_up, w_down)


def _combine_kernel(dest_ref, ys_hbm, x_ref, rt_ref, mod_ref, o_ref, ybuf, sem, *, nt):
    base = (pl.program_id(0) * nt + pl.program_id(1)) * TM

    def copy(t, k):
        d = dest_ref[2 * (base + t) + k]
        return pltpu.make_async_copy(ys_hbm.at[pl.ds(d * SUBLANES, SUBLANES), :],
                                     ybuf.at[k, pl.ds(t * SUBLANES, SUBLANES), :], sem)

    def issue(t, c):
        copy(t, 0).start()
        copy(t, 1).start()
        return c

    def drain(t, c):
        copy(t, 0).wait()
        copy(t, 1).wait()
        return c

    lax.fori_loop(0, TM, issue, 0)
    lax.fori_loop(0, TM, drain, 0)
    y0 = _load_token_tiles(ybuf.at[0], TM)
    y1 = _load_token_tiles(ybuf.at[1], TM)
    rt = rt_ref[...]
    y = rt[:, 2:3] * y0 + rt[:, 3:4] * y1
    o_ref[...] = x_ref[...] + mod_ref[5:6, :] * y


def _combine(dest, ys, x, rt, mods):
    nb, l, d = x.shape
    nt = l // TM
    return pl.pallas_call(
        functools.partial(_combine_kernel, nt=nt),
        grid_spec=pltpu.PrefetchScalarGridSpec(
            num_scalar_prefetch=1,
            grid=(nb, nt),
            in_specs=[pl.BlockSpec(memory_space=pl.ANY),
                      pl.BlockSpec((None, TM, d), lambda b, t, dr: (b, t, 0)),
                      pl.BlockSpec((TM, LANES), lambda b, t, dr: (b * nt + t, 0)),
                      pl.BlockSpec((None, 6, d), lambda b, t, dr: (jnp.where(t == 0, nb, b), 0, 0))],
            out_specs=pl.BlockSpec((None, TM, d), lambda b, t, dr: (b, t, 0)),
            scratch_shapes=[pltpu.VMEM((2, TM * SUBLANES, LANES), F32), pltpu.SemaphoreType.DMA(())]),
        out_shape=jax.ShapeDtypeStruct(x.shape, F32),
        compiler_params=_cparams("arbitrary", "arbitrary"),
        name="moe_combine",
    )(dest, ys, x, rt, mods)


def _moe(y_mix, x, mods, w_o, g_ffn, w_grp, b_grp, w_exp, b_exp, w_gate, w_up, w_down):
    nb, l, d = x.shape
    n_tok = nb * l
    nr = MOE_GROUPS + MOE_EXPERTS
    w_r = jnp.pad(jnp.concatenate([w_grp, w_exp], axis=1), ((0, 0), (0, LANES - nr)))
    b_r = jnp.pad(jnp.concatenate([b_grp, b_exp]), (0, LANES - nr)).reshape(1, LANES)
    x_new, f_tiles, rt, cnt = _post_call(y_mix, x, mods, w_o, g_ffn, w_r, b_r)
    counts = cnt[0, :MOE_EXPERTS].astype(jnp.int32)
    padded = (counts + MOE_BLOCK - 1) // MOE_BLOCK * MOE_BLOCK
    p_end = jnp.cumsum(padded)
    p_start = p_end - padded
    n_pairs = 2 * n_tok
    n_blk = -(-(n_pairs + MOE_EXPERTS * (MOE_BLOCK - 1)) // MOE_BLOCK)
    eid = rt[:, 0:2].astype(jnp.int32)
    first = jnp.sum(jnp.where(eid[:, :, None] == jnp.arange(MOE_EXPERTS, dtype=jnp.int32), p_start, 0), axis=-1)
    dest = (first + rt[:, 4:6].astype(jnp.int32)).reshape(-1)
    blk_row = jnp.arange(n_blk, dtype=jnp.int32) * MOE_BLOCK
    blk_e = jnp.minimum(jnp.sum((p_end[None, :] <= blk_row[:, None]).astype(jnp.int32), axis=1), MOE_EXPERTS - 1)
    nact = (p_end[-1:] // MOE_BLOCK).astype(jnp.int32)
    xs = _dispatch(dest, f_tiles, n_blk * MOE_BLOCK)
    ys = _experts(blk_e, nact, xs, w_gate, w_up, w_down)
    return _combine(dest, ys, x_new, rt, mods)


def _rope_tables(seq):
    nf = SW_HD // 4
    inv = ROPE_BASE ** (-jnp.arange(nf, dtype=F32) / nf)
    pos = jnp.arange(seq, dtype=jnp.int32)
    rows, cols = (pos // GRID_W).astype(F32), (pos % GRID_W).astype(F32)
    lane = jnp.arange(LANES)
    p = jnp.where((lane % SW_HD < SW_HD // 2)[None, :], rows[:, None], cols[:, None])
    ang = p * inv[lane % nf][None, :]
    sign = jnp.where((lane % (2 * nf)) < nf, -1.0, 1.0)[None, :]
    cs = jnp.concatenate([jnp.ones((CTX_LEN, LANES), F32), jnp.cos(ang)], axis=0)
    sn = jnp.concatenate([jnp.zeros((CTX_LEN, LANES), F32), jnp.sin(ang) * sign], axis=0)
    return cs, sn


def _pad_cols(w, n):
    return jnp.pad(w, ((0, 0), (0, n - w.shape[1])))


def kernel(x, c, ctx, c_ctx, ada_w, ada_b, norm_mix_g, norm_ffn_g, na_w_qkv, na_qk_g, na_rpb, na_w_o, ml_w_in, ml_b_gates, ml_norm_g, ml_w_o, sw_w_qkv, sw_qk_g, sw_sink, sw_w_o, gl_w_in, gl_w_a2, gl_b_a, gl_norm_g, gl_w_o, moe_w_grp, moe_b_grp, moe_w_exp, moe_b_exp, moe_w_gate, moe_w_up, moe_w_down):
    nb, seq, d = x.shape
    depth = ada_w.shape[0]
    assert d == D_MODEL and ctx.shape[1] == CTX_LEN == TM and seq % TM == 0
    rows = -(-(nb + 1) // SUBLANES) * SUBLANES
    cond = jnp.pad(jnp.concatenate([c, c_ctx[None]], axis=0), ((0, rows - nb - 1), (0, 0)))
    mods_all = _ada_mods(cond, ada_w, ada_b)
    xa = jnp.concatenate([ctx, x], axis=1)
    for i in range(depth):
        j, kind = divmod(i, 4)
        mods = mods_all[i]
        g_mix = norm_mix_g[i]
        if kind == 0:
            qkg = na_qk_g[j]
            (qkv,) = _proj_call(_proj_na_kernel, xa, mods, g_mix, na_w_qkv[j].astype(BF16), [(qkg, False)],
                                [3 * NA_HEADS * NA_HD], [BF16], "proj_na")
            y = _na_attention(qkv, _na_bias_table(na_rpb[j], seq // GRID_W))
            w_o = na_w_o[j]
        elif kind == 1:
            n_in = -(-ml_w_in.shape[2] // LANES) * LANES
            bg = jnp.pad(ml_b_gates[j], (0, LANES - ml_b_gates.shape[1])).reshape(1, LANES)
            qkv, og, gates = _proj_call(_proj_ml_kernel, xa, mods, g_mix, _pad_cols(ml_w_in[j], n_in).astype(BF16),
                                        [(bg, False)], [2 * ML_HEADS * ML_DK + ML_HEADS * ML_DV, ML_HEADS * ML_DV, LANES],
                                        [BF16, BF16, F32], "proj_ml")
            y = _ml_mixer(qkv, og, gates, ml_norm_g[j])
            w_o = ml_w_o[j]
        elif kind == 2:
            qkg = jnp.concatenate([sw_qk_g[j], sw_qk_g[j]], axis=1)
            cs, sn = _rope_tables(seq)
            q, kv = _proj_call(_proj_sw_kernel, xa, mods, g_mix, sw_w_qkv[j].astype(BF16),
                               [(qkg, False), (cs, True), (sn, True)],
                               [SW_HEADS * SW_HD, 2 * SW_KV * SW_HD], [BF16, BF16], "proj_sw")
            y = _sw_attention(q, kv, sw_sink[j])
            w_o = sw_w_o[j]
        else:
            n_in = -(-gl_w_in.shape[2] // LANES) * LANES
            nk = GL_HEADS * GL_DK
            wa = jnp.zeros((LANES, 2 * nk), F32)
            wa = wa.at[:GL_RANK, :nk].set(gl_w_a2[j, 0]).at[GL_RANK:2 * GL_RANK, nk:].set(gl_w_a2[j, 1])
            ba = gl_b_a[j].reshape(1, 2 * nk)
            qkv, gate, la = _proj_call(_proj_gl_kernel, xa, mods, g_mix, _pad_cols(gl_w_in[j], n_in).astype(BF16),
                                       [(wa.astype(BF16), False), (ba, False)],
                                       [2 * nk + GL_HEADS * GL_DV, GL_HEADS * GL_DV, 2 * nk], [BF16, BF16, F32], "proj_gl")
            y = _gl_mixer(qkv, gate, la, gl_norm_g[j])
            w_o = gl_w_o[j]
        xa = _moe(y, xa, mods, w_o.astype(BF16), norm_ffn_g[i], moe_w_grp[i], moe_b_grp[i], moe_w_exp[i], moe_b_exp[i],
                  moe_w_gate[i], moe_w_up[i], moe_w_down[i])
    return xa[:, CTX_LEN:]
```

```python
import functools

import jax
import jax.numpy as jnp
import numpy as np
from jax import lax
from jax.experimental import pallas as pl
from jax.experimental.pallas import tpu as pltpu

F32 = jnp.float32
BF16 = jnp.bfloat16
HIGHEST = lax.Precision.HIGHEST

D_MODEL = 1024
CTX_LEN = 256
GRID_W = 64
RMS_EPS = 1e-6
NEG_INF = -1e30
ROPE_BASE = 10000.0

NA_HEADS, NA_HD, NA_WIN_R, NA_WIN_C = 8, 128, 8, 16
NA_QROWS = 4
NA_KROWS = NA_QROWS + NA_WIN_R - 1
ML_HEADS, ML_DK, ML_DV, ML_CAP = 8, 64, 128, 15.0
ML_SCAN = 256
SW_HEADS, SW_KV, SW_HD, SW_WINDOW = 16, 4, 64, 128
GL_HEADS, GL_DK, GL_DV, GL_RANK, GL_TAU = 4, 128, 256, 16, 16.0
GL_SCAN = 128
MOE_GROUPS, MOE_PER_GROUP, MOE_EXPERTS, MOE_FF = 4, 8, 32, 512
MOE_BLOCK = 256

LANES = 128
SUBLANES = 8
TM = 256
VMEM_LIMIT = 48 * 1024 * 1024
DMA_UNROLL = 8


def _cparams(*sem):
    return pltpu.CompilerParams(dimension_semantics=sem, vmem_limit_bytes=VMEM_LIMIT)


def _norm_mod(x, g, mod_ref, k):
    y = x * lax.rsqrt(jnp.mean(x * x, axis=-1, keepdims=True) + RMS_EPS) * g
    return y * (1.0 + mod_ref[k + 1:k + 2, :]) + mod_ref[k:k + 1, :]


def _log_sigmoid(x):
    return jnp.minimum(x, 0.0) - jnp.log(1.0 + jnp.exp(-jnp.abs(x)))


def _silu(x):
    return x * jax.nn.sigmoid(x)


def _mod_kernel(c_ref, w_ref, b_ref, o_ref):
    s = _silu(c_ref[...])
    o_ref[...] = jnp.dot(s.astype(BF16), w_ref[...].astype(BF16), preferred_element_type=F32) + b_ref[...]


def _ada_mods(cond, ada_w, ada_b):
    depth, d, _ = ada_w.shape
    rows = cond.shape[0]
    out = pl.pallas_call(
        _mod_kernel,
        grid=(depth, 6),
        in_specs=[pl.BlockSpec((rows, d), lambda i, n: (0, 0)),
                  pl.BlockSpec((None, d, d), lambda i, n: (i, 0, n)),
                  pl.BlockSpec((None, 1, d), lambda i, n: (i, 0, n))],
        out_specs=pl.BlockSpec((None, rows, d), lambda i, n: (i, 0, n)),
        out_shape=jax.ShapeDtypeStruct((depth, rows, 6 * d), F32),
        compiler_params=_cparams("arbitrary", "arbitrary"),
        name="ada_mods",
    )(cond, ada_w, ada_b.reshape(depth, 1, 6 * d))
    return out.reshape(depth, rows, 6, d)


def _mod_spec(nb):
    return pl.BlockSpec((None, 6, D_MODEL), lambda b, t: (jnp.where(t == 0, nb, b), 0, 0))


def _proj_call(kernel, x, mods, g, w, extras, out_cols, out_dtypes, name):
    nb, l, d = x.shape
    nt = l // TM
    n = w.shape[1]
    in_specs = [pl.BlockSpec((None, TM, d), lambda b, t: (b, t, 0)),
                _mod_spec(nb),
                pl.BlockSpec((1, d), lambda b, t: (0, 0)),
                pl.BlockSpec((d, n), lambda b, t: (0, 0))]
    args = [x, mods, g.reshape(1, d), w]
    for e, per_tile in extras:
        if per_tile:
            in_specs.append(pl.BlockSpec((TM, e.shape[1]), lambda b, t: (t, 0)))
        else:
            in_specs.append(pl.BlockSpec(e.shape, lambda b, t: (0, 0)))
        args.append(e)
    return pl.pallas_call(
        kernel,
        grid=(nb, nt),
        in_specs=in_specs,
        out_specs=[pl.BlockSpec((None, TM, c), lambda b, t: (b, t, 0)) for c in out_cols],
        out_shape=[jax.ShapeDtypeStruct((nb, l, c), dt) for c, dt in zip(out_cols, out_dtypes)],
        compiler_params=_cparams("arbitrary", "arbitrary"),
        name=name,
    )(*args)


def _proj_na_kernel(x_ref, mod_ref, g_ref, w_ref, qkg_ref, o_ref):
    h = _norm_mod(x_ref[...], g_ref[...], mod_ref, 0)
    u = jnp.dot(h.astype(BF16), w_ref[...], preferred_element_type=F32)
    nq = NA_HEADS * NA_HD
    for part in range(2):
        gain = qkg_ref[part:part + 1, :]
        for hh in range(NA_HEADS):
            lo = part * nq + hh * NA_HD
            z = u[:, lo:lo + NA_HD]
            z = z * lax.rsqrt(jnp.mean(z * z, axis=-1, keepdims=True) + RMS_EPS) * gain
            o_ref[:, lo:lo + NA_HD] = z.astype(BF16)
    o_ref[:, 2 * nq:] = u[:, 2 * nq:].astype(BF16)


def _head64_rms(z, gain):
    lane = lax.broadcasted_iota(jnp.int32, z.shape, 1)
    lo = lane < SW_HD
    zz = z * z
    s_lo = jnp.sum(jnp.where(lo, zz, 0.0), axis=-1, keepdims=True)
    s_hi = jnp.sum(jnp.where(lo, 0.0, zz), axis=-1, keepdims=True)
    ms = jnp.where(lo, s_lo, s_hi) * (1.0 / SW_HD)
    return z * lax.rsqrt(ms + RMS_EPS) * gain


def _rope_slab(z, cs, sn):
    lane = lax.broadcasted_iota(jnp.int32, z.shape, 1)
    first = (lane % 32) < 16
    partner = jnp.where(first, pltpu.roll(z, LANES - 16, 1), pltpu.roll(z, 16, 1))
    return z * cs + partner * sn


def _proj_sw_kernel(x_ref, mod_ref, g_ref, w_ref, qkg_ref, cs_ref, sn_ref, q_ref, kv_ref):
    h = _norm_mod(x_ref[...], g_ref[...], mod_ref, 0)
    u = jnp.dot(h.astype(BF16), w_ref[...], preferred_element_type=F32)
    cs, sn = cs_ref[...], sn_ref[...]
    nq, nk = SW_HEADS * SW_HD, SW_KV * SW_HD
    scale = SW_HD ** -0.5
    for s in range(nq // LANES):
        z = _head64_rms(u[:, s * LANES:(s + 1) * LANES], qkg_ref[0:1, :])
        q_ref[:, s * LANES:(s + 1) * LANES] = (_rope_slab(z, cs, sn) * scale).astype(BF16)
    for s in range(nk // LANES):
        z = _head64_rms(u[:, nq + s * LANES:nq + (s + 1) * LANES], qkg_ref[1:2, :])
        kv_ref[:, s * LANES:(s + 1) * LANES] = _rope_slab(z, cs, sn).astype(BF16)
    kv_ref[:, nk:] = u[:, nq + nk:].astype(BF16)


def _proj_ml_kernel(x_ref, mod_ref, g_ref, w_ref, bg_ref, qkv_ref, og_ref, gt_ref):
    h = _norm_mod(x_ref[...], g_ref[...], mod_ref, 0)
    u = jnp.dot(h.astype(BF16), w_ref[...], preferred_element_type=F32)
    nq, nv = ML_HEADS * ML_DK, ML_HEADS * ML_DV
    qkv_ref[:, :nq] = (u[:, :nq] * (ML_DK ** -0.5)).astype(BF16)
    qkv_ref[:, nq:] = u[:, nq:2 * nq + nv].astype(BF16)
    og_ref[...] = jax.nn.sigmoid(u[:, 2 * nq + nv:2 * nq + 2 * nv]).astype(BF16)
    pre = u[:, 2 * nq + 2 * nv:] + bg_ref[...]
    pre = ML_CAP * jnp.tanh(pre / ML_CAP)
    lane = lax.broadcasted_iota(jnp.int32, pre.shape, 1)
    gt_ref[...] = jnp.where((lane % 16) >= 8, _log_sigmoid(pre), pre)


def _proj_gl_kernel(x_ref, mod_ref, g_ref, w_ref, wa_ref, ba_ref, qkv_ref, gate_ref, la_ref):
    h = _norm_mod(x_ref[...], g_ref[...], mod_ref, 0)
    u = jnp.dot(h.astype(BF16), w_ref[...], preferred_element_type=F32)
    nq, nv = GL_HEADS * GL_DK, GL_HEADS * GL_DV
    qkv_ref[...] = u[:, :2 * nq + nv].astype(BF16)
    gate_ref[...] = _silu(u[:, 2 * nq + nv:2 * nq + 2 * nv]).astype(BF16)
    z = u[:, 2 * nq + 2 * nv:]
    a = jnp.dot(z.astype(BF16), wa_ref[...], preferred_element_type=F32) + ba_ref[...]
    la_ref[...] = _log_sigmoid(a) * (1.0 / GL_TAU)


def _na_bias_table(rpb, n_rows):
    cq = np.arange(GRID_W)[:, None]
    ck = np.arange(GRID_W)[None, :]
    c0 = np.clip(cq - NA_WIN_C // 2, 0, GRID_W - NA_WIN_C)
    col_ok = (ck >= c0) & (ck < c0 + NA_WIN_C)
    dc = np.clip(ck - cq + NA_WIN_C - 1, 0, 2 * NA_WIN_C - 2)
    pick = ((dc[None] == np.arange(2 * NA_WIN_C - 1)[:, None, None]) & col_ok[None]).astype(np.float32)
    m = jnp.einsum('hrd,dqk->hqrk', rpb.astype(F32), jnp.asarray(pick), precision=HIGHEST)
    m = jnp.where(jnp.asarray(col_ok)[None, :, None, :], m, NEG_INF)
    blocks = []
    for jj in range(n_rows // NA_QROWS):
        ws = min(max(NA_QROWS * jj - NA_WIN_R // 2, 0), n_rows - NA_KROWS)
        per_row = []
        for ri in range(NA_QROWS):
            r = NA_QROWS * jj + ri
            r0 = min(max(r - NA_WIN_R // 2, 0), n_rows - NA_WIN_R)
            lead = r0 - ws
            d0 = r0 - r + NA_WIN_R - 1
            piece = jnp.pad(m[:, :, d0:d0 + NA_WIN_R, :],
                            ((0, 0), (0, 0), (lead, NA_KROWS - NA_WIN_R - lead), (0, 0)), constant_values=NEG_INF)
            per_row.append(piece.reshape(rpb.shape[0], GRID_W, NA_KROWS * GRID_W))
        blocks.append(jnp.concatenate(per_row, axis=1))
    return jnp.stack(blocks, axis=1)


def _na_kernel(q_ref, k_ref, v_ref, bias_ref, o_ref, *, n_rows):
    j = pl.program_id(1)
    nb = q_ref.shape[0]
    scale = NA_HD ** -0.5
    nt = (((1,), (1,)), ((), ()))
    nkw = NA_KROWS * GRID_W

    @pl.when(j == 0)
    def _ctx():
        def body(b, c):
            q = q_ref[b]
            s = lax.dot_general(q, k_ref[b, 0:CTX_LEN, :], nt, preferred_element_type=F32) * scale
            p = jnp.exp(s - jnp.max(s, axis=-1, keepdims=True))
            l = jnp.sum(p, axis=-1, keepdims=True)
            o = jnp.dot(p.astype(BF16), v_ref[b, 0:CTX_LEN, :], preferred_element_type=F32)
            o_ref[b] = (o / l).astype(BF16)
            return c
        lax.fori_loop(0, nb, body, 0)

    @pl.when(j > 0)
    def _lat():
        ws = jnp.clip(NA_QROWS * (j - 1) - NA_WIN_R // 2, 0, n_rows - NA_KROWS)
        start = pl.multiple_of(CTX_LEN + ws * GRID_W, GRID_W)
        bias = bias_ref[...]

        def body(b, c):
            q = q_ref[b]
            s_n = lax.dot_general(q, k_ref[b, pl.ds(start, nkw), :], nt, preferred_element_type=F32) * scale + bias
            s_c = lax.dot_general(q, k_ref[b, 0:CTX_LEN, :], nt, preferred_element_type=F32) * scale
            m = jnp.maximum(jnp.max(s_n, axis=-1, keepdims=True), jnp.max(s_c, axis=-1, keepdims=True))
            p_n = jnp.exp(s_n - m)
            p_c = jnp.exp(s_c - m)
            l = jnp.sum(p_n, axis=-1, keepdims=True) + jnp.sum(p_c, axis=-1, keepdims=True)
            o = (jnp.dot(p_n.astype(BF16), v_ref[b, pl.ds(start, nkw), :], preferred_element_type=F32)
                 + jnp.dot(p_c.astype(BF16), v_ref[b, 0:CTX_LEN, :], preferred_element_type=F32))
            o_ref[b] = (o / l).astype(BF16)
            return c
        lax.fori_loop(0, nb, body, 0)


def _na_attention(qkv, bias_tab):
    nb, l, _ = qkv.shape
    n_rows = (l - CTX_LEN) // GRID_W
    nt = l // TM
    hq = NA_HEADS
    return pl.pallas_call(
        functools.partial(_na_kernel, n_rows=n_rows),
        grid=(hq, nt),
        in_specs=[pl.BlockSpec((nb, TM, NA_HD), lambda h, j: (0, j, h)),
                  pl.BlockSpec((nb, l, NA_HD), lambda h, j: (0, 0, hq + h)),
                  pl.BlockSpec((nb, l, NA_HD), lambda h, j: (0, 0, 2 * hq + h)),
                  pl.BlockSpec((None, None, TM, NA_KROWS * GRID_W),
                               lambda h, j: (h, jnp.maximum(j - 1, 0), 0, 0))],
        out_specs=pl.BlockSpec((nb, TM, NA_HD), lambda h, j: (0, j, h)),
        out_shape=jax.ShapeDtypeStruct((nb, l, hq * NA_HD), BF16),
        compiler_params=_cparams("arbitrary", "arbitrary"),
        name="na_attention",
    )(qkv, qkv, qkv, bias_tab)


def _sw_kernel(sink_ref, q_ref, kv_ref, o_ref, *, seq):
    j = pl.program_id(1)
    nt = (((1,), (1,)), ((), ()))
    nkv = SW_KV * SW_HD
    grp = SW_HEADS // SW_KV
    kwin = TM + 2 * SW_WINDOW

    def head_cols(h):
        return slice(h * SW_HD, (h + 1) * SW_HD)

    @pl.when(j == 0)
    def _ctx():
        for h in range(SW_HEADS):
            hk = h // grp
            sink = sink_ref[h]
            q = q_ref[:, head_cols(h)]
            s = lax.dot_general(q, kv_ref[0:CTX_LEN, head_cols(hk)], nt, preferred_element_type=F32)
            m = jnp.maximum(jnp.max(s, axis=-1, keepdims=True), sink)
            p = jnp.exp(s - m)
            l = jnp.sum(p, axis=-1, keepdims=True) + jnp.exp(sink - m)
            o = jnp.dot(p.astype(BF16), kv_ref[0:CTX_LEN, nkv + hk * SW_HD:nkv + (hk + 1) * SW_HD],
                        preferred_element_type=F32)
            o_ref[:, head_cols(h)] = (o / l).astype(BF16)

    @pl.when(j > 0)
    def _lat():
        q0 = (j - 1) * TM
        ws = jnp.clip(q0 - SW_WINDOW, 0, seq - kwin)
        start = pl.multiple_of(CTX_LEN + ws, SW_WINDOW)
        qpos = q0 + lax.broadcasted_iota(jnp.int32, (TM, kwin), 0)
        kpos = ws + lax.broadcasted_iota(jnp.int32, (TM, kwin), 1)
        ok = jnp.abs(qpos - kpos) <= SW_WINDOW
        for h in range(SW_HEADS):
            hk = h // grp
            sink = sink_ref[h]
            q = q_ref[:, head_cols(h)]
            vcols = slice(nkv + hk * SW_HD, nkv + (hk + 1) * SW_HD)
            s_w = lax.dot_general(q, kv_ref[pl.ds(start, kwin), head_cols(hk)], nt, preferred_element_type=F32)
            s_w = jnp.where(ok, s_w, NEG_INF)
            s_c = lax.dot_general(q, kv_ref[0:CTX_LEN, head_cols(hk)], nt, preferred_element_type=F32)
            m = jnp.maximum(jnp.maximum(jnp.max(s_w, axis=-1, keepdims=True),
                                        jnp.max(s_c, axis=-1, keepdims=True)), sink)
            p_w = jnp.exp(s_w - m)
            p_c = jnp.exp(s_c - m)
            l = (jnp.sum(p_w, axis=-1, keepdims=True) + jnp.sum(p_c, axis=-1, keepdims=True)
                 + jnp.exp(sink - m))
            o = (jnp.dot(p_w.astype(BF16), kv_ref[pl.ds(start, kwin), vcols], preferred_element_type=F32)
                 + jnp.dot(p_c.astype(BF16), kv_ref[0:CTX_LEN, vcols], preferred_element_type=F32))
            o_ref[:, head_cols(h)] = (o / l).astype(BF16)


def _sw_attention(q, kv, sink):
    nb, l, nq = q.shape
    nt = l // TM
    return pl.pallas_call(
        functools.partial(_sw_kernel, seq=l - CTX_LEN),
        grid=(nb, nt),
        in_specs=[pl.BlockSpec(memory_space=pltpu.SMEM),
                  pl.BlockSpec((None, TM, nq), lambda b, j: (b, j, 0)),
                  pl.BlockSpec((None, l, kv.shape[2]), lambda b, j: (b, 0, 0))],
        out_specs=pl.BlockSpec((None, TM, nq), lambda b, j: (b, j, 0)),
        out_shape=jax.ShapeDtypeStruct((nb, l, nq), BF16),
        compiler_params=_cparams("arbitrary", "arbitrary"),
        name="sw_attention",
    )(sink, q, kv)


def _chunk_index(step, rev, n_ctx, n_all):
    if not rev:
        return step
    return jnp.where(step < n_ctx, n_ctx - 1 - step, n_all + n_ctx - 1 - step)


def _tri(n, rev):
    r = lax.broadcasted_iota(jnp.int32, (n, n), 0)
    c = lax.broadcasted_iota(jnp.int32, (n, n), 1)
    return (c >= r) if rev else (c <= r)


def _ml_kernel(q_ref, k_ref, v_ref, og_ref, gr_ref, ng_ref, y_ref, acc_ref, *, n_ctx, n_all):
    lc = ML_SCAN
    hps = q_ref.shape[1] // ML_DK
    t0 = (((0,), (0,)), ((), ()))
    nt = (((1,), (1,)), ((), ()))
    eye = (_tri(lc, False) & _tri(lc, True)).astype(F32)

    def gate_forms(step, rev):
        c = _chunk_index(step, rev, n_ctx, n_all)
        g8 = gr_ref[c]
        cum = jnp.dot(g8, _tri(lc, not rev).astype(F32), precision=HIGHEST, preferred_element_type=F32)
        rows = jnp.concatenate([g8, cum], axis=0)
        cols = lax.dot_general(eye, rows, nt, precision=HIGHEST, preferred_element_type=F32)
        return c, rows, cols

    def chain(forms, hl, rev, state):
        ct, n, m = state
        c, rows, cols = forms
        r0 = pl.multiple_of(c * lc, lc)
        base = hl * 4 + (2 if rev else 0)
        nr = rows.shape[0] // 2
        li_r = rows[base:base + 1, :]
        bf_r = rows[nr + base + 1:nr + base + 2, :]
        li_c = cols[:, base:base + 1]
        bf_c = cols[:, nr + base + 1:nr + base + 2]
        g = jnp.sum(rows[base + 1:base + 2, :], axis=1, keepdims=True)
        q = q_ref[pl.ds(r0, lc), hl * ML_DK:(hl + 1) * ML_DK]
        k = k_ref[pl.ds(r0, lc), hl * ML_DK:(hl + 1) * ML_DK]
        v = v_ref[pl.ds(r0, lc), hl * ML_DV:(hl + 1) * ML_DV]
        kf, vf = k.astype(F32), v.astype(F32)
        dm = jnp.where(_tri(lc, rev), bf_c - bf_r + li_r, NEG_INF)
        inter = bf_c + m
        m_t = jnp.maximum(inter, jnp.max(dm, axis=1, keepdims=True))
        sc = lax.dot_general(q, k, nt, preferred_element_type=F32) * jnp.exp(dm - m_t)
        e_int = jnp.exp(inter - m_t)
        num = (jnp.dot(sc.astype(BF16), v, preferred_element_type=F32)
               + e_int * jnp.dot(q, ct.astype(BF16), preferred_element_type=F32))
        den = jnp.sum(sc, axis=1, keepdims=True) + e_int * jnp.sum(q.astype(F32) * n, axis=1, keepdims=True)
        hout = num / jnp.maximum(jnp.abs(den), jnp.exp(-m_t))
        acc_ref[pl.ds(r0, lc), hl * ML_DV:(hl + 1) * ML_DV] += hout
        a = g - bf_c + li_c
        m_loc = jnp.max(a, axis=0, keepdims=True)
        w = jnp.exp(a - m_loc)
        c_loc = lax.dot_general(k, (vf * w).astype(BF16), t0, preferred_element_type=F32)
        n_loc = jnp.sum(kf * w, axis=0, keepdims=True)
        m_new = jnp.maximum(g + m, m_loc)
        dec = jnp.exp(g + m - m_new)
        inc = jnp.exp(m_loc - m_new)
        return dec * ct + inc * c_loc, dec * n + inc * n_loc, m_new

    acc_ref[...] = jnp.zeros_like(acc_ref)
    chains = [(hl, rev) for hl in range(hps) for rev in (False, True)]
    init = tuple((jnp.zeros((ML_DK, ML_DV), F32), jnp.zeros((1, ML_DK), F32), jnp.zeros((1, 1), F32))
                 for _ in chains)

    def step_fn(step, states):
        forms = {rev: gate_forms(step, rev) for rev in (False, True)}
        return tuple(chain(forms[rev], hl, rev, st) for (hl, rev), st in zip(chains, states))

    lax.fori_loop(0, n_all, step_fn, init)

    for hl in range(hps):
        cols = slice(hl * ML_DV, (hl + 1) * ML_DV)
        z = acc_ref[:, cols]
        z = z * lax.rsqrt(jnp.mean(z * z, axis=-1, keepdims=True) + RMS_EPS) * ng_ref[:, cols]
        y_ref[:, cols] = (z.astype(BF16) * og_ref[:, cols])


def _ml_mixer(qkv, og, gates, norm_g):
    nb, l, _ = qkv.shape
    lc = ML_SCAN
    n_all, n_ctx = l // lc, CTX_LEN // lc
    hps = 2
    ngrp = ML_HEADS // hps
    nq = ML_HEADS * ML_DK
    g = gates[:, :, :4 * ML_HEADS].reshape(nb, n_all, lc, 2, 2, ngrp, hps)
    g = g.transpose(0, 5, 1, 6, 3, 4, 2).reshape(nb, ngrp, n_all, hps * 4, lc)
    vb = hps * ML_DV
    return pl.pallas_call(
        functools.partial(_ml_kernel, n_ctx=n_ctx, n_all=n_all),
        grid=(nb, ngrp),
        in_specs=[pl.BlockSpec((None, l, hps * ML_DK), lambda b, h: (b, 0, h)),
                  pl.BlockSpec((None, l, hps * ML_DK), lambda b, h: (b, 0, nq // (hps * ML_DK) + h)),
                  pl.BlockSpec((None, l, vb), lambda b, h: (b, 0, 2 * nq // vb + h)),
                  pl.BlockSpec((None, l, vb), lambda b, h: (b, 0, h)),
                  pl.BlockSpec((None, None, n_all, hps * 4, lc), lambda b, h: (b, h, 0, 0, 0)),
                  pl.BlockSpec((1, vb), lambda b, h: (0, h))],
        out_specs=pl.BlockSpec((None, l, vb), lambda b, h: (b, 0, h)),
        out_shape=jax.ShapeDtypeStruct((nb, l, ML_HEADS * ML_DV), BF16),
        scratch_shapes=[pltpu.VMEM((l, vb), F32)],
        compiler_params=_cparams("arbitrary", "arbitrary"),
        name="mlstm_mixer",
    )(qkv, qkv, qkv, og, g, norm_g.reshape(1, -1))


def _gl_kernel(q_ref, k_ref, v_ref, gate_ref, la0_ref, la1_ref, ng_ref, y_ref, acc_ref, st_ref, *, n_ctx, n_all):
    lc = GL_SCAN
    t0 = (((0,), (0,)), ((), ()))
    nt = (((1,), (1,)), ((), ()))
    scale = GL_DK ** -0.5

    def chain(step, rev):
        la_ref = la1_ref if rev else la0_ref
        sidx = 1 if rev else 0
        c = _chunk_index(step, rev, n_ctx, n_all)
        r0 = pl.multiple_of(c * lc, lc)
        tri = _tri(lc, rev)
        la = la_ref[pl.ds(r0, lc), :]
        bc = jnp.dot(tri.astype(F32), la, precision=HIGHEST, preferred_element_type=F32)
        g = jnp.sum(la, axis=0, keepdims=True)
        q = q_ref[pl.ds(r0, lc), :].astype(F32) * scale
        k = k_ref[pl.ds(r0, lc), :].astype(F32)
        v = v_ref[pl.ds(r0, lc), :]
        st = st_ref[sidx]
        eg = jnp.exp(g)
        k_dec = k * jnp.exp(-bc)
        q_t = (q * jnp.exp(bc)).astype(BF16)
        k_t = k_dec.astype(BF16)
        att = jnp.where(tri, lax.dot_general(q_t, k_t, nt, preferred_element_type=F32), 0.0)
        o = (jnp.dot(att.astype(BF16), v, preferred_element_type=F32)
             + lax.dot_general(q_t, st.astype(BF16), nt, preferred_element_type=F32))
        acc_ref[pl.ds(r0, lc), :] += o
        kd = (k_dec * eg).astype(BF16)
        s_loc = lax.dot_general(v, kd, t0, preferred_element_type=F32)
        st_ref[sidx] = st * eg + s_loc

    acc_ref[...] = jnp.zeros_like(acc_ref)
    st_ref[...] = jnp.zeros_like(st_ref)

    def step_fn(step, carry):
        chain(step, False)
        chain(step, True)
        return carry

    lax.fori_loop(0, n_all, step_fn, 0)
    z = acc_ref[...]
    z = z * lax.rsqrt(jnp.mean(z * z, axis=-1, keepdims=True) + RMS_EPS) * ng_ref[...]
    y_ref[...] = z.astype(BF16) * gate_ref[...]


def _gl_mixer(qkv, gate, la, norm_g):
    nb, l, _ = qkv.shape
    lc = GL_SCAN
    n_all, n_ctx = l // lc, CTX_LEN // lc
    hq = GL_HEADS
    return pl.pallas_call(
        functools.partial(_gl_kernel, n_ctx=n_ctx, n_all=n_all),
        grid=(nb, hq),
        in_specs=[pl.BlockSpec((None, l, GL_DK), lambda b, h: (b, 0, h)),
                  pl.BlockSpec((None, l, GL_DK), lambda b, h: (b, 0, hq + h)),
                  pl.BlockSpec((None, l, GL_DV), lambda b, h: (b, 0, hq + h)),
                  pl.BlockSpec((None, l, GL_DV), lambda b, h: (b, 0, h)),
                  pl.BlockSpec((None, l, GL_DK), lambda b, h: (b, 0, h)),
                  pl.BlockSpec((None, l, GL_DK), lambda b, h: (b, 0, hq + h)),
                  pl.BlockSpec((1, GL_DV), lambda b, h: (0, h))],
        out_specs=pl.BlockSpec((None, l, GL_DV), lambda b, h: (b, 0, h)),
        out_shape=jax.ShapeDtypeStruct((nb, l, hq * GL_DV), BF16),
        scratch_shapes=[pltpu.VMEM((l, GL_DV), F32), pltpu.VMEM((2, GL_DV, GL_DK), F32)],
        compiler_params=_cparams("arbitrary", "arbitrary"),
        name="gla_mixer",
    )(qkv, qkv, qkv, gate, la, la, norm_g.reshape(1, -1))


def _store_token_tiles(ref, val):
    rows = val.shape[0]
    for s in range(val.shape[1] // LANES):
        ref[pl.ds(s, rows, stride=SUBLANES), :] = val[:, s * LANES:(s + 1) * LANES]


def _load_token_tiles(ref, rows):
    return jnp.concatenate([ref[pl.ds(s, rows, stride=SUBLANES), :] for s in range(SUBLANES)], axis=1)


def _post_kernel(y_ref, x_ref, mod_ref, wo_ref, g_ref, wr_ref, br_ref, xo_ref, f_ref, rt_ref, cnt_ref, run_ref):
    first = (pl.program_id(0) == 0) & (pl.program_id(1) == 0)

    @pl.when(first)
    def _():
        run_ref[...] = jnp.zeros_like(run_ref)

    o = jnp.dot(y_ref[...], wo_ref[...], preferred_element_type=F32)
    xn = x_ref[...] + mod_ref[2:3, :] * o
    xo_ref[...] = xn
    f = _norm_mod(xn, g_ref[...], mod_ref, 3)
    _store_token_tiles(f_ref, f)
    f_hi = f.astype(BF16)
    f_lo = (f - f_hi.astype(F32)).astype(BF16)
    hh = jnp.dot(f_hi, wr_ref[...], preferred_element_type=F32)
    lh = jnp.dot(f_lo, wr_ref[:, :LANES], preferred_element_type=F32)
    lg = hh[:, :LANES] + (hh[:, LANES:] + lh) + br_ref[...]
    lane = lax.broadcasted_iota(jnp.int32, lg.shape, 1).astype(F32)
    big = float(LANES)

    def masked_max(mask):
        return jnp.max(jnp.where(mask, lg, -jnp.inf), axis=-1, keepdims=True)

    def first_at(mask, val):
        return jnp.min(jnp.where(mask & (lg == val), lane, big), axis=-1, keepdims=True)

    gm = lane < MOE_GROUPS
    mg = masked_max(gm)
    g_idx = first_at(gm, mg)
    g_w = 1.0 / jnp.sum(jnp.where(gm, jnp.exp(lg - mg), 0.0), axis=-1, keepdims=True)
    lo = MOE_GROUPS + MOE_PER_GROUP * g_idx
    em = (lane >= lo) & (lane < lo + MOE_PER_GROUP)
    v0 = masked_max(em)
    i0 = first_at(em, v0)
    em1 = em & (lane != i0)
    v1 = masked_max(em1)
    i1 = first_at(em1, v1)
    e0, e1 = i0 - MOE_GROUPS, i1 - MOE_GROUPS
    t = jnp.exp(v1 - v0)
    w0 = g_w / (1.0 + t)
    w1 = g_w * t / (1.0 + t)
    oh = ((lane == e0) | (lane == e1)).astype(F32)
    rows = lg.shape[0]
    below = lax.broadcasted_iota(jnp.int32, (rows, rows), 0) > lax.broadcasted_iota(jnp.int32, (rows, rows), 1)
    tot = run_ref[...] + jnp.dot(below.astype(BF16), oh.astype(BF16), preferred_element_type=F32)
    r0 = jnp.sum(jnp.where(lane == e0, tot, 0.0), axis=-1, keepdims=True)
    r1 = jnp.sum(jnp.where(lane == e1, tot, 0.0), axis=-1, keepdims=True)
    run_ref[...] = run_ref[...] + jnp.sum(oh, axis=0, keepdims=True)
    cnt_ref[...] = jnp.broadcast_to(run_ref[...], cnt_ref.shape)
    out = jnp.zeros_like(lg)
    for i, val in enumerate((e0, e1, w0, w1, r0, r1)):
        out = jnp.where(lane == i, val, out)
    rt_ref[...] = out


def _post_call(y, x, mods, w_o, g_ffn, w_r, b_r):
    nb, l, d = x.shape
    nt = l // TM
    tile = lambda b, t: (b * nt + t, 0)
    return pl.pallas_call(
        _post_kernel,
        grid=(nb, nt),
        in_specs=[pl.BlockSpec((None, TM, d), lambda b, t: (b, t, 0)),
                  pl.BlockSpec((None, TM, d), lambda b, t: (b, t, 0)),
                  _mod_spec(nb),
                  pl.BlockSpec((d, d), lambda b, t: (0, 0)),
                  pl.BlockSpec((1, d), lambda b, t: (0, 0)),
                  pl.BlockSpec((d, 2 * LANES), lambda b, t: (0, 0)),
                  pl.BlockSpec((1, LANES), lambda b, t: (0, 0))],
        out_specs=[pl.BlockSpec((None, TM, d), lambda b, t: (b, t, 0)),
                   pl.BlockSpec((TM * SUBLANES, LANES), tile),
                   pl.BlockSpec((TM, LANES), tile),
                   pl.BlockSpec((SUBLANES, LANES), lambda b, t: (0, 0))],
        out_shape=[jax.ShapeDtypeStruct((nb, l, d), F32),
                   jax.ShapeDtypeStruct((nb * l * SUBLANES, LANES), F32),
                   jax.ShapeDtypeStruct((nb * l, LANES), F32),
                   jax.ShapeDtypeStruct((SUBLANES, LANES), F32)],
        scratch_shapes=[pltpu.VMEM((1, LANES), F32)],
        compiler_params=_cparams("arbitrary", "arbitrary"),
        name="outproj_router",
    )(y, x, mods, w_o, g_ffn.reshape(1, d), w_r, b_r)


def _dispatch_kernel(dest_ref, f_ref, xs_in, xs_out, sem):
    del xs_in
    base = pl.program_id(0) * TM

    def copy(t, k):
        d = dest_ref[2 * (base + t) + k]
        return pltpu.make_async_copy(f_ref.at[pl.ds(t * SUBLANES, SUBLANES), :],
                                     xs_out.at[pl.ds(d * SUBLANES, SUBLANES), :], sem)

    def issue(t, c):
        copy(t, 0).start()
        copy(t, 1).start()
        return c

    lax.fori_loop(0, TM, issue, 0, unroll=DMA_UNROLL)
    for _ in range(2):
        pltpu.make_async_copy(f_ref, xs_out.at[pl.ds(0, TM * SUBLANES), :], sem).wait()


def _dispatch(dest, f_tiles, n_slots):
    n_tok = f_tiles.shape[0] // SUBLANES
    xs0 = jnp.zeros((n_slots * SUBLANES, LANES), F32)
    return pl.pallas_call(
        _dispatch_kernel,
        grid_spec=pltpu.PrefetchScalarGridSpec(
            num_scalar_prefetch=1,
            grid=(n_tok // TM,),
            in_specs=[pl.BlockSpec((TM * SUBLANES, LANES), lambda i, dr: (i, 0)),
                      pl.BlockSpec(memory_space=pl.ANY)],
            out_specs=pl.BlockSpec(memory_space=pl.ANY),
            scratch_shapes=[pltpu.SemaphoreType.DMA(())]),
        out_shape=jax.ShapeDtypeStruct(xs0.shape, F32),
        input_output_aliases={2: 0},
        compiler_params=_cparams("arbitrary"),
        name="moe_dispatch",
    )(dest, f_tiles, xs0)


def _expert_kernel(blk_e_ref, nact_ref, xs_ref, wg_ref, wu_ref, wd_ref, ys_ref, wg_s, wu_s, wd_s):
    i = pl.program_id(0)
    active = i < nact_ref[0]
    changed = (i == 0) | (blk_e_ref[i] != blk_e_ref[jnp.maximum(i - 1, 0)])

    @pl.when(active & changed)
    def _():
        wg_s[...] = wg_ref[...].astype(BF16)
        wu_s[...] = wu_ref[...].astype(BF16)
        wd_s[...] = wd_ref[...].astype(BF16)

    @pl.when(active)
    def _():
        x = _load_token_tiles(xs_ref, MOE_BLOCK).astype(BF16)
        a = jnp.dot(x, wg_s[...], preferred_element_type=F32)
        u = jnp.dot(x, wu_s[...], preferred_element_type=F32)
        y = jnp.dot((_silu(a) * u).astype(BF16), wd_s[...], preferred_element_type=F32)
        _store_token_tiles(ys_ref, y)

    @pl.when(jnp.logical_not(active))
    def _():
        ys_ref[...] = jnp.zeros_like(ys_ref)


def _experts(blk_e, nact, xs, w_gate, w_up, w_down, layer):
    n_blk = blk_e.shape[0]
    _, _, d, ff = w_gate.shape
    rows = MOE_BLOCK * SUBLANES
    last = lambda i, na: jnp.minimum(i, jnp.maximum(na[0] - 1, 0))
    blk = lambda i, be, na: (last(i, na), 0)
    wsel = lambda i, be, na: (layer, be[last(i, na)], 0, 0)
    return pl.pallas_call(
        _expert_kernel,
        grid_spec=pltpu.PrefetchScalarGridSpec(
            num_scalar_prefetch=2,
            grid=(n_blk,),
            in_specs=[pl.BlockSpec((rows, LANES), blk),
                      pl.BlockSpec((None, None, d, ff), wsel),
                      pl.BlockSpec((None, None, d, ff), wsel),
                      pl.BlockSpec((None, None, ff, d), wsel)],
            out_specs=pl.BlockSpec((rows, LANES), lambda i, be, na: (i, 0)),
            scratch_shapes=[pltpu.VMEM((d, ff), BF16), pltpu.VMEM((d, ff), BF16), pltpu.VMEM((ff, d), BF16)]),
        out_shape=jax.ShapeDtypeStruct(xs.shape, F32),
        compiler_params=_cparams("arbitrary"),
        name="moe_experts",
    )(blk_e, nact, xs, w_gate, w_up, w_down)


def _combine_kernel(dest_ref, ys_hbm, x_ref, rt_ref, mod_ref, o_ref, ybuf, sem, *, nt):
    base = (pl.program_id(0) * nt + pl.program_id(1)) * TM

    def copy(t, k):
        d = dest_ref[2 * (base + t) + k]
        return pltpu.make_async_copy(ys_hbm.at[pl.ds(d * SUBLANES, SUBLANES), :],
                                     ybuf.at[k, pl.ds(t * SUBLANES, SUBLANES), :], sem)

    def issue(t, c):
        copy(t, 0).start()
        copy(t, 1).start()
        return c

    lax.fori_loop(0, TM, issue, 0, unroll=DMA_UNROLL)
    for k in range(2):
        pltpu.make_async_copy(ys_hbm.at[pl.ds(0, TM * SUBLANES), :], ybuf.at[k], sem).wait()
    y0 = _load_token_tiles(ybuf.at[0], TM)
    y1 = _load_token_tiles(ybuf.at[1], TM)
    rt = rt_ref[...]
    y = rt[:, 2:3] * y0 + rt[:, 3:4] * y1
    o_ref[...] = x_ref[...] + mod_ref[5:6, :] * y


def _combine(dest, ys, x, rt, mods):
    nb, l, d = x.shape
    nt = l // TM
    return pl.pallas_call(
        functools.partial(_combine_kernel, nt=nt),
        grid_spec=pltpu.PrefetchScalarGridSpec(
            num_scalar_prefetch=1,
            grid=(nb, nt),
            in_specs=[pl.BlockSpec(memory_space=pl.ANY),
                      pl.BlockSpec((None, TM, d), lambda b, t, dr: (b, t, 0)),
                      pl.BlockSpec((TM, LANES), lambda b, t, dr: (b * nt + t, 0)),
                      pl.BlockSpec((None, 6, d), lambda b, t, dr: (jnp.where(t == 0, nb, b), 0, 0))],
            out_specs=pl.BlockSpec((None, TM, d), lambda b, t, dr: (b, t, 0)),
            scratch_shapes=[pltpu.VMEM((2, TM * SUBLANES, LANES), F32), pltpu.SemaphoreType.DMA(())]),
        out_shape=jax.ShapeDtypeStruct(x.shape, F32),
        compiler_params=_cparams("arbitrary", "arbitrary"),
        name="moe_combine",
    )(dest, ys, x, rt, mods)


def _moe(y_mix, x, mods, w_o, g_ffn, w_grp, b_grp, w_exp, b_exp, w_gate, w_up, w_down, layer):
    nb, l, d = x.shape
    n_tok = nb * l
    nr = MOE_GROUPS + MOE_EXPERTS
    w_r = jnp.pad(jnp.concatenate([w_grp, w_exp], axis=1), ((0, 0), (0, LANES - nr)))
    b_r = jnp.pad(jnp.concatenate([b_grp, b_exp]), (0, LANES - nr)).reshape(1, LANES)
    w_hi = w_r.astype(BF16)
    w_r = jnp.concatenate([w_hi, (w_r - w_hi.astype(F32)).astype(BF16)], axis=1)
    x_new, f_tiles, rt, cnt = _post_call(y_mix, x, mods, w_o, g_ffn, w_r, b_r)
    counts = cnt[0, :MOE_EXPERTS].astype(jnp.int32)
    padded = (counts + MOE_BLOCK - 1) // MOE_BLOCK * MOE_BLOCK
    p_end = jnp.cumsum(padded)
    p_start = p_end - padded
    n_pairs = 2 * n_tok
    n_blk = -(-(n_pairs + MOE_EXPERTS * (MOE_BLOCK - 1)) // MOE_BLOCK)
    eid = rt[:, 0:2].astype(jnp.int32)
    first = jnp.sum(jnp.where(eid[:, :, None] == jnp.arange(MOE_EXPERTS, dtype=jnp.int32), p_start, 0), axis=-1)
    dest = (first + rt[:, 4:6].astype(jnp.int32)).reshape(-1)
    blk_row = jnp.arange(n_blk, dtype=jnp.int32) * MOE_BLOCK
    blk_e = jnp.minimum(jnp.sum((p_end[None, :] <= blk_row[:, None]).astype(jnp.int32), axis=1), MOE_EXPERTS - 1)
    nact = (p_end[-1:] // MOE_BLOCK).astype(jnp.int32)
    xs = _dispatch(dest, f_tiles, n_blk * MOE_BLOCK)
    ys = _experts(blk_e, nact, xs, w_gate, w_up, w_down, layer)
    return _combine(dest, ys, x_new, rt, mods)


def _rope_tables(seq):
    nf = SW_HD // 4
    inv = ROPE_BASE ** (-jnp.arange(nf, dtype=F32) / nf)
    pos = jnp.arange(seq, dtype=jnp.int32)
    rows, cols = (pos // GRID_W).astype(F32), (pos % GRID_W).astype(F32)
    lane = jnp.arange(LANES)
    p = jnp.where((lane % SW_HD < SW_HD // 2)[None, :], rows[:, None], cols[:, None])
    ang = p * inv[lane % nf][None, :]
    sign = jnp.where((lane % (2 * nf)) < nf, -1.0, 1.0)[None, :]
    cs = jnp.concatenate([jnp.ones((CTX_LEN, LANES), F32), jnp.cos(ang)], axis=0)
    sn = jnp.concatenate([jnp.zeros((CTX_LEN, LANES), F32), jnp.sin(ang) * sign], axis=0)
    return cs, sn


def _pad_cols(w, n):
    return jnp.pad(w, ((0, 0), (0, n - w.shape[1])))


def kernel(x, c, ctx, c_ctx, ada_w, ada_b, norm_mix_g, norm_ffn_g, na_w_qkv, na_qk_g, na_rpb, na_w_o, ml_w_in, ml_b_gates, ml_norm_g, ml_w_o, sw_w_qkv, sw_qk_g, sw_sink, sw_w_o, gl_w_in, gl_w_a2, gl_b_a, gl_norm_g, gl_w_o, moe_w_grp, moe_b_grp, moe_w_exp, moe_b_exp, moe_w_gate, moe_w_up, moe_w_down):
    nb, seq, d = x.shape
    depth = ada_w.shape[0]
    assert d == D_MODEL and ctx.shape[1] == CTX_LEN == TM and seq % TM == 0
    rows = -(-(nb + 1) // SUBLANES) * SUBLANES
    cond = jnp.pad(jnp.concatenate([c, c_ctx[None]], axis=0), ((0, rows - nb - 1), (0, 0)))
    mods_all = _ada_mods(cond, ada_w, ada_b)
    xa = jnp.concatenate([ctx, x], axis=1)
    for i in range(depth):
        j, kind = divmod(i, 4)
        mods = mods_all[i]
        g_mix = norm_mix_g[i]
        if kind == 0:
            qkg = na_qk_g[j]
            (qkv,) = _proj_call(_proj_na_kernel, xa, mods, g_mix, na_w_qkv[j].astype(BF16), [(qkg, False)],
                                [3 * NA_HEADS * NA_HD], [BF16], "proj_na")
            y = _na_attention(qkv, _na_bias_table(na_rpb[j], seq // GRID_W))
            w_o = na_w_o[j]
        elif kind == 1:
            n_in = -(-ml_w_in.shape[2] // LANES) * LANES
            bg = jnp.pad(ml_b_gates[j], (0, LANES - ml_b_gates.shape[1])).reshape(1, LANES)
            qkv, og, gates = _proj_call(_proj_ml_kernel, xa, mods, g_mix, _pad_cols(ml_w_in[j], n_in).astype(BF16),
                                        [(bg, False)], [2 * ML_HEADS * ML_DK + ML_HEADS * ML_DV, ML_HEADS * ML_DV, LANES],
                                        [BF16, BF16, F32], "proj_ml")
            y = _ml_mixer(qkv, og, gates, ml_norm_g[j])
            w_o = ml_w_o[j]
        elif kind == 2:
            qkg = jnp.concatenate([sw_qk_g[j], sw_qk_g[j]], axis=1)
            cs, sn = _rope_tables(seq)
            q, kv = _proj_call(_proj_sw_kernel, xa, mods, g_mix, sw_w_qkv[j].astype(BF16),
                               [(qkg, False), (cs, True), (sn, True)],
                               [SW_HEADS * SW_HD, 2 * SW_KV * SW_HD], [BF16, BF16], "proj_sw")
            y = _sw_attention(q, kv, sw_sink[j])
            w_o = sw_w_o[j]
        else:
            n_in = -(-gl_w_in.shape[2] // LANES) * LANES
            nk = GL_HEADS * GL_DK
            wa = jnp.zeros((LANES, 2 * nk), F32)
            wa = wa.at[:GL_RANK, :nk].set(gl_w_a2[j, 0]).at[GL_RANK:2 * GL_RANK, nk:].set(gl_w_a2[j, 1])
            ba = gl_b_a[j].reshape(1, 2 * nk)
            qkv, gate, la = _proj_call(_proj_gl_kernel, xa, mods, g_mix, _pad_cols(gl_w_in[j], n_in).astype(BF16),
                                       [(wa.astype(BF16), False), (ba, False)],
                                       [2 * nk + GL_HEADS * GL_DV, GL_HEADS * GL_DV, 2 * nk], [BF16, BF16, F32], "proj_gl")
            y = _gl_mixer(qkv, gate, la, gl_norm_g[j])
            w_o = gl_w_o[j]
        xa = _moe(y, xa, mods, w_o.astype(BF16), norm_ffn_g[i], moe_w_grp[i], moe_b_grp[i], moe_w_exp[i], moe_b_exp[i],
                  moe_w_gate, moe_w_up, moe_w_down, i)
    return xa[:, CTX_LEN:]
```

```python
import functools

import jax
import jax.numpy as jnp
import numpy as np
from jax import lax
from jax.experimental import pallas as pl
from jax.experimental.pallas import tpu as pltpu

F32 = jnp.float32
BF16 = jnp.bfloat16
HIGHEST = lax.Precision.HIGHEST

D_MODEL = 1024
CTX_LEN = 256
GRID_W = 64
RMS_EPS = 1e-6
NEG_INF = -1e30
ROPE_BASE = 10000.0

NA_HEADS, NA_HD, NA_WIN_R, NA_WIN_C = 8, 128, 8, 16
NA_QROWS = 4
NA_KROWS = NA_QROWS + NA_WIN_R - 1
ML_HEADS, ML_DK, ML_DV, ML_CAP = 8, 64, 128, 15.0
ML_SCAN = 256
SW_HEADS, SW_KV, SW_HD, SW_WINDOW = 16, 4, 64, 128
GL_HEADS, GL_DK, GL_DV, GL_RANK, GL_TAU = 4, 128, 256, 16, 16.0
GL_SCAN = 128
MOE_GROUPS, MOE_PER_GROUP, MOE_EXPERTS, MOE_FF = 4, 8, 32, 512
MOE_BLOCK = 256

LANES = 128
SUBLANES = 8
TM = 256
VMEM_LIMIT = 48 * 1024 * 1024
DMA_UNROLL = 8
DISPATCH_TILE = 3 * TM


def _cparams(*sem):
    return pltpu.CompilerParams(dimension_semantics=sem, vmem_limit_bytes=VMEM_LIMIT)


def _norm_mod(x, g, mod_ref, k):
    y = x * lax.rsqrt(jnp.mean(x * x, axis=-1, keepdims=True) + RMS_EPS) * g
    return y * (1.0 + mod_ref[k + 1:k + 2, :]) + mod_ref[k:k + 1, :]


def _log_sigmoid(x):
    return jnp.minimum(x, 0.0) - jnp.log(1.0 + jnp.exp(-jnp.abs(x)))


def _silu(x):
    return x * jax.nn.sigmoid(x)


def _mod_kernel(c_ref, w_ref, b_ref, o_ref):
    s = _silu(c_ref[...])
    o_ref[...] = jnp.dot(s.astype(BF16), w_ref[...].astype(BF16), preferred_element_type=F32) + b_ref[...]


def _ada_mods(cond, ada_w, ada_b):
    depth, d, _ = ada_w.shape
    rows = cond.shape[0]
    out = pl.pallas_call(
        _mod_kernel,
        grid=(depth, 6),
        in_specs=[pl.BlockSpec((rows, d), lambda i, n: (0, 0)),
                  pl.BlockSpec((None, d, d), lambda i, n: (i, 0, n)),
                  pl.BlockSpec((None, 1, d), lambda i, n: (i, 0, n))],
        out_specs=pl.BlockSpec((None, rows, d), lambda i, n: (i, 0, n)),
        out_shape=jax.ShapeDtypeStruct((depth, rows, 6 * d), F32),
        compiler_params=_cparams("arbitrary", "arbitrary"),
        name="ada_mods",
    )(cond, ada_w, ada_b.reshape(depth, 1, 6 * d))
    return out.reshape(depth, rows, 6, d)


def _mod_spec(nb):
    return pl.BlockSpec((None, 6, D_MODEL), lambda b, t: (jnp.where(t == 0, nb, b), 0, 0))


def _proj_call(kernel, x, mods, g, w, extras, out_cols, out_dtypes, name):
    nb, l, d = x.shape
    nt = l // TM
    n = w.shape[1]
    in_specs = [pl.BlockSpec((None, TM, d), lambda b, t: (b, t, 0)),
                _mod_spec(nb),
                pl.BlockSpec((1, d), lambda b, t: (0, 0)),
                pl.BlockSpec((d, n), lambda b, t: (0, 0))]
    args = [x, mods, g.reshape(1, d), w]
    for e, per_tile in extras:
        if per_tile:
            in_specs.append(pl.BlockSpec((TM, e.shape[1]), lambda b, t: (t, 0)))
        else:
            in_specs.append(pl.BlockSpec(e.shape, lambda b, t: (0, 0)))
        args.append(e)
    return pl.pallas_call(
        kernel,
        grid=(nb, nt),
        in_specs=in_specs,
        out_specs=[pl.BlockSpec((None, TM, c), lambda b, t: (b, t, 0)) for c in out_cols],
        out_shape=[jax.ShapeDtypeStruct((nb, l, c), dt) for c, dt in zip(out_cols, out_dtypes)],
        compiler_params=_cparams("arbitrary", "arbitrary"),
        name=name,
    )(*args)


def _proj_na_kernel(x_ref, mod_ref, g_ref, w_ref, qkg_ref, o_ref):
    h = _norm_mod(x_ref[...], g_ref[...], mod_ref, 0)
    u = jnp.dot(h.astype(BF16), w_ref[...], preferred_element_type=F32)
    nq = NA_HEADS * NA_HD
    for part in range(2):
        gain = qkg_ref[part:part + 1, :]
        for hh in range(NA_HEADS):
            lo = part * nq + hh * NA_HD
            z = u[:, lo:lo + NA_HD]
            z = z * lax.rsqrt(jnp.mean(z * z, axis=-1, keepdims=True) + RMS_EPS) * gain
            o_ref[:, lo:lo + NA_HD] = z.astype(BF16)
    o_ref[:, 2 * nq:] = u[:, 2 * nq:].astype(BF16)


def _head64_rms(z, gain):
    lane = lax.broadcasted_iota(jnp.int32, z.shape, 1)
    lo = lane < SW_HD
    zz = z * z
    s_lo = jnp.sum(jnp.where(lo, zz, 0.0), axis=-1, keepdims=True)
    s_hi = jnp.sum(jnp.where(lo, 0.0, zz), axis=-1, keepdims=True)
    ms = jnp.where(lo, s_lo, s_hi) * (1.0 / SW_HD)
    return z * lax.rsqrt(ms + RMS_EPS) * gain


def _rope_slab(z, cs, sn):
    lane = lax.broadcasted_iota(jnp.int32, z.shape, 1)
    first = (lane % 32) < 16
    partner = jnp.where(first, pltpu.roll(z, LANES - 16, 1), pltpu.roll(z, 16, 1))
    return z * cs + partner * sn


def _proj_sw_kernel(x_ref, mod_ref, g_ref, w_ref, qkg_ref, cs_ref, sn_ref, q_ref, kv_ref):
    h = _norm_mod(x_ref[...], g_ref[...], mod_ref, 0)
    u = jnp.dot(h.astype(BF16), w_ref[...], preferred_element_type=F32)
    cs, sn = cs_ref[...], sn_ref[...]
    nq, nk = SW_HEADS * SW_HD, SW_KV * SW_HD
    scale = SW_HD ** -0.5
    for s in range(nq // LANES):
        z = _head64_rms(u[:, s * LANES:(s + 1) * LANES], qkg_ref[0:1, :])
        q_ref[:, s * LANES:(s + 1) * LANES] = (_rope_slab(z, cs, sn) * scale).astype(BF16)
    for s in range(nk // LANES):
        z = _head64_rms(u[:, nq + s * LANES:nq + (s + 1) * LANES], qkg_ref[1:2, :])
        kv_ref[:, s * LANES:(s + 1) * LANES] = _rope_slab(z, cs, sn).astype(BF16)
    kv_ref[:, nk:] = u[:, nq + nk:].astype(BF16)


def _proj_ml_kernel(x_ref, mod_ref, g_ref, w_ref, bg_ref, qkv_ref, og_ref, gt_ref):
    h = _norm_mod(x_ref[...], g_ref[...], mod_ref, 0)
    u = jnp.dot(h.astype(BF16), w_ref[...], preferred_element_type=F32)
    nq, nv = ML_HEADS * ML_DK, ML_HEADS * ML_DV
    qkv_ref[:, :nq] = (u[:, :nq] * (ML_DK ** -0.5)).astype(BF16)
    qkv_ref[:, nq:] = u[:, nq:2 * nq + nv].astype(BF16)
    og_ref[...] = jax.nn.sigmoid(u[:, 2 * nq + nv:2 * nq + 2 * nv]).astype(BF16)
    pre = u[:, 2 * nq + 2 * nv:] + bg_ref[...]
    pre = ML_CAP * jnp.tanh(pre / ML_CAP)
    lane = lax.broadcasted_iota(jnp.int32, pre.shape, 1)
    gt_ref[...] = jnp.where((lane % 16) >= 8, _log_sigmoid(pre), pre)


def _proj_gl_kernel(x_ref, mod_ref, g_ref, w_ref, wa_ref, ba_ref, qkv_ref, gate_ref, la_ref):
    h = _norm_mod(x_ref[...], g_ref[...], mod_ref, 0)
    u = jnp.dot(h.astype(BF16), w_ref[...], preferred_element_type=F32)
    nq, nv = GL_HEADS * GL_DK, GL_HEADS * GL_DV
    qkv_ref[...] = u[:, :2 * nq + nv].astype(BF16)
    gate_ref[...] = _silu(u[:, 2 * nq + nv:2 * nq + 2 * nv]).astype(BF16)
    z = u[:, 2 * nq + 2 * nv:]
    a = jnp.dot(z.astype(BF16), wa_ref[...], preferred_element_type=F32) + ba_ref[...]
    la_ref[...] = _log_sigmoid(a) * (1.0 / GL_TAU)


def _na_bias_table(rpb, n_rows):
    cq = np.arange(GRID_W)[:, None]
    ck = np.arange(GRID_W)[None, :]
    c0 = np.clip(cq - NA_WIN_C // 2, 0, GRID_W - NA_WIN_C)
    col_ok = (ck >= c0) & (ck < c0 + NA_WIN_C)
    dc = np.clip(ck - cq + NA_WIN_C - 1, 0, 2 * NA_WIN_C - 2)
    pick = ((dc[None] == np.arange(2 * NA_WIN_C - 1)[:, None, None]) & col_ok[None]).astype(np.float32)
    m = jnp.einsum('hrd,dqk->hqrk', rpb.astype(F32), jnp.asarray(pick), precision=HIGHEST)
    m = jnp.where(jnp.asarray(col_ok)[None, :, None, :], m, NEG_INF)
    blocks = []
    for jj in range(n_rows // NA_QROWS):
        ws = min(max(NA_QROWS * jj - NA_WIN_R // 2, 0), n_rows - NA_KROWS)
        per_row = []
        for ri in range(NA_QROWS):
            r = NA_QROWS * jj + ri
            r0 = min(max(r - NA_WIN_R // 2, 0), n_rows - NA_WIN_R)
            lead = r0 - ws
            d0 = r0 - r + NA_WIN_R - 1
            piece = jnp.pad(m[:, :, d0:d0 + NA_WIN_R, :],
                            ((0, 0), (0, 0), (lead, NA_KROWS - NA_WIN_R - lead), (0, 0)), constant_values=NEG_INF)
            per_row.append(piece.reshape(rpb.shape[0], GRID_W, NA_KROWS * GRID_W))
        blocks.append(jnp.concatenate(per_row, axis=1))
    return jnp.stack(blocks, axis=1)


def _na_kernel(q_ref, k_ref, v_ref, bias_ref, o_ref, *, n_rows):
    j = pl.program_id(1)
    nb = q_ref.shape[0]
    scale = NA_HD ** -0.5
    nt = (((1,), (1,)), ((), ()))
    nkw = NA_KROWS * GRID_W

    @pl.when(j == 0)
    def _ctx():
        def body(b, c):
            q = q_ref[b]
            s = lax.dot_general(q, k_ref[b, 0:CTX_LEN, :], nt, preferred_element_type=F32) * scale
            p = jnp.exp(s - jnp.max(s, axis=-1, keepdims=True))
            l = jnp.sum(p, axis=-1, keepdims=True)
            o = jnp.dot(p.astype(BF16), v_ref[b, 0:CTX_LEN, :], preferred_element_type=F32)
            o_ref[b] = (o / l).astype(BF16)
            return c
        lax.fori_loop(0, nb, body, 0)

    @pl.when(j > 0)
    def _lat():
        ws = jnp.clip(NA_QROWS * (j - 1) - NA_WIN_R // 2, 0, n_rows - NA_KROWS)
        start = pl.multiple_of(CTX_LEN + ws * GRID_W, GRID_W)
        bias = bias_ref[...]

        def body(b, c):
            q = q_ref[b]
            s_n = lax.dot_general(q, k_ref[b, pl.ds(start, nkw), :], nt, preferred_element_type=F32) * scale + bias
            s_c = lax.dot_general(q, k_ref[b, 0:CTX_LEN, :], nt, preferred_element_type=F32) * scale
            m = jnp.maximum(jnp.max(s_n, axis=-1, keepdims=True), jnp.max(s_c, axis=-1, keepdims=True))
            p_n = jnp.exp(s_n - m)
            p_c = jnp.exp(s_c - m)
            l = jnp.sum(p_n, axis=-1, keepdims=True) + jnp.sum(p_c, axis=-1, keepdims=True)
            o = (jnp.dot(p_n.astype(BF16), v_ref[b, pl.ds(start, nkw), :], preferred_element_type=F32)
                 + jnp.dot(p_c.astype(BF16), v_ref[b, 0:CTX_LEN, :], preferred_element_type=F32))
            o_ref[b] = (o / l).astype(BF16)
            return c
        lax.fori_loop(0, nb, body, 0)


def _na_attention(qkv, bias_tab):
    nb, l, _ = qkv.shape
    n_rows = (l - CTX_LEN) // GRID_W
    nt = l // TM
    hq = NA_HEADS
    return pl.pallas_call(
        functools.partial(_na_kernel, n_rows=n_rows),
        grid=(hq, nt),
        in_specs=[pl.BlockSpec((nb, TM, NA_HD), lambda h, j: (0, j, h)),
                  pl.BlockSpec((nb, l, NA_HD), lambda h, j: (0, 0, hq + h)),
                  pl.BlockSpec((nb, l, NA_HD), lambda h, j: (0, 0, 2 * hq + h)),
                  pl.BlockSpec((None, None, TM, NA_KROWS * GRID_W),
                               lambda h, j: (h, jnp.maximum(j - 1, 0), 0, 0))],
        out_specs=pl.BlockSpec((nb, TM, NA_HD), lambda h, j: (0, j, h)),
        out_shape=jax.ShapeDtypeStruct((nb, l, hq * NA_HD), BF16),
        compiler_params=_cparams("arbitrary", "arbitrary"),
        name="na_attention",
    )(qkv, qkv, qkv, bias_tab)


def _sw_kernel(sink_ref, q_ref, kv_ref, o_ref, *, seq):
    j = pl.program_id(1)
    nt = (((1,), (1,)), ((), ()))
    nkv = SW_KV * SW_HD
    grp = SW_HEADS // SW_KV
    kwin = TM + 2 * SW_WINDOW

    def head_cols(h):
        return slice(h * SW_HD, (h + 1) * SW_HD)

    @pl.when(j == 0)
    def _ctx():
        for h in range(SW_HEADS):
            hk = h // grp
            sink = sink_ref[h]
            q = q_ref[:, head_cols(h)]
            s = lax.dot_general(q, kv_ref[0:CTX_LEN, head_cols(hk)], nt, preferred_element_type=F32)
            m = jnp.maximum(jnp.max(s, axis=-1, keepdims=True), sink)
            p = jnp.exp(s - m)
            l = jnp.sum(p, axis=-1, keepdims=True) + jnp.exp(sink - m)
            o = jnp.dot(p.astype(BF16), kv_ref[0:CTX_LEN, nkv + hk * SW_HD:nkv + (hk + 1) * SW_HD],
                        preferred_element_type=F32)
            o_ref[:, head_cols(h)] = (o / l).astype(BF16)

    @pl.when(j > 0)
    def _lat():
        q0 = (j - 1) * TM
        ws = jnp.clip(q0 - SW_WINDOW, 0, seq - kwin)
        start = pl.multiple_of(CTX_LEN + ws, SW_WINDOW)
        qpos = q0 + lax.broadcasted_iota(jnp.int32, (TM, kwin), 0)
        kpos = ws + lax.broadcasted_iota(jnp.int32, (TM, kwin), 1)
        ok = jnp.abs(qpos - kpos) <= SW_WINDOW
        for h in range(SW_HEADS):
            hk = h // grp
            sink = sink_ref[h]
            q = q_ref[:, head_cols(h)]
            vcols = slice(nkv + hk * SW_HD, nkv + (hk + 1) * SW_HD)
            s_w = lax.dot_general(q, kv_ref[pl.ds(start, kwin), head_cols(hk)], nt, preferred_element_type=F32)
            s_w = jnp.where(ok, s_w, NEG_INF)
            s_c = lax.dot_general(q, kv_ref[0:CTX_LEN, head_cols(hk)], nt, preferred_element_type=F32)
            m = jnp.maximum(jnp.maximum(jnp.max(s_w, axis=-1, keepdims=True),
                                        jnp.max(s_c, axis=-1, keepdims=True)), sink)
            p_w = jnp.exp(s_w - m)
            p_c = jnp.exp(s_c - m)
            l = (jnp.sum(p_w, axis=-1, keepdims=True) + jnp.sum(p_c, axis=-1, keepdims=True)
                 + jnp.exp(sink - m))
            o = (jnp.dot(p_w.astype(BF16), kv_ref[pl.ds(start, kwin), vcols], preferred_element_type=F32)
                 + jnp.dot(p_c.astype(BF16), kv_ref[0:CTX_LEN, vcols], preferred_element_type=F32))
            o_ref[:, head_cols(h)] = (o / l).astype(BF16)


def _sw_attention(q, kv, sink):
    nb, l, nq = q.shape
    nt = l // TM
    return pl.pallas_call(
        functools.partial(_sw_kernel, seq=l - CTX_LEN),
        grid=(nb, nt),
        in_specs=[pl.BlockSpec(memory_space=pltpu.SMEM),
                  pl.BlockSpec((None, TM, nq), lambda b, j: (b, j, 0)),
                  pl.BlockSpec((None, l, kv.shape[2]), lambda b, j: (b, 0, 0))],
        out_specs=pl.BlockSpec((None, TM, nq), lambda b, j: (b, j, 0)),
        out_shape=jax.ShapeDtypeStruct((nb, l, nq), BF16),
        compiler_params=_cparams("arbitrary", "arbitrary"),
        name="sw_attention",
    )(sink, q, kv)


def _chunk_index(step, rev, n_ctx, n_all):
    if not rev:
        return step
    return jnp.where(step < n_ctx, n_ctx - 1 - step, n_all + n_ctx - 1 - step)


def _tri(n, rev):
    r = lax.broadcasted_iota(jnp.int32, (n, n), 0)
    c = lax.broadcasted_iota(jnp.int32, (n, n), 1)
    return (c >= r) if rev else (c <= r)


def _lane_cummax(x, rev):
    lane = lax.broadcasted_iota(jnp.int32, (x.shape[0], LANES), 1)
    slabs = [x[:, i:i + LANES] for i in range(0, x.shape[1], LANES)]
    if rev:
        slabs = slabs[::-1]
    carry, out = None, []
    for h in slabs:
        s = 1
        while s < LANES:
            if rev:
                h = jnp.maximum(h, jnp.where(lane + s < LANES, pltpu.roll(h, LANES - s, 1), -jnp.inf))
            else:
                h = jnp.maximum(h, jnp.where(lane >= s, pltpu.roll(h, s, 1), -jnp.inf))
            s *= 2
        if carry is not None:
            h = jnp.maximum(h, carry)
        carry = jnp.max(h, axis=1, keepdims=True)
        out.append(h)
    return jnp.concatenate(out[::-1] if rev else out, axis=1)


def _ml_kernel(q_ref, k_ref, v_ref, og_ref, gr_ref, ng_ref, y_ref, acc_ref, *, n_ctx, n_all):
    lc = ML_SCAN
    hps = q_ref.shape[1] // ML_DK
    t0 = (((0,), (0,)), ((), ()))
    nt = (((1,), (1,)), ((), ()))
    eye = (_tri(lc, False) & _tri(lc, True)).astype(F32)

    ones_v = jnp.ones((lc, ML_DV), BF16)

    def gate_forms(step, rev):
        c = _chunk_index(step, rev, n_ctx, n_all)
        g8 = gr_ref[c]
        cum = jnp.dot(g8, _tri(lc, not rev).astype(F32), precision=HIGHEST, preferred_element_type=F32)
        d = 2 if rev else 0
        x = jnp.concatenate([g8[hl * 4 + d:hl * 4 + d + 1, :] - cum[hl * 4 + d + 1:hl * 4 + d + 2, :]
                             for hl in range(hps)] + [jnp.zeros((SUBLANES - hps, lc), F32)], axis=0)
        pm = _lane_cummax(x, rev)
        xmax = jnp.max(x, axis=1, keepdims=True)
        w = jnp.exp(x - xmax)
        rows = jnp.concatenate([g8, cum, x, pm, w], axis=0)
        cols = lax.dot_general(eye, rows, nt, precision=HIGHEST, preferred_element_type=F32)
        return c, rows, cols, xmax

    def chain(forms, hl, rev, state):
        ct, m = state
        c, rows, cols, xmax = forms
        r0 = pl.multiple_of(c * lc, lc)
        base = hl * 4 + (2 if rev else 0)
        li_r = rows[base:base + 1, :]
        bf_r = rows[SUBLANES + base + 1:SUBLANES + base + 2, :]
        bf_c = cols[:, SUBLANES + base + 1:SUBLANES + base + 2]
        pm_c = cols[:, 3 * SUBLANES + hl:3 * SUBLANES + hl + 1]
        w_c = cols[:, 4 * SUBLANES + hl:4 * SUBLANES + hl + 1]
        g = jnp.sum(rows[base + 1:base + 2, :], axis=1, keepdims=True)
        q = q_ref[pl.ds(r0, lc), hl * ML_DK:(hl + 1) * ML_DK]
        k = k_ref[pl.ds(r0, lc), hl * ML_DK:(hl + 1) * ML_DK]
        v1 = jnp.concatenate([v_ref[pl.ds(r0, lc), hl * ML_DV:(hl + 1) * ML_DV], ones_v], axis=1)
        mx = jnp.maximum(m, pm_c)
        e = jnp.exp(jnp.where(_tri(lc, rev), (li_r - bf_r) - mx, NEG_INF))
        sc = lax.dot_general(q, k, nt, preferred_element_type=F32) * e
        e_int = jnp.exp(m - mx)
        intra = jnp.dot(sc.astype(BF16), v1, preferred_element_type=F32)
        inter = jnp.dot(q, ct.astype(BF16), preferred_element_type=F32)
        num = intra[:, :ML_DV] + e_int * inter[:, :ML_DV]
        den = intra[:, ML_DV:ML_DV + 1] + e_int * inter[:, ML_DV:ML_DV + 1]
        hout = num / jnp.maximum(jnp.abs(den), jnp.exp(-mx - bf_c))
        acc_ref[pl.ds(r0, lc), hl * ML_DV:(hl + 1) * ML_DV] += hout
        m_loc = g + xmax[hl:hl + 1, :]
        c_loc = lax.dot_general((k.astype(F32) * w_c).astype(BF16), v1, t0, preferred_element_type=F32)
        m_new = jnp.maximum(g + m, m_loc)
        dec = jnp.exp(g + m - m_new)
        inc = jnp.exp(m_loc - m_new)
        return dec * ct + inc * c_loc, m_new

    acc_ref[...] = jnp.zeros_like(acc_ref)
    chains = [(hl, rev) for hl in range(hps) for rev in (False, True)]
    init = tuple((jnp.zeros((ML_DK, 2 * ML_DV), F32), jnp.zeros((1, 1), F32)) for _ in chains)

    def step_fn(step, states):
        forms = {rev: gate_forms(step, rev) for rev in (False, True)}
        return tuple(chain(forms[rev], hl, rev, st) for (hl, rev), st in zip(chains, states))

    lax.fori_loop(0, n_all, step_fn, init)

    for hl in range(hps):
        cols = slice(hl * ML_DV, (hl + 1) * ML_DV)
        z = acc_ref[:, cols]
        z = z * lax.rsqrt(jnp.mean(z * z, axis=-1, keepdims=True) + RMS_EPS) * ng_ref[:, cols]
        y_ref[:, cols] = (z.astype(BF16) * og_ref[:, cols])


def _ml_mixer(qkv, og, gates, norm_g):
    nb, l, _ = qkv.shape
    lc = ML_SCAN
    n_all, n_ctx = l // lc, CTX_LEN // lc
    hps = 2
    ngrp = ML_HEADS // hps
    nq = ML_HEADS * ML_DK
    g = gates[:, :, :4 * ML_HEADS].reshape(nb, n_all, lc, 2, 2, ngrp, hps)
    g = g.transpose(0, 5, 1, 6, 3, 4, 2).reshape(nb, ngrp, n_all, hps * 4, lc)
    vb = hps * ML_DV
    return pl.pallas_call(
        functools.partial(_ml_kernel, n_ctx=n_ctx, n_all=n_all),
        grid=(nb, ngrp),
        in_specs=[pl.BlockSpec((None, l, hps * ML_DK), lambda b, h: (b, 0, h)),
                  pl.BlockSpec((None, l, hps * ML_DK), lambda b, h: (b, 0, nq // (hps * ML_DK) + h)),
                  pl.BlockSpec((None, l, vb), lambda b, h: (b, 0, 2 * nq // vb + h)),
                  pl.BlockSpec((None, l, vb), lambda b, h: (b, 0, h)),
                  pl.BlockSpec((None, None, n_all, hps * 4, lc), lambda b, h: (b, h, 0, 0, 0)),
                  pl.BlockSpec((1, vb), lambda b, h: (0, h))],
        out_specs=pl.BlockSpec((None, l, vb), lambda b, h: (b, 0, h)),
        out_shape=jax.ShapeDtypeStruct((nb, l, ML_HEADS * ML_DV), BF16),
        scratch_shapes=[pltpu.VMEM((l, vb), F32)],
        compiler_params=_cparams("arbitrary", "arbitrary"),
        name="mlstm_mixer",
    )(qkv, qkv, qkv, og, g, norm_g.reshape(1, -1))


def _gl_kernel(q_ref, k_ref, v_ref, gate_ref, la0_ref, la1_ref, ng_ref, y_ref, acc_ref, st_ref, *, n_ctx, n_all):
    lc = GL_SCAN
    t0 = (((0,), (0,)), ((), ()))
    nt = (((1,), (1,)), ((), ()))
    scale = GL_DK ** -0.5

    def chain(step, rev):
        la_ref = la1_ref if rev else la0_ref
        sidx = 1 if rev else 0
        c = _chunk_index(step, rev, n_ctx, n_all)
        r0 = pl.multiple_of(c * lc, lc)
        tri = _tri(lc, rev)
        la = la_ref[pl.ds(r0, lc), :]
        bc = jnp.dot(tri.astype(F32), la, precision=HIGHEST, preferred_element_type=F32)
        g = jnp.sum(la, axis=0, keepdims=True)
        q = q_ref[pl.ds(r0, lc), :].astype(F32) * scale
        k = k_ref[pl.ds(r0, lc), :].astype(F32)
        v = v_ref[pl.ds(r0, lc), :]
        st = st_ref[sidx]
        eg = jnp.exp(g)
        k_dec = k * jnp.exp(-bc)
        q_t = (q * jnp.exp(bc)).astype(BF16)
        k_t = k_dec.astype(BF16)
        att = jnp.where(tri, lax.dot_general(q_t, k_t, nt, preferred_element_type=F32), 0.0)
        o = (jnp.dot(att.astype(BF16), v, preferred_element_type=F32)
             + lax.dot_general(q_t, st.astype(BF16), nt, preferred_element_type=F32))
        acc_ref[pl.ds(r0, lc), :] += o
        kd = (k_dec * eg).astype(BF16)
        s_loc = lax.dot_general(v, kd, t0, preferred_element_type=F32)
        st_ref[sidx] = st * eg + s_loc

    acc_ref[...] = jnp.zeros_like(acc_ref)
    st_ref[...] = jnp.zeros_like(st_ref)

    def step_fn(step, carry):
        chain(step, False)
        chain(step, True)
        return carry

    lax.fori_loop(0, n_all, step_fn, 0)
    z = acc_ref[...]
    z = z * lax.rsqrt(jnp.mean(z * z, axis=-1, keepdims=True) + RMS_EPS) * ng_ref[...]
    y_ref[...] = z.astype(BF16) * gate_ref[...]


def _gl_mixer(qkv, gate, la, norm_g):
    nb, l, _ = qkv.shape
    lc = GL_SCAN
    n_all, n_ctx = l // lc, CTX_LEN // lc
    hq = GL_HEADS
    return pl.pallas_call(
        functools.partial(_gl_kernel, n_ctx=n_ctx, n_all=n_all),
        grid=(nb, hq),
        in_specs=[pl.BlockSpec((None, l, GL_DK), lambda b, h: (b, 0, h)),
                  pl.BlockSpec((None, l, GL_DK), lambda b, h: (b, 0, hq + h)),
                  pl.BlockSpec((None, l, GL_DV), lambda b, h: (b, 0, hq + h)),
                  pl.BlockSpec((None, l, GL_DV), lambda b, h: (b, 0, h)),
                  pl.BlockSpec((None, l, GL_DK), lambda b, h: (b, 0, h)),
                  pl.BlockSpec((None, l, GL_DK), lambda b, h: (b, 0, hq + h)),
                  pl.BlockSpec((1, GL_DV), lambda b, h: (0, h))],
        out_specs=pl.BlockSpec((None, l, GL_DV), lambda b, h: (b, 0, h)),
        out_shape=jax.ShapeDtypeStruct((nb, l, hq * GL_DV), BF16),
        scratch_shapes=[pltpu.VMEM((l, GL_DV), F32), pltpu.VMEM((2, GL_DV, GL_DK), F32)],
        compiler_params=_cparams("arbitrary", "arbitrary"),
        name="gla_mixer",
    )(qkv, qkv, qkv, gate, la, la, norm_g.reshape(1, -1))


def _store_token_tiles(ref, val):
    rows = val.shape[0]
    for s in range(val.shape[1] // LANES):
        ref[pl.ds(s, rows, stride=SUBLANES), :] = val[:, s * LANES:(s + 1) * LANES]


def _load_token_tiles(ref, rows):
    return jnp.concatenate([ref[pl.ds(s, rows, stride=SUBLANES), :] for s in range(SUBLANES)], axis=1)


def _post_kernel(y_ref, x_ref, mod_ref, wo_ref, g_ref, wr_ref, br_ref, xo_ref, f_ref, rt_ref, cnt_ref, run_ref):
    first = (pl.program_id(0) == 0) & (pl.program_id(1) == 0)

    @pl.when(first)
    def _():
        run_ref[...] = jnp.zeros_like(run_ref)

    o = jnp.dot(y_ref[...], wo_ref[...], preferred_element_type=F32)
    xn = x_ref[...] + mod_ref[2:3, :] * o
    xo_ref[...] = xn
    f = _norm_mod(xn, g_ref[...], mod_ref, 3)
    _store_token_tiles(f_ref, f)
    f_hi = f.astype(BF16)
    f_lo = (f - f_hi.astype(F32)).astype(BF16)
    hh = jnp.dot(f_hi, wr_ref[...], preferred_element_type=F32)
    lh = jnp.dot(f_lo, wr_ref[:, :LANES], preferred_element_type=F32)
    lg = hh[:, :LANES] + (hh[:, LANES:] + lh) + br_ref[...]
    lane = lax.broadcasted_iota(jnp.int32, lg.shape, 1).astype(F32)
    big = float(LANES)

    def masked_max(mask):
        return jnp.max(jnp.where(mask, lg, -jnp.inf), axis=-1, keepdims=True)

    def first_at(mask, val):
        return jnp.min(jnp.where(mask & (lg == val), lane, big), axis=-1, keepdims=True)

    gm = lane < MOE_GROUPS
    mg = masked_max(gm)
    g_idx = first_at(gm, mg)
    g_w = 1.0 / jnp.sum(jnp.where(gm, jnp.exp(lg - mg), 0.0), axis=-1, keepdims=True)
    lo = MOE_GROUPS + MOE_PER_GROUP * g_idx
    em = (lane >= lo) & (lane < lo + MOE_PER_GROUP)
    v0 = masked_max(em)
    i0 = first_at(em, v0)
    em1 = em & (lane != i0)
    v1 = masked_max(em1)
    i1 = first_at(em1, v1)
    e0, e1 = i0 - MOE_GROUPS, i1 - MOE_GROUPS
    t = jnp.exp(v1 - v0)
    w0 = g_w / (1.0 + t)
    w1 = g_w * t / (1.0 + t)
    oh = ((lane == e0) | (lane == e1)).astype(F32)
    rows = lg.shape[0]
    below = lax.broadcasted_iota(jnp.int32, (rows, rows), 0) > lax.broadcasted_iota(jnp.int32, (rows, rows), 1)
    tot = run_ref[...] + jnp.dot(below.astype(BF16), oh.astype(BF16), preferred_element_type=F32)
    r0 = jnp.sum(jnp.where(lane == e0, tot, 0.0), axis=-1, keepdims=True)
    r1 = jnp.sum(jnp.where(lane == e1, tot, 0.0), axis=-1, keepdims=True)
    run_ref[...] = run_ref[...] + jnp.sum(oh, axis=0, keepdims=True)
    cnt_ref[...] = jnp.broadcast_to(run_ref[...], cnt_ref.shape)
    out = jnp.zeros_like(lg)
    for i, val in enumerate((e0, e1, w0, w1, r0, r1)):
        out = jnp.where(lane == i, val, out)
    rt_ref[...] = out


def _post_call(y, x, mods, w_o, g_ffn, w_r, b_r):
    nb, l, d = x.shape
    nt = l // TM
    tile = lambda b, t: (b * nt + t, 0)
    return pl.pallas_call(
        _post_kernel,
        grid=(nb, nt),
        in_specs=[pl.BlockSpec((None, TM, d), lambda b, t: (b, t, 0)),
                  pl.BlockSpec((None, TM, d), lambda b, t: (b, t, 0)),
                  _mod_spec(nb),
                  pl.BlockSpec((d, d), lambda b, t: (0, 0)),
                  pl.BlockSpec((1, d), lambda b, t: (0, 0)),
                  pl.BlockSpec((d, 2 * LANES), lambda b, t: (0, 0)),
                  pl.BlockSpec((1, LANES), lambda b, t: (0, 0))],
        out_specs=[pl.BlockSpec((None, TM, d), lambda b, t: (b, t, 0)),
                   pl.BlockSpec((TM * SUBLANES, LANES), tile),
                   pl.BlockSpec((TM, LANES), tile),
                   pl.BlockSpec((SUBLANES, LANES), lambda b, t: (0, 0))],
        out_shape=[jax.ShapeDtypeStruct((nb, l, d), F32),
                   jax.ShapeDtypeStruct((nb * l * SUBLANES, LANES), F32),
                   jax.ShapeDtypeStruct((nb * l, LANES), F32),
                   jax.ShapeDtypeStruct((SUBLANES, LANES), F32)],
        scratch_shapes=[pltpu.VMEM((1, LANES), F32)],
        compiler_params=_cparams("arbitrary", "arbitrary"),
        name="outproj_router",
    )(y, x, mods, w_o, g_ffn.reshape(1, d), w_r, b_r)


def _dispatch_kernel(dest_ref, pend_ref, f_ref, xs_out, zbuf, sem):
    tile = f_ref.shape[0] // SUBLANES
    base = pl.program_id(0) * tile

    @pl.when(pl.program_id(0) == 0)
    def _():
        zbuf[...] = jnp.zeros_like(zbuf)

        def zero_block(blk):
            start = pl.multiple_of(blk * (MOE_BLOCK * SUBLANES), MOE_BLOCK * SUBLANES)
            return pltpu.make_async_copy(zbuf, xs_out.at[pl.ds(start, MOE_BLOCK * SUBLANES), :], sem)

        def has_rows(e):
            return pend_ref[e] > (0 if e == 0 else pend_ref[e - 1])

        for e in range(MOE_EXPERTS):
            pl.when(has_rows(e))(lambda e=e: zero_block(pend_ref[e] // MOE_BLOCK - 1).start())
        for e in range(MOE_EXPERTS):
            pl.when(has_rows(e))(lambda e=e: zero_block(pend_ref[e] // MOE_BLOCK - 1).wait())
        n_act = pend_ref[MOE_EXPERTS - 1] // MOE_BLOCK
        n_blk = xs_out.shape[0] // (MOE_BLOCK * SUBLANES)
        lax.fori_loop(n_act, n_blk, lambda b, c: (zero_block(b).start(), c)[1], 0)
        lax.fori_loop(n_act, n_blk, lambda b, c: (zero_block(b).wait(), c)[1], 0)

    def copy(t, k):
        d = dest_ref[2 * (base + t) + k]
        return pltpu.make_async_copy(f_ref.at[pl.ds(t * SUBLANES, SUBLANES), :],
                                     xs_out.at[pl.ds(d * SUBLANES, SUBLANES), :], sem)

    def issue(t, c):
        copy(t, 0).start()
        copy(t, 1).start()
        return c

    lax.fori_loop(0, tile, issue, 0, unroll=DMA_UNROLL)
    for _ in range(2):
        pltpu.make_async_copy(f_ref, xs_out.at[pl.ds(0, tile * SUBLANES), :], sem).wait()


def _dispatch(dest, p_end, f_tiles, n_slots):
    n_tok = f_tiles.shape[0] // SUBLANES
    tile = DISPATCH_TILE
    assert n_tok % tile == 0
    return pl.pallas_call(
        _dispatch_kernel,
        grid_spec=pltpu.PrefetchScalarGridSpec(
            num_scalar_prefetch=2,
            grid=(n_tok // tile,),
            in_specs=[pl.BlockSpec((tile * SUBLANES, LANES), lambda i, dr, pe: (i, 0))],
            out_specs=pl.BlockSpec(memory_space=pl.ANY),
            scratch_shapes=[pltpu.VMEM((MOE_BLOCK * SUBLANES, LANES), F32), pltpu.SemaphoreType.DMA(())]),
        out_shape=jax.ShapeDtypeStruct((n_slots * SUBLANES, LANES), F32),
        compiler_params=_cparams("arbitrary"),
        name="moe_dispatch",
    )(dest, p_end, f_tiles)


def _expert_kernel(blk_e_ref, nact_ref, xs_ref, wg_ref, wu_ref, wd_ref, ys_ref, wg_s, wu_s, wd_s):
    i = pl.program_id(0)
    active = i < nact_ref[0]
    changed = (i == 0) | (blk_e_ref[i] != blk_e_ref[jnp.maximum(i - 1, 0)])

    @pl.when(active & changed)
    def _():
        wg_s[...] = wg_ref[...].astype(BF16)
        wu_s[...] = wu_ref[...].astype(BF16)
        wd_s[...] = wd_ref[...].astype(BF16)

    @pl.when(active)
    def _():
        x = _load_token_tiles(xs_ref, MOE_BLOCK).astype(BF16)
        a = jnp.dot(x, wg_s[...], preferred_element_type=F32)
        u = jnp.dot(x, wu_s[...], preferred_element_type=F32)
        y = jnp.dot((_silu(a) * u).astype(BF16), wd_s[...], preferred_element_type=F32)
        _store_token_tiles(ys_ref, y)

    @pl.when(jnp.logical_not(active))
    def _():
        ys_ref[...] = jnp.zeros_like(ys_ref)


def _experts(blk_e, nact, xs, w_gate, w_up, w_down, layer):
    n_blk = blk_e.shape[0]
    _, _, d, ff = w_gate.shape
    rows = MOE_BLOCK * SUBLANES
    last = lambda i, na: jnp.minimum(i, jnp.maximum(na[0] - 1, 0))
    blk = lambda i, be, na: (last(i, na), 0)
    wsel = lambda i, be, na: (layer, be[last(i, na)], 0, 0)
    return pl.pallas_call(
        _expert_kernel,
        grid_spec=pltpu.PrefetchScalarGridSpec(
            num_scalar_prefetch=2,
            grid=(n_blk,),
            in_specs=[pl.BlockSpec((rows, LANES), blk),
                      pl.BlockSpec((None, None, d, ff), wsel),
                      pl.BlockSpec((None, None, d, ff), wsel),
                      pl.BlockSpec((None, None, ff, d), wsel)],
            out_specs=pl.BlockSpec((rows, LANES), lambda i, be, na: (i, 0)),
            scratch_shapes=[pltpu.VMEM((d, ff), BF16), pltpu.VMEM((d, ff), BF16), pltpu.VMEM((ff, d), BF16)]),
        out_shape=jax.ShapeDtypeStruct(xs.shape, F32),
        compiler_params=_cparams("arbitrary"),
        name="moe_experts",
    )(blk_e, nact, xs, w_gate, w_up, w_down)


def _combine_kernel(dest_ref, ys_hbm, x_ref, rt_ref, mod_ref, o_ref, ybuf, sem, *, nt, t0):
    i = pl.program_id(0)
    n = pl.num_programs(0)
    nte = nt - t0

    def gather(step, slot, start):
        base = ((step // nte) * nt + step % nte + t0) * TM

        def copy(t, k):
            d = dest_ref[2 * (base + t) + k]
            return pltpu.make_async_copy(ys_hbm.at[pl.ds(d * SUBLANES, SUBLANES), :],
                                         ybuf.at[slot, k, pl.ds(t * SUBLANES, SUBLANES), :], sem.at[slot])

        def issue(t, c):
            copy(t, 0).start()
            copy(t, 1).start()
            return c

        if start:
            lax.fori_loop(0, TM, issue, 0, unroll=DMA_UNROLL)
        else:
            for k in range(2):
                pltpu.make_async_copy(ys_hbm.at[pl.ds(0, TM * SUBLANES), :], ybuf.at[slot, k], sem.at[slot]).wait()

    slot = i % 2
    pl.when(i == 0)(lambda: gather(i, slot, True))
    pl.when(i + 1 < n)(lambda: gather(i + 1, 1 - slot, True))
    gather(i, slot, False)
    y0 = _load_token_tiles(ybuf.at[slot, 0], TM)
    y1 = _load_token_tiles(ybuf.at[slot, 1], TM)
    rt = rt_ref[...]
    y = rt[:, 2:3] * y0 + rt[:, 3:4] * y1
    o_ref[...] = x_ref[...] + mod_ref[5:6, :] * y


def _combine(dest, ys, x, rt, mods, skip_ctx):
    nb, l, d = x.shape
    nt = l // TM
    t0 = 1 if skip_ctx else 0
    nte = nt - t0
    bt = lambda i: (i // nte, i % nte + t0)
    mod_row = (lambda b, t: b) if skip_ctx else (lambda b, t: jnp.where(t == 0, nb, b))
    return pl.pallas_call(
        functools.partial(_combine_kernel, nt=nt, t0=t0),
        grid_spec=pltpu.PrefetchScalarGridSpec(
            num_scalar_prefetch=1,
            grid=(nb * nte,),
            in_specs=[pl.BlockSpec(memory_space=pl.ANY),
                      pl.BlockSpec((None, TM, d), lambda i, dr: (*bt(i), 0)),
                      pl.BlockSpec((TM, LANES), lambda i, dr: (bt(i)[0] * nt + bt(i)[1], 0)),
                      pl.BlockSpec((None, 6, d), lambda i, dr: (mod_row(*bt(i)), 0, 0))],
            out_specs=pl.BlockSpec((None, TM, d), lambda i, dr: (i // nte, i % nte, 0)),
            scratch_shapes=[pltpu.VMEM((2, 2, TM * SUBLANES, LANES), F32), pltpu.SemaphoreType.DMA((2,))]),
        out_shape=jax.ShapeDtypeStruct((nb, nte * TM, d), F32),
        compiler_params=_cparams("arbitrary"),
        name="moe_combine",
    )(dest, ys, x, rt, mods)


def _moe(y_mix, x, mods, w_o, g_ffn, w_grp, b_grp, w_exp, b_exp, w_gate, w_up, w_down, layer, skip_ctx):
    nb, l, d = x.shape
    n_tok = nb * l
    nr = MOE_GROUPS + MOE_EXPERTS
    w_r = jnp.pad(jnp.concatenate([w_grp, w_exp], axis=1), ((0, 0), (0, LANES - nr)))
    b_r = jnp.pad(jnp.concatenate([b_grp, b_exp]), (0, LANES - nr)).reshape(1, LANES)
    w_hi = w_r.astype(BF16)
    w_r = jnp.concatenate([w_hi, (w_r - w_hi.astype(F32)).astype(BF16)], axis=1)
    x_new, f_tiles, rt, cnt = _post_call(y_mix, x, mods, w_o, g_ffn, w_r, b_r)
    counts = cnt[0, :MOE_EXPERTS].astype(jnp.int32)
    padded = (counts + MOE_BLOCK - 1) // MOE_BLOCK * MOE_BLOCK
    p_end = jnp.cumsum(padded)
    p_start = p_end - padded
    n_pairs = 2 * n_tok
    n_blk = -(-(n_pairs + MOE_EXPERTS * (MOE_BLOCK - 1)) // MOE_BLOCK)
    eid = rt[:, 0:2].astype(jnp.int32)
    first = jnp.sum(jnp.where(eid[:, :, None] == jnp.arange(MOE_EXPERTS, dtype=jnp.int32), p_start, 0), axis=-1)
    dest = (first + rt[:, 4:6].astype(jnp.int32)).reshape(-1)
    blk_row = jnp.arange(n_blk, dtype=jnp.int32) * MOE_BLOCK
    blk_e = jnp.minimum(jnp.sum((p_end[None, :] <= blk_row[:, None]).astype(jnp.int32), axis=1), MOE_EXPERTS - 1)
    nact = (p_end[-1:] // MOE_BLOCK).astype(jnp.int32)
    xs = _dispatch(dest, p_end.astype(jnp.int32), f_tiles, n_blk * MOE_BLOCK)
    ys = _experts(blk_e, nact, xs, w_gate, w_up, w_down, layer)
    return _combine(dest, ys, x_new, rt, mods, skip_ctx)


def _rope_tables(seq):
    nf = SW_HD // 4
    inv = ROPE_BASE ** (-jnp.arange(nf, dtype=F32) / nf)
    pos = jnp.arange(seq, dtype=jnp.int32)
    rows, cols = (pos // GRID_W).astype(F32), (pos % GRID_W).astype(F32)
    lane = jnp.arange(LANES)
    p = jnp.where((lane % SW_HD < SW_HD // 2)[None, :], rows[:, None], cols[:, None])
    ang = p * inv[lane % nf][None, :]
    sign = jnp.where((lane % (2 * nf)) < nf, -1.0, 1.0)[None, :]
    cs = jnp.concatenate([jnp.ones((CTX_LEN, LANES), F32), jnp.cos(ang)], axis=0)
    sn = jnp.concatenate([jnp.zeros((CTX_LEN, LANES), F32), jnp.sin(ang) * sign], axis=0)
    return cs, sn


def _pad_cols(w, n):
    return jnp.pad(w, ((0, 0), (0, n - w.shape[1])))


def kernel(x, c, ctx, c_ctx, ada_w, ada_b, norm_mix_g, norm_ffn_g, na_w_qkv, na_qk_g, na_rpb, na_w_o, ml_w_in, ml_b_gates, ml_norm_g, ml_w_o, sw_w_qkv, sw_qk_g, sw_sink, sw_w_o, gl_w_in, gl_w_a2, gl_b_a, gl_norm_g, gl_w_o, moe_w_grp, moe_b_grp, moe_w_exp, moe_b_exp, moe_w_gate, moe_w_up, moe_w_down):
    nb, seq, d = x.shape
    depth = ada_w.shape[0]
    assert d == D_MODEL and ctx.shape[1] == CTX_LEN == TM and seq % TM == 0
    rows = -(-(nb + 1) // SUBLANES) * SUBLANES
    cond = jnp.pad(jnp.concatenate([c, c_ctx[None]], axis=0), ((0, rows - nb - 1), (0, 0)))
    mods_all = _ada_mods(cond, ada_w, ada_b)
    xa = jnp.concatenate([ctx, x], axis=1)
    for i in range(depth):
        j, kind = divmod(i, 4)
        mods = mods_all[i]
        g_mix = norm_mix_g[i]
        if kind == 0:
            qkg = na_qk_g[j]
            (qkv,) = _proj_call(_proj_na_kernel, xa, mods, g_mix, na_w_qkv[j].astype(BF16), [(qkg, False)],
                                [3 * NA_HEADS * NA_HD], [BF16], "proj_na")
            y = _na_attention(qkv, _na_bias_table(na_rpb[j], seq // GRID_W))
            w_o = na_w_o[j]
        elif kind == 1:
            n_in = -(-ml_w_in.shape[2] // LANES) * LANES
            bg = jnp.pad(ml_b_gates[j], (0, LANES - ml_b_gates.shape[1])).reshape(1, LANES)
            qkv, og, gates = _proj_call(_proj_ml_kernel, xa, mods, g_mix, _pad_cols(ml_w_in[j], n_in).astype(BF16),
                                        [(bg, False)], [2 * ML_HEADS * ML_DK + ML_HEADS * ML_DV, ML_HEADS * ML_DV, LANES],
                                        [BF16, BF16, F32], "proj_ml")
            y = _ml_mixer(qkv, og, gates, ml_norm_g[j])
            w_o = ml_w_o[j]
        elif kind == 2:
            qkg = jnp.concatenate([sw_qk_g[j], sw_qk_g[j]], axis=1)
            cs, sn = _rope_tables(seq)
            q, kv = _proj_call(_proj_sw_kernel, xa, mods, g_mix, sw_w_qkv[j].astype(BF16),
                               [(qkg, False), (cs, True), (sn, True)],
                               [SW_HEADS * SW_HD, 2 * SW_KV * SW_HD], [BF16, BF16], "proj_sw")
            y = _sw_attention(q, kv, sw_sink[j])
            w_o = sw_w_o[j]
        else:
            n_in = -(-gl_w_in.shape[2] // LANES) * LANES
            nk = GL_HEADS * GL_DK
            wa = jnp.zeros((LANES, 2 * nk), F32)
            wa = wa.at[:GL_RANK, :nk].set(gl_w_a2[j, 0]).at[GL_RANK:2 * GL_RANK, nk:].set(gl_w_a2[j, 1])
            ba = gl_b_a[j].reshape(1, 2 * nk)
            qkv, gate, la = _proj_call(_proj_gl_kernel, xa, mods, g_mix, _pad_cols(gl_w_in[j], n_in).astype(BF16),
                                       [(wa.astype(BF16), False), (ba, False)],
                                       [2 * nk + GL_HEADS * GL_DV, GL_HEADS * GL_DV, 2 * nk], [BF16, BF16, F32], "proj_gl")
            y = _gl_mixer(qkv, gate, la, gl_norm_g[j])
            w_o = gl_w_o[j]
        xa = _moe(y, xa, mods, w_o.astype(BF16), norm_ffn_g[i], moe_w_grp[i], moe_b_grp[i], moe_w_exp[i], moe_b_exp[i],
                  moe_w_gate, moe_w_up, moe_w_down, i, i == depth - 1)
    return xa
```

```python
import functools

import jax
import jax.numpy as jnp
import numpy as np
from jax import lax
from jax.experimental import pallas as pl
from jax.experimental.pallas import tpu as pltpu

F32 = jnp.float32
BF16 = jnp.bfloat16
HIGHEST = lax.Precision.HIGHEST

D_MODEL = 1024
CTX_LEN = 256
GRID_W = 64
RMS_EPS = 1e-6
NEG_INF = -1e30
ROPE_BASE = 10000.0

NA_HEADS, NA_HD, NA_WIN_R, NA_WIN_C = 8, 128, 8, 16
NA_QROWS = 4
NA_KROWS = NA_QROWS + NA_WIN_R - 1
ML_HEADS, ML_DK, ML_DV, ML_CAP = 8, 64, 128, 15.0
ML_SCAN = 256
SW_HEADS, SW_KV, SW_HD, SW_WINDOW = 16, 4, 64, 128
GL_HEADS, GL_DK, GL_DV, GL_RANK, GL_TAU = 4, 128, 256, 16, 16.0
GL_SCAN = 128
MOE_GROUPS, MOE_PER_GROUP, MOE_EXPERTS, MOE_FF = 4, 8, 32, 512
MOE_BLOCK = 512

LANES = 128
SUBLANES = 8
TM = 256
VMEM_LIMIT = 48 * 1024 * 1024
DMA_UNROLL = 8
DISPATCH_TILE = 3 * TM


def _cparams(*sem):
    return pltpu.CompilerParams(dimension_semantics=sem, vmem_limit_bytes=VMEM_LIMIT)


def _norm_mod(x, g, mod_ref, k):
    y = x * lax.rsqrt(jnp.mean(x * x, axis=-1, keepdims=True) + RMS_EPS) * g
    return y * (1.0 + mod_ref[k + 1:k + 2, :]) + mod_ref[k:k + 1, :]


def _log_sigmoid(x):
    return jnp.minimum(x, 0.0) - jnp.log(1.0 + jnp.exp(-jnp.abs(x)))


def _silu(x):
    return x * jax.nn.sigmoid(x)


def _mod_kernel(c_ref, w_ref, b_ref, o_ref):
    s = _silu(c_ref[...])
    o_ref[...] = jnp.dot(s.astype(BF16), w_ref[...].astype(BF16), preferred_element_type=F32) + b_ref[...]


def _ada_mods(cond, ada_w, ada_b):
    depth, d, _ = ada_w.shape
    rows = cond.shape[0]
    out = pl.pallas_call(
        _mod_kernel,
        grid=(depth, 6),
        in_specs=[pl.BlockSpec((rows, d), lambda i, n: (0, 0)),
                  pl.BlockSpec((None, d, d), lambda i, n: (i, 0, n)),
                  pl.BlockSpec((None, 1, d), lambda i, n: (i, 0, n))],
        out_specs=pl.BlockSpec((None, rows, d), lambda i, n: (i, 0, n)),
        out_shape=jax.ShapeDtypeStruct((depth, rows, 6 * d), F32),
        compiler_params=_cparams("arbitrary", "arbitrary"),
        name="ada_mods",
    )(cond, ada_w, ada_b.reshape(depth, 1, 6 * d))
    return out.reshape(depth, rows, 6, d)


def _mod_spec(nb):
    return pl.BlockSpec((None, 6, D_MODEL), lambda b, t: (jnp.where(t == 0, nb, b), 0, 0))


def _proj_call(kernel, x, mods, g, w, extras, out_cols, out_dtypes, name):
    nb, l, d = x.shape
    nt = l // TM
    n = w.shape[1]
    in_specs = [pl.BlockSpec((None, TM, d), lambda b, t: (b, t, 0)),
                _mod_spec(nb),
                pl.BlockSpec((1, d), lambda b, t: (0, 0)),
                pl.BlockSpec((d, n), lambda b, t: (0, 0))]
    args = [x, mods, g.reshape(1, d), w]
    for e, per_tile in extras:
        if per_tile:
            in_specs.append(pl.BlockSpec((TM, e.shape[1]), lambda b, t: (t, 0)))
        else:
            in_specs.append(pl.BlockSpec(e.shape, lambda b, t: (0, 0)))
        args.append(e)
    return pl.pallas_call(
        kernel,
        grid=(nb, nt),
        in_specs=in_specs,
        out_specs=[pl.BlockSpec((None, TM, c), lambda b, t: (b, t, 0)) for c in out_cols],
        out_shape=[jax.ShapeDtypeStruct((nb, l, c), dt) for c, dt in zip(out_cols, out_dtypes)],
        compiler_params=_cparams("arbitrary", "arbitrary"),
        name=name,
    )(*args)


def _proj_na_kernel(x_ref, mod_ref, g_ref, w_ref, qkg_ref, o_ref):
    h = _norm_mod(x_ref[...], g_ref[...], mod_ref, 0)
    u = jnp.dot(h.astype(BF16), w_ref[...], preferred_element_type=F32)
    nq = NA_HEADS * NA_HD
    for part in range(2):
        gain = qkg_ref[part:part + 1, :]
        for hh in range(NA_HEADS):
            lo = part * nq + hh * NA_HD
            z = u[:, lo:lo + NA_HD]
            z = z * lax.rsqrt(jnp.mean(z * z, axis=-1, keepdims=True) + RMS_EPS) * gain
            o_ref[:, lo:lo + NA_HD] = z.astype(BF16)
    o_ref[:, 2 * nq:] = u[:, 2 * nq:].astype(BF16)


def _head64_rms(z, gain):
    lane = lax.broadcasted_iota(jnp.int32, z.shape, 1)
    lo = lane < SW_HD
    zz = z * z
    s_lo = jnp.sum(jnp.where(lo, zz, 0.0), axis=-1, keepdims=True)
    s_hi = jnp.sum(jnp.where(lo, 0.0, zz), axis=-1, keepdims=True)
    ms = jnp.where(lo, s_lo, s_hi) * (1.0 / SW_HD)
    return z * lax.rsqrt(ms + RMS_EPS) * gain


def _rope_slab(z, cs, sn):
    lane = lax.broadcasted_iota(jnp.int32, z.shape, 1)
    first = (lane % 32) < 16
    partner = jnp.where(first, pltpu.roll(z, LANES - 16, 1), pltpu.roll(z, 16, 1))
    return z * cs + partner * sn


def _proj_sw_kernel(x_ref, mod_ref, g_ref, w_ref, qkg_ref, cs_ref, sn_ref, q_ref, kv_ref):
    h = _norm_mod(x_ref[...], g_ref[...], mod_ref, 0)
    u = jnp.dot(h.astype(BF16), w_ref[...], preferred_element_type=F32)
    cs, sn = cs_ref[...], sn_ref[...]
    nq, nk = SW_HEADS * SW_HD, SW_KV * SW_HD
    scale = SW_HD ** -0.5
    for s in range(nq // LANES):
        z = _head64_rms(u[:, s * LANES:(s + 1) * LANES], qkg_ref[0:1, :])
        q_ref[:, s * LANES:(s + 1) * LANES] = (_rope_slab(z, cs, sn) * scale).astype(BF16)
    for s in range(nk // LANES):
        z = _head64_rms(u[:, nq + s * LANES:nq + (s + 1) * LANES], qkg_ref[1:2, :])
        kv_ref[:, s * LANES:(s + 1) * LANES] = _rope_slab(z, cs, sn).astype(BF16)
    kv_ref[:, nk:] = u[:, nq + nk:].astype(BF16)


def _proj_ml_kernel(x_ref, mod_ref, g_ref, w_ref, bg_ref, qkv_ref, og_ref, gt_ref):
    h = _norm_mod(x_ref[...], g_ref[...], mod_ref, 0)
    u = jnp.dot(h.astype(BF16), w_ref[...], preferred_element_type=F32)
    nq, nv = ML_HEADS * ML_DK, ML_HEADS * ML_DV
    qkv_ref[:, :nq] = (u[:, :nq] * (ML_DK ** -0.5)).astype(BF16)
    qkv_ref[:, nq:] = u[:, nq:2 * nq + nv].astype(BF16)
    og_ref[...] = jax.nn.sigmoid(u[:, 2 * nq + nv:2 * nq + 2 * nv]).astype(BF16)
    pre = u[:, 2 * nq + 2 * nv:] + bg_ref[...]
    pre = ML_CAP * jnp.tanh(pre / ML_CAP)
    lane = lax.broadcasted_iota(jnp.int32, pre.shape, 1)
    gt_ref[...] = jnp.where((lane % 16) >= 8, _log_sigmoid(pre), pre)


def _proj_gl_kernel(x_ref, mod_ref, g_ref, w_ref, wa_ref, ba_ref, qkv_ref, gate_ref, la_ref):
    h = _norm_mod(x_ref[...], g_ref[...], mod_ref, 0)
    u = jnp.dot(h.astype(BF16), w_ref[...], preferred_element_type=F32)
    nq, nv = GL_HEADS * GL_DK, GL_HEADS * GL_DV
    qkv_ref[...] = u[:, :2 * nq + nv].astype(BF16)
    gate_ref[...] = _silu(u[:, 2 * nq + nv:2 * nq + 2 * nv]).astype(BF16)
    z = u[:, 2 * nq + 2 * nv:]
    a = jnp.dot(z.astype(BF16), wa_ref[...], preferred_element_type=F32) + ba_ref[...]
    la_ref[...] = _log_sigmoid(a) * (1.0 / GL_TAU)


def _na_bias_table(rpb, n_rows):
    cq = np.arange(GRID_W)[:, None]
    ck = np.arange(GRID_W)[None, :]
    c0 = np.clip(cq - NA_WIN_C // 2, 0, GRID_W - NA_WIN_C)
    col_ok = (ck >= c0) & (ck < c0 + NA_WIN_C)
    dc = np.clip(ck - cq + NA_WIN_C - 1, 0, 2 * NA_WIN_C - 2)
    pick = ((dc[None] == np.arange(2 * NA_WIN_C - 1)[:, None, None]) & col_ok[None]).astype(np.float32)
    m = jnp.einsum('hrd,dqk->hqrk', rpb.astype(F32), jnp.asarray(pick), precision=HIGHEST)
    m = jnp.where(jnp.asarray(col_ok)[None, :, None, :], m, NEG_INF)
    blocks = []
    for jj in range(n_rows // NA_QROWS):
        ws = min(max(NA_QROWS * jj - NA_WIN_R // 2, 0), n_rows - NA_KROWS)
        per_row = []
        for ri in range(NA_QROWS):
            r = NA_QROWS * jj + ri
            r0 = min(max(r - NA_WIN_R // 2, 0), n_rows - NA_WIN_R)
            lead = r0 - ws
            d0 = r0 - r + NA_WIN_R - 1
            piece = jnp.pad(m[:, :, d0:d0 + NA_WIN_R, :],
                            ((0, 0), (0, 0), (lead, NA_KROWS - NA_WIN_R - lead), (0, 0)), constant_values=NEG_INF)
            per_row.append(piece.reshape(rpb.shape[0], GRID_W, NA_KROWS * GRID_W))
        blocks.append(jnp.concatenate(per_row, axis=1))
    return jnp.stack(blocks, axis=1)


def _na_kernel(q_ref, k_ref, v_ref, bias_ref, o_ref, *, n_rows):
    j = pl.program_id(1)
    nb = q_ref.shape[0]
    scale = NA_HD ** -0.5
    nt = (((1,), (1,)), ((), ()))
    nkw = NA_KROWS * GRID_W

    @pl.when(j == 0)
    def _ctx():
        def body(b, c):
            q = q_ref[b]
            s = lax.dot_general(q, k_ref[b, 0:CTX_LEN, :], nt, preferred_element_type=F32) * scale
            p = jnp.exp(s - jnp.max(s, axis=-1, keepdims=True))
            l = jnp.sum(p, axis=-1, keepdims=True)
            o = jnp.dot(p.astype(BF16), v_ref[b, 0:CTX_LEN, :], preferred_element_type=F32)
            o_ref[b] = (o / l).astype(BF16)
            return c
        lax.fori_loop(0, nb, body, 0)

    @pl.when(j > 0)
    def _lat():
        ws = jnp.clip(NA_QROWS * (j - 1) - NA_WIN_R // 2, 0, n_rows - NA_KROWS)
        start = pl.multiple_of(CTX_LEN + ws * GRID_W, GRID_W)
        bias = bias_ref[...]

        def body(b, c):
            q = q_ref[b]
            s_n = lax.dot_general(q, k_ref[b, pl.ds(start, nkw), :], nt, preferred_element_type=F32) * scale + bias
            s_c = lax.dot_general(q, k_ref[b, 0:CTX_LEN, :], nt, preferred_element_type=F32) * scale
            m = jnp.maximum(jnp.max(s_n, axis=-1, keepdims=True), jnp.max(s_c, axis=-1, keepdims=True))
            p_n = jnp.exp(s_n - m)
            p_c = jnp.exp(s_c - m)
            l = jnp.sum(p_n, axis=-1, keepdims=True) + jnp.sum(p_c, axis=-1, keepdims=True)
            o = (jnp.dot(p_n.astype(BF16), v_ref[b, pl.ds(start, nkw), :], preferred_element_type=F32)
                 + jnp.dot(p_c.astype(BF16), v_ref[b, 0:CTX_LEN, :], preferred_element_type=F32))
            o_ref[b] = (o / l).astype(BF16)
            return c
        lax.fori_loop(0, nb, body, 0)


def _na_attention(qkv, bias_tab):
    nb, l, _ = qkv.shape
    n_rows = (l - CTX_LEN) // GRID_W
    nt = l // TM
    hq = NA_HEADS
    return pl.pallas_call(
        functools.partial(_na_kernel, n_rows=n_rows),
        grid=(hq, nt),
        in_specs=[pl.BlockSpec((nb, TM, NA_HD), lambda h, j: (0, j, h)),
                  pl.BlockSpec((nb, l, NA_HD), lambda h, j: (0, 0, hq + h)),
                  pl.BlockSpec((nb, l, NA_HD), lambda h, j: (0, 0, 2 * hq + h)),
                  pl.BlockSpec((None, None, TM, NA_KROWS * GRID_W),
                               lambda h, j: (h, jnp.maximum(j - 1, 0), 0, 0))],
        out_specs=pl.BlockSpec((nb, TM, NA_HD), lambda h, j: (0, j, h)),
        out_shape=jax.ShapeDtypeStruct((nb, l, hq * NA_HD), BF16),
        compiler_params=_cparams("arbitrary", "arbitrary"),
        name="na_attention",
    )(qkv, qkv, qkv, bias_tab)


def _sw_kernel(sink_ref, q_ref, kv_ref, o_ref, *, seq):
    j = pl.program_id(1)
    nt = (((1,), (1,)), ((), ()))
    nkv = SW_KV * SW_HD
    kwin = TM + 2 * SW_WINDOW
    low = lax.broadcasted_iota(jnp.int32, (1, LANES), 1) < SW_HD

    def attend(segs):
        for pair in range(SW_KV // 2):
            kc = slice(pair * LANES, (pair + 1) * LANES)
            vc = slice(nkv + pair * LANES, nkv + (pair + 1) * LANES)
            ks = [kv_ref[rows, kc] for rows, _ in segs]
            vs = [kv_ref[rows, vc] for rows, _ in segs]
            one = jnp.ones((), BF16)
            v_half = [[jnp.where(low, v, one) for v in vs], [jnp.where(low, one, v) for v in vs]]
            for i in range(SW_HEADS // SW_KV):
                cols = slice((4 * pair + i) * LANES, (4 * pair + i + 1) * LANES)
                qs = q_ref[:, cols]
                outs = []
                for half in range(2):
                    sink = sink_ref[8 * pair + 4 * half + i]
                    qm = jnp.where(low if half == 0 else jnp.logical_not(low), qs, jnp.zeros((), BF16))
                    sc = []
                    for k, (_, mask) in zip(ks, segs):
                        s = lax.dot_general(qm, k, nt, preferred_element_type=F32)
                        sc.append(s if mask is None else jnp.where(mask, s, NEG_INF))
                    m = sink
                    for s in sc:
                        m = jnp.maximum(m, jnp.max(s, axis=-1, keepdims=True))
                    acc = None
                    for s, v in zip(sc, v_half[half]):
                        pv = jnp.dot(jnp.exp(s - m).astype(BF16), v, preferred_element_type=F32)
                        acc = pv if acc is None else acc + pv
                    denom = pltpu.roll(acc, SW_HD, 1) + jnp.exp(sink - m)
                    outs.append(acc / denom)
                o_ref[:, cols] = jnp.where(low, outs[0], outs[1]).astype(BF16)

    @pl.when(j == 0)
    def _ctx():
        attend([(slice(0, CTX_LEN), None)])

    @pl.when(j > 0)
    def _lat():
        q0 = (j - 1) * TM
        ws = jnp.clip(q0 - SW_WINDOW, 0, seq - kwin)
        start = pl.multiple_of(CTX_LEN + ws, SW_WINDOW)
        qpos = q0 + lax.broadcasted_iota(jnp.int32, (TM, kwin), 0)
        kpos = ws + lax.broadcasted_iota(jnp.int32, (TM, kwin), 1)
        ok = jnp.abs(qpos - kpos) <= SW_WINDOW
        attend([(pl.ds(start, kwin), ok), (slice(0, CTX_LEN), None)])


def _sw_attention(q, kv, sink):
    nb, l, nq = q.shape
    nt = l // TM
    return pl.pallas_call(
        functools.partial(_sw_kernel, seq=l - CTX_LEN),
        grid=(nb, nt),
        in_specs=[pl.BlockSpec(memory_space=pltpu.SMEM),
                  pl.BlockSpec((None, TM, nq), lambda b, j: (b, j, 0)),
                  pl.BlockSpec((None, l, kv.shape[2]), lambda b, j: (b, 0, 0))],
        out_specs=pl.BlockSpec((None, TM, nq), lambda b, j: (b, j, 0)),
        out_shape=jax.ShapeDtypeStruct((nb, l, nq), BF16),
        compiler_params=_cparams("arbitrary", "arbitrary"),
        name="sw_attention",
    )(sink, q, kv)


def _chunk_index(step, rev, n_ctx, n_all):
    if not rev:
        return step
    return jnp.where(step < n_ctx, n_ctx - 1 - step, n_all + n_ctx - 1 - step)


def _tri(n, rev):
    r = lax.broadcasted_iota(jnp.int32, (n, n), 0)
    c = lax.broadcasted_iota(jnp.int32, (n, n), 1)
    return (c >= r) if rev else (c <= r)


def _lane_cummax(x, rev):
    lane = lax.broadcasted_iota(jnp.int32, (x.shape[0], LANES), 1)
    slabs = [x[:, i:i + LANES] for i in range(0, x.shape[1], LANES)]
    if rev:
        slabs = slabs[::-1]
    carry, out = None, []
    for h in slabs:
        s = 1
        while s < LANES:
            if rev:
                h = jnp.maximum(h, jnp.where(lane + s < LANES, pltpu.roll(h, LANES - s, 1), -jnp.inf))
            else:
                h = jnp.maximum(h, jnp.where(lane >= s, pltpu.roll(h, s, 1), -jnp.inf))
            s *= 2
        if carry is not None:
            h = jnp.maximum(h, carry)
        carry = jnp.max(h, axis=1, keepdims=True)
        out.append(h)
    return jnp.concatenate(out[::-1] if rev else out, axis=1)


def _ml_kernel(q_ref, k_ref, v_ref, og_ref, gr_ref, ng_ref, y_ref, acc_ref, *, n_ctx, n_all):
    lc = ML_SCAN
    hps = q_ref.shape[1] // ML_DK
    t0 = (((0,), (0,)), ((), ()))
    nt = (((1,), (1,)), ((), ()))
    eye = (_tri(lc, False) & _tri(lc, True)).astype(F32)

    ones_v = jnp.ones((lc, ML_DV), BF16)

    def gate_forms(step, rev):
        c = _chunk_index(step, rev, n_ctx, n_all)
        g8 = gr_ref[c]
        cum = jnp.dot(g8, _tri(lc, not rev).astype(F32), precision=HIGHEST, preferred_element_type=F32)
        d = 2 if rev else 0
        x = jnp.concatenate([g8[hl * 4 + d:hl * 4 + d + 1, :] - cum[hl * 4 + d + 1:hl * 4 + d + 2, :]
                             for hl in range(hps)] + [jnp.zeros((SUBLANES - hps, lc), F32)], axis=0)
        pm = _lane_cummax(x, rev)
        xmax = jnp.max(x, axis=1, keepdims=True)
        w = jnp.exp(x - xmax)
        rows = jnp.concatenate([g8, cum, x, pm, w], axis=0)
        cols = lax.dot_general(eye, rows, nt, precision=HIGHEST, preferred_element_type=F32)
        return c, rows, cols, xmax

    def chain(forms, hl, rev, state):
        ct, m = state
        c, rows, cols, xmax = forms
        r0 = pl.multiple_of(c * lc, lc)
        base = hl * 4 + (2 if rev else 0)
        li_r = rows[base:base + 1, :]
        bf_r = rows[SUBLANES + base + 1:SUBLANES + base + 2, :]
        bf_c = cols[:, SUBLANES + base + 1:SUBLANES + base + 2]
        pm_c = cols[:, 3 * SUBLANES + hl:3 * SUBLANES + hl + 1]
        w_c = cols[:, 4 * SUBLANES + hl:4 * SUBLANES + hl + 1]
        g = jnp.sum(rows[base + 1:base + 2, :], axis=1, keepdims=True)
        q = q_ref[pl.ds(r0, lc), hl * ML_DK:(hl + 1) * ML_DK]
        k = k_ref[pl.ds(r0, lc), hl * ML_DK:(hl + 1) * ML_DK]
        v1 = jnp.concatenate([v_ref[pl.ds(r0, lc), hl * ML_DV:(hl + 1) * ML_DV], ones_v], axis=1)
        mx = jnp.maximum(m, pm_c)
        e = jnp.exp(jnp.where(_tri(lc, rev), (li_r - bf_r) - mx, NEG_INF))
        sc = lax.dot_general(q, k, nt, preferred_element_type=F32) * e
        e_int = jnp.exp(m - mx)
        intra = jnp.dot(sc.astype(BF16), v1, preferred_element_type=F32)
        inter = jnp.dot(q, ct.astype(BF16), preferred_element_type=F32)
        num = intra[:, :ML_DV] + e_int * inter[:, :ML_DV]
        den = intra[:, ML_DV:ML_DV + 1] + e_int * inter[:, ML_DV:ML_DV + 1]
        hout = num / jnp.maximum(jnp.abs(den), jnp.exp(-mx - bf_c))
        acc_ref[pl.ds(r0, lc), hl * ML_DV:(hl + 1) * ML_DV] += hout
        m_loc = g + xmax[hl:hl + 1, :]
        c_loc = lax.dot_general((k.astype(F32) * w_c).astype(BF16), v1, t0, preferred_element_type=F32)
        m_new = jnp.maximum(g + m, m_loc)
        dec = jnp.exp(g + m - m_new)
        inc = jnp.exp(m_loc - m_new)
        return dec * ct + inc * c_loc, m_new

    acc_ref[...] = jnp.zeros_like(acc_ref)
    chains = [(hl, rev) for hl in range(hps) for rev in (False, True)]
    init = tuple((jnp.zeros((ML_DK, 2 * ML_DV), F32), jnp.zeros((1, 1), F32)) for _ in chains)

    def step_fn(step, carry):
        states, forms = carry
        nxt = jnp.minimum(step + 1, n_all - 1)
        new_forms = tuple(gate_forms(nxt, rev)[1:] for rev in (False, True))
        full = {rev: (_chunk_index(step, rev, n_ctx, n_all),) + forms[int(rev)] for rev in (False, True)}
        return tuple(chain(full[rev], hl, rev, st) for (hl, rev), st in zip(chains, states)), new_forms

    first = tuple(gate_forms(0, rev)[1:] for rev in (False, True))
    lax.fori_loop(0, n_all, step_fn, (init, first))

    for hl in range(hps):
        cols = slice(hl * ML_DV, (hl + 1) * ML_DV)
        z = acc_ref[:, cols]
        z = z * lax.rsqrt(jnp.mean(z * z, axis=-1, keepdims=True) + RMS_EPS) * ng_ref[:, cols]
        y_ref[:, cols] = (z.astype(BF16) * og_ref[:, cols])


def _ml_mixer(qkv, og, gates, norm_g):
    nb, l, _ = qkv.shape
    lc = ML_SCAN
    n_all, n_ctx = l // lc, CTX_LEN // lc
    hps = 2
    ngrp = ML_HEADS // hps
    nq = ML_HEADS * ML_DK
    g = gates[:, :, :4 * ML_HEADS].reshape(nb, n_all, lc, 2, 2, ngrp, hps)
    g = g.transpose(0, 5, 1, 6, 3, 4, 2).reshape(nb, ngrp, n_all, hps * 4, lc)
    vb = hps * ML_DV
    return pl.pallas_call(
        functools.partial(_ml_kernel, n_ctx=n_ctx, n_all=n_all),
        grid=(nb, ngrp),
        in_specs=[pl.BlockSpec((None, l, hps * ML_DK), lambda b, h: (b, 0, h)),
                  pl.BlockSpec((None, l, hps * ML_DK), lambda b, h: (b, 0, nq // (hps * ML_DK) + h)),
                  pl.BlockSpec((None, l, vb), lambda b, h: (b, 0, 2 * nq // vb + h)),
                  pl.BlockSpec((None, l, vb), lambda b, h: (b, 0, h)),
                  pl.BlockSpec((None, None, n_all, hps * 4, lc), lambda b, h: (b, h, 0, 0, 0)),
                  pl.BlockSpec((1, vb), lambda b, h: (0, h))],
        out_specs=pl.BlockSpec((None, l, vb), lambda b, h: (b, 0, h)),
        out_shape=jax.ShapeDtypeStruct((nb, l, ML_HEADS * ML_DV), BF16),
        scratch_shapes=[pltpu.VMEM((l, vb), F32)],
        compiler_params=_cparams("arbitrary", "arbitrary"),
        name="mlstm_mixer",
    )(qkv, qkv, qkv, og, g, norm_g.reshape(1, -1))


def _gl_kernel(q_ref, k_ref, v_ref, gate_ref, la0_ref, la1_ref, ng_ref, y_ref, acc_ref, st_ref, *, n_ctx, n_all):
    lc = GL_SCAN
    hps = q_ref.shape[1] // GL_DK
    t0 = (((0,), (0,)), ((), ()))
    nt = (((1,), (1,)), ((), ()))
    scale = GL_DK ** -0.5

    def chain(step, hl, rev):
        la_ref = la1_ref if rev else la0_ref
        sidx = 2 * hl + (1 if rev else 0)
        kc = slice(hl * GL_DK, (hl + 1) * GL_DK)
        vc = slice(hl * GL_DV, (hl + 1) * GL_DV)
        c = _chunk_index(step, rev, n_ctx, n_all)
        r0 = pl.multiple_of(c * lc, lc)
        tri = _tri(lc, rev)
        la = la_ref[pl.ds(r0, lc), kc]
        bc = jnp.dot(tri.astype(F32), la, precision=HIGHEST, preferred_element_type=F32)
        g = jnp.sum(la, axis=0, keepdims=True)
        q = q_ref[pl.ds(r0, lc), kc].astype(F32) * scale
        k = k_ref[pl.ds(r0, lc), kc].astype(F32)
        v = v_ref[pl.ds(r0, lc), vc]
        st = st_ref[sidx]
        eg = jnp.exp(g)
        k_dec = k * jnp.exp(-bc)
        q_t = (q * jnp.exp(bc)).astype(BF16)
        k_t = k_dec.astype(BF16)
        att = jnp.where(tri, lax.dot_general(q_t, k_t, nt, preferred_element_type=F32), 0.0)
        o = (jnp.dot(att.astype(BF16), v, preferred_element_type=F32)
             + lax.dot_general(q_t, st.astype(BF16), nt, preferred_element_type=F32))
        acc_ref[pl.ds(r0, lc), vc] += o
        kd = (k_dec * eg).astype(BF16)
        s_loc = lax.dot_general(v, kd, t0, preferred_element_type=F32)
        st_ref[sidx] = st * eg + s_loc

    acc_ref[...] = jnp.zeros_like(acc_ref)
    st_ref[...] = jnp.zeros_like(st_ref)

    def step_fn(step, carry):
        for hl in range(hps):
            chain(step, hl, False)
            chain(step, hl, True)
        return carry

    lax.fori_loop(0, n_all, step_fn, 0)
    for hl in range(hps):
        vc = slice(hl * GL_DV, (hl + 1) * GL_DV)
        z = acc_ref[:, vc]
        z = z * lax.rsqrt(jnp.mean(z * z, axis=-1, keepdims=True) + RMS_EPS) * ng_ref[:, vc]
        y_ref[:, vc] = z.astype(BF16) * gate_ref[:, vc]


def _gl_mixer(qkv, gate, la, norm_g):
    nb, l, _ = qkv.shape
    lc = GL_SCAN
    n_all, n_ctx = l // lc, CTX_LEN // lc
    hps = 2
    ngrp = GL_HEADS // hps
    kb, vb = hps * GL_DK, hps * GL_DV
    nk = GL_HEADS * GL_DK
    return pl.pallas_call(
        functools.partial(_gl_kernel, n_ctx=n_ctx, n_all=n_all),
        grid=(nb, ngrp),
        in_specs=[pl.BlockSpec((None, l, kb), lambda b, h: (b, 0, h)),
                  pl.BlockSpec((None, l, kb), lambda b, h: (b, 0, nk // kb + h)),
                  pl.BlockSpec((None, l, vb), lambda b, h: (b, 0, 2 * nk // vb + h)),
                  pl.BlockSpec((None, l, vb), lambda b, h: (b, 0, h)),
                  pl.BlockSpec((None, l, kb), lambda b, h: (b, 0, h)),
                  pl.BlockSpec((None, l, kb), lambda b, h: (b, 0, nk // kb + h)),
                  pl.BlockSpec((1, vb), lambda b, h: (0, h))],
        out_specs=pl.BlockSpec((None, l, vb), lambda b, h: (b, 0, h)),
        out_shape=jax.ShapeDtypeStruct((nb, l, GL_HEADS * GL_DV), BF16),
        scratch_shapes=[pltpu.VMEM((l, vb), F32), pltpu.VMEM((2 * hps, GL_DV, GL_DK), F32)],
        compiler_params=_cparams("arbitrary", "arbitrary"),
        name="gla_mixer",
    )(qkv, qkv, qkv, gate, la, la, norm_g.reshape(1, -1))


def _store_token_tiles(ref, val):
    rows = val.shape[0]
    for s in range(val.shape[1] // LANES):
        ref[pl.ds(s, rows, stride=SUBLANES), :] = val[:, s * LANES:(s + 1) * LANES]


def _load_token_tiles(ref, rows):
    return jnp.concatenate([ref[pl.ds(s, rows, stride=SUBLANES), :] for s in range(SUBLANES)], axis=1)


def _post_kernel(y_ref, x_ref, mod_ref, wo_ref, g_ref, wr_ref, br_ref, xo_ref, f_ref, rt_ref, cnt_ref, run_ref):
    first = (pl.program_id(0) == 0) & (pl.program_id(1) == 0)

    @pl.when(first)
    def _():
        run_ref[...] = jnp.zeros_like(run_ref)

    o = jnp.dot(y_ref[...], wo_ref[...], preferred_element_type=F32)
    xn = x_ref[...] + mod_ref[2:3, :] * o
    xo_ref[...] = xn
    f = _norm_mod(xn, g_ref[...], mod_ref, 3)
    _store_token_tiles(f_ref, f)
    f_hi = f.astype(BF16)
    f_lo = (f - f_hi.astype(F32)).astype(BF16)
    hh = jnp.dot(f_hi, wr_ref[...], preferred_element_type=F32)
    lh = jnp.dot(f_lo, wr_ref[:, :LANES], preferred_element_type=F32)
    lg = hh[:, :LANES] + (hh[:, LANES:] + lh) + br_ref[...]
    lane = lax.broadcasted_iota(jnp.int32, lg.shape, 1).astype(F32)
    big = float(LANES)

    def masked_max(mask):
        return jnp.max(jnp.where(mask, lg, -jnp.inf), axis=-1, keepdims=True)

    def first_at(mask, val):
        return jnp.min(jnp.where(mask & (lg == val), lane, big), axis=-1, keepdims=True)

    gm = lane < MOE_GROUPS
    mg = masked_max(gm)
    g_idx = first_at(gm, mg)
    g_w = 1.0 / jnp.sum(jnp.where(gm, jnp.exp(lg - mg), 0.0), axis=-1, keepdims=True)
    lo = MOE_GROUPS + MOE_PER_GROUP * g_idx
    em = (lane >= lo) & (lane < lo + MOE_PER_GROUP)
    v0 = masked_max(em)
    i0 = first_at(em, v0)
    em1 = em & (lane != i0)
    v1 = masked_max(em1)
    i1 = first_at(em1, v1)
    e0, e1 = i0 - MOE_GROUPS, i1 - MOE_GROUPS
    t = jnp.exp(v1 - v0)
    w0 = g_w / (1.0 + t)
    w1 = g_w * t / (1.0 + t)
    oh = ((lane == e0) | (lane == e1)).astype(F32)
    rows = lg.shape[0]
    below = lax.broadcasted_iota(jnp.int32, (rows, rows), 0) > lax.broadcasted_iota(jnp.int32, (rows, rows), 1)
    tot = run_ref[...] + jnp.dot(below.astype(BF16), oh.astype(BF16), preferred_element_type=F32)
    r0 = jnp.sum(jnp.where(lane == e0, tot, 0.0), axis=-1, keepdims=True)
    r1 = jnp.sum(jnp.where(lane == e1, tot, 0.0), axis=-1, keepdims=True)
    run_ref[...] = run_ref[...] + jnp.sum(oh, axis=0, keepdims=True)
    cnt_ref[...] = jnp.broadcast_to(run_ref[...], cnt_ref.shape)
    out = jnp.zeros_like(lg)
    for i, val in enumerate((e0, e1, w0, w1, r0, r1)):
        out = jnp.where(lane == i, val, out)
    rt_ref[...] = out


def _post_call(y, x, mods, w_o, g_ffn, w_r, b_r):
    nb, l, d = x.shape
    nt = l // TM
    tile = lambda b, t: (b * nt + t, 0)
    return pl.pallas_call(
        _post_kernel,
        grid=(nb, nt),
        in_specs=[pl.BlockSpec((None, TM, d), lambda b, t: (b, t, 0)),
                  pl.BlockSpec((None, TM, d), lambda b, t: (b, t, 0)),
                  _mod_spec(nb),
                  pl.BlockSpec((d, d), lambda b, t: (0, 0)),
                  pl.BlockSpec((1, d), lambda b, t: (0, 0)),
                  pl.BlockSpec((d, 2 * LANES), lambda b, t: (0, 0)),
                  pl.BlockSpec((1, LANES), lambda b, t: (0, 0))],
        out_specs=[pl.BlockSpec((None, TM, d), lambda b, t: (b, t, 0)),
                   pl.BlockSpec((TM * SUBLANES, LANES), tile),
                   pl.BlockSpec((TM, LANES), tile),
                   pl.BlockSpec((SUBLANES, LANES), lambda b, t: (0, 0))],
        out_shape=[jax.ShapeDtypeStruct((nb, l, d), F32),
                   jax.ShapeDtypeStruct((nb * l * SUBLANES, LANES), F32),
                   jax.ShapeDtypeStruct((nb * l, LANES), F32),
                   jax.ShapeDtypeStruct((SUBLANES, LANES), F32)],
        scratch_shapes=[pltpu.VMEM((1, LANES), F32)],
        compiler_params=_cparams("arbitrary", "arbitrary"),
        name="outproj_router",
    )(y, x, mods, w_o, g_ffn.reshape(1, d), w_r, b_r)


def _dispatch_kernel(dest_ref, pend_ref, f_ref, xs_out, zbuf, sem):
    tile = f_ref.shape[0] // SUBLANES
    base = pl.program_id(0) * tile

    @pl.when(pl.program_id(0) == 0)
    def _():
        zbuf[...] = jnp.zeros_like(zbuf)

        def zero_block(blk):
            start = pl.multiple_of(blk * (MOE_BLOCK * SUBLANES), MOE_BLOCK * SUBLANES)
            return pltpu.make_async_copy(zbuf, xs_out.at[pl.ds(start, MOE_BLOCK * SUBLANES), :], sem)

        def has_rows(e):
            return pend_ref[e] > (0 if e == 0 else pend_ref[e - 1])

        for e in range(MOE_EXPERTS):
            pl.when(has_rows(e))(lambda e=e: zero_block(pend_ref[e] // MOE_BLOCK - 1).start())
        for e in range(MOE_EXPERTS):
            pl.when(has_rows(e))(lambda e=e: zero_block(pend_ref[e] // MOE_BLOCK - 1).wait())
        n_act = pend_ref[MOE_EXPERTS - 1] // MOE_BLOCK
        n_blk = xs_out.shape[0] // (MOE_BLOCK * SUBLANES)
        lax.fori_loop(n_act, n_blk, lambda b, c: (zero_block(b).start(), c)[1], 0)
        lax.fori_loop(n_act, n_blk, lambda b, c: (zero_block(b).wait(), c)[1], 0)

    def copy(t, k):
        d = dest_ref[2 * (base + t) + k]
        return pltpu.make_async_copy(f_ref.at[pl.ds(t * SUBLANES, SUBLANES), :],
                                     xs_out.at[pl.ds(d * SUBLANES, SUBLANES), :], sem)

    def issue(t, c):
        copy(t, 0).start()
        copy(t, 1).start(priority=1)
        return c

    lax.fori_loop(0, tile, issue, 0, unroll=DMA_UNROLL)
    for _ in range(2):
        pltpu.make_async_copy(f_ref, xs_out.at[pl.ds(0, tile * SUBLANES), :], sem).wait()


def _dispatch(dest, p_end, f_tiles, n_slots):
    n_tok = f_tiles.shape[0] // SUBLANES
    tile = DISPATCH_TILE
    assert n_tok % tile == 0
    return pl.pallas_call(
        _dispatch_kernel,
        grid_spec=pltpu.PrefetchScalarGridSpec(
            num_scalar_prefetch=2,
            grid=(n_tok // tile,),
            in_specs=[pl.BlockSpec((tile * SUBLANES, LANES), lambda i, dr, pe: (i, 0))],
            out_specs=pl.BlockSpec(memory_space=pl.ANY),
            scratch_shapes=[pltpu.VMEM((MOE_BLOCK * SUBLANES, LANES), F32), pltpu.SemaphoreType.DMA(())]),
        out_shape=jax.ShapeDtypeStruct((n_slots * SUBLANES, LANES), F32),
        compiler_params=_cparams("arbitrary"),
        name="moe_dispatch",
    )(dest, p_end, f_tiles)


def _expert_kernel(blk_e_ref, nact_ref, xs_ref, wg_ref, wu_ref, wd_ref, ys_ref, wg_s, wu_s, wd_s):
    i = pl.program_id(0)
    active = i < nact_ref[0]
    changed = (i == 0) | (blk_e_ref[i] != blk_e_ref[jnp.maximum(i - 1, 0)])

    @pl.when(active & changed)
    def _():
        wg_s[...] = wg_ref[...].astype(BF16)
        wu_s[...] = wu_ref[...].astype(BF16)
        wd_s[...] = wd_ref[...].astype(BF16)

    @pl.when(active)
    def _():
        x = _load_token_tiles(xs_ref, MOE_BLOCK).astype(BF16)
        a = jnp.dot(x, wg_s[...], preferred_element_type=F32)
        u = jnp.dot(x, wu_s[...], preferred_element_type=F32)
        y = jnp.dot((_silu(a) * u).astype(BF16), wd_s[...], preferred_element_type=F32)
        _store_token_tiles(ys_ref, y)

    @pl.when(jnp.logical_not(active))
    def _():
        ys_ref[...] = jnp.zeros_like(ys_ref)


def _experts(blk_e, nact, xs, w_gate, w_up, w_down, layer):
    n_blk = blk_e.shape[0]
    _, _, d, ff = w_gate.shape
    rows = MOE_BLOCK * SUBLANES
    last = lambda i, na: jnp.minimum(i, jnp.maximum(na[0] - 1, 0))
    blk = lambda i, be, na: (last(i, na), 0)
    wsel = lambda i, be, na: (layer, be[last(i, na)], 0, 0)
    return pl.pallas_call(
        _expert_kernel,
        grid_spec=pltpu.PrefetchScalarGridSpec(
            num_scalar_prefetch=2,
            grid=(n_blk,),
            in_specs=[pl.BlockSpec((rows, LANES), blk),
                      pl.BlockSpec((None, None, d, ff), wsel),
                      pl.BlockSpec((None, None, d, ff), wsel),
                      pl.BlockSpec((None, None, ff, d), wsel)],
            out_specs=pl.BlockSpec((rows, LANES), lambda i, be, na: (i, 0)),
            scratch_shapes=[pltpu.VMEM((d, ff), BF16), pltpu.VMEM((d, ff), BF16), pltpu.VMEM((ff, d), BF16)]),
        out_shape=jax.ShapeDtypeStruct(xs.shape, F32),
        compiler_params=_cparams("arbitrary"),
        name="moe_experts",
    )(blk_e, nact, xs, w_gate, w_up, w_down)


def _combine_kernel(dest_ref, ys_hbm, x_ref, rt_ref, mod_ref, o_ref, ybuf, sem, *, nt, t0):
    i = pl.program_id(0)
    n = pl.num_programs(0)
    nte = nt - t0

    def gather(step, slot, start):
        base = ((step // nte) * nt + step % nte + t0) * TM

        def copy(t, k):
            d = dest_ref[2 * (base + t) + k]
            return pltpu.make_async_copy(ys_hbm.at[pl.ds(d * SUBLANES, SUBLANES), :],
                                         ybuf.at[slot, k, pl.ds(t * SUBLANES, SUBLANES), :], sem.at[slot])

        def issue(t, c):
            copy(t, 0).start()
            copy(t, 1).start(priority=1)
            return c

        if start:
            lax.fori_loop(0, TM, issue, 0, unroll=DMA_UNROLL)
        else:
            for k in range(2):
                pltpu.make_async_copy(ys_hbm.at[pl.ds(0, TM * SUBLANES), :], ybuf.at[slot, k], sem.at[slot]).wait()

    slot = i % 2
    pl.when(i == 0)(lambda: gather(i, slot, True))
    pl.when(i + 1 < n)(lambda: gather(i + 1, 1 - slot, True))
    gather(i, slot, False)
    y0 = _load_token_tiles(ybuf.at[slot, 0], TM)
    y1 = _load_token_tiles(ybuf.at[slot, 1], TM)
    rt = rt_ref[...]
    y = rt[:, 2:3] * y0 + rt[:, 3:4] * y1
    o_ref[...] = x_ref[...] + mod_ref[5:6, :] * y


def _combine(dest, ys, x, rt, mods, skip_ctx):
    nb, l, d = x.shape
    nt = l // TM
    t0 = 1 if skip_ctx else 0
    nte = nt - t0
    bt = lambda i: (i // nte, i % nte + t0)
    mod_row = (lambda b, t: b) if skip_ctx else (lambda b, t: jnp.where(t == 0, nb, b))
    return pl.pallas_call(
        functools.partial(_combine_kernel, nt=nt, t0=t0),
        grid_spec=pltpu.PrefetchScalarGridSpec(
            num_scalar_prefetch=1,
            grid=(nb * nte,),
            in_specs=[pl.BlockSpec(memory_space=pl.ANY),
                      pl.BlockSpec((None, TM, d), lambda i, dr: (*bt(i), 0)),
                      pl.BlockSpec((TM, LANES), lambda i, dr: (bt(i)[0] * nt + bt(i)[1], 0)),
                      pl.BlockSpec((None, 6, d), lambda i, dr: (mod_row(*bt(i)), 0, 0))],
            out_specs=pl.BlockSpec((None, TM, d), lambda i, dr: (i // nte, i % nte, 0)),
            scratch_shapes=[pltpu.VMEM((2, 2, TM * SUBLANES, LANES), F32), pltpu.SemaphoreType.DMA((2,))]),
        out_shape=jax.ShapeDtypeStruct((nb, nte * TM, d), F32),
        compiler_params=_cparams("arbitrary"),
        name="moe_combine",
    )(dest, ys, x, rt, mods)


def _moe(y_mix, x, mods, w_o, g_ffn, w_grp, b_grp, w_exp, b_exp, w_gate, w_up, w_down, layer, skip_ctx):
    nb, l, d = x.shape
    n_tok = nb * l
    nr = MOE_GROUPS + MOE_EXPERTS
    w_r = jnp.pad(jnp.concatenate([w_grp, w_exp], axis=1), ((0, 0), (0, LANES - nr)))
    b_r = jnp.pad(jnp.concatenate([b_grp, b_exp]), (0, LANES - nr)).reshape(1, LANES)
    w_hi = w_r.astype(BF16)
    w_r = jnp.concatenate([w_hi, (w_r - w_hi.astype(F32)).astype(BF16)], axis=1)
    x_new, f_tiles, rt, cnt = _post_call(y_mix, x, mods, w_o, g_ffn, w_r, b_r)
    counts = cnt[0, :MOE_EXPERTS].astype(jnp.int32)
    padded = (counts + MOE_BLOCK - 1) // MOE_BLOCK * MOE_BLOCK
    p_end = jnp.cumsum(padded)
    p_start = p_end - padded
    n_pairs = 2 * n_tok
    n_blk = -(-(n_pairs + MOE_EXPERTS * (MOE_BLOCK - 1)) // MOE_BLOCK)
    eid = rt[:, 0:2].astype(jnp.int32)
    first = jnp.sum(jnp.where(eid[:, :, None] == jnp.arange(MOE_EXPERTS, dtype=jnp.int32), p_start, 0), axis=-1)
    dest = (first + rt[:, 4:6].astype(jnp.int32)).reshape(-1)
    blk_row = jnp.arange(n_blk, dtype=jnp.int32) * MOE_BLOCK
    blk_e = jnp.minimum(jnp.sum((p_end[None, :] <= blk_row[:, None]).astype(jnp.int32), axis=1), MOE_EXPERTS - 1)
    nact = (p_end[-1:] // MOE_BLOCK).astype(jnp.int32)
    xs = _dispatch(dest, p_end.astype(jnp.int32), f_tiles, n_blk * MOE_BLOCK)
    ys = _experts(blk_e, nact, xs, w_gate, w_up, w_down, layer)
    return _combine(dest, ys, x_new, rt, mods, skip_ctx)


def _rope_tables(seq):
    nf = SW_HD // 4
    inv = ROPE_BASE ** (-jnp.arange(nf, dtype=F32) / nf)
    pos = jnp.arange(seq, dtype=jnp.int32)
    rows, cols = (pos // GRID_W).astype(F32), (pos % GRID_W).astype(F32)
    lane = jnp.arange(LANES)
    p = jnp.where((lane % SW_HD < SW_HD // 2)[None, :], rows[:, None], cols[:, None])
    ang = p * inv[lane % nf][None, :]
    sign = jnp.where((lane % (2 * nf)) < nf, -1.0, 1.0)[None, :]
    cs = jnp.concatenate([jnp.ones((CTX_LEN, LANES), F32), jnp.cos(ang)], axis=0)
    sn = jnp.concatenate([jnp.zeros((CTX_LEN, LANES), F32), jnp.sin(ang) * sign], axis=0)
    return cs, sn


def _pad_cols(w, n):
    return jnp.pad(w, ((0, 0), (0, n - w.shape[1])))


def kernel(x, c, ctx, c_ctx, ada_w, ada_b, norm_mix_g, norm_ffn_g, na_w_qkv, na_qk_g, na_rpb, na_w_o, ml_w_in, ml_b_gates, ml_norm_g, ml_w_o, sw_w_qkv, sw_qk_g, sw_sink, sw_w_o, gl_w_in, gl_w_a2, gl_b_a, gl_norm_g, gl_w_o, moe_w_grp, moe_b_grp, moe_w_exp, moe_b_exp, moe_w_gate, moe_w_up, moe_w_down):
    nb, seq, d = x.shape
    depth = ada_w.shape[0]
    assert d == D_MODEL and ctx.shape[1] == CTX_LEN == TM and seq % TM == 0
    rows = -(-(nb + 1) // SUBLANES) * SUBLANES
    cond = jnp.pad(jnp.concatenate([c, c_ctx[None]], axis=0), ((0, rows - nb - 1), (0, 0)))
    mods_all = _ada_mods(cond, ada_w, ada_b)
    xa = jnp.concatenate([ctx, x], axis=1)
    for i in range(depth):
        j, kind = divmod(i, 4)
        mods = mods_all[i]
        g_mix = norm_mix_g[i]
        if kind == 0:
            qkg = na_qk_g[j]
            (qkv,) = _proj_call(_proj_na_kernel, xa, mods, g_mix, na_w_qkv[j].astype(BF16), [(qkg, False)],
                                [3 * NA_HEADS * NA_HD], [BF16], "proj_na")
            y = _na_attention(qkv, _na_bias_table(na_rpb[j], seq // GRID_W))
            w_o = na_w_o[j]
        elif kind == 1:
            n_in = -(-ml_w_in.shape[2] // LANES) * LANES
            bg = jnp.pad(ml_b_gates[j], (0, LANES - ml_b_gates.shape[1])).reshape(1, LANES)
            qkv, og, gates = _proj_call(_proj_ml_kernel, xa, mods, g_mix, _pad_cols(ml_w_in[j], n_in).astype(BF16),
                                        [(bg, False)], [2 * ML_HEADS * ML_DK + ML_HEADS * ML_DV, ML_HEADS * ML_DV, LANES],
                                        [BF16, BF16, F32], "proj_ml")
            y = _ml_mixer(qkv, og, gates, ml_norm_g[j])
            w_o = ml_w_o[j]
        elif kind == 2:
            qkg = jnp.concatenate([sw_qk_g[j], sw_qk_g[j]], axis=1)
            cs, sn = _rope_tables(seq)
            nqc = SW_HEADS * SW_HD
            w_in = sw_w_qkv[j]
            w_q = w_in[:, :nqc].reshape(d, 2, 2, 4, SW_HD).transpose(0, 1, 3, 2, 4).reshape(d, nqc)
            w_in = jnp.concatenate([w_q, w_in[:, nqc:]], axis=1)
            q, kv = _proj_call(_proj_sw_kernel, xa, mods, g_mix, w_in.astype(BF16),
                               [(qkg, False), (cs, True), (sn, True)],
                               [nqc, 2 * SW_KV * SW_HD], [BF16, BF16], "proj_sw")
            y = _sw_attention(q, kv, sw_sink[j])
            w_o = sw_w_o[j].reshape(2, 2, 4, SW_HD, d).transpose(0, 2, 1, 3, 4).reshape(nqc, d)
        else:
            n_in = -(-gl_w_in.shape[2] // LANES) * LANES
            nk = GL_HEADS * GL_DK
            wa = jnp.zeros((LANES, 2 * nk), F32)
            wa = wa.at[:GL_RANK, :nk].set(gl_w_a2[j, 0]).at[GL_RANK:2 * GL_RANK, nk:].set(gl_w_a2[j, 1])
            ba = gl_b_a[j].reshape(1, 2 * nk)
            qkv, gate, la = _proj_call(_proj_gl_kernel, xa, mods, g_mix, _pad_cols(gl_w_in[j], n_in).astype(BF16),
                                       [(wa.astype(BF16), False), (ba, False)],
                                       [2 * nk + GL_HEADS * GL_DV, GL_HEADS * GL_DV, 2 * nk], [BF16, BF16, F32], "proj_gl")
            y = _gl_mixer(qkv, gate, la, gl_norm_g[j])
            w_o = gl_w_o[j]
        xa = _moe(y, xa, mods, w_o.astype(BF16), norm_ffn_g[i], moe_w_grp[i], moe_b_grp[i], moe_w_exp[i], moe_b_exp[i],
                  moe_w_gate, moe_w_up, moe_w_down, i, i == depth - 1)
    return xa
```

```python
import functools

import jax
import jax.numpy as jnp
import numpy as np
from jax import lax
from jax.experimental import pallas as pl
from jax.experimental.pallas import tpu as pltpu

F32 = jnp.float32
BF16 = jnp.bfloat16
HIGHEST = lax.Precision.HIGHEST

D_MODEL = 1024
CTX_LEN = 256
GRID_W = 64
RMS_EPS = 1e-6
NEG_INF = -1e30
ROPE_BASE = 10000.0

NA_HEADS, NA_HD, NA_WIN_R, NA_WIN_C = 8, 128, 8, 16
NA_QROWS = 4
NA_KROWS = NA_QROWS + NA_WIN_R - 1
ML_HEADS, ML_DK, ML_DV, ML_CAP = 8, 64, 128, 15.0
ML_SCAN = 256
SW_HEADS, SW_KV, SW_HD, SW_WINDOW = 16, 4, 64, 128
GL_HEADS, GL_DK, GL_DV, GL_RANK, GL_TAU = 4, 128, 256, 16, 16.0
GL_SCAN = 128
MOE_GROUPS, MOE_PER_GROUP, MOE_EXPERTS, MOE_FF = 4, 8, 32, 512
MOE_BLOCK = 512

LANES = 128
SUBLANES = 8
TM = 256
VMEM_LIMIT = 48 * 1024 * 1024
DMA_UNROLL = 8
DISPATCH_TILE = 3 * TM


def _cparams(*sem):
    return pltpu.CompilerParams(dimension_semantics=sem, vmem_limit_bytes=VMEM_LIMIT)


def _norm_mod(x, g, mod_ref, k):
    y = x * lax.rsqrt(jnp.mean(x * x, axis=-1, keepdims=True) + RMS_EPS) * g
    return y * (1.0 + mod_ref[k + 1:k + 2, :]) + mod_ref[k:k + 1, :]


def _log_sigmoid(x):
    return jnp.minimum(x, 0.0) - jnp.log(1.0 + jnp.exp(-jnp.abs(x)))


def _silu(x):
    return x * jax.nn.sigmoid(x)


def _mod_kernel(c_ref, w_ref, b_ref, o_ref):
    s = _silu(c_ref[...])
    o_ref[...] = jnp.dot(s.astype(BF16), w_ref[...].astype(BF16), preferred_element_type=F32) + b_ref[...]


def _ada_mods(cond, ada_w, ada_b):
    depth, d, _ = ada_w.shape
    rows = cond.shape[0]
    out = pl.pallas_call(
        _mod_kernel,
        grid=(depth, 6),
        in_specs=[pl.BlockSpec((rows, d), lambda i, n: (0, 0)),
                  pl.BlockSpec((None, d, d), lambda i, n: (i, 0, n)),
                  pl.BlockSpec((None, 1, d), lambda i, n: (i, 0, n))],
        out_specs=pl.BlockSpec((None, rows, d), lambda i, n: (i, 0, n)),
        out_shape=jax.ShapeDtypeStruct((depth, rows, 6 * d), F32),
        compiler_params=_cparams("arbitrary", "arbitrary"),
        name="ada_mods",
    )(cond, ada_w, ada_b.reshape(depth, 1, 6 * d))
    return out.reshape(depth, rows, 6, d)


def _mod_spec(nb):
    return pl.BlockSpec((None, 6, D_MODEL), lambda b, t: (jnp.where(t == 0, nb, b), 0, 0))


def _proj_call(kernel, x, mods, g, w, extras, out_cols, out_dtypes, name):
    nb, l, d = x.shape
    nt = l // TM
    n = w.shape[1]
    in_specs = [pl.BlockSpec((None, TM, d), lambda b, t: (b, t, 0)),
                _mod_spec(nb),
                pl.BlockSpec((1, d), lambda b, t: (0, 0)),
                pl.BlockSpec((d, n), lambda b, t: (0, 0))]
    args = [x, mods, g.reshape(1, d), w]
    for e, per_tile in extras:
        if per_tile:
            in_specs.append(pl.BlockSpec((TM, e.shape[1]), lambda b, t: (t, 0)))
        else:
            in_specs.append(pl.BlockSpec(e.shape, lambda b, t: (0, 0)))
        args.append(e)
    return pl.pallas_call(
        kernel,
        grid=(nb, nt),
        in_specs=in_specs,
        out_specs=[pl.BlockSpec((None, TM, c), lambda b, t: (b, t, 0)) for c in out_cols],
        out_shape=[jax.ShapeDtypeStruct((nb, l, c), dt) for c, dt in zip(out_cols, out_dtypes)],
        compiler_params=_cparams("arbitrary", "arbitrary"),
        name=name,
    )(*args)


def _proj_na_kernel(x_ref, mod_ref, g_ref, w_ref, qkg_ref, o_ref):
    h = _norm_mod(x_ref[...], g_ref[...], mod_ref, 0)
    u = jnp.dot(h.astype(BF16), w_ref[...], preferred_element_type=F32)
    nq = NA_HEADS * NA_HD
    for part in range(2):
        gain = qkg_ref[part:part + 1, :]
        for hh in range(NA_HEADS):
            lo = part * nq + hh * NA_HD
            z = u[:, lo:lo + NA_HD]
            z = z * lax.rsqrt(jnp.mean(z * z, axis=-1, keepdims=True) + RMS_EPS) * gain
            o_ref[:, lo:lo + NA_HD] = z.astype(BF16)
    o_ref[:, 2 * nq:] = u[:, 2 * nq:].astype(BF16)


def _head64_rms(z, gain):
    lane = lax.broadcasted_iota(jnp.int32, z.shape, 1)
    lo = lane < SW_HD
    zz = z * z
    s_lo = jnp.sum(jnp.where(lo, zz, 0.0), axis=-1, keepdims=True)
    s_hi = jnp.sum(jnp.where(lo, 0.0, zz), axis=-1, keepdims=True)
    ms = jnp.where(lo, s_lo, s_hi) * (1.0 / SW_HD)
    return z * lax.rsqrt(ms + RMS_EPS) * gain


def _rope_slab(z, cs, sn):
    lane = lax.broadcasted_iota(jnp.int32, z.shape, 1)
    first = (lane % 32) < 16
    partner = jnp.where(first, pltpu.roll(z, LANES - 16, 1), pltpu.roll(z, 16, 1))
    return z * cs + partner * sn


def _proj_sw_kernel(x_ref, mod_ref, g_ref, w_ref, qkg_ref, cs_ref, sn_ref, q_ref, kv_ref):
    h = _norm_mod(x_ref[...], g_ref[...], mod_ref, 0)
    u = jnp.dot(h.astype(BF16), w_ref[...], preferred_element_type=F32)
    cs, sn = cs_ref[...], sn_ref[...]
    nq, nk = SW_HEADS * SW_HD, SW_KV * SW_HD
    scale = SW_HD ** -0.5
    for s in range(nq // LANES):
        z = _head64_rms(u[:, s * LANES:(s + 1) * LANES], qkg_ref[0:1, :])
        q_ref[:, s * LANES:(s + 1) * LANES] = (_rope_slab(z, cs, sn) * scale).astype(BF16)
    for s in range(nk // LANES):
        z = _head64_rms(u[:, nq + s * LANES:nq + (s + 1) * LANES], qkg_ref[1:2, :])
        kv_ref[:, s * LANES:(s + 1) * LANES] = _rope_slab(z, cs, sn).astype(BF16)
    kv_ref[:, nk:] = u[:, nq + nk:].astype(BF16)


def _proj_ml_kernel(x_ref, mod_ref, g_ref, w_ref, bg_ref, qkv_ref, og_ref, gt_ref):
    h = _norm_mod(x_ref[...], g_ref[...], mod_ref, 0)
    u = jnp.dot(h.astype(BF16), w_ref[...], preferred_element_type=F32)
    nq, nv = ML_HEADS * ML_DK, ML_HEADS * ML_DV
    qkv_ref[:, :nq] = (u[:, :nq] * (ML_DK ** -0.5)).astype(BF16)
    qkv_ref[:, nq:] = u[:, nq:2 * nq + nv].astype(BF16)
    og_ref[...] = jax.nn.sigmoid(u[:, 2 * nq + nv:2 * nq + 2 * nv]).astype(BF16)
    pre = u[:, 2 * nq + 2 * nv:] + bg_ref[...]
    pre = ML_CAP * jnp.tanh(pre / ML_CAP)
    lane = lax.broadcasted_iota(jnp.int32, pre.shape, 1)
    gt_ref[...] = jnp.where((lane % 16) >= 8, _log_sigmoid(pre), pre)


def _proj_gl_kernel(x_ref, mod_ref, g_ref, w_ref, wa_ref, ba_ref, qkv_ref, gate_ref, la_ref):
    h = _norm_mod(x_ref[...], g_ref[...], mod_ref, 0)
    u = jnp.dot(h.astype(BF16), w_ref[...], preferred_element_type=F32)
    nq, nv = GL_HEADS * GL_DK, GL_HEADS * GL_DV
    qkv_ref[...] = u[:, :2 * nq + nv].astype(BF16)
    gate_ref[...] = _silu(u[:, 2 * nq + nv:2 * nq + 2 * nv]).astype(BF16)
    z = u[:, 2 * nq + 2 * nv:]
    a = jnp.dot(z.astype(BF16), wa_ref[...], preferred_element_type=F32) + ba_ref[...]
    la_ref[...] = _log_sigmoid(a) * (1.0 / GL_TAU)


def _na_bias_table(rpb, n_rows):
    cq = np.arange(GRID_W)[:, None]
    ck = np.arange(GRID_W)[None, :]
    c0 = np.clip(cq - NA_WIN_C // 2, 0, GRID_W - NA_WIN_C)
    col_ok = (ck >= c0) & (ck < c0 + NA_WIN_C)
    dc = np.clip(ck - cq + NA_WIN_C - 1, 0, 2 * NA_WIN_C - 2)
    pick = ((dc[None] == np.arange(2 * NA_WIN_C - 1)[:, None, None]) & col_ok[None]).astype(np.float32)
    m = jnp.einsum('hrd,dqk->hqrk', rpb.astype(F32), jnp.asarray(pick), precision=HIGHEST)
    m = jnp.where(jnp.asarray(col_ok)[None, :, None, :], m, NEG_INF)
    blocks = []
    for jj in range(n_rows // NA_QROWS):
        ws = min(max(NA_QROWS * jj - NA_WIN_R // 2, 0), n_rows - NA_KROWS)
        per_row = []
        for ri in range(NA_QROWS):
            r = NA_QROWS * jj + ri
            r0 = min(max(r - NA_WIN_R // 2, 0), n_rows - NA_WIN_R)
            lead = r0 - ws
            d0 = r0 - r + NA_WIN_R - 1
            piece = jnp.pad(m[:, :, d0:d0 + NA_WIN_R, :],
                            ((0, 0), (0, 0), (lead, NA_KROWS - NA_WIN_R - lead), (0, 0)), constant_values=NEG_INF)
            per_row.append(piece.reshape(rpb.shape[0], GRID_W, NA_KROWS * GRID_W))
        blocks.append(jnp.concatenate(per_row, axis=1))
    return jnp.stack(blocks, axis=1)


def _na_kernel(q_ref, k_ref, v_ref, bias_ref, o_ref, *, n_rows):
    j = pl.program_id(1)
    nb = q_ref.shape[0]
    scale = NA_HD ** -0.5
    nt = (((1,), (1,)), ((), ()))
    nkw = NA_KROWS * GRID_W

    @pl.when(j == 0)
    def _ctx():
        def body(b, c):
            q = q_ref[b]
            s = lax.dot_general(q, k_ref[b, 0:CTX_LEN, :], nt, preferred_element_type=F32) * scale
            p = jnp.exp(s - jnp.max(s, axis=-1, keepdims=True))
            l = jnp.sum(p, axis=-1, keepdims=True)
            o = jnp.dot(p.astype(BF16), v_ref[b, 0:CTX_LEN, :], preferred_element_type=F32)
            o_ref[b] = (o / l).astype(BF16)
            return c
        lax.fori_loop(0, nb, body, 0)

    @pl.when(j > 0)
    def _lat():
        ws = jnp.clip(NA_QROWS * (j - 1) - NA_WIN_R // 2, 0, n_rows - NA_KROWS)
        start = pl.multiple_of(CTX_LEN + ws * GRID_W, GRID_W)
        bias = bias_ref[...]

        def body(b, c):
            q = q_ref[b]
            s_n = lax.dot_general(q, k_ref[b, pl.ds(start, nkw), :], nt, preferred_element_type=F32) * scale + bias
            s_c = lax.dot_general(q, k_ref[b, 0:CTX_LEN, :], nt, preferred_element_type=F32) * scale
            m = jnp.maximum(jnp.max(s_n, axis=-1, keepdims=True), jnp.max(s_c, axis=-1, keepdims=True))
            p_n = jnp.exp(s_n - m)
            p_c = jnp.exp(s_c - m)
            l = jnp.sum(p_n, axis=-1, keepdims=True) + jnp.sum(p_c, axis=-1, keepdims=True)
            o = (jnp.dot(p_n.astype(BF16), v_ref[b, pl.ds(start, nkw), :], preferred_element_type=F32)
                 + jnp.dot(p_c.astype(BF16), v_ref[b, 0:CTX_LEN, :], preferred_element_type=F32))
            o_ref[b] = (o / l).astype(BF16)
            return c
        lax.fori_loop(0, nb, body, 0)


def _na_attention(qkv, bias_tab):
    nb, l, _ = qkv.shape
    n_rows = (l - CTX_LEN) // GRID_W
    nt = l // TM
    hq = NA_HEADS
    return pl.pallas_call(
        functools.partial(_na_kernel, n_rows=n_rows),
        grid=(hq, nt),
        in_specs=[pl.BlockSpec((nb, TM, NA_HD), lambda h, j: (0, j, h)),
                  pl.BlockSpec((nb, l, NA_HD), lambda h, j: (0, 0, hq + h)),
                  pl.BlockSpec((nb, l, NA_HD), lambda h, j: (0, 0, 2 * hq + h)),
                  pl.BlockSpec((None, None, TM, NA_KROWS * GRID_W),
                               lambda h, j: (h, jnp.maximum(j - 1, 0), 0, 0))],
        out_specs=pl.BlockSpec((nb, TM, NA_HD), lambda h, j: (0, j, h)),
        out_shape=jax.ShapeDtypeStruct((nb, l, hq * NA_HD), BF16),
        compiler_params=_cparams("arbitrary", "arbitrary"),
        name="na_attention",
    )(qkv, qkv, qkv, bias_tab)


def _sw_kernel(sink_ref, q_ref, kv_ref, o_ref, *, seq):
    j = pl.program_id(1)
    nt = (((1,), (1,)), ((), ()))
    nkv = SW_KV * SW_HD
    kwin = TM + 2 * SW_WINDOW
    low = lax.broadcasted_iota(jnp.int32, (1, LANES), 1) < SW_HD

    def attend(segs):
        for pair in range(SW_KV // 2):
            kc = slice(pair * LANES, (pair + 1) * LANES)
            vc = slice(nkv + pair * LANES, nkv + (pair + 1) * LANES)
            ks = [kv_ref[rows, kc] for rows, _ in segs]
            vs = [kv_ref[rows, vc] for rows, _ in segs]
            one = jnp.ones((), BF16)
            v_half = [[jnp.where(low, v, one) for v in vs], [jnp.where(low, one, v) for v in vs]]
            for i in range(SW_HEADS // SW_KV):
                cols = slice((4 * pair + i) * LANES, (4 * pair + i + 1) * LANES)
                qs = q_ref[:, cols]
                outs = []
                for half in range(2):
                    sink = sink_ref[8 * pair + 4 * half + i]
                    qm = jnp.where(low if half == 0 else jnp.logical_not(low), qs, jnp.zeros((), BF16))
                    sc = []
                    for k, (_, mask) in zip(ks, segs):
                        s = lax.dot_general(qm, k, nt, preferred_element_type=F32)
                        sc.append(s if mask is None else jnp.where(mask, s, NEG_INF))
                    m = sink
                    for s in sc:
                        m = jnp.maximum(m, jnp.max(s, axis=-1, keepdims=True))
                    acc = None
                    for s, v in zip(sc, v_half[half]):
                        pv = jnp.dot(jnp.exp(s - m).astype(BF16), v, preferred_element_type=F32)
                        acc = pv if acc is None else acc + pv
                    denom = pltpu.roll(acc, SW_HD, 1) + jnp.exp(sink - m)
                    outs.append(acc / denom)
                o_ref[:, cols] = jnp.where(low, outs[0], outs[1]).astype(BF16)

    @pl.when(j == 0)
    def _ctx():
        attend([(slice(0, CTX_LEN), None)])

    @pl.when(j > 0)
    def _lat():
        q0 = (j - 1) * TM
        ws = jnp.clip(q0 - SW_WINDOW, 0, seq - kwin)
        start = pl.multiple_of(CTX_LEN + ws, SW_WINDOW)
        qpos = q0 + lax.broadcasted_iota(jnp.int32, (TM, kwin), 0)
        kpos = ws + lax.broadcasted_iota(jnp.int32, (TM, kwin), 1)
        ok = jnp.abs(qpos - kpos) <= SW_WINDOW
        attend([(pl.ds(start, kwin), ok), (slice(0, CTX_LEN), None)])


def _sw_attention(q, kv, sink):
    nb, l, nq = q.shape
    nt = l // TM
    return pl.pallas_call(
        functools.partial(_sw_kernel, seq=l - CTX_LEN),
        grid=(nb, nt),
        in_specs=[pl.BlockSpec(memory_space=pltpu.SMEM),
                  pl.BlockSpec((None, TM, nq), lambda b, j: (b, j, 0)),
                  pl.BlockSpec((None, l, kv.shape[2]), lambda b, j: (b, 0, 0))],
        out_specs=pl.BlockSpec((None, TM, nq), lambda b, j: (b, j, 0)),
        out_shape=jax.ShapeDtypeStruct((nb, l, nq), BF16),
        compiler_params=_cparams("arbitrary", "arbitrary"),
        name="sw_attention",
    )(sink, q, kv)


def _chunk_index(step, rev, n_ctx, n_all):
    if not rev:
        return step
    return jnp.where(step < n_ctx, n_ctx - 1 - step, n_all + n_ctx - 1 - step)


def _tri(n, rev):
    r = lax.broadcasted_iota(jnp.int32, (n, n), 0)
    c = lax.broadcasted_iota(jnp.int32, (n, n), 1)
    return (c >= r) if rev else (c <= r)


def _split3(x):
    hi = x.astype(BF16)
    r = x - hi.astype(F32)
    mid = r.astype(BF16)
    return hi, mid, (r - mid.astype(F32)).astype(BF16)


def _ml_kernel(q_ref, kt_ref, v_ref, og_ref, gr_ref, gc_ref, ng_ref, y_ref, acc_ref, *, n_ctx, n_all):
    lc = ML_SCAN
    hps = q_ref.shape[1] // ML_DK
    ones_v = jnp.ones((lc, ML_DV), BF16)
    lane_head = lax.broadcasted_iota(jnp.int32, (1, hps * ML_DK), 1) // ML_DK
    row_head = lax.broadcasted_iota(jnp.int32, (hps * ML_DK, 1), 0) // ML_DK

    def gate_forms(step, rev):
        c = _chunk_index(step, rev, n_ctx, n_all)
        r0 = pl.multiple_of(c * lc, lc)
        g_r = gr_ref[c]
        g_c = gc_ref[pl.ds(r0, lc), :]
        cum3 = jnp.dot(jnp.concatenate(_split3(g_r), axis=0), _tri(lc, not rev).astype(BF16),
                       preferred_element_type=F32)
        cum_r = cum3[0:SUBLANES] + cum3[SUBLANES:2 * SUBLANES] + cum3[2 * SUBLANES:]
        tri_c = _tri(lc, rev).astype(BF16)
        cum_c = sum(jnp.dot(tri_c, part, preferred_element_type=F32) for part in _split3(g_c))
        return c, g_r, cum_r, cum_c

    def chain(forms, hl, rev, state):
        ct, m = state
        c, g_r, cum_r, cum_c = forms
        r0 = pl.multiple_of(c * lc, lc)
        base = hl * 4 + (2 if rev else 0)
        bf_r = cum_r[base + 1:base + 2, :]
        bf_c = cum_c[:, base + 1:base + 2]
        x_r = g_r[base:base + 1, :] - bf_r
        g = jnp.sum(g_r[base + 1:base + 2, :], axis=1, keepdims=True)
        tri = _tri(lc, rev)
        pm_c = jnp.max(jnp.where(tri, x_r, -jnp.inf), axis=1, keepdims=True)
        xmax = jnp.max(x_r, axis=1, keepdims=True)
        q = jnp.where(lane_head == hl, q_ref[pl.ds(r0, lc), :], jnp.zeros((), BF16))
        kt = kt_ref[:, pl.ds(r0, lc)]
        v1 = jnp.concatenate([v_ref[pl.ds(r0, lc), hl * ML_DV:(hl + 1) * ML_DV], ones_v], axis=1)
        mx = jnp.maximum(m, pm_c)
        e = jnp.exp(jnp.where(tri, x_r - mx, NEG_INF))
        sc = jnp.dot(q, kt, preferred_element_type=F32) * e
        e_int = jnp.exp(m - mx)
        intra = jnp.dot(sc.astype(BF16), v1, preferred_element_type=F32)
        inter = jnp.dot(q, ct.astype(BF16), preferred_element_type=F32)
        num = intra[:, :ML_DV] + e_int * inter[:, :ML_DV]
        den = intra[:, ML_DV:ML_DV + 1] + e_int * inter[:, ML_DV:ML_DV + 1]
        hout = num / jnp.maximum(jnp.abs(den), jnp.exp(-mx - bf_c))
        acc_ref[pl.ds(r0, lc), hl * ML_DV:(hl + 1) * ML_DV] += hout
        m_loc = g + xmax
        w_r = jnp.exp(x_r - xmax)
        ktw = jnp.where(row_head == hl, kt.astype(F32) * w_r, 0.0).astype(BF16)
        c_loc = jnp.dot(ktw, v1, preferred_element_type=F32)
        m_new = jnp.maximum(g + m, m_loc)
        dec = jnp.exp(g + m - m_new)
        inc = jnp.exp(m_loc - m_new)
        return dec * ct + inc * c_loc, m_new

    acc_ref[...] = jnp.zeros_like(acc_ref)
    chains = [(hl, rev) for hl in range(hps) for rev in (False, True)]
    init = tuple((jnp.zeros((hps * ML_DK, 2 * ML_DV), F32), jnp.zeros((1, 1), F32)) for _ in chains)

    def step_fn(step, states):
        forms = {rev: gate_forms(step, rev) for rev in (False, True)}
        return tuple(chain(forms[rev], hl, rev, st) for (hl, rev), st in zip(chains, states))

    lax.fori_loop(0, n_all, step_fn, init)

    for hl in range(hps):
        cols = slice(hl * ML_DV, (hl + 1) * ML_DV)
        z = acc_ref[:, cols]
        z = z * lax.rsqrt(jnp.mean(z * z, axis=-1, keepdims=True) + RMS_EPS) * ng_ref[:, cols]
        y_ref[:, cols] = (z.astype(BF16) * og_ref[:, cols])


def _ml_mixer(qkv, og, gates, norm_g):
    nb, l, _ = qkv.shape
    lc = ML_SCAN
    n_all, n_ctx = l // lc, CTX_LEN // lc
    hps = 2
    ngrp = ML_HEADS // hps
    nq = ML_HEADS * ML_DK
    g = gates[:, :, :4 * ML_HEADS].reshape(nb, n_all, lc, 2, 2, ngrp, hps)
    g = g.transpose(0, 5, 1, 6, 3, 4, 2).reshape(nb, ngrp, n_all, hps * 4, lc)
    gc = jnp.pad(g.transpose(0, 1, 2, 4, 3).reshape(nb, ngrp, l, hps * 4), ((0, 0),) * 3 + ((0, LANES - hps * 4),))
    kt = qkv[:, :, nq:2 * nq].transpose(0, 2, 1)
    vb = hps * ML_DV
    return pl.pallas_call(
        functools.partial(_ml_kernel, n_ctx=n_ctx, n_all=n_all),
        grid=(nb, ngrp),
        in_specs=[pl.BlockSpec((None, l, hps * ML_DK), lambda b, h: (b, 0, h)),
                  pl.BlockSpec((None, hps * ML_DK, l), lambda b, h: (b, h, 0)),
                  pl.BlockSpec((None, l, vb), lambda b, h: (b, 0, 2 * nq // vb + h)),
                  pl.BlockSpec((None, l, vb), lambda b, h: (b, 0, h)),
                  pl.BlockSpec((None, None, n_all, hps * 4, lc), lambda b, h: (b, h, 0, 0, 0)),
                  pl.BlockSpec((None, None, l, LANES), lambda b, h: (b, h, 0, 0)),
                  pl.BlockSpec((1, vb), lambda b, h: (0, h))],
        out_specs=pl.BlockSpec((None, l, vb), lambda b, h: (b, 0, h)),
        out_shape=jax.ShapeDtypeStruct((nb, l, ML_HEADS * ML_DV), BF16),
        scratch_shapes=[pltpu.VMEM((l, vb), F32)],
        compiler_params=_cparams("arbitrary", "arbitrary"),
        name="mlstm_mixer",
    )(qkv, kt, qkv, og, g, gc, norm_g.reshape(1, -1))


def _gl_kernel(q_ref, k_ref, v_ref, gate_ref, la0_ref, la1_ref, ng_ref, y_ref, acc_ref, st_ref, *, n_ctx, n_all):
    lc = GL_SCAN
    hps = q_ref.shape[1] // GL_DK
    t0 = (((0,), (0,)), ((), ()))
    nt = (((1,), (1,)), ((), ()))
    scale = GL_DK ** -0.5

    def chain(step, hl, rev):
        la_ref = la1_ref if rev else la0_ref
        sidx = 2 * hl + (1 if rev else 0)
        kc = slice(hl * GL_DK, (hl + 1) * GL_DK)
        vc = slice(hl * GL_DV, (hl + 1) * GL_DV)
        c = _chunk_index(step, rev, n_ctx, n_all)
        r0 = pl.multiple_of(c * lc, lc)
        tri = _tri(lc, rev)
        la = la_ref[pl.ds(r0, lc), kc]
        bc3 = jnp.dot(tri.astype(BF16), jnp.concatenate(_split3(la), axis=1), preferred_element_type=F32)
        bc = bc3[:, :GL_DK] + bc3[:, GL_DK:2 * GL_DK] + bc3[:, 2 * GL_DK:]
        g = jnp.sum(la, axis=0, keepdims=True)
        q = q_ref[pl.ds(r0, lc), kc].astype(F32) * scale
        k = k_ref[pl.ds(r0, lc), kc].astype(F32)
        v = v_ref[pl.ds(r0, lc), vc]
        st = st_ref[sidx]
        eg = jnp.exp(g)
        k_dec = k * jnp.exp(-bc)
        q_t = (q * jnp.exp(bc)).astype(BF16)
        k_t = k_dec.astype(BF16)
        att = jnp.where(tri, lax.dot_general(q_t, k_t, nt, preferred_element_type=F32), 0.0)
        o = (jnp.dot(att.astype(BF16), v, preferred_element_type=F32)
             + lax.dot_general(q_t, st.astype(BF16), nt, preferred_element_type=F32))
        acc_ref[pl.ds(r0, lc), vc] += o
        kd = (k_dec * eg).astype(BF16)
        s_loc = lax.dot_general(v, kd, t0, preferred_element_type=F32)
        st_ref[sidx] = st * eg + s_loc

    acc_ref[...] = jnp.zeros_like(acc_ref)
    st_ref[...] = jnp.zeros_like(st_ref)

    def step_fn(step, carry):
        for hl in range(hps):
            chain(step, hl, False)
            chain(step, hl, True)
        return carry

    lax.fori_loop(0, n_all, step_fn, 0)
    for hl in range(hps):
        vc = slice(hl * GL_DV, (hl + 1) * GL_DV)
        z = acc_ref[:, vc]
        z = z * lax.rsqrt(jnp.mean(z * z, axis=-1, keepdims=True) + RMS_EPS) * ng_ref[:, vc]
        y_ref[:, vc] = z.astype(BF16) * gate_ref[:, vc]


def _gl_mixer(qkv, gate, la, norm_g):
    nb, l, _ = qkv.shape
    lc = GL_SCAN
    n_all, n_ctx = l // lc, CTX_LEN // lc
    hps = 2
    ngrp = GL_HEADS // hps
    kb, vb = hps * GL_DK, hps * GL_DV
    nk = GL_HEADS * GL_DK
    return pl.pallas_call(
        functools.partial(_gl_kernel, n_ctx=n_ctx, n_all=n_all),
        grid=(nb, ngrp),
        in_specs=[pl.BlockSpec((None, l, kb), lambda b, h: (b, 0, h)),
                  pl.BlockSpec((None, l, kb), lambda b, h: (b, 0, nk // kb + h)),
                  pl.BlockSpec((None, l, vb), lambda b, h: (b, 0, 2 * nk // vb + h)),
                  pl.BlockSpec((None, l, vb), lambda b, h: (b, 0, h)),
                  pl.BlockSpec((None, l, kb), lambda b, h: (b, 0, h)),
                  pl.BlockSpec((None, l, kb), lambda b, h: (b, 0, nk // kb + h)),
                  pl.BlockSpec((1, vb), lambda b, h: (0, h))],
        out_specs=pl.BlockSpec((None, l, vb), lambda b, h: (b, 0, h)),
        out_shape=jax.ShapeDtypeStruct((nb, l, GL_HEADS * GL_DV), BF16),
        scratch_shapes=[pltpu.VMEM((l, vb), F32), pltpu.VMEM((2 * hps, GL_DV, GL_DK), F32)],
        compiler_params=_cparams("arbitrary", "arbitrary"),
        name="gla_mixer",
    )(qkv, qkv, qkv, gate, la, la, norm_g.reshape(1, -1))


def _store_token_tiles(ref, val):
    rows = val.shape[0]
    for s in range(val.shape[1] // LANES):
        ref[pl.ds(s, rows, stride=SUBLANES), :] = val[:, s * LANES:(s + 1) * LANES]


def _load_token_tiles(ref, rows):
    return jnp.concatenate([ref[pl.ds(s, rows, stride=SUBLANES), :] for s in range(SUBLANES)], axis=1)


def _post_kernel(y_ref, x_ref, mod_ref, wo_ref, g_ref, wr_ref, br_ref, xo_ref, f_ref, rt_ref, cnt_ref, run_ref):
    first = (pl.program_id(0) == 0) & (pl.program_id(1) == 0)

    @pl.when(first)
    def _():
        run_ref[...] = jnp.zeros_like(run_ref)

    o = jnp.dot(y_ref[...], wo_ref[...], preferred_element_type=F32)
    xn = x_ref[...] + mod_ref[2:3, :] * o
    xo_ref[...] = xn
    f = _norm_mod(xn, g_ref[...], mod_ref, 3)
    _store_token_tiles(f_ref, f)
    f_hi = f.astype(BF16)
    f_lo = (f - f_hi.astype(F32)).astype(BF16)
    hh = jnp.dot(f_hi, wr_ref[...], preferred_element_type=F32)
    lh = jnp.dot(f_lo, wr_ref[:, :LANES], preferred_element_type=F32)
    lg = hh[:, :LANES] + (hh[:, LANES:] + lh) + br_ref[...]
    lane = lax.broadcasted_iota(jnp.int32, lg.shape, 1).astype(F32)
    big = float(LANES)

    def masked_max(mask):
        return jnp.max(jnp.where(mask, lg, -jnp.inf), axis=-1, keepdims=True)

    def first_at(mask, val):
        return jnp.min(jnp.where(mask & (lg == val), lane, big), axis=-1, keepdims=True)

    gm = lane < MOE_GROUPS
    mg = masked_max(gm)
    g_idx = first_at(gm, mg)
    g_w = 1.0 / jnp.sum(jnp.where(gm, jnp.exp(lg - mg), 0.0), axis=-1, keepdims=True)
    lo = MOE_GROUPS + MOE_PER_GROUP * g_idx
    em = (lane >= lo) & (lane < lo + MOE_PER_GROUP)
    v0 = masked_max(em)
    i0 = first_at(em, v0)
    em1 = em & (lane != i0)
    v1 = masked_max(em1)
    i1 = first_at(em1, v1)
    e0, e1 = i0 - MOE_GROUPS, i1 - MOE_GROUPS
    t = jnp.exp(v1 - v0)
    w0 = g_w / (1.0 + t)
    w1 = g_w * t / (1.0 + t)
    oh = ((lane == e0) | (lane == e1)).astype(F32)
    rows = lg.shape[0]
    below = lax.broadcasted_iota(jnp.int32, (rows, rows), 0) > lax.broadcasted_iota(jnp.int32, (rows, rows), 1)
    tot = run_ref[...] + jnp.dot(below.astype(BF16), oh.astype(BF16), preferred_element_type=F32)
    r0 = jnp.sum(jnp.where(lane == e0, tot, 0.0), axis=-1, keepdims=True)
    r1 = jnp.sum(jnp.where(lane == e1, tot, 0.0), axis=-1, keepdims=True)
    run_ref[...] = run_ref[...] + jnp.sum(oh, axis=0, keepdims=True)
    cnt_ref[...] = jnp.broadcast_to(run_ref[...], cnt_ref.shape)
    out = jnp.zeros_like(lg)
    for i, val in enumerate((e0, e1, w0, w1, r0, r1)):
        out = jnp.where(lane == i, val, out)
    rt_ref[...] = out


def _post_call(y, x, mods, w_o, g_ffn, w_r, b_r):
    nb, l, d = x.shape
    nt = l // TM
    tile = lambda b, t: (b * nt + t, 0)
    return pl.pallas_call(
        _post_kernel,
        grid=(nb, nt),
        in_specs=[pl.BlockSpec((None, TM, d), lambda b, t: (b, t, 0)),
                  pl.BlockSpec((None, TM, d), lambda b, t: (b, t, 0)),
                  _mod_spec(nb),
                  pl.BlockSpec((d, d), lambda b, t: (0, 0)),
                  pl.BlockSpec((1, d), lambda b, t: (0, 0)),
                  pl.BlockSpec((d, 2 * LANES), lambda b, t: (0, 0)),
                  pl.BlockSpec((1, LANES), lambda b, t: (0, 0))],
        out_specs=[pl.BlockSpec((None, TM, d), lambda b, t: (b, t, 0)),
                   pl.BlockSpec((TM * SUBLANES, LANES), tile),
                   pl.BlockSpec((TM, LANES), tile),
                   pl.BlockSpec((SUBLANES, LANES), lambda b, t: (0, 0))],
        out_shape=[jax.ShapeDtypeStruct((nb, l, d), F32),
                   jax.ShapeDtypeStruct((nb * l * SUBLANES, LANES), F32),
                   jax.ShapeDtypeStruct((nb * l, LANES), F32),
                   jax.ShapeDtypeStruct((SUBLANES, LANES), F32)],
        scratch_shapes=[pltpu.VMEM((1, LANES), F32)],
        compiler_params=_cparams("arbitrary", "arbitrary"),
        name="outproj_router",
    )(y, x, mods, w_o, g_ffn.reshape(1, d), w_r, b_r)


def _dispatch_kernel(dest_ref, pend_ref, f_ref, xs_out, zbuf, sem):
    tile = f_ref.shape[0] // SUBLANES
    base = pl.program_id(0) * tile

    @pl.when(pl.program_id(0) == 0)
    def _():
        zbuf[...] = jnp.zeros_like(zbuf)

        def zero_block(blk):
            start = pl.multiple_of(blk * (MOE_BLOCK * SUBLANES), MOE_BLOCK * SUBLANES)
            return pltpu.make_async_copy(zbuf, xs_out.at[pl.ds(start, MOE_BLOCK * SUBLANES), :], sem)

        def has_rows(e):
            return pend_ref[e] > (0 if e == 0 else pend_ref[e - 1])

        for e in range(MOE_EXPERTS):
            pl.when(has_rows(e))(lambda e=e: zero_block(pend_ref[e] // MOE_BLOCK - 1).start())
        for e in range(MOE_EXPERTS):
            pl.when(has_rows(e))(lambda e=e: zero_block(pend_ref[e] // MOE_BLOCK - 1).wait())
        n_act = pend_ref[MOE_EXPERTS - 1] // MOE_BLOCK
        n_blk = xs_out.shape[0] // (MOE_BLOCK * SUBLANES)
        lax.fori_loop(n_act, n_blk, lambda b, c: (zero_block(b).start(), c)[1], 0)
        lax.fori_loop(n_act, n_blk, lambda b, c: (zero_block(b).wait(), c)[1], 0)

    def copy(t, k):
        d = dest_ref[2 * (base + t) + k]
        return pltpu.make_async_copy(f_ref.at[pl.ds(t * SUBLANES, SUBLANES), :],
                                     xs_out.at[pl.ds(d * SUBLANES, SUBLANES), :], sem)

    def issue(t, c):
        copy(t, 0).start()
        copy(t, 1).start(priority=1)
        return c

    lax.fori_loop(0, tile, issue, 0, unroll=DMA_UNROLL)
    for _ in range(2):
        pltpu.make_async_copy(f_ref, xs_out.at[pl.ds(0, tile * SUBLANES), :], sem).wait()


def _dispatch(dest, p_end, f_tiles, n_slots):
    n_tok = f_tiles.shape[0] // SUBLANES
    tile = DISPATCH_TILE
    assert n_tok % tile == 0
    return pl.pallas_call(
        _dispatch_kernel,
        grid_spec=pltpu.PrefetchScalarGridSpec(
            num_scalar_prefetch=2,
            grid=(n_tok // tile,),
            in_specs=[pl.BlockSpec((tile * SUBLANES, LANES), lambda i, dr, pe: (i, 0))],
            out_specs=pl.BlockSpec(memory_space=pl.ANY),
            scratch_shapes=[pltpu.VMEM((MOE_BLOCK * SUBLANES, LANES), F32), pltpu.SemaphoreType.DMA(())]),
        out_shape=jax.ShapeDtypeStruct((n_slots * SUBLANES, LANES), F32),
        compiler_params=_cparams("arbitrary"),
        name="moe_dispatch",
    )(dest, p_end, f_tiles)


def _expert_kernel(blk_e_ref, nact_ref, xs_ref, wg_ref, wu_ref, wd_ref, ys_ref, wg_s, wu_s, wd_s):
    i = pl.program_id(0)
    active = i < nact_ref[0]
    changed = (i == 0) | (blk_e_ref[i] != blk_e_ref[jnp.maximum(i - 1, 0)])

    @pl.when(active & changed)
    def _():
        wg_s[...] = wg_ref[...].astype(BF16)
        wu_s[...] = wu_ref[...].astype(BF16)
        wd_s[...] = wd_ref[...].astype(BF16)

    @pl.when(active)
    def _():
        x = _load_token_tiles(xs_ref, MOE_BLOCK).astype(BF16)
        a = jnp.dot(x, wg_s[...], preferred_element_type=F32)
        u = jnp.dot(x, wu_s[...], preferred_element_type=F32)
        y = jnp.dot((_silu(a) * u).astype(BF16), wd_s[...], preferred_element_type=F32)
        _store_token_tiles(ys_ref, y)

    @pl.when(jnp.logical_not(active))
    def _():
        ys_ref[...] = jnp.zeros_like(ys_ref)


def _experts(blk_e, nact, xs, w_gate, w_up, w_down, layer):
    n_blk = blk_e.shape[0]
    _, _, d, ff = w_gate.shape
    rows = MOE_BLOCK * SUBLANES
    last = lambda i, na: jnp.minimum(i, jnp.maximum(na[0] - 1, 0))
    blk = lambda i, be, na: (last(i, na), 0)
    wsel = lambda i, be, na: (layer, be[last(i, na)], 0, 0)
    return pl.pallas_call(
        _expert_kernel,
        grid_spec=pltpu.PrefetchScalarGridSpec(
            num_scalar_prefetch=2,
            grid=(n_blk,),
            in_specs=[pl.BlockSpec((rows, LANES), blk),
                      pl.BlockSpec((None, None, d, ff), wsel),
                      pl.BlockSpec((None, None, d, ff), wsel),
                      pl.BlockSpec((None, None, ff, d), wsel)],
            out_specs=pl.BlockSpec((rows, LANES), lambda i, be, na: (i, 0)),
            scratch_shapes=[pltpu.VMEM((d, ff), BF16), pltpu.VMEM((d, ff), BF16), pltpu.VMEM((ff, d), BF16)]),
        out_shape=jax.ShapeDtypeStruct(xs.shape, F32),
        compiler_params=_cparams("arbitrary"),
        name="moe_experts",
    )(blk_e, nact, xs, w_gate, w_up, w_down)


def _combine_kernel(dest_ref, ys_hbm, x_ref, rt_ref, mod_ref, o_ref, ybuf, sem, *, nt, t0):
    i = pl.program_id(0)
    n = pl.num_programs(0)
    nte = nt - t0

    def gather(step, slot, start):
        base = ((step // nte) * nt + step % nte + t0) * TM

        def copy(t, k):
            d = dest_ref[2 * (base + t) + k]
            return pltpu.make_async_copy(ys_hbm.at[pl.ds(d * SUBLANES, SUBLANES), :],
                                         ybuf.at[slot, k, pl.ds(t * SUBLANES, SUBLANES), :], sem.at[slot])

        def issue(t, c):
            copy(t, 0).start()
            copy(t, 1).start(priority=1)
            return c

        if start:
            lax.fori_loop(0, TM, issue, 0, unroll=DMA_UNROLL)
        else:
            for k in range(2):
                pltpu.make_async_copy(ys_hbm.at[pl.ds(0, TM * SUBLANES), :], ybuf.at[slot, k], sem.at[slot]).wait()

    slot = i % 2
    pl.when(i == 0)(lambda: gather(i, slot, True))
    pl.when(i + 1 < n)(lambda: gather(i + 1, 1 - slot, True))
    gather(i, slot, False)
    y0 = _load_token_tiles(ybuf.at[slot, 0], TM)
    y1 = _load_token_tiles(ybuf.at[slot, 1], TM)
    rt = rt_ref[...]
    y = rt[:, 2:3] * y0 + rt[:, 3:4] * y1
    o_ref[...] = x_ref[...] + mod_ref[5:6, :] * y


def _combine(dest, ys, x, rt, mods, skip_ctx):
    nb, l, d = x.shape
    nt = l // TM
    t0 = 1 if skip_ctx else 0
    nte = nt - t0
    bt = lambda i: (i // nte, i % nte + t0)
    mod_row = (lambda b, t: b) if skip_ctx else (lambda b, t: jnp.where(t == 0, nb, b))
    return pl.pallas_call(
        functools.partial(_combine_kernel, nt=nt, t0=t0),
        grid_spec=pltpu.PrefetchScalarGridSpec(
            num_scalar_prefetch=1,
            grid=(nb * nte,),
            in_specs=[pl.BlockSpec(memory_space=pl.ANY),
                      pl.BlockSpec((None, TM, d), lambda i, dr: (*bt(i), 0)),
                      pl.BlockSpec((TM, LANES), lambda i, dr: (bt(i)[0] * nt + bt(i)[1], 0)),
                      pl.BlockSpec((None, 6, d), lambda i, dr: (mod_row(*bt(i)), 0, 0))],
            out_specs=pl.BlockSpec((None, TM, d), lambda i, dr: (i // nte, i % nte, 0)),
            scratch_shapes=[pltpu.VMEM((2, 2, TM * SUBLANES, LANES), F32), pltpu.SemaphoreType.DMA((2,))]),
        out_shape=jax.ShapeDtypeStruct((nb, nte * TM, d), F32),
        compiler_params=_cparams("arbitrary"),
        name="moe_combine",
    )(dest, ys, x, rt, mods)


def _moe(y_mix, x, mods, w_o, g_ffn, w_grp, b_grp, w_exp, b_exp, w_gate, w_up, w_down, layer, skip_ctx):
    nb, l, d = x.shape
    n_tok = nb * l
    nr = MOE_GROUPS + MOE_EXPERTS
    w_r = jnp.pad(jnp.concatenate([w_grp, w_exp], axis=1), ((0, 0), (0, LANES - nr)))
    b_r = jnp.pad(jnp.concatenate([b_grp, b_exp]), (0, LANES - nr)).reshape(1, LANES)
    w_hi = w_r.astype(BF16)
    w_r = jnp.concatenate([w_hi, (w_r - w_hi.astype(F32)).astype(BF16)], axis=1)
    x_new, f_tiles, rt, cnt = _post_call(y_mix, x, mods, w_o, g_ffn, w_r, b_r)
    counts = cnt[0, :MOE_EXPERTS].astype(jnp.int32)
    padded = (counts + MOE_BLOCK - 1) // MOE_BLOCK * MOE_BLOCK
    p_end = jnp.cumsum(padded)
    p_start = p_end - padded
    n_pairs = 2 * n_tok
    n_blk = -(-(n_pairs + MOE_EXPERTS * (MOE_BLOCK - 1)) // MOE_BLOCK)
    eid = rt[:, 0:2].astype(jnp.int32)
    first = jnp.sum(jnp.where(eid[:, :, None] == jnp.arange(MOE_EXPERTS, dtype=jnp.int32), p_start, 0), axis=-1)
    dest = (first + rt[:, 4:6].astype(jnp.int32)).reshape(-1)
    blk_row = jnp.arange(n_blk, dtype=jnp.int32) * MOE_BLOCK
    blk_e = jnp.minimum(jnp.sum((p_end[None, :] <= blk_row[:, None]).astype(jnp.int32), axis=1), MOE_EXPERTS - 1)
    nact = (p_end[-1:] // MOE_BLOCK).astype(jnp.int32)
    xs = _dispatch(dest, p_end.astype(jnp.int32), f_tiles, n_blk * MOE_BLOCK)
    ys = _experts(blk_e, nact, xs, w_gate, w_up, w_down, layer)
    return _combine(dest, ys, x_new, rt, mods, skip_ctx)


def _rope_tables(seq):
    nf = SW_HD // 4
    inv = ROPE_BASE ** (-jnp.arange(nf, dtype=F32) / nf)
    pos = jnp.arange(seq, dtype=jnp.int32)
    rows, cols = (pos // GRID_W).astype(F32), (pos % GRID_W).astype(F32)
    lane = jnp.arange(LANES)
    p = jnp.where((lane % SW_HD < SW_HD // 2)[None, :], rows[:, None], cols[:, None])
    ang = p * inv[lane % nf][None, :]
    sign = jnp.where((lane % (2 * nf)) < nf, -1.0, 1.0)[None, :]
    cs = jnp.concatenate([jnp.ones((CTX_LEN, LANES), F32), jnp.cos(ang)], axis=0)
    sn = jnp.concatenate([jnp.zeros((CTX_LEN, LANES), F32), jnp.sin(ang) * sign], axis=0)
    return cs, sn


def _pad_cols(w, n):
    return jnp.pad(w, ((0, 0), (0, n - w.shape[1])))


def kernel(x, c, ctx, c_ctx, ada_w, ada_b, norm_mix_g, norm_ffn_g, na_w_qkv, na_qk_g, na_rpb, na_w_o, ml_w_in, ml_b_gates, ml_norm_g, ml_w_o, sw_w_qkv, sw_qk_g, sw_sink, sw_w_o, gl_w_in, gl_w_a2, gl_b_a, gl_norm_g, gl_w_o, moe_w_grp, moe_b_grp, moe_w_exp, moe_b_exp, moe_w_gate, moe_w_up, moe_w_down):
    nb, seq, d = x.shape
    depth = ada_w.shape[0]
    assert d == D_MODEL and ctx.shape[1] == CTX_LEN == TM and seq % TM == 0
    rows = -(-(nb + 1) // SUBLANES) * SUBLANES
    cond = jnp.pad(jnp.concatenate([c, c_ctx[None]], axis=0), ((0, rows - nb - 1), (0, 0)))
    mods_all = _ada_mods(cond, ada_w, ada_b)
    xa = jnp.concatenate([ctx, x], axis=1)
    for i in range(depth):
        j, kind = divmod(i, 4)
        mods = mods_all[i]
        g_mix = norm_mix_g[i]
        if kind == 0:
            qkg = na_qk_g[j]
            (qkv,) = _proj_call(_proj_na_kernel, xa, mods, g_mix, na_w_qkv[j].astype(BF16), [(qkg, False)],
                                [3 * NA_HEADS * NA_HD], [BF16], "proj_na")
            y = _na_attention(qkv, _na_bias_table(na_rpb[j], seq // GRID_W))
            w_o = na_w_o[j]
        elif kind == 1:
            n_in = -(-ml_w_in.shape[2] // LANES) * LANES
            bg = jnp.pad(ml_b_gates[j], (0, LANES - ml_b_gates.shape[1])).reshape(1, LANES)
            qkv, og, gates = _proj_call(_proj_ml_kernel, xa, mods, g_mix, _pad_cols(ml_w_in[j], n_in).astype(BF16),
                                        [(bg, False)], [2 * ML_HEADS * ML_DK + ML_HEADS * ML_DV, ML_HEADS * ML_DV, LANES],
                                        [BF16, BF16, F32], "proj_ml")
            y = _ml_mixer(qkv, og, gates, ml_norm_g[j])
            w_o = ml_w_o[j]
        elif kind == 2:
            qkg = jnp.concatenate([sw_qk_g[j], sw_qk_g[j]], axis=1)
            cs, sn = _rope_tables(seq)
            nqc = SW_HEADS * SW_HD
            w_in = sw_w_qkv[j]
            w_q = w_in[:, :nqc].reshape(d, 2, 2, 4, SW_HD).transpose(0, 1, 3, 2, 4).reshape(d, nqc)
            w_in = jnp.concatenate([w_q, w_in[:, nqc:]], axis=1)
            q, kv = _proj_call(_proj_sw_kernel, xa, mods, g_mix, w_in.astype(BF16),
                               [(qkg, False), (cs, True), (sn, True)],
                               [nqc, 2 * SW_KV * SW_HD], [BF16, BF16], "proj_sw")
            y = _sw_attention(q, kv, sw_sink[j])
            w_o = sw_w_o[j].reshape(2, 2, 4, SW_HD, d).transpose(0, 2, 1, 3, 4).reshape(nqc, d)
        else:
            n_in = -(-gl_w_in.shape[2] // LANES) * LANES
            nk = GL_HEADS * GL_DK
            wa = jnp.zeros((LANES, 2 * nk), F32)
            wa = wa.at[:GL_RANK, :nk].set(gl_w_a2[j, 0]).at[GL_RANK:2 * GL_RANK, nk:].set(gl_w_a2[j, 1])
            ba = gl_b_a[j].reshape(1, 2 * nk)
            qkv, gate, la = _proj_call(_proj_gl_kernel, xa, mods, g_mix, _pad_cols(gl_w_in[j], n_in).astype(BF16),
                                       [(wa.astype(BF16), False), (ba, False)],
                                       [2 * nk + GL_HEADS * GL_DV, GL_HEADS * GL_DV, 2 * nk], [BF16, BF16, F32], "proj_gl")
            y = _gl_mixer(qkv, gate, la, gl_norm_g[j])
            w_o = gl_w_o[j]
        xa = _moe(y, xa, mods, w_o.astype(BF16), norm_ffn_g[i], moe_w_grp[i], moe_b_grp[i], moe_w_exp[i], moe_b_exp[i],
                  moe_w_gate, moe_w_up, moe_w_down, i, i == depth - 1)
    return xa
```

```python
import functools

import jax
import jax.numpy as jnp
import numpy as np
from jax import lax
from jax.experimental import pallas as pl
from jax.experimental.pallas import tpu as pltpu

F32 = jnp.float32
BF16 = jnp.bfloat16
HIGHEST = lax.Precision.HIGHEST

D_MODEL = 1024
CTX_LEN = 256
GRID_W = 64
RMS_EPS = 1e-6
NEG_INF = -1e30
ROPE_BASE = 10000.0

NA_HEADS, NA_HD, NA_WIN_R, NA_WIN_C = 8, 128, 8, 16
NA_QROWS = 4
NA_KROWS = NA_QROWS + NA_WIN_R - 1
ML_HEADS, ML_DK, ML_DV, ML_CAP = 8, 64, 128, 15.0
ML_SCAN = 256
SW_HEADS, SW_KV, SW_HD, SW_WINDOW = 16, 4, 64, 128
GL_HEADS, GL_DK, GL_DV, GL_RANK, GL_TAU = 4, 128, 256, 16, 16.0
GL_SCAN = 128
MOE_GROUPS, MOE_PER_GROUP, MOE_EXPERTS, MOE_FF = 4, 8, 32, 512
MOE_BLOCK = 512

LANES = 128
SUBLANES = 8
TM = 256
VMEM_LIMIT = 48 * 1024 * 1024
DMA_UNROLL = 8
DISPATCH_TILE = 3 * TM


def _cparams(*sem):
    return pltpu.CompilerParams(dimension_semantics=sem, vmem_limit_bytes=VMEM_LIMIT)


def _norm_mod(x, g, mod_ref, k):
    y = x * lax.rsqrt(jnp.mean(x * x, axis=-1, keepdims=True) + RMS_EPS) * g
    return y * (1.0 + mod_ref[k + 1:k + 2, :]) + mod_ref[k:k + 1, :]


def _log_sigmoid(x):
    return jnp.minimum(x, 0.0) - jnp.log(1.0 + jnp.exp(-jnp.abs(x)))


def _silu(x):
    return x * jax.nn.sigmoid(x)


def _mod_kernel(c_ref, w_ref, b_ref, o_ref):
    s = _silu(c_ref[...])
    o_ref[...] = jnp.dot(s.astype(BF16), w_ref[...].astype(BF16), preferred_element_type=F32) + b_ref[...]


def _ada_mods(cond, ada_w, ada_b):
    depth, d, _ = ada_w.shape
    rows = cond.shape[0]
    out = pl.pallas_call(
        _mod_kernel,
        grid=(depth, 6),
        in_specs=[pl.BlockSpec((rows, d), lambda i, n: (0, 0)),
                  pl.BlockSpec((None, d, d), lambda i, n: (i, 0, n)),
                  pl.BlockSpec((None, 1, d), lambda i, n: (i, 0, n))],
        out_specs=pl.BlockSpec((None, rows, d), lambda i, n: (i, 0, n)),
        out_shape=jax.ShapeDtypeStruct((depth, rows, 6 * d), F32),
        compiler_params=_cparams("arbitrary", "arbitrary"),
        name="ada_mods",
    )(cond, ada_w, ada_b.reshape(depth, 1, 6 * d))
    return out.reshape(depth, rows, 6, d)


def _mod_spec(nb):
    return pl.BlockSpec((None, 6, D_MODEL), lambda b, t: (jnp.where(t == 0, nb, b), 0, 0))


def _proj_call(kernel, x, mods, g, w, extras, out_cols, out_dtypes, name):
    nb, l, d = x.shape
    nt = l // TM
    n = w.shape[1]
    in_specs = [pl.BlockSpec((None, TM, d), lambda b, t: (b, t, 0)),
                _mod_spec(nb),
                pl.BlockSpec((1, d), lambda b, t: (0, 0)),
                pl.BlockSpec((d, n), lambda b, t: (0, 0))]
    args = [x, mods, g.reshape(1, d), w]
    for e, per_tile in extras:
        if per_tile:
            in_specs.append(pl.BlockSpec((TM, e.shape[1]), lambda b, t: (t, 0)))
        else:
            in_specs.append(pl.BlockSpec(e.shape, lambda b, t: (0, 0)))
        args.append(e)
    return pl.pallas_call(
        kernel,
        grid=(nb, nt),
        in_specs=in_specs,
        out_specs=[pl.BlockSpec((None, TM, c), lambda b, t: (b, t, 0)) for c in out_cols],
        out_shape=[jax.ShapeDtypeStruct((nb, l, c), dt) for c, dt in zip(out_cols, out_dtypes)],
        compiler_params=_cparams("arbitrary", "arbitrary"),
        name=name,
    )(*args)


def _proj_na_kernel(x_ref, mod_ref, g_ref, w_ref, qkg_ref, o_ref):
    h = _norm_mod(x_ref[...], g_ref[...], mod_ref, 0)
    u = jnp.dot(h.astype(BF16), w_ref[...], preferred_element_type=F32)
    nq = NA_HEADS * NA_HD
    for part in range(2):
        gain = qkg_ref[part:part + 1, :]
        if part == 0:
            gain = gain * (NA_HD ** -0.5)
        for hh in range(NA_HEADS):
            lo = part * nq + hh * NA_HD
            z = u[:, lo:lo + NA_HD]
            z = z * lax.rsqrt(jnp.mean(z * z, axis=-1, keepdims=True) + RMS_EPS) * gain
            o_ref[:, lo:lo + NA_HD] = z.astype(BF16)
    o_ref[:, 2 * nq:] = u[:, 2 * nq:].astype(BF16)


def _head64_rms(z, gain):
    lane = lax.broadcasted_iota(jnp.int32, z.shape, 1)
    lo = lane < SW_HD
    zz = z * z
    s_lo = jnp.sum(jnp.where(lo, zz, 0.0), axis=-1, keepdims=True)
    s_hi = jnp.sum(jnp.where(lo, 0.0, zz), axis=-1, keepdims=True)
    ms = jnp.where(lo, s_lo, s_hi) * (1.0 / SW_HD)
    return z * lax.rsqrt(ms + RMS_EPS) * gain


def _rope_slab(z, cs, sn):
    lane = lax.broadcasted_iota(jnp.int32, z.shape, 1)
    first = (lane % 32) < 16
    partner = jnp.where(first, pltpu.roll(z, LANES - 16, 1), pltpu.roll(z, 16, 1))
    return z * cs + partner * sn


def _proj_sw_kernel(x_ref, mod_ref, g_ref, w_ref, qkg_ref, cs_ref, sn_ref, q_ref, kv_ref):
    h = _norm_mod(x_ref[...], g_ref[...], mod_ref, 0)
    u = jnp.dot(h.astype(BF16), w_ref[...], preferred_element_type=F32)
    cs, sn = cs_ref[...], sn_ref[...]
    nq, nk = SW_HEADS * SW_HD, SW_KV * SW_HD
    scale = SW_HD ** -0.5
    for s in range(nq // LANES):
        z = _head64_rms(u[:, s * LANES:(s + 1) * LANES], qkg_ref[0:1, :])
        q_ref[:, s * LANES:(s + 1) * LANES] = (_rope_slab(z, cs, sn) * scale).astype(BF16)
    for s in range(nk // LANES):
        z = _head64_rms(u[:, nq + s * LANES:nq + (s + 1) * LANES], qkg_ref[1:2, :])
        kv_ref[:, s * LANES:(s + 1) * LANES] = _rope_slab(z, cs, sn).astype(BF16)
    kv_ref[:, nk:] = u[:, nq + nk:].astype(BF16)


def _proj_ml_kernel(x_ref, mod_ref, g_ref, w_ref, bg_ref, qkv_ref, og_ref, gt_ref):
    h = _norm_mod(x_ref[...], g_ref[...], mod_ref, 0)
    u = jnp.dot(h.astype(BF16), w_ref[...], preferred_element_type=F32)
    nq, nv = ML_HEADS * ML_DK, ML_HEADS * ML_DV
    qkv_ref[:, :nq] = (u[:, :nq] * (ML_DK ** -0.5)).astype(BF16)
    qkv_ref[:, nq:] = u[:, nq:2 * nq + nv].astype(BF16)
    og_ref[...] = jax.nn.sigmoid(u[:, 2 * nq + nv:2 * nq + 2 * nv]).astype(BF16)
    pre = u[:, 2 * nq + 2 * nv:] + bg_ref[...]
    pre = ML_CAP * jnp.tanh(pre / ML_CAP)
    lane = lax.broadcasted_iota(jnp.int32, pre.shape, 1)
    gt_ref[...] = jnp.where((lane % 16) >= 8, _log_sigmoid(pre), pre)


def _proj_gl_kernel(x_ref, mod_ref, g_ref, w_ref, wa_ref, ba_ref, qkv_ref, gate_ref, la_ref):
    h = _norm_mod(x_ref[...], g_ref[...], mod_ref, 0)
    u = jnp.dot(h.astype(BF16), w_ref[...], preferred_element_type=F32)
    nq, nv = GL_HEADS * GL_DK, GL_HEADS * GL_DV
    qkv_ref[...] = u[:, :2 * nq + nv].astype(BF16)
    gate_ref[...] = _silu(u[:, 2 * nq + nv:2 * nq + 2 * nv]).astype(BF16)
    z = u[:, 2 * nq + 2 * nv:]
    a = jnp.dot(z.astype(BF16), wa_ref[...], preferred_element_type=F32) + ba_ref[...]
    la_ref[...] = _log_sigmoid(a) * (1.0 / GL_TAU)


def _na_bias_table(rpb, n_rows):
    cq = np.arange(GRID_W)[:, None]
    ck = np.arange(GRID_W)[None, :]
    c0 = np.clip(cq - NA_WIN_C // 2, 0, GRID_W - NA_WIN_C)
    col_ok = (ck >= c0) & (ck < c0 + NA_WIN_C)
    dc = np.clip(ck - cq + NA_WIN_C - 1, 0, 2 * NA_WIN_C - 2)
    pick = ((dc[None] == np.arange(2 * NA_WIN_C - 1)[:, None, None]) & col_ok[None]).astype(np.float32)
    m = jnp.einsum('hrd,dqk->hqrk', rpb.astype(F32), jnp.asarray(pick), precision=HIGHEST)
    m = jnp.where(jnp.asarray(col_ok)[None, :, None, :], m, NEG_INF)
    blocks = []
    for jj in range(n_rows // NA_QROWS):
        ws = min(max(NA_QROWS * jj - NA_WIN_R // 2, 0), n_rows - NA_KROWS)
        per_row = []
        for ri in range(NA_QROWS):
            r = NA_QROWS * jj + ri
            r0 = min(max(r - NA_WIN_R // 2, 0), n_rows - NA_WIN_R)
            lead = r0 - ws
            d0 = r0 - r + NA_WIN_R - 1
            piece = jnp.pad(m[:, :, d0:d0 + NA_WIN_R, :],
                            ((0, 0), (0, 0), (lead, NA_KROWS - NA_WIN_R - lead), (0, 0)), constant_values=NEG_INF)
            per_row.append(piece.reshape(rpb.shape[0], GRID_W, NA_KROWS * GRID_W))
        blocks.append(jnp.concatenate(per_row, axis=1))
    return jnp.stack(blocks, axis=1)


def _na_kernel(q_ref, k_ref, v_ref, bias_ref, o_ref, *, n_rows):
    j = pl.program_id(1)
    nb = q_ref.shape[0]
    nt = (((1,), (1,)), ((), ()))
    nkw = NA_KROWS * GRID_W

    @pl.when(j == 0)
    def _ctx():
        def body(b, c):
            q = q_ref[b]
            s = lax.dot_general(q, k_ref[b, 0:CTX_LEN, :], nt, preferred_element_type=F32)
            p = jnp.exp(s - jnp.max(s, axis=-1, keepdims=True))
            l = jnp.sum(p, axis=-1, keepdims=True)
            o = jnp.dot(p.astype(BF16), v_ref[b, 0:CTX_LEN, :], preferred_element_type=F32)
            o_ref[b] = (o / l).astype(BF16)
            return c
        lax.fori_loop(0, nb, body, 0)

    @pl.when(j > 0)
    def _lat():
        ws = jnp.clip(NA_QROWS * (j - 1) - NA_WIN_R // 2, 0, n_rows - NA_KROWS)
        start = pl.multiple_of(CTX_LEN + ws * GRID_W, GRID_W)
        bias = bias_ref[...]

        def body(b, c):
            q = q_ref[b]
            s_n = lax.dot_general(q, k_ref[b, pl.ds(start, nkw), :], nt, preferred_element_type=F32) + bias
            s_c = lax.dot_general(q, k_ref[b, 0:CTX_LEN, :], nt, preferred_element_type=F32)
            m = jnp.maximum(jnp.max(s_n, axis=-1, keepdims=True), jnp.max(s_c, axis=-1, keepdims=True))
            p_n = jnp.exp(s_n - m)
            p_c = jnp.exp(s_c - m)
            l = jnp.sum(p_n, axis=-1, keepdims=True) + jnp.sum(p_c, axis=-1, keepdims=True)
            o = (jnp.dot(p_n.astype(BF16), v_ref[b, pl.ds(start, nkw), :], preferred_element_type=F32)
                 + jnp.dot(p_c.astype(BF16), v_ref[b, 0:CTX_LEN, :], preferred_element_type=F32))
            o_ref[b] = (o / l).astype(BF16)
            return c
        lax.fori_loop(0, nb, body, 0)


def _na_attention(qkv, bias_tab):
    nb, l, _ = qkv.shape
    n_rows = (l - CTX_LEN) // GRID_W
    nt = l // TM
    hq = NA_HEADS
    return pl.pallas_call(
        functools.partial(_na_kernel, n_rows=n_rows),
        grid=(hq, nt),
        in_specs=[pl.BlockSpec((nb, TM, NA_HD), lambda h, j: (0, j, h)),
                  pl.BlockSpec((nb, l, NA_HD), lambda h, j: (0, 0, hq + h)),
                  pl.BlockSpec((nb, l, NA_HD), lambda h, j: (0, 0, 2 * hq + h)),
                  pl.BlockSpec((None, None, TM, NA_KROWS * GRID_W),
                               lambda h, j: (h, jnp.maximum(j - 1, 0), 0, 0))],
        out_specs=pl.BlockSpec((nb, TM, NA_HD), lambda h, j: (0, j, h)),
        out_shape=jax.ShapeDtypeStruct((nb, l, hq * NA_HD), BF16),
        compiler_params=_cparams("arbitrary", "arbitrary"),
        name="na_attention",
    )(qkv, qkv, qkv, bias_tab)


def _sw_kernel(sink_ref, q_ref, kv_ref, o_ref, *, seq):
    j = pl.program_id(1)
    nt = (((1,), (1,)), ((), ()))
    nkv = SW_KV * SW_HD
    kwin = TM + 2 * SW_WINDOW
    low = lax.broadcasted_iota(jnp.int32, (1, LANES), 1) < SW_HD

    def attend(segs):
        for pair in range(SW_KV // 2):
            kc = slice(pair * LANES, (pair + 1) * LANES)
            vc = slice(nkv + pair * LANES, nkv + (pair + 1) * LANES)
            ks = [kv_ref[rows, kc] for rows, _ in segs]
            vs = [kv_ref[rows, vc] for rows, _ in segs]
            one = jnp.ones((), BF16)
            v_half = [[jnp.where(low, v, one) for v in vs], [jnp.where(low, one, v) for v in vs]]
            for i in range(SW_HEADS // SW_KV):
                cols = slice((4 * pair + i) * LANES, (4 * pair + i + 1) * LANES)
                qs = q_ref[:, cols]
                outs = []
                for half in range(2):
                    sink = sink_ref[8 * pair + 4 * half + i]
                    qm = jnp.where(low if half == 0 else jnp.logical_not(low), qs, jnp.zeros((), BF16))
                    sc = []
                    for k, (_, mask) in zip(ks, segs):
                        s = lax.dot_general(qm, k, nt, preferred_element_type=F32)
                        sc.append(s if mask is None else jnp.where(mask, s, NEG_INF))
                    m = sink
                    for s in sc:
                        m = jnp.maximum(m, jnp.max(s, axis=-1, keepdims=True))
                    acc = None
                    for s, v in zip(sc, v_half[half]):
                        pv = jnp.dot(jnp.exp(s - m).astype(BF16), v, preferred_element_type=F32)
                        acc = pv if acc is None else acc + pv
                    denom = pltpu.roll(acc, SW_HD, 1) + jnp.exp(sink - m)
                    outs.append(acc / denom)
                o_ref[:, cols] = jnp.where(low, outs[0], outs[1]).astype(BF16)

    @pl.when(j == 0)
    def _ctx():
        attend([(slice(0, CTX_LEN), None)])

    @pl.when(j > 0)
    def _lat():
        q0 = (j - 1) * TM
        ws = jnp.clip(q0 - SW_WINDOW, 0, seq - kwin)
        start = pl.multiple_of(CTX_LEN + ws, SW_WINDOW)
        qpos = q0 + lax.broadcasted_iota(jnp.int32, (TM, kwin), 0)
        kpos = ws + lax.broadcasted_iota(jnp.int32, (TM, kwin), 1)
        ok = jnp.abs(qpos - kpos) <= SW_WINDOW
        attend([(pl.ds(start, kwin), ok), (slice(0, CTX_LEN), None)])


def _sw_attention(q, kv, sink):
    nb, l, nq = q.shape
    nt = l // TM
    return pl.pallas_call(
        functools.partial(_sw_kernel, seq=l - CTX_LEN),
        grid=(nb, nt),
        in_specs=[pl.BlockSpec(memory_space=pltpu.SMEM),
                  pl.BlockSpec((None, TM, nq), lambda b, j: (b, j, 0)),
                  pl.BlockSpec((None, l, kv.shape[2]), lambda b, j: (b, 0, 0))],
        out_specs=pl.BlockSpec((None, TM, nq), lambda b, j: (b, j, 0)),
        out_shape=jax.ShapeDtypeStruct((nb, l, nq), BF16),
        compiler_params=_cparams("arbitrary", "arbitrary"),
        name="sw_attention",
    )(sink, q, kv)


def _chunk_index(step, rev, n_ctx, n_all):
    if not rev:
        return step
    return jnp.where(step < n_ctx, n_ctx - 1 - step, n_all + n_ctx - 1 - step)


def _tri(n, rev):
    r = lax.broadcasted_iota(jnp.int32, (n, n), 0)
    c = lax.broadcasted_iota(jnp.int32, (n, n), 1)
    return (c >= r) if rev else (c <= r)


def _split3(x):
    hi = x.astype(BF16)
    r = x - hi.astype(F32)
    mid = r.astype(BF16)
    return hi, mid, (r - mid.astype(F32)).astype(BF16)


def _ml_kernel(q_ref, kt_ref, v_ref, og_ref, gr_ref, gc_ref, ng_ref, y_ref, acc_ref, *, n_ctx, n_all):
    lc = ML_SCAN
    hps = q_ref.shape[1] // ML_DK
    ones_v = jnp.ones((lc, ML_DV), BF16)
    lane_head = lax.broadcasted_iota(jnp.int32, (1, hps * ML_DK), 1) // ML_DK
    row_head = lax.broadcasted_iota(jnp.int32, (hps * ML_DK, 1), 0) // ML_DK

    def gate_forms(step, rev):
        c = _chunk_index(step, rev, n_ctx, n_all)
        r0 = pl.multiple_of(c * lc, lc)
        g_r = gr_ref[c]
        g_c = gc_ref[pl.ds(r0, lc), :]
        cum3 = jnp.dot(jnp.concatenate(_split3(g_r), axis=0), _tri(lc, not rev).astype(BF16),
                       preferred_element_type=F32)
        cum_r = cum3[0:SUBLANES] + cum3[SUBLANES:2 * SUBLANES] + cum3[2 * SUBLANES:]
        tri_c = _tri(lc, rev).astype(BF16)
        cum_c = sum(jnp.dot(tri_c, part, preferred_element_type=F32) for part in _split3(g_c))
        return c, g_r, cum_r, cum_c

    def chain(forms, hl, rev, state):
        ct, m = state
        c, g_r, cum_r, cum_c = forms
        r0 = pl.multiple_of(c * lc, lc)
        base = hl * 4 + (2 if rev else 0)
        bf_r = cum_r[base + 1:base + 2, :]
        bf_c = cum_c[:, base + 1:base + 2]
        x_r = g_r[base:base + 1, :] - bf_r
        g = jnp.sum(g_r[base + 1:base + 2, :], axis=1, keepdims=True)
        tri = _tri(lc, rev)
        pm_c = jnp.max(jnp.where(tri, x_r, -jnp.inf), axis=1, keepdims=True)
        xmax = jnp.max(x_r, axis=1, keepdims=True)
        q = jnp.where(lane_head == hl, q_ref[pl.ds(r0, lc), :], jnp.zeros((), BF16))
        kt = kt_ref[:, pl.ds(r0, lc)]
        v1 = jnp.concatenate([v_ref[pl.ds(r0, lc), hl * ML_DV:(hl + 1) * ML_DV], ones_v], axis=1)
        mx = jnp.maximum(m, pm_c)
        e = jnp.exp(jnp.where(tri, x_r - mx, NEG_INF))
        sc = jnp.dot(q, kt, preferred_element_type=F32) * e
        e_int = jnp.exp(m - mx)
        intra = jnp.dot(sc.astype(BF16), v1, preferred_element_type=F32)
        inter = jnp.dot(q, ct.astype(BF16), preferred_element_type=F32)
        num = intra[:, :ML_DV] + e_int * inter[:, :ML_DV]
        den = intra[:, ML_DV:ML_DV + 1] + e_int * inter[:, ML_DV:ML_DV + 1]
        hout = num / jnp.maximum(jnp.abs(den), jnp.exp(-mx - bf_c))
        acc_ref[pl.ds(r0, lc), hl * ML_DV:(hl + 1) * ML_DV] += hout
        m_loc = g + xmax
        w_r = jnp.exp(x_r - xmax)
        ktw = jnp.where(row_head == hl, kt.astype(F32) * w_r, 0.0).astype(BF16)
        c_loc = jnp.dot(ktw, v1, preferred_element_type=F32)
        m_new = jnp.maximum(g + m, m_loc)
        dec = jnp.exp(g + m - m_new)
        inc = jnp.exp(m_loc - m_new)
        return dec * ct + inc * c_loc, m_new

    acc_ref[...] = jnp.zeros_like(acc_ref)
    chains = [(hl, rev) for hl in range(hps) for rev in (False, True)]
    init = tuple((jnp.zeros((hps * ML_DK, 2 * ML_DV), F32), jnp.zeros((1, 1), F32)) for _ in chains)

    def step_fn(step, states):
        forms = {rev: gate_forms(step, rev) for rev in (False, True)}
        return tuple(chain(forms[rev], hl, rev, st) for (hl, rev), st in zip(chains, states))

    lax.fori_loop(0, n_all, step_fn, init)

    for hl in range(hps):
        cols = slice(hl * ML_DV, (hl + 1) * ML_DV)
        z = acc_ref[:, cols]
        z = z * lax.rsqrt(jnp.mean(z * z, axis=-1, keepdims=True) + RMS_EPS) * ng_ref[:, cols]
        y_ref[:, cols] = (z.astype(BF16) * og_ref[:, cols])


def _ml_mixer(qkv, og, gates, norm_g):
    nb, l, _ = qkv.shape
    lc = ML_SCAN
    n_all, n_ctx = l // lc, CTX_LEN // lc
    hps = 2
    ngrp = ML_HEADS // hps
    nq = ML_HEADS * ML_DK
    g = gates[:, :, :4 * ML_HEADS].reshape(nb, n_all, lc, 2, 2, ngrp, hps)
    g = g.transpose(0, 5, 1, 6, 3, 4, 2).reshape(nb, ngrp, n_all, hps * 4, lc)
    gc = jnp.pad(g.transpose(0, 1, 2, 4, 3).reshape(nb, ngrp, l, hps * 4), ((0, 0),) * 3 + ((0, LANES - hps * 4),))
    kt = qkv[:, :, nq:2 * nq].transpose(0, 2, 1)
    vb = hps * ML_DV
    return pl.pallas_call(
        functools.partial(_ml_kernel, n_ctx=n_ctx, n_all=n_all),
        grid=(nb, ngrp),
        in_specs=[pl.BlockSpec((None, l, hps * ML_DK), lambda b, h: (b, 0, h)),
                  pl.BlockSpec((None, hps * ML_DK, l), lambda b, h: (b, h, 0)),
                  pl.BlockSpec((None, l, vb), lambda b, h: (b, 0, 2 * nq // vb + h)),
                  pl.BlockSpec((None, l, vb), lambda b, h: (b, 0, h)),
                  pl.BlockSpec((None, None, n_all, hps * 4, lc), lambda b, h: (b, h, 0, 0, 0)),
                  pl.BlockSpec((None, None, l, LANES), lambda b, h: (b, h, 0, 0)),
                  pl.BlockSpec((1, vb), lambda b, h: (0, h))],
        out_specs=pl.BlockSpec((None, l, vb), lambda b, h: (b, 0, h)),
        out_shape=jax.ShapeDtypeStruct((nb, l, ML_HEADS * ML_DV), BF16),
        scratch_shapes=[pltpu.VMEM((l, vb), F32)],
        compiler_params=_cparams("arbitrary", "arbitrary"),
        name="mlstm_mixer",
    )(qkv, kt, qkv, og, g, gc, norm_g.reshape(1, -1))


def _gl_kernel(q_ref, k_ref, v_ref, gate_ref, la0_ref, la1_ref, ng_ref, y_ref, acc_ref, st_ref, *, n_ctx, n_all):
    lc = GL_SCAN
    hps = q_ref.shape[1] // GL_DK
    t0 = (((0,), (0,)), ((), ()))
    nt = (((1,), (1,)), ((), ()))
    scale = GL_DK ** -0.5

    def chain(step, hl, rev):
        la_ref = la1_ref if rev else la0_ref
        sidx = 2 * hl + (1 if rev else 0)
        kc = slice(hl * GL_DK, (hl + 1) * GL_DK)
        vc = slice(hl * GL_DV, (hl + 1) * GL_DV)
        c = _chunk_index(step, rev, n_ctx, n_all)
        r0 = pl.multiple_of(c * lc, lc)
        tri = _tri(lc, rev)
        la = la_ref[pl.ds(r0, lc), kc]
        bc3 = jnp.dot(tri.astype(BF16), jnp.concatenate(_split3(la), axis=1), preferred_element_type=F32)
        bc = bc3[:, :GL_DK] + bc3[:, GL_DK:2 * GL_DK] + bc3[:, 2 * GL_DK:]
        g = jnp.sum(la, axis=0, keepdims=True)
        q = q_ref[pl.ds(r0, lc), kc].astype(F32) * scale
        k = k_ref[pl.ds(r0, lc), kc].astype(F32)
        v = v_ref[pl.ds(r0, lc), vc]
        st = st_ref[sidx]
        eg = jnp.exp(g)
        k_dec = k * jnp.exp(-bc)
        q_t = (q * jnp.exp(bc)).astype(BF16)
        k_t = k_dec.astype(BF16)
        att = jnp.where(tri, lax.dot_general(q_t, k_t, nt, preferred_element_type=F32), 0.0)
        o = (jnp.dot(att.astype(BF16), v, preferred_element_type=F32)
             + lax.dot_general(q_t, st.astype(BF16), nt, preferred_element_type=F32))
        acc_ref[pl.ds(r0, lc), vc] += o
        kd = (k_dec * eg).astype(BF16)
        s_loc = lax.dot_general(v, kd, t0, preferred_element_type=F32)
        st_ref[sidx] = st * eg + s_loc

    acc_ref[...] = jnp.zeros_like(acc_ref)
    st_ref[...] = jnp.zeros_like(st_ref)

    def step_fn(step, carry):
        for hl in range(hps):
            chain(step, hl, False)
            chain(step, hl, True)
        return carry

    lax.fori_loop(0, n_all, step_fn, 0)
    for hl in range(hps):
        vc = slice(hl * GL_DV, (hl + 1) * GL_DV)
        z = acc_ref[:, vc]
        z = z * lax.rsqrt(jnp.mean(z * z, axis=-1, keepdims=True) + RMS_EPS) * ng_ref[:, vc]
        y_ref[:, vc] = z.astype(BF16) * gate_ref[:, vc]


def _gl_mixer(qkv, gate, la, norm_g):
    nb, l, _ = qkv.shape
    lc = GL_SCAN
    n_all, n_ctx = l // lc, CTX_LEN // lc
    hps = 2
    ngrp = GL_HEADS // hps
    kb, vb = hps * GL_DK, hps * GL_DV
    nk = GL_HEADS * GL_DK
    return pl.pallas_call(
        functools.partial(_gl_kernel, n_ctx=n_ctx, n_all=n_all),
        grid=(nb, ngrp),
        in_specs=[pl.BlockSpec((None, l, kb), lambda b, h: (b, 0, h)),
                  pl.BlockSpec((None, l, kb), lambda b, h: (b, 0, nk // kb + h)),
                  pl.BlockSpec((None, l, vb), lambda b, h: (b, 0, 2 * nk // vb + h)),
                  pl.BlockSpec((None, l, vb), lambda b, h: (b, 0, h)),
                  pl.BlockSpec((None, l, kb), lambda b, h: (b, 0, h)),
                  pl.BlockSpec((None, l, kb), lambda b, h: (b, 0, nk // kb + h)),
                  pl.BlockSpec((1, vb), lambda b, h: (0, h))],
        out_specs=pl.BlockSpec((None, l, vb), lambda b, h: (b, 0, h)),
        out_shape=jax.ShapeDtypeStruct((nb, l, GL_HEADS * GL_DV), BF16),
        scratch_shapes=[pltpu.VMEM((l, vb), F32), pltpu.VMEM((2 * hps, GL_DV, GL_DK), F32)],
        compiler_params=_cparams("arbitrary", "arbitrary"),
        name="gla_mixer",
    )(qkv, qkv, qkv, gate, la, la, norm_g.reshape(1, -1))


def _store_token_tiles(ref, val):
    rows = val.shape[0]
    for s in range(val.shape[1] // LANES):
        ref[pl.ds(s, rows, stride=SUBLANES), :] = val[:, s * LANES:(s + 1) * LANES]


def _load_token_tiles(ref, rows):
    return jnp.concatenate([ref[pl.ds(s, rows, stride=SUBLANES), :] for s in range(SUBLANES)], axis=1)


def _post_kernel(y_ref, x_ref, mod_ref, wo_ref, g_ref, wr_ref, br_ref, xo_ref, f_ref, rt_ref, cnt_ref, run_ref):
    first = (pl.program_id(0) == 0) & (pl.program_id(1) == 0)

    @pl.when(first)
    def _():
        run_ref[...] = jnp.zeros_like(run_ref)

    o = jnp.dot(y_ref[...], wo_ref[...], preferred_element_type=F32)
    xn = x_ref[...] + mod_ref[2:3, :] * o
    xo_ref[...] = xn
    f = _norm_mod(xn, g_ref[...], mod_ref, 3)
    _store_token_tiles(f_ref, f)
    f_hi = f.astype(BF16)
    f_lo = (f - f_hi.astype(F32)).astype(BF16)
    hh = jnp.dot(f_hi, wr_ref[...], preferred_element_type=F32)
    lh = jnp.dot(f_lo, wr_ref[:, :LANES], preferred_element_type=F32)
    lg = hh[:, :LANES] + (hh[:, LANES:] + lh) + br_ref[...]
    lane = lax.broadcasted_iota(jnp.int32, lg.shape, 1).astype(F32)
    big = float(LANES)

    def masked_max(mask):
        return jnp.max(jnp.where(mask, lg, -jnp.inf), axis=-1, keepdims=True)

    def first_at(mask, val):
        return jnp.min(jnp.where(mask & (lg == val), lane, big), axis=-1, keepdims=True)

    gm = lane < MOE_GROUPS
    mg = masked_max(gm)
    g_idx = first_at(gm, mg)
    g_w = 1.0 / jnp.sum(jnp.where(gm, jnp.exp(lg - mg), 0.0), axis=-1, keepdims=True)
    lo = MOE_GROUPS + MOE_PER_GROUP * g_idx
    em = (lane >= lo) & (lane < lo + MOE_PER_GROUP)
    v0 = masked_max(em)
    i0 = first_at(em, v0)
    em1 = em & (lane != i0)
    v1 = masked_max(em1)
    i1 = first_at(em1, v1)
    e0, e1 = i0 - MOE_GROUPS, i1 - MOE_GROUPS
    t = jnp.exp(v1 - v0)
    w0 = g_w / (1.0 + t)
    w1 = g_w * t / (1.0 + t)
    oh = ((lane == e0) | (lane == e1)).astype(F32)
    rows = lg.shape[0]
    below = lax.broadcasted_iota(jnp.int32, (rows, rows), 0) > lax.broadcasted_iota(jnp.int32, (rows, rows), 1)
    tot = run_ref[...] + jnp.dot(below.astype(BF16), oh.astype(BF16), preferred_element_type=F32)
    r0 = jnp.sum(jnp.where(lane == e0, tot, 0.0), axis=-1, keepdims=True)
    r1 = jnp.sum(jnp.where(lane == e1, tot, 0.0), axis=-1, keepdims=True)
    run_ref[...] = run_ref[...] + jnp.sum(oh, axis=0, keepdims=True)
    cnt_ref[...] = jnp.broadcast_to(run_ref[...], cnt_ref.shape)
    out = jnp.zeros_like(lg)
    for i, val in enumerate((e0, e1, w0, w1, r0, r1)):
        out = jnp.where(lane == i, val, out)
    rt_ref[...] = out


def _post_call(y, x, mods, w_o, g_ffn, w_r, b_r):
    nb, l, d = x.shape
    nt = l // TM
    tile = lambda b, t: (b * nt + t, 0)
    return pl.pallas_call(
        _post_kernel,
        grid=(nb, nt),
        in_specs=[pl.BlockSpec((None, TM, d), lambda b, t: (b, t, 0)),
                  pl.BlockSpec((None, TM, d), lambda b, t: (b, t, 0)),
                  _mod_spec(nb),
                  pl.BlockSpec((d, d), lambda b, t: (0, 0)),
                  pl.BlockSpec((1, d), lambda b, t: (0, 0)),
                  pl.BlockSpec((d, 2 * LANES), lambda b, t: (0, 0)),
                  pl.BlockSpec((1, LANES), lambda b, t: (0, 0))],
        out_specs=[pl.BlockSpec((None, TM, d), lambda b, t: (b, t, 0)),
                   pl.BlockSpec((TM * SUBLANES, LANES), tile),
                   pl.BlockSpec((TM, LANES), tile),
                   pl.BlockSpec((SUBLANES, LANES), lambda b, t: (0, 0))],
        out_shape=[jax.ShapeDtypeStruct((nb, l, d), F32),
                   jax.ShapeDtypeStruct((nb * l * SUBLANES, LANES), F32),
                   jax.ShapeDtypeStruct((nb * l, LANES), F32),
                   jax.ShapeDtypeStruct((SUBLANES, LANES), F32)],
        scratch_shapes=[pltpu.VMEM((1, LANES), F32)],
        compiler_params=_cparams("arbitrary", "arbitrary"),
        name="outproj_router",
    )(y, x, mods, w_o, g_ffn.reshape(1, d), w_r, b_r)


def _dispatch_kernel(dest_ref, pend_ref, f_ref, xs_out, zbuf, sem):
    tile = f_ref.shape[0] // SUBLANES
    base = pl.program_id(0) * tile

    @pl.when(pl.program_id(0) == 0)
    def _():
        zbuf[...] = jnp.zeros_like(zbuf)

        def zero_block(blk):
            start = pl.multiple_of(blk * (MOE_BLOCK * SUBLANES), MOE_BLOCK * SUBLANES)
            return pltpu.make_async_copy(zbuf, xs_out.at[pl.ds(start, MOE_BLOCK * SUBLANES), :], sem)

        def has_rows(e):
            return pend_ref[e] > (0 if e == 0 else pend_ref[e - 1])

        for e in range(MOE_EXPERTS):
            pl.when(has_rows(e))(lambda e=e: zero_block(pend_ref[e] // MOE_BLOCK - 1).start())
        for e in range(MOE_EXPERTS):
            pl.when(has_rows(e))(lambda e=e: zero_block(pend_ref[e] // MOE_BLOCK - 1).wait())
        n_act = pend_ref[MOE_EXPERTS - 1] // MOE_BLOCK
        n_blk = xs_out.shape[0] // (MOE_BLOCK * SUBLANES)
        lax.fori_loop(n_act, n_blk, lambda b, c: (zero_block(b).start(), c)[1], 0)
        lax.fori_loop(n_act, n_blk, lambda b, c: (zero_block(b).wait(), c)[1], 0)

    def copy(t, k):
        d = dest_ref[2 * (base + t) + k]
        return pltpu.make_async_copy(f_ref.at[pl.ds(t * SUBLANES, SUBLANES), :],
                                     xs_out.at[pl.ds(d * SUBLANES, SUBLANES), :], sem)

    def issue(t, c):
        copy(t, 0).start()
        copy(t, 1).start(priority=1)
        return c

    lax.fori_loop(0, tile, issue, 0, unroll=DMA_UNROLL)
    for _ in range(2):
        pltpu.make_async_copy(f_ref, xs_out.at[pl.ds(0, tile * SUBLANES), :], sem).wait()


def _dispatch(dest, p_end, f_tiles, n_slots):
    n_tok = f_tiles.shape[0] // SUBLANES
    tile = DISPATCH_TILE
    assert n_tok % tile == 0
    return pl.pallas_call(
        _dispatch_kernel,
        grid_spec=pltpu.PrefetchScalarGridSpec(
            num_scalar_prefetch=2,
            grid=(n_tok // tile,),
            in_specs=[pl.BlockSpec((tile * SUBLANES, LANES), lambda i, dr, pe: (i, 0))],
            out_specs=pl.BlockSpec(memory_space=pl.ANY),
            scratch_shapes=[pltpu.VMEM((MOE_BLOCK * SUBLANES, LANES), F32), pltpu.SemaphoreType.DMA(())]),
        out_shape=jax.ShapeDtypeStruct((n_slots * SUBLANES, LANES), F32),
        compiler_params=_cparams("arbitrary"),
        name="moe_dispatch",
    )(dest, p_end, f_tiles)


def _expert_kernel(blk_e_ref, nact_ref, nxt_ref, slot_ref, xs_ref, wg_hbm, wu_hbm, wd_hbm, ys_ref,
                   wg_f, wu_f, wd_f, wg_s, wu_s, wd_s, sem, *, layer):
    i = pl.program_id(0)
    active = i < nact_ref[0]
    e = blk_e_ref[i]
    changed = (i == 0) | (e != blk_e_ref[jnp.maximum(i - 1, 0)])
    slot = slot_ref[i]

    def fetch(expert, s):
        return [pltpu.make_async_copy(w.at[layer, expert], buf.at[s], sem.at[s, k])
                for k, (w, buf) in enumerate(((wg_hbm, wg_f), (wu_hbm, wu_f), (wd_hbm, wd_f)))]

    @pl.when(i == 0)
    def _():
        for c in fetch(e, slot):
            c.start()

    @pl.when(active & changed)
    def _():
        nxt = nxt_ref[i]

        @pl.when(nxt >= 0)
        def _():
            for c in fetch(nxt, 1 - slot):
                c.start()

        for c in fetch(e, slot):
            c.wait()
        wg_s[...] = wg_f[slot].astype(BF16)
        wu_s[...] = wu_f[slot].astype(BF16)
        wd_s[...] = wd_f[slot].astype(BF16)

    @pl.when(active)
    def _():
        x = _load_token_tiles(xs_ref, MOE_BLOCK).astype(BF16)
        a = jnp.dot(x, wg_s[...], preferred_element_type=F32)
        u = jnp.dot(x, wu_s[...], preferred_element_type=F32)
        y = jnp.dot((_silu(a) * u).astype(BF16), wd_s[...], preferred_element_type=F32)
        _store_token_tiles(ys_ref, y)

    @pl.when(jnp.logical_not(active))
    def _():
        ys_ref[...] = jnp.zeros_like(ys_ref)


def _experts(blk_e, nact, nxt_e, slot, xs, w_gate, w_up, w_down, layer):
    n_blk = blk_e.shape[0]
    _, _, d, ff = w_gate.shape
    rows = MOE_BLOCK * SUBLANES
    last = lambda i, na: jnp.minimum(i, jnp.maximum(na[0] - 1, 0))
    hbm = pl.BlockSpec(memory_space=pl.ANY)
    return pl.pallas_call(
        functools.partial(_expert_kernel, layer=layer),
        grid_spec=pltpu.PrefetchScalarGridSpec(
            num_scalar_prefetch=4,
            grid=(n_blk,),
            in_specs=[pl.BlockSpec((rows, LANES), lambda i, be, na, nx, sl: (last(i, na), 0)), hbm, hbm, hbm],
            out_specs=pl.BlockSpec((rows, LANES), lambda i, be, na, nx, sl: (i, 0)),
            scratch_shapes=[pltpu.VMEM((2, d, ff), F32), pltpu.VMEM((2, d, ff), F32), pltpu.VMEM((2, ff, d), F32),
                            pltpu.VMEM((d, ff), BF16), pltpu.VMEM((d, ff), BF16), pltpu.VMEM((ff, d), BF16),
                            pltpu.SemaphoreType.DMA((2, 3))]),
        out_shape=jax.ShapeDtypeStruct(xs.shape, F32),
        compiler_params=_cparams("arbitrary"),
        name="moe_experts",
    )(blk_e, nact, nxt_e, slot, xs, w_gate, w_up, w_down)


def _combine_kernel(dest_ref, ys_hbm, x_ref, rt_ref, mod_ref, o_ref, ybuf, sem, *, nt, t0):
    i = pl.program_id(0)
    n = pl.num_programs(0)
    nte = nt - t0

    def gather(step, slot, start):
        base = ((step // nte) * nt + step % nte + t0) * TM

        def copy(t, k):
            d = dest_ref[2 * (base + t) + k]
            return pltpu.make_async_copy(ys_hbm.at[pl.ds(d * SUBLANES, SUBLANES), :],
                                         ybuf.at[slot, k, pl.ds(t * SUBLANES, SUBLANES), :], sem.at[slot])

        def issue(t, c):
            copy(t, 0).start()
            copy(t, 1).start(priority=1)
            return c

        if start:
            lax.fori_loop(0, TM, issue, 0, unroll=DMA_UNROLL)
        else:
            for k in range(2):
                pltpu.make_async_copy(ys_hbm.at[pl.ds(0, TM * SUBLANES), :], ybuf.at[slot, k], sem.at[slot]).wait()

    slot = i % 2
    pl.when(i == 0)(lambda: gather(i, slot, True))
    pl.when(i + 1 < n)(lambda: gather(i + 1, 1 - slot, True))
    gather(i, slot, False)
    y0 = _load_token_tiles(ybuf.at[slot, 0], TM)
    y1 = _load_token_tiles(ybuf.at[slot, 1], TM)
    rt = rt_ref[...]
    y = rt[:, 2:3] * y0 + rt[:, 3:4] * y1
    o_ref[...] = x_ref[...] + mod_ref[5:6, :] * y


def _combine(dest, ys, x, rt, mods, skip_ctx):
    nb, l, d = x.shape
    nt = l // TM
    t0 = 1 if skip_ctx else 0
    nte = nt - t0
    bt = lambda i: (i // nte, i % nte + t0)
    mod_row = (lambda b, t: b) if skip_ctx else (lambda b, t: jnp.where(t == 0, nb, b))
    return pl.pallas_call(
        functools.partial(_combine_kernel, nt=nt, t0=t0),
        grid_spec=pltpu.PrefetchScalarGridSpec(
            num_scalar_prefetch=1,
            grid=(nb * nte,),
            in_specs=[pl.BlockSpec(memory_space=pl.ANY),
                      pl.BlockSpec((None, TM, d), lambda i, dr: (*bt(i), 0)),
                      pl.BlockSpec((TM, LANES), lambda i, dr: (bt(i)[0] * nt + bt(i)[1], 0)),
                      pl.BlockSpec((None, 6, d), lambda i, dr: (mod_row(*bt(i)), 0, 0))],
            out_specs=pl.BlockSpec((None, TM, d), lambda i, dr: (i // nte, i % nte, 0)),
            scratch_shapes=[pltpu.VMEM((2, 2, TM * SUBLANES, LANES), F32), pltpu.SemaphoreType.DMA((2,))]),
        out_shape=jax.ShapeDtypeStruct((nb, nte * TM, d), F32),
        compiler_params=_cparams("arbitrary"),
        name="moe_combine",
    )(dest, ys, x, rt, mods)


def _moe(y_mix, x, mods, w_o, g_ffn, w_grp, b_grp, w_exp, b_exp, w_gate, w_up, w_down, layer, skip_ctx):
    nb, l, d = x.shape
    n_tok = nb * l
    nr = MOE_GROUPS + MOE_EXPERTS
    w_r = jnp.pad(jnp.concatenate([w_grp, w_exp], axis=1), ((0, 0), (0, LANES - nr)))
    b_r = jnp.pad(jnp.concatenate([b_grp, b_exp]), (0, LANES - nr)).reshape(1, LANES)
    w_hi = w_r.astype(BF16)
    w_r = jnp.concatenate([w_hi, (w_r - w_hi.astype(F32)).astype(BF16)], axis=1)
    x_new, f_tiles, rt, cnt = _post_call(y_mix, x, mods, w_o, g_ffn, w_r, b_r)
    counts = cnt[0, :MOE_EXPERTS].astype(jnp.int32)
    padded = (counts + MOE_BLOCK - 1) // MOE_BLOCK * MOE_BLOCK
    p_end = jnp.cumsum(padded)
    p_start = p_end - padded
    n_pairs = 2 * n_tok
    n_blk = -(-(n_pairs + MOE_EXPERTS * (MOE_BLOCK - 1)) // MOE_BLOCK)
    eid = rt[:, 0:2].astype(jnp.int32)
    first = jnp.sum(jnp.where(eid[:, :, None] == jnp.arange(MOE_EXPERTS, dtype=jnp.int32), p_start, 0), axis=-1)
    dest = (first + rt[:, 4:6].astype(jnp.int32)).reshape(-1)
    blk_row = jnp.arange(n_blk, dtype=jnp.int32) * MOE_BLOCK
    blk_e = jnp.minimum(jnp.sum((p_end[None, :] <= blk_row[:, None]).astype(jnp.int32), axis=1), MOE_EXPERTS - 1)
    nact = (p_end[-1:] // MOE_BLOCK).astype(jnp.int32)
    xs = _dispatch(dest, p_end.astype(jnp.int32), f_tiles, n_blk * MOE_BLOCK)
    has = padded > 0
    ids = jnp.arange(MOE_EXPERTS, dtype=jnp.int32)
    rank = jnp.cumsum(has.astype(jnp.int32)) - has.astype(jnp.int32)
    later = jnp.where((ids[None, :] > ids[:, None]) & has[None, :], ids[None, :], MOE_EXPERTS)
    nxt = jnp.min(later, axis=1)
    nxt = jnp.where(nxt == MOE_EXPERTS, -1, nxt)
    sel = blk_e[:, None] == ids[None, :]
    slot = jnp.sum(jnp.where(sel, rank[None, :] % 2, 0), axis=1).astype(jnp.int32)
    nxt_e = jnp.sum(jnp.where(sel, nxt[None, :], 0), axis=1).astype(jnp.int32)
    ys = _experts(blk_e, nact, nxt_e, slot, xs, w_gate, w_up, w_down, layer)
    return _combine(dest, ys, x_new, rt, mods, skip_ctx)


def _rope_tables(seq):
    nf = SW_HD // 4
    inv = ROPE_BASE ** (-jnp.arange(nf, dtype=F32) / nf)
    pos = jnp.arange(seq, dtype=jnp.int32)
    rows, cols = (pos // GRID_W).astype(F32), (pos % GRID_W).astype(F32)
    lane = jnp.arange(LANES)
    p = jnp.where((lane % SW_HD < SW_HD // 2)[None, :], rows[:, None], cols[:, None])
    ang = p * inv[lane % nf][None, :]
    sign = jnp.where((lane % (2 * nf)) < nf, -1.0, 1.0)[None, :]
    cs = jnp.concatenate([jnp.ones((CTX_LEN, LANES), F32), jnp.cos(ang)], axis=0)
    sn = jnp.concatenate([jnp.zeros((CTX_LEN, LANES), F32), jnp.sin(ang) * sign], axis=0)
    return cs, sn


def _pad_cols(w, n):
    return jnp.pad(w, ((0, 0), (0, n - w.shape[1])))


def kernel(x, c, ctx, c_ctx, ada_w, ada_b, norm_mix_g, norm_ffn_g, na_w_qkv, na_qk_g, na_rpb, na_w_o, ml_w_in, ml_b_gates, ml_norm_g, ml_w_o, sw_w_qkv, sw_qk_g, sw_sink, sw_w_o, gl_w_in, gl_w_a2, gl_b_a, gl_norm_g, gl_w_o, moe_w_grp, moe_b_grp, moe_w_exp, moe_b_exp, moe_w_gate, moe_w_up, moe_w_down):
    nb, seq, d = x.shape
    depth = ada_w.shape[0]
    assert d == D_MODEL and ctx.shape[1] == CTX_LEN == TM and seq % TM == 0
    rows = -(-(nb + 1) // SUBLANES) * SUBLANES
    cond = jnp.pad(jnp.concatenate([c, c_ctx[None]], axis=0), ((0, rows - nb - 1), (0, 0)))
    mods_all = _ada_mods(cond, ada_w, ada_b)
    xa = jnp.concatenate([ctx, x], axis=1)
    for i in range(depth):
        j, kind = divmod(i, 4)
        mods = mods_all[i]
        g_mix = norm_mix_g[i]
        if kind == 0:
            qkg = na_qk_g[j]
            (qkv,) = _proj_call(_proj_na_kernel, xa, mods, g_mix, na_w_qkv[j].astype(BF16), [(qkg, False)],
                                [3 * NA_HEADS * NA_HD], [BF16], "proj_na")
            y = _na_attention(qkv, _na_bias_table(na_rpb[j], seq // GRID_W))
            w_o = na_w_o[j]
        elif kind == 1:
            n_in = -(-ml_w_in.shape[2] // LANES) * LANES
            bg = jnp.pad(ml_b_gates[j], (0, LANES - ml_b_gates.shape[1])).reshape(1, LANES)
            qkv, og, gates = _proj_call(_proj_ml_kernel, xa, mods, g_mix, _pad_cols(ml_w_in[j], n_in).astype(BF16),
                                        [(bg, False)], [2 * ML_HEADS * ML_DK + ML_HEADS * ML_DV, ML_HEADS * ML_DV, LANES],
                                        [BF16, BF16, F32], "proj_ml")
            y = _ml_mixer(qkv, og, gates, ml_norm_g[j])
            w_o = ml_w_o[j]
        elif kind == 2:
            qkg = jnp.concatenate([sw_qk_g[j], sw_qk_g[j]], axis=1)
            cs, sn = _rope_tables(seq)
            nqc = SW_HEADS * SW_HD
            w_in = sw_w_qkv[j]
            w_q = w_in[:, :nqc].reshape(d, 2, 2, 4, SW_HD).transpose(0, 1, 3, 2, 4).reshape(d, nqc)
            w_in = jnp.concatenate([w_q, w_in[:, nqc:]], axis=1)
            q, kv = _proj_call(_proj_sw_kernel, xa, mods, g_mix, w_in.astype(BF16),
                               [(qkg, False), (cs, True), (sn, True)],
                               [nqc, 2 * SW_KV * SW_HD], [BF16, BF16], "proj_sw")
            y = _sw_attention(q, kv, sw_sink[j])
            w_o = sw_w_o[j].reshape(2, 2, 4, SW_HD, d).transpose(0, 2, 1, 3, 4).reshape(nqc, d)
        else:
            n_in = -(-gl_w_in.shape[2] // LANES) * LANES
            nk = GL_HEADS * GL_DK
            wa = jnp.zeros((LANES, 2 * nk), F32)
            wa = wa.at[:GL_RANK, :nk].set(gl_w_a2[j, 0]).at[GL_RANK:2 * GL_RANK, nk:].set(gl_w_a2[j, 1])
            ba = gl_b_a[j].reshape(1, 2 * nk)
            qkv, gate, la = _proj_call(_proj_gl_kernel, xa, mods, g_mix, _pad_cols(gl_w_in[j], n_in).astype(BF16),
                                       [(wa.astype(BF16), False), (ba, False)],
                                       [2 * nk + GL_HEADS * GL_DV, GL_HEADS * GL_DV, 2 * nk], [BF16, BF16, F32], "proj_gl")
            y = _gl_mixer(qkv, gate, la, gl_norm_g[j])
            w_o = gl_w_o[j]
        xa = _moe(y, xa, mods, w_o.astype(BF16), norm_ffn_g[i], moe_w_grp[i], moe_b_grp[i], moe_w_exp[i], moe_b_exp[i],
                  moe_w_gate, moe_w_up, moe_w_down, i, i == depth - 1)
    return xa
```

```python
import functools

import jax
import jax.numpy as jnp
import numpy as np
from jax import lax
from jax.experimental import pallas as pl
from jax.experimental.pallas import tpu as pltpu

F32 = jnp.float32
BF16 = jnp.bfloat16
HIGHEST = lax.Precision.HIGHEST

D_MODEL = 1024
CTX_LEN = 256
GRID_W = 64
RMS_EPS = 1e-6
NEG_INF = -1e30
ROPE_BASE = 10000.0

NA_HEADS, NA_HD, NA_WIN_R, NA_WIN_C = 8, 128, 8, 16
NA_QROWS = 4
NA_KROWS = NA_QROWS + NA_WIN_R - 1
ML_HEADS, ML_DK, ML_DV, ML_CAP = 8, 64, 128, 15.0
ML_SCAN = 256
SW_HEADS, SW_KV, SW_HD, SW_WINDOW = 16, 4, 64, 128
GL_HEADS, GL_DK, GL_DV, GL_RANK, GL_TAU = 4, 128, 256, 16, 16.0
GL_SCAN = 128
MOE_GROUPS, MOE_PER_GROUP, MOE_EXPERTS, MOE_FF = 4, 8, 32, 512
MOE_BLOCK = 512

LANES = 128
SUBLANES = 8
TM = 256
VMEM_LIMIT = 48 * 1024 * 1024
DMA_UNROLL = 8
DISPATCH_TILE = 3 * TM


def _cparams(*sem):
    return pltpu.CompilerParams(dimension_semantics=sem, vmem_limit_bytes=VMEM_LIMIT)


def _norm_mod(x, g, mod_ref, k):
    y = x * lax.rsqrt(jnp.mean(x * x, axis=-1, keepdims=True) + RMS_EPS) * g
    return y * (1.0 + mod_ref[k + 1:k + 2, :]) + mod_ref[k:k + 1, :]


def _log_sigmoid(x):
    return jnp.minimum(x, 0.0) - jnp.log(1.0 + jnp.exp(-jnp.abs(x)))


def _silu(x):
    return x * jax.nn.sigmoid(x)


def _mod_kernel(c_ref, w_ref, b_ref, o_ref):
    s = _silu(c_ref[...])
    o_ref[...] = jnp.dot(s.astype(BF16), w_ref[...].astype(BF16), preferred_element_type=F32) + b_ref[...]


def _ada_mods(cond, ada_w, ada_b):
    depth, d, _ = ada_w.shape
    rows = cond.shape[0]
    out = pl.pallas_call(
        _mod_kernel,
        grid=(depth, 6),
        in_specs=[pl.BlockSpec((rows, d), lambda i, n: (0, 0)),
                  pl.BlockSpec((None, d, d), lambda i, n: (i, 0, n)),
                  pl.BlockSpec((None, 1, d), lambda i, n: (i, 0, n))],
        out_specs=pl.BlockSpec((None, rows, d), lambda i, n: (i, 0, n)),
        out_shape=jax.ShapeDtypeStruct((depth, rows, 6 * d), F32),
        compiler_params=_cparams("arbitrary", "arbitrary"),
        name="ada_mods",
    )(cond, ada_w, ada_b.reshape(depth, 1, 6 * d))
    return out.reshape(depth, rows, 6, d)


def _mod_spec(nb):
    return pl.BlockSpec((None, 6, D_MODEL), lambda b, t: (jnp.where(t == 0, nb, b), 0, 0))


def _proj_call(kernel, x, mods, g, w, extras, out_cols, out_dtypes, name):
    nb, l, d = x.shape
    nt = l // TM
    n = w.shape[1]
    in_specs = [pl.BlockSpec((None, TM, d), lambda b, t: (b, t, 0)),
                _mod_spec(nb),
                pl.BlockSpec((1, d), lambda b, t: (0, 0)),
                pl.BlockSpec((d, n), lambda b, t: (0, 0))]
    args = [x, mods, g.reshape(1, d), w]
    for e, per_tile in extras:
        if per_tile:
            in_specs.append(pl.BlockSpec((TM, e.shape[1]), lambda b, t: (t, 0)))
        else:
            in_specs.append(pl.BlockSpec(e.shape, lambda b, t: (0, 0)))
        args.append(e)
    return pl.pallas_call(
        kernel,
        grid=(nb, nt),
        in_specs=in_specs,
        out_specs=[pl.BlockSpec((None, TM, c), lambda b, t: (b, t, 0)) for c in out_cols],
        out_shape=[jax.ShapeDtypeStruct((nb, l, c), dt) for c, dt in zip(out_cols, out_dtypes)],
        compiler_params=_cparams("arbitrary", "arbitrary"),
        name=name,
    )(*args)


def _proj_na_kernel(x_ref, mod_ref, g_ref, w_ref, qkg_ref, o_ref):
    h = _norm_mod(x_ref[...], g_ref[...], mod_ref, 0)
    u = jnp.dot(h.astype(BF16), w_ref[...], preferred_element_type=F32)
    nq = NA_HEADS * NA_HD
    for part in range(2):
        gain = qkg_ref[part:part + 1, :]
        if part == 0:
            gain = gain * (NA_HD ** -0.5)
        for hh in range(NA_HEADS):
            lo = part * nq + hh * NA_HD
            z = u[:, lo:lo + NA_HD]
            z = z * lax.rsqrt(jnp.mean(z * z, axis=-1, keepdims=True) + RMS_EPS) * gain
            o_ref[:, lo:lo + NA_HD] = z.astype(BF16)
    o_ref[:, 2 * nq:] = u[:, 2 * nq:].astype(BF16)


def _head64_rms(z, gain):
    lane = lax.broadcasted_iota(jnp.int32, z.shape, 1)
    lo = lane < SW_HD
    zz = z * z
    s_lo = jnp.sum(jnp.where(lo, zz, 0.0), axis=-1, keepdims=True)
    s_hi = jnp.sum(jnp.where(lo, 0.0, zz), axis=-1, keepdims=True)
    ms = jnp.where(lo, s_lo, s_hi) * (1.0 / SW_HD)
    return z * lax.rsqrt(ms + RMS_EPS) * gain


def _rope_slab(z, cs, sn):
    lane = lax.broadcasted_iota(jnp.int32, z.shape, 1)
    first = (lane % 32) < 16
    partner = jnp.where(first, pltpu.roll(z, LANES - 16, 1), pltpu.roll(z, 16, 1))
    return z * cs + partner * sn


def _proj_sw_kernel(x_ref, mod_ref, g_ref, w_ref, qkg_ref, cs_ref, sn_ref, q_ref, kv_ref):
    h = _norm_mod(x_ref[...], g_ref[...], mod_ref, 0)
    u = jnp.dot(h.astype(BF16), w_ref[...], preferred_element_type=F32)
    cs, sn = cs_ref[...], sn_ref[...]
    nq, nk = SW_HEADS * SW_HD, SW_KV * SW_HD
    scale = SW_HD ** -0.5
    for s in range(nq // LANES):
        z = _head64_rms(u[:, s * LANES:(s + 1) * LANES], qkg_ref[0:1, :])
        q_ref[:, s * LANES:(s + 1) * LANES] = (_rope_slab(z, cs, sn) * scale).astype(BF16)
    for s in range(nk // LANES):
        z = _head64_rms(u[:, nq + s * LANES:nq + (s + 1) * LANES], qkg_ref[1:2, :])
        kv_ref[:, s * LANES:(s + 1) * LANES] = _rope_slab(z, cs, sn).astype(BF16)
    kv_ref[:, nk:] = u[:, nq + nk:].astype(BF16)


def _proj_ml_kernel(x_ref, mod_ref, g_ref, w_ref, bg_ref, qkv_ref, og_ref, gt_ref):
    h = _norm_mod(x_ref[...], g_ref[...], mod_ref, 0)
    u = jnp.dot(h.astype(BF16), w_ref[...], preferred_element_type=F32)
    nq, nv = ML_HEADS * ML_DK, ML_HEADS * ML_DV
    qkv_ref[:, :nq] = (u[:, :nq] * (ML_DK ** -0.5)).astype(BF16)
    qkv_ref[:, nq:] = u[:, nq:2 * nq + nv].astype(BF16)
    og_ref[...] = jax.nn.sigmoid(u[:, 2 * nq + nv:2 * nq + 2 * nv]).astype(BF16)
    pre = u[:, 2 * nq + 2 * nv:] + bg_ref[...]
    pre = ML_CAP * jnp.tanh(pre / ML_CAP)
    lane = lax.broadcasted_iota(jnp.int32, pre.shape, 1)
    gt_ref[...] = jnp.where((lane % 2) == 1, _log_sigmoid(pre), pre)


def _proj_gl_kernel(x_ref, mod_ref, g_ref, w_ref, wa_ref, ba_ref, qkv_ref, gate_ref, la_ref):
    h = _norm_mod(x_ref[...], g_ref[...], mod_ref, 0)
    u = jnp.dot(h.astype(BF16), w_ref[...], preferred_element_type=F32)
    nq, nv = GL_HEADS * GL_DK, GL_HEADS * GL_DV
    qkv_ref[...] = u[:, :2 * nq + nv].astype(BF16)
    gate_ref[...] = _silu(u[:, 2 * nq + nv:2 * nq + 2 * nv]).astype(BF16)
    z = u[:, 2 * nq + 2 * nv:]
    a = jnp.dot(z.astype(BF16), wa_ref[...], preferred_element_type=F32) + ba_ref[...]
    la_ref[...] = _log_sigmoid(a) * (1.0 / GL_TAU)


def _na_bias_table(rpb, n_rows):
    cq = np.arange(GRID_W)[:, None]
    ck = np.arange(GRID_W)[None, :]
    c0 = np.clip(cq - NA_WIN_C // 2, 0, GRID_W - NA_WIN_C)
    col_ok = (ck >= c0) & (ck < c0 + NA_WIN_C)
    dc = np.clip(ck - cq + NA_WIN_C - 1, 0, 2 * NA_WIN_C - 2)
    pick = ((dc[None] == np.arange(2 * NA_WIN_C - 1)[:, None, None]) & col_ok[None]).astype(np.float32)
    m = jnp.einsum('hrd,dqk->hqrk', rpb.astype(F32), jnp.asarray(pick), precision=HIGHEST)
    m = jnp.where(jnp.asarray(col_ok)[None, :, None, :], m, NEG_INF)
    blocks = []
    for jj in range(n_rows // NA_QROWS):
        ws = min(max(NA_QROWS * jj - NA_WIN_R // 2, 0), n_rows - NA_KROWS)
        per_row = []
        for ri in range(NA_QROWS):
            r = NA_QROWS * jj + ri
            r0 = min(max(r - NA_WIN_R // 2, 0), n_rows - NA_WIN_R)
            lead = r0 - ws
            d0 = r0 - r + NA_WIN_R - 1
            piece = jnp.pad(m[:, :, d0:d0 + NA_WIN_R, :],
                            ((0, 0), (0, 0), (lead, NA_KROWS - NA_WIN_R - lead), (0, 0)), constant_values=NEG_INF)
            per_row.append(piece.reshape(rpb.shape[0], GRID_W, NA_KROWS * GRID_W))
        blocks.append(jnp.concatenate(per_row, axis=1))
    return jnp.stack(blocks, axis=1)


def _na_kernel(q_ref, k_ref, v_ref, bias_ref, o_ref, *, n_rows):
    j = pl.program_id(1)
    nb = q_ref.shape[0]
    nt = (((1,), (1,)), ((), ()))
    nkw = NA_KROWS * GRID_W

    @pl.when(j == 0)
    def _ctx():
        def body(b, c):
            q = q_ref[b]
            s = lax.dot_general(q, k_ref[b, 0:CTX_LEN, :], nt, preferred_element_type=F32)
            p = jnp.exp(s - jnp.max(s, axis=-1, keepdims=True))
            l = jnp.sum(p, axis=-1, keepdims=True)
            o = jnp.dot(p.astype(BF16), v_ref[b, 0:CTX_LEN, :], preferred_element_type=F32)
            o_ref[b] = (o / l).astype(BF16)
            return c
        lax.fori_loop(0, nb, body, 0, unroll=4)

    @pl.when(j > 0)
    def _lat():
        ws = jnp.clip(NA_QROWS * (j - 1) - NA_WIN_R // 2, 0, n_rows - NA_KROWS)
        start = pl.multiple_of(CTX_LEN + ws * GRID_W, GRID_W)
        bias = bias_ref[...]

        def body(b, c):
            q = q_ref[b]
            s_n = lax.dot_general(q, k_ref[b, pl.ds(start, nkw), :], nt, preferred_element_type=F32) + bias
            s_c = lax.dot_general(q, k_ref[b, 0:CTX_LEN, :], nt, preferred_element_type=F32)
            m = jnp.maximum(jnp.max(s_n, axis=-1, keepdims=True), jnp.max(s_c, axis=-1, keepdims=True))
            p_n = jnp.exp(s_n - m)
            p_c = jnp.exp(s_c - m)
            l = jnp.sum(p_n, axis=-1, keepdims=True) + jnp.sum(p_c, axis=-1, keepdims=True)
            o = (jnp.dot(p_n.astype(BF16), v_ref[b, pl.ds(start, nkw), :], preferred_element_type=F32)
                 + jnp.dot(p_c.astype(BF16), v_ref[b, 0:CTX_LEN, :], preferred_element_type=F32))
            o_ref[b] = (o / l).astype(BF16)
            return c
        lax.fori_loop(0, nb, body, 0, unroll=4)


def _na_attention(qkv, bias_tab):
    nb, l, _ = qkv.shape
    n_rows = (l - CTX_LEN) // GRID_W
    nt = l // TM
    hq = NA_HEADS
    return pl.pallas_call(
        functools.partial(_na_kernel, n_rows=n_rows),
        grid=(hq, nt),
        in_specs=[pl.BlockSpec((nb, TM, NA_HD), lambda h, j: (0, j, h)),
                  pl.BlockSpec((nb, l, NA_HD), lambda h, j: (0, 0, hq + h)),
                  pl.BlockSpec((nb, l, NA_HD), lambda h, j: (0, 0, 2 * hq + h)),
                  pl.BlockSpec((None, None, TM, NA_KROWS * GRID_W),
                               lambda h, j: (h, jnp.maximum(j - 1, 0), 0, 0))],
        out_specs=pl.BlockSpec((nb, TM, NA_HD), lambda h, j: (0, j, h)),
        out_shape=jax.ShapeDtypeStruct((nb, l, hq * NA_HD), BF16),
        compiler_params=_cparams("arbitrary", "arbitrary"),
        name="na_attention",
    )(qkv, qkv, qkv, bias_tab)


def _sw_kernel(sink_ref, q_ref, kv_ref, o_ref, *, seq):
    j = pl.program_id(1)
    nt = (((1,), (1,)), ((), ()))
    nkv = SW_KV * SW_HD
    kwin = TM + 2 * SW_WINDOW
    low = lax.broadcasted_iota(jnp.int32, (1, LANES), 1) < SW_HD

    def attend(segs):
        for pair in range(SW_KV // 2):
            kc = slice(pair * LANES, (pair + 1) * LANES)
            vc = slice(nkv + pair * LANES, nkv + (pair + 1) * LANES)
            ks = [kv_ref[rows, kc] for rows, _ in segs]
            vs = [kv_ref[rows, vc] for rows, _ in segs]
            one = jnp.ones((), BF16)
            v_half = [[jnp.where(low, v, one) for v in vs], [jnp.where(low, one, v) for v in vs]]
            for i in range(SW_HEADS // SW_KV):
                cols = slice((4 * pair + i) * LANES, (4 * pair + i + 1) * LANES)
                qs = q_ref[:, cols]
                outs = []
                for half in range(2):
                    sink = sink_ref[8 * pair + 4 * half + i]
                    qm = jnp.where(low if half == 0 else jnp.logical_not(low), qs, jnp.zeros((), BF16))
                    sc = []
                    for k, (_, mask) in zip(ks, segs):
                        s = lax.dot_general(qm, k, nt, preferred_element_type=F32)
                        sc.append(s if mask is None else jnp.where(mask, s, NEG_INF))
                    m = sink
                    for s in sc:
                        m = jnp.maximum(m, jnp.max(s, axis=-1, keepdims=True))
                    acc = None
                    for s, v in zip(sc, v_half[half]):
                        pv = jnp.dot(jnp.exp(s - m).astype(BF16), v, preferred_element_type=F32)
                        acc = pv if acc is None else acc + pv
                    denom = pltpu.roll(acc, SW_HD, 1) + jnp.exp(sink - m)
                    outs.append(acc / denom)
                o_ref[:, cols] = jnp.where(low, outs[0], outs[1]).astype(BF16)

    @pl.when(j == 0)
    def _ctx():
        attend([(slice(0, CTX_LEN), None)])

    @pl.when(j > 0)
    def _lat():
        q0 = (j - 1) * TM
        ws = jnp.clip(q0 - SW_WINDOW, 0, seq - kwin)
        start = pl.multiple_of(CTX_LEN + ws, SW_WINDOW)
        qpos = q0 + lax.broadcasted_iota(jnp.int32, (TM, kwin), 0)
        kpos = ws + lax.broadcasted_iota(jnp.int32, (TM, kwin), 1)
        ok = jnp.abs(qpos - kpos) <= SW_WINDOW
        attend([(pl.ds(start, kwin), ok), (slice(0, CTX_LEN), None)])


def _sw_attention(q, kv, sink):
    nb, l, nq = q.shape
    nt = l // TM
    return pl.pallas_call(
        functools.partial(_sw_kernel, seq=l - CTX_LEN),
        grid=(nb, nt),
        in_specs=[pl.BlockSpec(memory_space=pltpu.SMEM),
                  pl.BlockSpec((None, TM, nq), lambda b, j: (b, j, 0)),
                  pl.BlockSpec((None, l, kv.shape[2]), lambda b, j: (b, 0, 0))],
        out_specs=pl.BlockSpec((None, TM, nq), lambda b, j: (b, j, 0)),
        out_shape=jax.ShapeDtypeStruct((nb, l, nq), BF16),
        compiler_params=_cparams("arbitrary", "arbitrary"),
        name="sw_attention",
    )(sink, q, kv)


def _chunk_index(step, rev, n_ctx, n_all):
    if not rev:
        return step
    return jnp.where(step < n_ctx, n_ctx - 1 - step, n_all + n_ctx - 1 - step)


def _tri(n, rev):
    r = lax.broadcasted_iota(jnp.int32, (n, n), 0)
    c = lax.broadcasted_iota(jnp.int32, (n, n), 1)
    return (c >= r) if rev else (c <= r)


def _split3(x):
    hi = x.astype(BF16)
    r = x - hi.astype(F32)
    mid = r.astype(BF16)
    return hi, mid, (r - mid.astype(F32)).astype(BF16)


def _ml_kernel(q_ref, kt_ref, v_ref, og_ref, gr_ref, gc_ref, ng_ref, y_ref, acc_ref, *, n_ctx, n_all):
    lc = ML_SCAN
    hps = q_ref.shape[1] // ML_DK
    ones_v = jnp.ones((lc, ML_DV), BF16)
    lane_head = lax.broadcasted_iota(jnp.int32, (1, hps * ML_DK), 1) // ML_DK
    row_head = lax.broadcasted_iota(jnp.int32, (hps * ML_DK, 1), 0) // ML_DK

    def gate_forms(step, rev):
        c = _chunk_index(step, rev, n_ctx, n_all)
        r0 = pl.multiple_of(c * lc, lc)
        g_r = gr_ref[:, pl.ds(r0, lc)]
        g_c = pltpu.roll(gc_ref[pl.ds(r0, lc), :], (LANES - pl.program_id(1) * SUBLANES) % LANES, 1)
        cum3 = jnp.dot(jnp.concatenate(_split3(g_r), axis=0), _tri(lc, not rev).astype(BF16),
                       preferred_element_type=F32)
        cum_r = cum3[0:SUBLANES] + cum3[SUBLANES:2 * SUBLANES] + cum3[2 * SUBLANES:]
        tri_c = _tri(lc, rev).astype(BF16)
        cum_c = sum(jnp.dot(tri_c, part, preferred_element_type=F32) for part in _split3(g_c))
        return c, g_r, cum_r, cum_c

    def chain(forms, hl, rev, state):
        ct, m = state
        c, g_r, cum_r, cum_c = forms
        r0 = pl.multiple_of(c * lc, lc)
        base = hl * 4 + (2 if rev else 0)
        bf_r = cum_r[base + 1:base + 2, :]
        bf_c = cum_c[:, base + 1:base + 2]
        x_r = g_r[base:base + 1, :] - bf_r
        g = jnp.sum(g_r[base + 1:base + 2, :], axis=1, keepdims=True)
        tri = _tri(lc, rev)
        pm_c = jnp.max(jnp.where(tri, x_r, -jnp.inf), axis=1, keepdims=True)
        xmax = jnp.max(x_r, axis=1, keepdims=True)
        q = jnp.where(lane_head == hl, q_ref[pl.ds(r0, lc), :], jnp.zeros((), BF16))
        kt = kt_ref[:, pl.ds(r0, lc)]
        v1 = jnp.concatenate([v_ref[pl.ds(r0, lc), hl * ML_DV:(hl + 1) * ML_DV], ones_v], axis=1)
        mx = jnp.maximum(m, pm_c)
        e = jnp.exp(jnp.where(tri, x_r - mx, NEG_INF))
        sc = jnp.dot(q, kt, preferred_element_type=F32) * e
        e_int = jnp.exp(m - mx)
        intra = jnp.dot(sc.astype(BF16), v1, preferred_element_type=F32)
        inter = jnp.dot(q, ct.astype(BF16), preferred_element_type=F32)
        num = intra[:, :ML_DV] + e_int * inter[:, :ML_DV]
        den = intra[:, ML_DV:ML_DV + 1] + e_int * inter[:, ML_DV:ML_DV + 1]
        hout = num / jnp.maximum(jnp.abs(den), jnp.exp(-mx - bf_c))
        acc_ref[pl.ds(r0, lc), hl * ML_DV:(hl + 1) * ML_DV] += hout
        m_loc = g + xmax
        w_r = jnp.exp(x_r - xmax)
        ktw = jnp.where(row_head == hl, kt.astype(F32) * w_r, 0.0).astype(BF16)
        c_loc = jnp.dot(ktw, v1, preferred_element_type=F32)
        m_new = jnp.maximum(g + m, m_loc)
        dec = jnp.exp(g + m - m_new)
        inc = jnp.exp(m_loc - m_new)
        return dec * ct + inc * c_loc, m_new

    acc_ref[...] = jnp.zeros_like(acc_ref)
    chains = [(hl, rev) for hl in range(hps) for rev in (False, True)]
    init = tuple((jnp.zeros((hps * ML_DK, 2 * ML_DV), F32), jnp.zeros((1, 1), F32)) for _ in chains)

    def step_fn(step, states):
        forms = {rev: gate_forms(step, rev) for rev in (False, True)}
        return tuple(chain(forms[rev], hl, rev, st) for (hl, rev), st in zip(chains, states))

    lax.fori_loop(0, n_all, step_fn, init)

    for hl in range(hps):
        cols = slice(hl * ML_DV, (hl + 1) * ML_DV)
        z = acc_ref[:, cols]
        z = z * lax.rsqrt(jnp.mean(z * z, axis=-1, keepdims=True) + RMS_EPS) * ng_ref[:, cols]
        y_ref[:, cols] = (z.astype(BF16) * og_ref[:, cols])


def _ml_mixer(qkv, og, gates, norm_g):
    nb, l, _ = qkv.shape
    lc = ML_SCAN
    n_all, n_ctx = l // lc, CTX_LEN // lc
    hps = 2
    ngrp = ML_HEADS // hps
    nq = ML_HEADS * ML_DK
    g = gates[:, :, :4 * ML_HEADS].transpose(0, 2, 1)
    kt = qkv[:, :, nq:2 * nq].transpose(0, 2, 1)
    vb = hps * ML_DV
    return pl.pallas_call(
        functools.partial(_ml_kernel, n_ctx=n_ctx, n_all=n_all),
        grid=(nb, ngrp),
        in_specs=[pl.BlockSpec((None, l, hps * ML_DK), lambda b, h: (b, 0, h)),
                  pl.BlockSpec((None, hps * ML_DK, l), lambda b, h: (b, h, 0)),
                  pl.BlockSpec((None, l, vb), lambda b, h: (b, 0, 2 * nq // vb + h)),
                  pl.BlockSpec((None, l, vb), lambda b, h: (b, 0, h)),
                  pl.BlockSpec((None, hps * 4, l), lambda b, h: (b, h, 0)),
                  pl.BlockSpec((None, l, LANES), lambda b, h: (b, 0, 0)),
                  pl.BlockSpec((1, vb), lambda b, h: (0, h))],
        out_specs=pl.BlockSpec((None, l, vb), lambda b, h: (b, 0, h)),
        out_shape=jax.ShapeDtypeStruct((nb, l, ML_HEADS * ML_DV), BF16),
        scratch_shapes=[pltpu.VMEM((l, vb), F32)],
        compiler_params=_cparams("arbitrary", "arbitrary"),
        name="mlstm_mixer",
    )(qkv, kt, qkv, og, g, gates, norm_g.reshape(1, -1))


def _gl_kernel(q_ref, k_ref, v_ref, gate_ref, la0_ref, la1_ref, ng_ref, y_ref, acc_ref, st_ref, *, n_ctx, n_all):
    lc = GL_SCAN
    hps = q_ref.shape[1] // GL_DK
    t0 = (((0,), (0,)), ((), ()))
    nt = (((1,), (1,)), ((), ()))
    scale = GL_DK ** -0.5

    def chain(step, hl, rev):
        la_ref = la1_ref if rev else la0_ref
        sidx = 2 * hl + (1 if rev else 0)
        kc = slice(hl * GL_DK, (hl + 1) * GL_DK)
        vc = slice(hl * GL_DV, (hl + 1) * GL_DV)
        c = _chunk_index(step, rev, n_ctx, n_all)
        r0 = pl.multiple_of(c * lc, lc)
        tri = _tri(lc, rev)
        la = la_ref[pl.ds(r0, lc), kc]
        bc3 = jnp.dot(tri.astype(BF16), jnp.concatenate(_split3(la), axis=1), preferred_element_type=F32)
        bc = bc3[:, :GL_DK] + bc3[:, GL_DK:2 * GL_DK] + bc3[:, 2 * GL_DK:]
        g = jnp.sum(la, axis=0, keepdims=True)
        q = q_ref[pl.ds(r0, lc), kc].astype(F32) * scale
        k = k_ref[pl.ds(r0, lc), kc].astype(F32)
        v = v_ref[pl.ds(r0, lc), vc]
        st = st_ref[sidx]
        eg = jnp.exp(g)
        k_dec = k * jnp.exp(-bc)
        q_t = (q * jnp.exp(bc)).astype(BF16)
        k_t = k_dec.astype(BF16)
        att = jnp.where(tri, lax.dot_general(q_t, k_t, nt, preferred_element_type=F32), 0.0)
        o = (jnp.dot(att.astype(BF16), v, preferred_element_type=F32)
             + lax.dot_general(q_t, st.astype(BF16), nt, preferred_element_type=F32))
        acc_ref[pl.ds(r0, lc), vc] += o
        kd = (k_dec * eg).astype(BF16)
        s_loc = lax.dot_general(v, kd, t0, preferred_element_type=F32)
        st_ref[sidx] = st * eg + s_loc

    acc_ref[...] = jnp.zeros_like(acc_ref)
    st_ref[...] = jnp.zeros_like(st_ref)

    def step_fn(step, carry):
        for hl in range(hps):
            chain(step, hl, False)
            chain(step, hl, True)
        return carry

    lax.fori_loop(0, n_all, step_fn, 0)
    for hl in range(hps):
        vc = slice(hl * GL_DV, (hl + 1) * GL_DV)
        z = acc_ref[:, vc]
        z = z * lax.rsqrt(jnp.mean(z * z, axis=-1, keepdims=True) + RMS_EPS) * ng_ref[:, vc]
        y_ref[:, vc] = z.astype(BF16) * gate_ref[:, vc]


def _gl_mixer(qkv, gate, la, norm_g):
    nb, l, _ = qkv.shape
    lc = GL_SCAN
    n_all, n_ctx = l // lc, CTX_LEN // lc
    hps = 2
    ngrp = GL_HEADS // hps
    kb, vb = hps * GL_DK, hps * GL_DV
    nk = GL_HEADS * GL_DK
    return pl.pallas_call(
        functools.partial(_gl_kernel, n_ctx=n_ctx, n_all=n_all),
        grid=(nb, ngrp),
        in_specs=[pl.BlockSpec((None, l, kb), lambda b, h: (b, 0, h)),
                  pl.BlockSpec((None, l, kb), lambda b, h: (b, 0, nk // kb + h)),
                  pl.BlockSpec((None, l, vb), lambda b, h: (b, 0, 2 * nk // vb + h)),
                  pl.BlockSpec((None, l, vb), lambda b, h: (b, 0, h)),
                  pl.BlockSpec((None, l, kb), lambda b, h: (b, 0, h)),
                  pl.BlockSpec((None, l, kb), lambda b, h: (b, 0, nk // kb + h)),
                  pl.BlockSpec((1, vb), lambda b, h: (0, h))],
        out_specs=pl.BlockSpec((None, l, vb), lambda b, h: (b, 0, h)),
        out_shape=jax.ShapeDtypeStruct((nb, l, GL_HEADS * GL_DV), BF16),
        scratch_shapes=[pltpu.VMEM((l, vb), F32), pltpu.VMEM((2 * hps, GL_DV, GL_DK), F32)],
        compiler_params=_cparams("arbitrary", "arbitrary"),
        name="gla_mixer",
    )(qkv, qkv, qkv, gate, la, la, norm_g.reshape(1, -1))


def _store_token_tiles(ref, val):
    rows = val.shape[0]
    for s in range(val.shape[1] // LANES):
        ref[pl.ds(s, rows, stride=SUBLANES), :] = val[:, s * LANES:(s + 1) * LANES]


def _load_token_tiles(ref, rows):
    return jnp.concatenate([ref[pl.ds(s, rows, stride=SUBLANES), :] for s in range(SUBLANES)], axis=1)


def _post_kernel(y_ref, x_ref, mod_ref, wo_ref, g_ref, wr_ref, br_ref, xo_ref, f_ref, rt_ref, cnt_ref, run_ref):
    first = (pl.program_id(0) == 0) & (pl.program_id(1) == 0)

    @pl.when(first)
    def _():
        run_ref[...] = jnp.zeros_like(run_ref)

    o = jnp.dot(y_ref[...], wo_ref[...], preferred_element_type=F32)
    xn = x_ref[...] + mod_ref[2:3, :] * o
    xo_ref[...] = xn
    f = _norm_mod(xn, g_ref[...], mod_ref, 3)
    _store_token_tiles(f_ref, f)
    f_hi = f.astype(BF16)
    f_lo = (f - f_hi.astype(F32)).astype(BF16)
    hh = jnp.dot(f_hi, wr_ref[...], preferred_element_type=F32)
    lh = jnp.dot(f_lo, wr_ref[:, :LANES], preferred_element_type=F32)
    lg = hh[:, :LANES] + (hh[:, LANES:] + lh) + br_ref[...]
    lane = lax.broadcasted_iota(jnp.int32, lg.shape, 1).astype(F32)

    def top(mask):
        v = jnp.max(jnp.where(mask, lg, -jnp.inf), axis=-1, keepdims=True)
        return v, jnp.min(jnp.where(mask & (lg == v), lane, float(LANES)), axis=-1, keepdims=True)

    gm = lane < MOE_GROUPS
    mg, g_idx = top(gm)
    g_w = 1.0 / jnp.sum(jnp.where(gm, jnp.exp(lg - mg), 0.0), axis=-1, keepdims=True)
    lo = MOE_GROUPS + MOE_PER_GROUP * g_idx
    em = (lane >= lo) & (lane < lo + MOE_PER_GROUP)
    v0, i0 = top(em)
    v1, i1 = top(em & (lane != i0))
    e0, e1 = i0 - MOE_GROUPS, i1 - MOE_GROUPS
    t = jnp.exp(v1 - v0)
    w0 = g_w / (1.0 + t)
    w1 = g_w * t / (1.0 + t)
    oh = ((lane == e0) | (lane == e1)).astype(F32)
    rows = lg.shape[0]
    below = lax.broadcasted_iota(jnp.int32, (rows, rows), 0) > lax.broadcasted_iota(jnp.int32, (rows, rows), 1)
    tot = run_ref[...] + jnp.dot(below.astype(BF16), oh.astype(BF16), preferred_element_type=F32)
    r0 = jnp.sum(jnp.where(lane == e0, tot, 0.0), axis=-1, keepdims=True)
    r1 = jnp.sum(jnp.where(lane == e1, tot, 0.0), axis=-1, keepdims=True)
    run_ref[...] = run_ref[...] + jnp.sum(oh, axis=0, keepdims=True)
    cnt_ref[...] = jnp.broadcast_to(run_ref[...], cnt_ref.shape)
    out = jnp.zeros_like(lg)
    for i, val in enumerate((e0, e1, w0, w1, r0, r1)):
        out = jnp.where(lane == i, val, out)
    rt_ref[...] = out


def _post_call(y, x, mods, w_o, g_ffn, w_r, b_r):
    nb, l, d = x.shape
    nt = l // TM
    tile = lambda b, t: (b * nt + t, 0)
    return pl.pallas_call(
        _post_kernel,
        grid=(nb, nt),
        in_specs=[pl.BlockSpec((None, TM, d), lambda b, t: (b, t, 0)),
                  pl.BlockSpec((None, TM, d), lambda b, t: (b, t, 0)),
                  _mod_spec(nb),
                  pl.BlockSpec((d, d), lambda b, t: (0, 0)),
                  pl.BlockSpec((1, d), lambda b, t: (0, 0)),
                  pl.BlockSpec((d, 2 * LANES), lambda b, t: (0, 0)),
                  pl.BlockSpec((1, LANES), lambda b, t: (0, 0))],
        out_specs=[pl.BlockSpec((None, TM, d), lambda b, t: (b, t, 0)),
                   pl.BlockSpec((TM * SUBLANES, LANES), tile),
                   pl.BlockSpec((TM, LANES), tile),
                   pl.BlockSpec((SUBLANES, LANES), lambda b, t: (0, 0))],
        out_shape=[jax.ShapeDtypeStruct((nb, l, d), F32),
                   jax.ShapeDtypeStruct((nb * l * SUBLANES, LANES), F32),
                   jax.ShapeDtypeStruct((nb * l, LANES), F32),
                   jax.ShapeDtypeStruct((SUBLANES, LANES), F32)],
        scratch_shapes=[pltpu.VMEM((1, LANES), F32)],
        compiler_params=_cparams("arbitrary", "arbitrary"),
        name="outproj_router",
    )(y, x, mods, w_o, g_ffn.reshape(1, d), w_r, b_r)


def _dispatch_kernel(dest_ref, pend_ref, f_ref, xs_out, zbuf, sem):
    tile = f_ref.shape[0] // SUBLANES
    base = pl.program_id(0) * tile

    @pl.when(pl.program_id(0) == 0)
    def _():
        zbuf[...] = jnp.zeros_like(zbuf)

        def zero_block(blk):
            start = pl.multiple_of(blk * (MOE_BLOCK * SUBLANES), MOE_BLOCK * SUBLANES)
            return pltpu.make_async_copy(zbuf, xs_out.at[pl.ds(start, MOE_BLOCK * SUBLANES), :], sem)

        def has_rows(e):
            return pend_ref[e] > (0 if e == 0 else pend_ref[e - 1])

        for e in range(MOE_EXPERTS):
            pl.when(has_rows(e))(lambda e=e: zero_block(pend_ref[e] // MOE_BLOCK - 1).start())
        for e in range(MOE_EXPERTS):
            pl.when(has_rows(e))(lambda e=e: zero_block(pend_ref[e] // MOE_BLOCK - 1).wait())
        n_act = pend_ref[MOE_EXPERTS - 1] // MOE_BLOCK
        n_blk = xs_out.shape[0] // (MOE_BLOCK * SUBLANES)
        lax.fori_loop(n_act, n_blk, lambda b, c: (zero_block(b).start(), c)[1], 0)
        lax.fori_loop(n_act, n_blk, lambda b, c: (zero_block(b).wait(), c)[1], 0)

    def copy(t, k):
        d = dest_ref[2 * (base + t) + k]
        return pltpu.make_async_copy(f_ref.at[pl.ds(t * SUBLANES, SUBLANES), :],
                                     xs_out.at[pl.ds(d * SUBLANES, SUBLANES), :], sem)

    def issue(t, c):
        copy(t, 0).start()
        copy(t, 1).start(priority=1)
        return c

    lax.fori_loop(0, tile, issue, 0, unroll=DMA_UNROLL)
    for _ in range(2):
        pltpu.make_async_copy(f_ref, xs_out.at[pl.ds(0, tile * SUBLANES), :], sem).wait()


def _dispatch(dest, p_end, f_tiles, n_slots):
    n_tok = f_tiles.shape[0] // SUBLANES
    tile = DISPATCH_TILE
    assert n_tok % tile == 0
    return pl.pallas_call(
        _dispatch_kernel,
        grid_spec=pltpu.PrefetchScalarGridSpec(
            num_scalar_prefetch=2,
            grid=(n_tok // tile,),
            in_specs=[pl.BlockSpec((tile * SUBLANES, LANES), lambda i, dr, pe: (i, 0))],
            out_specs=pl.BlockSpec(memory_space=pl.ANY),
            scratch_shapes=[pltpu.VMEM((MOE_BLOCK * SUBLANES, LANES), F32), pltpu.SemaphoreType.DMA(())]),
        out_shape=jax.ShapeDtypeStruct((n_slots * SUBLANES, LANES), F32),
        compiler_params=_cparams("arbitrary"),
        name="moe_dispatch",
    )(dest, p_end, f_tiles)


def _expert_kernel(blk_e_ref, nact_ref, nxt_ref, slot_ref, xs_ref, wg_hbm, wu_hbm, wd_hbm, ys_ref,
                   wg_f, wu_f, wd_f, wg_s, wu_s, wd_s, sem, *, layer):
    i = pl.program_id(0)
    active = i < nact_ref[0]
    e = blk_e_ref[i]
    changed = (i == 0) | (e != blk_e_ref[jnp.maximum(i - 1, 0)])
    slot = slot_ref[i]

    def fetch(expert, s):
        return [pltpu.make_async_copy(w.at[layer, expert], buf.at[s], sem.at[s, k])
                for k, (w, buf) in enumerate(((wg_hbm, wg_f), (wu_hbm, wu_f), (wd_hbm, wd_f)))]

    @pl.when(i == 0)
    def _():
        for c in fetch(e, slot):
            c.start()

    @pl.when(active & changed)
    def _():
        nxt = nxt_ref[i]

        @pl.when(nxt >= 0)
        def _():
            for c in fetch(nxt, 1 - slot):
                c.start()

        for c in fetch(e, slot):
            c.wait()
        wg_s[...] = wg_f[slot].astype(BF16)
        wu_s[...] = wu_f[slot].astype(BF16)
        wd_s[...] = wd_f[slot].astype(BF16)

    @pl.when(active)
    def _():
        x = _load_token_tiles(xs_ref, MOE_BLOCK).astype(BF16)
        a = jnp.dot(x, wg_s[...], preferred_element_type=F32)
        u = jnp.dot(x, wu_s[...], preferred_element_type=F32)
        y = jnp.dot((_silu(a) * u).astype(BF16), wd_s[...], preferred_element_type=F32)
        _store_token_tiles(ys_ref, y)

    @pl.when(jnp.logical_not(active))
    def _():
        ys_ref[...] = jnp.zeros_like(ys_ref)


def _experts(blk_e, nact, nxt_e, slot, xs, w_gate, w_up, w_down, layer):
    n_blk = blk_e.shape[0]
    _, _, d, ff = w_gate.shape
    rows = MOE_BLOCK * SUBLANES
    last = lambda i, na: jnp.minimum(i, jnp.maximum(na[0] - 1, 0))
    hbm = pl.BlockSpec(memory_space=pl.ANY)
    return pl.pallas_call(
        functools.partial(_expert_kernel, layer=layer),
        grid_spec=pltpu.PrefetchScalarGridSpec(
            num_scalar_prefetch=4,
            grid=(n_blk,),
            in_specs=[pl.BlockSpec((rows, LANES), lambda i, be, na, nx, sl: (last(i, na), 0)), hbm, hbm, hbm],
            out_specs=pl.BlockSpec((rows, LANES), lambda i, be, na, nx, sl: (i, 0)),
            scratch_shapes=[pltpu.VMEM((2, d, ff), F32), pltpu.VMEM((2, d, ff), F32), pltpu.VMEM((2, ff, d), F32),
                            pltpu.VMEM((d, ff), BF16), pltpu.VMEM((d, ff), BF16), pltpu.VMEM((ff, d), BF16),
                            pltpu.SemaphoreType.DMA((2, 3))]),
        out_shape=jax.ShapeDtypeStruct(xs.shape, F32),
        compiler_params=_cparams("arbitrary"),
        name="moe_experts",
    )(blk_e, nact, nxt_e, slot, xs, w_gate, w_up, w_down)


def _combine_kernel(dest_ref, ys_hbm, x_ref, rt_ref, mod_ref, o_ref, ybuf, sem, *, nt, t0):
    i = pl.program_id(0)
    n = pl.num_programs(0)
    nte = nt - t0

    def gather(step, slot, start):
        base = ((step // nte) * nt + step % nte + t0) * TM

        def copy(t, k):
            d = dest_ref[2 * (base + t) + k]
            return pltpu.make_async_copy(ys_hbm.at[pl.ds(d * SUBLANES, SUBLANES), :],
                                         ybuf.at[slot, k, pl.ds(t * SUBLANES, SUBLANES), :], sem.at[slot])

        def issue(t, c):
            copy(t, 0).start()
            copy(t, 1).start(priority=1)
            return c

        if start:
            lax.fori_loop(0, TM, issue, 0, unroll=DMA_UNROLL)
        else:
            for k in range(2):
                pltpu.make_async_copy(ys_hbm.at[pl.ds(0, TM * SUBLANES), :], ybuf.at[slot, k], sem.at[slot]).wait()

    slot = i % 2
    pl.when(i == 0)(lambda: gather(i, slot, True))
    pl.when(i + 1 < n)(lambda: gather(i + 1, 1 - slot, True))
    gather(i, slot, False)
    y0 = _load_token_tiles(ybuf.at[slot, 0], TM)
    y1 = _load_token_tiles(ybuf.at[slot, 1], TM)
    rt = rt_ref[...]
    y = rt[:, 2:3] * y0 + rt[:, 3:4] * y1
    o_ref[...] = x_ref[...] + mod_ref[5:6, :] * y


def _combine(dest, ys, x, rt, mods, skip_ctx):
    nb, l, d = x.shape
    nt = l // TM
    t0 = 1 if skip_ctx else 0
    nte = nt - t0
    bt = lambda i: (i // nte, i % nte + t0)
    mod_row = (lambda b, t: b) if skip_ctx else (lambda b, t: jnp.where(t == 0, nb, b))
    return pl.pallas_call(
        functools.partial(_combine_kernel, nt=nt, t0=t0),
        grid_spec=pltpu.PrefetchScalarGridSpec(
            num_scalar_prefetch=1,
            grid=(nb * nte,),
            in_specs=[pl.BlockSpec(memory_space=pl.ANY),
                      pl.BlockSpec((None, TM, d), lambda i, dr: (*bt(i), 0)),
                      pl.BlockSpec((TM, LANES), lambda i, dr: (bt(i)[0] * nt + bt(i)[1], 0)),
                      pl.BlockSpec((None, 6, d), lambda i, dr: (mod_row(*bt(i)), 0, 0))],
            out_specs=pl.BlockSpec((None, TM, d), lambda i, dr: (i // nte, i % nte, 0)),
            scratch_shapes=[pltpu.VMEM((2, 2, TM * SUBLANES, LANES), F32), pltpu.SemaphoreType.DMA((2,))]),
        out_shape=jax.ShapeDtypeStruct((nb, nte * TM, d), F32),
        compiler_params=_cparams("arbitrary"),
        name="moe_combine",
    )(dest, ys, x, rt, mods)


def _moe(y_mix, x, mods, w_o, g_ffn, w_grp, b_grp, w_exp, b_exp, w_gate, w_up, w_down, layer, skip_ctx):
    nb, l, d = x.shape
    n_tok = nb * l
    nr = MOE_GROUPS + MOE_EXPERTS
    w_r = jnp.pad(jnp.concatenate([w_grp, w_exp], axis=1), ((0, 0), (0, LANES - nr)))
    b_r = jnp.pad(jnp.concatenate([b_grp, b_exp]), (0, LANES - nr)).reshape(1, LANES)
    w_hi = w_r.astype(BF16)
    w_r = jnp.concatenate([w_hi, (w_r - w_hi.astype(F32)).astype(BF16)], axis=1)
    x_new, f_tiles, rt, cnt = _post_call(y_mix, x, mods, w_o, g_ffn, w_r, b_r)
    counts = cnt[0, :MOE_EXPERTS].astype(jnp.int32)
    padded = (counts + MOE_BLOCK - 1) // MOE_BLOCK * MOE_BLOCK
    p_end = jnp.cumsum(padded)
    p_start = p_end - padded
    n_pairs = 2 * n_tok
    n_blk = -(-(n_pairs + MOE_EXPERTS * (MOE_BLOCK - 1)) // MOE_BLOCK)
    eid = rt[:, 0:2].astype(jnp.int32)
    first = jnp.sum(jnp.where(eid[:, :, None] == jnp.arange(MOE_EXPERTS, dtype=jnp.int32), p_start, 0), axis=-1)
    dest = (first + rt[:, 4:6].astype(jnp.int32)).reshape(-1)
    blk_row = jnp.arange(n_blk, dtype=jnp.int32) * MOE_BLOCK
    blk_e = jnp.minimum(jnp.sum((p_end[None, :] <= blk_row[:, None]).astype(jnp.int32), axis=1), MOE_EXPERTS - 1)
    nact = (p_end[-1:] // MOE_BLOCK).astype(jnp.int32)
    xs = _dispatch(dest, p_end.astype(jnp.int32), f_tiles, n_blk * MOE_BLOCK)
    has = padded > 0
    ids = jnp.arange(MOE_EXPERTS, dtype=jnp.int32)
    rank = jnp.cumsum(has.astype(jnp.int32)) - has.astype(jnp.int32)
    later = jnp.where((ids[None, :] > ids[:, None]) & has[None, :], ids[None, :], MOE_EXPERTS)
    nxt = jnp.min(later, axis=1)
    nxt = jnp.where(nxt == MOE_EXPERTS, -1, nxt)
    sel = blk_e[:, None] == ids[None, :]
    slot = jnp.sum(jnp.where(sel, rank[None, :] % 2, 0), axis=1).astype(jnp.int32)
    nxt_e = jnp.sum(jnp.where(sel, nxt[None, :], 0), axis=1).astype(jnp.int32)
    ys = _experts(blk_e, nact, nxt_e, slot, xs, w_gate, w_up, w_down, layer)
    return _combine(dest, ys, x_new, rt, mods, skip_ctx)


def _rope_tables(seq):
    nf = SW_HD // 4
    inv = ROPE_BASE ** (-jnp.arange(nf, dtype=F32) / nf)
    pos = jnp.arange(seq, dtype=jnp.int32)
    rows, cols = (pos // GRID_W).astype(F32), (pos % GRID_W).astype(F32)
    lane = jnp.arange(LANES)
    p = jnp.where((lane % SW_HD < SW_HD // 2)[None, :], rows[:, None], cols[:, None])
    ang = p * inv[lane % nf][None, :]
    sign = jnp.where((lane % (2 * nf)) < nf, -1.0, 1.0)[None, :]
    cs = jnp.concatenate([jnp.ones((CTX_LEN, LANES), F32), jnp.cos(ang)], axis=0)
    sn = jnp.concatenate([jnp.zeros((CTX_LEN, LANES), F32), jnp.sin(ang) * sign], axis=0)
    return cs, sn


def _pad_cols(w, n):
    return jnp.pad(w, ((0, 0), (0, n - w.shape[1])))


def kernel(x, c, ctx, c_ctx, ada_w, ada_b, norm_mix_g, norm_ffn_g, na_w_qkv, na_qk_g, na_rpb, na_w_o, ml_w_in, ml_b_gates, ml_norm_g, ml_w_o, sw_w_qkv, sw_qk_g, sw_sink, sw_w_o, gl_w_in, gl_w_a2, gl_b_a, gl_norm_g, gl_w_o, moe_w_grp, moe_b_grp, moe_w_exp, moe_b_exp, moe_w_gate, moe_w_up, moe_w_down):
    nb, seq, d = x.shape
    depth = ada_w.shape[0]
    assert d == D_MODEL and ctx.shape[1] == CTX_LEN == TM and seq % TM == 0
    rows = -(-(nb + 1) // SUBLANES) * SUBLANES
    cond = jnp.pad(jnp.concatenate([c, c_ctx[None]], axis=0), ((0, rows - nb - 1), (0, 0)))
    mods_all = _ada_mods(cond, ada_w, ada_b)
    xa = jnp.concatenate([ctx, x], axis=1)
    for i in range(depth):
        j, kind = divmod(i, 4)
        mods = mods_all[i]
        g_mix = norm_mix_g[i]
        if kind == 0:
            qkg = na_qk_g[j]
            (qkv,) = _proj_call(_proj_na_kernel, xa, mods, g_mix, na_w_qkv[j].astype(BF16), [(qkg, False)],
                                [3 * NA_HEADS * NA_HD], [BF16], "proj_na")
            y = _na_attention(qkv, _na_bias_table(na_rpb[j], seq // GRID_W))
            w_o = na_w_o[j]
        elif kind == 1:
            n_in = -(-ml_w_in.shape[2] // LANES) * LANES
            ng = 4 * ML_HEADS
            w_in = ml_w_in[j]
            w_g = w_in[:, -ng:].reshape(d, 2, 2, ML_HEADS).transpose(0, 3, 1, 2).reshape(d, ng)
            w_in = jnp.concatenate([w_in[:, :-ng], w_g], axis=1)
            bg = ml_b_gates[j].reshape(2, 2, ML_HEADS).transpose(2, 0, 1).reshape(ng)
            bg = jnp.pad(bg, (0, LANES - ng)).reshape(1, LANES)
            qkv, og, gates = _proj_call(_proj_ml_kernel, xa, mods, g_mix, _pad_cols(w_in, n_in).astype(BF16),
                                        [(bg, False)], [2 * ML_HEADS * ML_DK + ML_HEADS * ML_DV, ML_HEADS * ML_DV, LANES],
                                        [BF16, BF16, F32], "proj_ml")
            y = _ml_mixer(qkv, og, gates, ml_norm_g[j])
            w_o = ml_w_o[j]
        elif kind == 2:
            qkg = jnp.concatenate([sw_qk_g[j], sw_qk_g[j]], axis=1)
            cs, sn = _rope_tables(seq)
            nqc = SW_HEADS * SW_HD
            w_in = sw_w_qkv[j]
            w_q = w_in[:, :nqc].reshape(d, 2, 2, 4, SW_HD).transpose(0, 1, 3, 2, 4).reshape(d, nqc)
            w_in = jnp.concatenate([w_q, w_in[:, nqc:]], axis=1)
            q, kv = _proj_call(_proj_sw_kernel, xa, mods, g_mix, w_in.astype(BF16),
                               [(qkg, False), (cs, True), (sn, True)],
                               [nqc, 2 * SW_KV * SW_HD], [BF16, BF16], "proj_sw")
            y = _sw_attention(q, kv, sw_sink[j])
            w_o = sw_w_o[j].reshape(2, 2, 4, SW_HD, d).transpose(0, 2, 1, 3, 4).reshape(nqc, d)
        else:
            n_in = -(-gl_w_in.shape[2] // LANES) * LANES
            nk = GL_HEADS * GL_DK
            wa = jnp.zeros((LANES, 2 * nk), F32)
            wa = wa.at[:GL_RANK, :nk].set(gl_w_a2[j, 0]).at[GL_RANK:2 * GL_RANK, nk:].set(gl_w_a2[j, 1])
            ba = gl_b_a[j].reshape(1, 2 * nk)
            qkv, gate, la = _proj_call(_proj_gl_kernel, xa, mods, g_mix, _pad_cols(gl_w_in[j], n_in).astype(BF16),
                                       [(wa.astype(BF16), False), (ba, False)],
                                       [2 * nk + GL_HEADS * GL_DV, GL_HEADS * GL_DV, 2 * nk], [BF16, BF16, F32], "proj_gl")
            y = _gl_mixer(qkv, gate, la, gl_norm_g[j])
            w_o = gl_w_o[j]
        xa = _moe(y, xa, mods, w_o.astype(BF16), norm_ffn_g[i], moe_w_grp[i], moe_b_grp[i], moe_w_exp[i], moe_b_exp[i],
                  moe_w_gate, moe_w_up, moe_w_down, i, i == depth - 1)
    return xa
```

```python
import functools

import jax
import jax.numpy as jnp
import numpy as np
from jax import lax
from jax.experimental import pallas as pl
from jax.experimental.pallas import tpu as pltpu

F32 = jnp.float32
BF16 = jnp.bfloat16
HIGHEST = lax.Precision.HIGHEST

D_MODEL = 1024
CTX_LEN = 256
GRID_W = 64
RMS_EPS = 1e-6
NEG_INF = -1e30
ROPE_BASE = 10000.0

NA_HEADS, NA_HD, NA_WIN_R, NA_WIN_C = 8, 128, 8, 16
NA_QROWS = 4
NA_KROWS = NA_QROWS + NA_WIN_R - 1
ML_HEADS, ML_DK, ML_DV, ML_CAP = 8, 64, 128, 15.0
ML_SCAN = 256
SW_HEADS, SW_KV, SW_HD, SW_WINDOW = 16, 4, 64, 128
GL_HEADS, GL_DK, GL_DV, GL_RANK, GL_TAU = 4, 128, 256, 16, 16.0
GL_SCAN = 128
MOE_GROUPS, MOE_PER_GROUP, MOE_EXPERTS, MOE_FF = 4, 8, 32, 512
MOE_BLOCK = 512

LANES = 128
SUBLANES = 8
TM = 256
PAIR = 2
VMEM_LIMIT = 48 * 1024 * 1024
DMA_UNROLL = 8
DISPATCH_TILE = 3 * TM


def _cparams(*sem):
    return pltpu.CompilerParams(dimension_semantics=sem, vmem_limit_bytes=VMEM_LIMIT)


def _norm_mod(x, g, mod_ref, k):
    y = x * lax.rsqrt(jnp.mean(x * x, axis=-1, keepdims=True) + RMS_EPS) * g
    return y * (1.0 + mod_ref[k + 1:k + 2, :]) + mod_ref[k:k + 1, :]


def _log_sigmoid(x):
    return jnp.minimum(x, 0.0) - jnp.log(1.0 + jnp.exp(-jnp.abs(x)))


def _silu(x):
    return x * jax.nn.sigmoid(x)


def _mod_kernel(c_ref, w_ref, b_ref, o_ref):
    s = _silu(c_ref[...])
    o_ref[...] = jnp.dot(s.astype(BF16), w_ref[...].astype(BF16), preferred_element_type=F32) + b_ref[...]


def _ada_mods(cond, ada_w, ada_b):
    depth, d, _ = ada_w.shape
    rows = cond.shape[0]
    out = pl.pallas_call(
        _mod_kernel,
        grid=(depth, 6),
        in_specs=[pl.BlockSpec((rows, d), lambda i, n: (0, 0)),
                  pl.BlockSpec((None, d, d), lambda i, n: (i, 0, n)),
                  pl.BlockSpec((None, 1, d), lambda i, n: (i, 0, n))],
        out_specs=pl.BlockSpec((None, rows, d), lambda i, n: (i, 0, n)),
        out_shape=jax.ShapeDtypeStruct((depth, rows, 6 * d), F32),
        compiler_params=_cparams("arbitrary", "arbitrary"),
        name="ada_mods",
    )(cond, ada_w, ada_b.reshape(depth, 1, 6 * d))
    return out.reshape(depth, rows, 6, d)


def _mod_spec(nb):
    return pl.BlockSpec((PAIR, 6, D_MODEL), lambda b, t: (jnp.where(t == 0, nb // PAIR, b), 0, 0))


def _paired(body, paired, *refs):
    for sub in range(PAIR):
        body(*[r.at[sub] if p else r for r, p in zip(refs, paired)])


def _proj_call(kernel, x, mods, g, w, extras, out_cols, out_dtypes, name):
    nb, l, d = x.shape
    nt = l // TM
    n = w.shape[1]
    in_specs = [pl.BlockSpec((PAIR, TM, d), lambda b, t: (b, t, 0)),
                _mod_spec(nb),
                pl.BlockSpec((1, d), lambda b, t: (0, 0)),
                pl.BlockSpec((d, n), lambda b, t: (0, 0))]
    args = [x, mods, g.reshape(1, d), w]
    for e, per_tile in extras:
        if per_tile:
            in_specs.append(pl.BlockSpec((TM, e.shape[1]), lambda b, t: (t, 0)))
        else:
            in_specs.append(pl.BlockSpec(e.shape, lambda b, t: (0, 0)))
        args.append(e)
    paired = (True, True) + (False,) * (len(in_specs) - 2) + (True,) * len(out_cols)
    return pl.pallas_call(
        functools.partial(_paired, kernel, paired),
        grid=(nb // PAIR, nt),
        in_specs=in_specs,
        out_specs=[pl.BlockSpec((PAIR, TM, c), lambda b, t: (b, t, 0)) for c in out_cols],
        out_shape=[jax.ShapeDtypeStruct((nb, l, c), dt) for c, dt in zip(out_cols, out_dtypes)],
        compiler_params=_cparams("arbitrary", "arbitrary"),
        name=name,
    )(*args)


def _proj_na_kernel(x_ref, mod_ref, g_ref, w_ref, qkg_ref, o_ref):
    h = _norm_mod(x_ref[...], g_ref[...], mod_ref, 0)
    u = jnp.dot(h.astype(BF16), w_ref[...], preferred_element_type=F32)
    nq = NA_HEADS * NA_HD
    for part in range(2):
        gain = qkg_ref[part:part + 1, :]
        if part == 0:
            gain = gain * (NA_HD ** -0.5)
        for hh in range(NA_HEADS):
            lo = part * nq + hh * NA_HD
            z = u[:, lo:lo + NA_HD]
            z = z * lax.rsqrt(jnp.mean(z * z, axis=-1, keepdims=True) + RMS_EPS) * gain
            o_ref[:, lo:lo + NA_HD] = z.astype(BF16)
    o_ref[:, 2 * nq:] = u[:, 2 * nq:].astype(BF16)


def _head64_rms(z, gain):
    lane = lax.broadcasted_iota(jnp.int32, z.shape, 1)
    lo = lane < SW_HD
    zz = z * z
    s_lo = jnp.sum(jnp.where(lo, zz, 0.0), axis=-1, keepdims=True)
    s_hi = jnp.sum(jnp.where(lo, 0.0, zz), axis=-1, keepdims=True)
    ms = jnp.where(lo, s_lo, s_hi) * (1.0 / SW_HD)
    return z * lax.rsqrt(ms + RMS_EPS) * gain


def _rope_slab(z, cs, sn):
    lane = lax.broadcasted_iota(jnp.int32, z.shape, 1)
    first = (lane % 32) < 16
    partner = jnp.where(first, pltpu.roll(z, LANES - 16, 1), pltpu.roll(z, 16, 1))
    return z * cs + partner * sn


def _proj_sw_kernel(x_ref, mod_ref, g_ref, w_ref, qkg_ref, cs_ref, sn_ref, q_ref, kv_ref):
    h = _norm_mod(x_ref[...], g_ref[...], mod_ref, 0)
    u = jnp.dot(h.astype(BF16), w_ref[...], preferred_element_type=F32)
    cs, sn = cs_ref[...], sn_ref[...]
    nq, nk = SW_HEADS * SW_HD, SW_KV * SW_HD
    scale = SW_HD ** -0.5
    for s in range(nq // LANES):
        z = _head64_rms(u[:, s * LANES:(s + 1) * LANES], qkg_ref[0:1, :])
        q_ref[:, s * LANES:(s + 1) * LANES] = (_rope_slab(z, cs, sn) * scale).astype(BF16)
    for s in range(nk // LANES):
        z = _head64_rms(u[:, nq + s * LANES:nq + (s + 1) * LANES], qkg_ref[1:2, :])
        kv_ref[:, s * LANES:(s + 1) * LANES] = _rope_slab(z, cs, sn).astype(BF16)
    kv_ref[:, nk:] = u[:, nq + nk:].astype(BF16)


def _proj_ml_kernel(x_ref, mod_ref, g_ref, w_ref, bg_ref, qkv_ref, og_ref, gt_ref):
    h = _norm_mod(x_ref[...], g_ref[...], mod_ref, 0)
    u = jnp.dot(h.astype(BF16), w_ref[...], preferred_element_type=F32)
    nq, nv = ML_HEADS * ML_DK, ML_HEADS * ML_DV
    qkv_ref[:, :nq] = (u[:, :nq] * (ML_DK ** -0.5)).astype(BF16)
    qkv_ref[:, nq:] = u[:, nq:2 * nq + nv].astype(BF16)
    og_ref[...] = jax.nn.sigmoid(u[:, 2 * nq + nv:2 * nq + 2 * nv]).astype(BF16)
    pre = u[:, 2 * nq + 2 * nv:] + bg_ref[...]
    pre = ML_CAP * jnp.tanh(pre / ML_CAP)
    lane = lax.broadcasted_iota(jnp.int32, pre.shape, 1)
    gt_ref[...] = jnp.where((lane % 2) == 1, _log_sigmoid(pre), pre)


def _proj_gl_kernel(x_ref, mod_ref, g_ref, w_ref, wa_ref, ba_ref, qkv_ref, gate_ref, la_ref):
    h = _norm_mod(x_ref[...], g_ref[...], mod_ref, 0)
    u = jnp.dot(h.astype(BF16), w_ref[...], preferred_element_type=F32)
    nq, nv = GL_HEADS * GL_DK, GL_HEADS * GL_DV
    qkv_ref[...] = u[:, :2 * nq + nv].astype(BF16)
    gate_ref[...] = _silu(u[:, 2 * nq + nv:2 * nq + 2 * nv]).astype(BF16)
    z = u[:, 2 * nq + 2 * nv:]
    a = jnp.dot(z.astype(BF16), wa_ref[...], preferred_element_type=F32) + ba_ref[...]
    la_ref[...] = _log_sigmoid(a) * (1.0 / GL_TAU)


def _na_bias_table(rpb, n_rows):
    cq = np.arange(GRID_W)[:, None]
    ck = np.arange(GRID_W)[None, :]
    c0 = np.clip(cq - NA_WIN_C // 2, 0, GRID_W - NA_WIN_C)
    col_ok = (ck >= c0) & (ck < c0 + NA_WIN_C)
    dc = np.clip(ck - cq + NA_WIN_C - 1, 0, 2 * NA_WIN_C - 2)
    pick = ((dc[None] == np.arange(2 * NA_WIN_C - 1)[:, None, None]) & col_ok[None]).astype(np.float32)
    m = jnp.einsum('hrd,dqk->hqrk', rpb.astype(F32), jnp.asarray(pick), precision=HIGHEST)
    m = jnp.where(jnp.asarray(col_ok)[None, :, None, :], m, NEG_INF)
    blocks = []
    for jj in range(n_rows // NA_QROWS):
        ws = min(max(NA_QROWS * jj - NA_WIN_R // 2, 0), n_rows - NA_KROWS)
        per_row = []
        for ri in range(NA_QROWS):
            r = NA_QROWS * jj + ri
            r0 = min(max(r - NA_WIN_R // 2, 0), n_rows - NA_WIN_R)
            lead = r0 - ws
            d0 = r0 - r + NA_WIN_R - 1
            piece = jnp.pad(m[:, :, d0:d0 + NA_WIN_R, :],
                            ((0, 0), (0, 0), (lead, NA_KROWS - NA_WIN_R - lead), (0, 0)), constant_values=NEG_INF)
            per_row.append(piece.reshape(rpb.shape[0], GRID_W, NA_KROWS * GRID_W))
        blocks.append(jnp.concatenate(per_row, axis=1))
    return jnp.stack(blocks, axis=1)


def _na_kernel(q_ref, k_ref, v_ref, bias_ref, o_ref, *, n_rows):
    j = pl.program_id(1)
    nb = q_ref.shape[0]
    nt = (((1,), (1,)), ((), ()))
    nkw = NA_KROWS * GRID_W

    @pl.when(j == 0)
    def _ctx():
        def body(b, c):
            q = q_ref[b]
            s = lax.dot_general(q, k_ref[b, 0:CTX_LEN, :], nt, preferred_element_type=F32)
            p = jnp.exp(s - jnp.max(s, axis=-1, keepdims=True))
            l = jnp.sum(p, axis=-1, keepdims=True)
            o = jnp.dot(p.astype(BF16), v_ref[b, 0:CTX_LEN, :], preferred_element_type=F32)
            o_ref[b] = (o / l).astype(BF16)
            return c
        lax.fori_loop(0, nb, body, 0, unroll=4)

    @pl.when(j > 0)
    def _lat():
        ws = jnp.clip(NA_QROWS * (j - 1) - NA_WIN_R // 2, 0, n_rows - NA_KROWS)
        start = pl.multiple_of(CTX_LEN + ws * GRID_W, GRID_W)
        bias = bias_ref[...]

        def body(b, c):
            q = q_ref[b]
            s_n = lax.dot_general(q, k_ref[b, pl.ds(start, nkw), :], nt, preferred_element_type=F32) + bias
            s_c = lax.dot_general(q, k_ref[b, 0:CTX_LEN, :], nt, preferred_element_type=F32)
            m = jnp.maximum(jnp.max(s_n, axis=-1, keepdims=True), jnp.max(s_c, axis=-1, keepdims=True))
            p_n = jnp.exp(s_n - m)
            p_c = jnp.exp(s_c - m)
            l = jnp.sum(p_n, axis=-1, keepdims=True) + jnp.sum(p_c, axis=-1, keepdims=True)
            o = (jnp.dot(p_n.astype(BF16), v_ref[b, pl.ds(start, nkw), :], preferred_element_type=F32)
                 + jnp.dot(p_c.astype(BF16), v_ref[b, 0:CTX_LEN, :], preferred_element_type=F32))
            o_ref[b] = (o / l).astype(BF16)
            return c
        lax.fori_loop(0, nb, body, 0, unroll=4)


def _na_attention(qkv, bias_tab):
    nb, l, _ = qkv.shape
    n_rows = (l - CTX_LEN) // GRID_W
    nt = l // TM
    hq = NA_HEADS
    return pl.pallas_call(
        functools.partial(_na_kernel, n_rows=n_rows),
        grid=(hq, nt),
        in_specs=[pl.BlockSpec((nb, TM, NA_HD), lambda h, j: (0, j, h)),
                  pl.BlockSpec((nb, l, NA_HD), lambda h, j: (0, 0, hq + h)),
                  pl.BlockSpec((nb, l, NA_HD), lambda h, j: (0, 0, 2 * hq + h)),
                  pl.BlockSpec((None, None, TM, NA_KROWS * GRID_W),
                               lambda h, j: (h, jnp.maximum(j - 1, 0), 0, 0))],
        out_specs=pl.BlockSpec((nb, TM, NA_HD), lambda h, j: (0, j, h)),
        out_shape=jax.ShapeDtypeStruct((nb, l, hq * NA_HD), BF16),
        compiler_params=_cparams("arbitrary", "arbitrary"),
        name="na_attention",
    )(qkv, qkv, qkv, bias_tab)


def _sw_kernel(sink_ref, q_ref, kv_ref, o_ref, *, seq):
    j = pl.program_id(1)
    nt = (((1,), (1,)), ((), ()))
    nkv = SW_KV * SW_HD
    kwin = TM + 2 * SW_WINDOW
    low = lax.broadcasted_iota(jnp.int32, (1, LANES), 1) < SW_HD

    def attend(segs):
        for pair in range(SW_KV // 2):
            kc = slice(pair * LANES, (pair + 1) * LANES)
            vc = slice(nkv + pair * LANES, nkv + (pair + 1) * LANES)
            ks = [kv_ref[rows, kc] for rows, _ in segs]
            vs = [kv_ref[rows, vc] for rows, _ in segs]
            one = jnp.ones((), BF16)
            v_half = [[jnp.where(low, v, one) for v in vs], [jnp.where(low, one, v) for v in vs]]
            for i in range(SW_HEADS // SW_KV):
                cols = slice((4 * pair + i) * LANES, (4 * pair + i + 1) * LANES)
                qs = q_ref[:, cols]
                outs = []
                for half in range(2):
                    sink = sink_ref[8 * pair + 4 * half + i]
                    qm = jnp.where(low if half == 0 else jnp.logical_not(low), qs, jnp.zeros((), BF16))
                    sc = []
                    for k, (_, mask) in zip(ks, segs):
                        s = lax.dot_general(qm, k, nt, preferred_element_type=F32)
                        sc.append(s if mask is None else jnp.where(mask, s, NEG_INF))
                    m = sink
                    for s in sc:
                        m = jnp.maximum(m, jnp.max(s, axis=-1, keepdims=True))
                    acc = None
                    for s, v in zip(sc, v_half[half]):
                        pv = jnp.dot(jnp.exp(s - m).astype(BF16), v, preferred_element_type=F32)
                        acc = pv if acc is None else acc + pv
                    denom = pltpu.roll(acc, SW_HD, 1) + jnp.exp(sink - m)
                    outs.append(acc / denom)
                o_ref[:, cols] = jnp.where(low, outs[0], outs[1]).astype(BF16)

    @pl.when(j == 0)
    def _ctx():
        attend([(slice(0, CTX_LEN), None)])

    @pl.when(j > 0)
    def _lat():
        q0 = (j - 1) * TM
        ws = jnp.clip(q0 - SW_WINDOW, 0, seq - kwin)
        start = pl.multiple_of(CTX_LEN + ws, SW_WINDOW)
        qpos = q0 + lax.broadcasted_iota(jnp.int32, (TM, kwin), 0)
        kpos = ws + lax.broadcasted_iota(jnp.int32, (TM, kwin), 1)
        ok = jnp.abs(qpos - kpos) <= SW_WINDOW
        attend([(pl.ds(start, kwin), ok), (slice(0, CTX_LEN), None)])


def _sw_attention(q, kv, sink):
    nb, l, nq = q.shape
    nt = l // TM
    return pl.pallas_call(
        functools.partial(_sw_kernel, seq=l - CTX_LEN),
        grid=(nb, nt),
        in_specs=[pl.BlockSpec(memory_space=pltpu.SMEM),
                  pl.BlockSpec((None, TM, nq), lambda b, j: (b, j, 0)),
                  pl.BlockSpec((None, l, kv.shape[2]), lambda b, j: (b, 0, 0))],
        out_specs=pl.BlockSpec((None, TM, nq), lambda b, j: (b, j, 0)),
        out_shape=jax.ShapeDtypeStruct((nb, l, nq), BF16),
        compiler_params=_cparams("arbitrary", "arbitrary"),
        name="sw_attention",
    )(sink, q, kv)


def _chunk_index(step, rev, n_ctx, n_all):
    if not rev:
        return step
    return jnp.where(step < n_ctx, n_ctx - 1 - step, n_all + n_ctx - 1 - step)


def _tri(n, rev):
    r = lax.broadcasted_iota(jnp.int32, (n, n), 0)
    c = lax.broadcasted_iota(jnp.int32, (n, n), 1)
    return (c >= r) if rev else (c <= r)


def _split3(x):
    hi = x.astype(BF16)
    r = x - hi.astype(F32)
    mid = r.astype(BF16)
    return hi, mid, (r - mid.astype(F32)).astype(BF16)


def _ml_kernel(q_ref, kt_ref, v_ref, og_ref, gr_ref, gc_ref, ng_ref, y_ref, acc_ref, *, n_ctx, n_all):
    lc = ML_SCAN
    hps = q_ref.shape[1] // ML_DK
    ones_v = jnp.ones((lc, ML_DV), BF16)
    lane_head = lax.broadcasted_iota(jnp.int32, (1, hps * ML_DK), 1) // ML_DK
    row_head = lax.broadcasted_iota(jnp.int32, (hps * ML_DK, 1), 0) // ML_DK

    def gate_forms(step, rev):
        c = _chunk_index(step, rev, n_ctx, n_all)
        r0 = pl.multiple_of(c * lc, lc)
        g_r = gr_ref[:, pl.ds(r0, lc)]
        g_c = pltpu.roll(gc_ref[pl.ds(r0, lc), :], (LANES - pl.program_id(1) * SUBLANES) % LANES, 1)
        cum3 = jnp.dot(jnp.concatenate(_split3(g_r), axis=0), _tri(lc, not rev).astype(BF16),
                       preferred_element_type=F32)
        cum_r = cum3[0:SUBLANES] + cum3[SUBLANES:2 * SUBLANES] + cum3[2 * SUBLANES:]
        tri_c = _tri(lc, rev).astype(BF16)
        cum_c = sum(jnp.dot(tri_c, part, preferred_element_type=F32) for part in _split3(g_c))
        return c, g_r, cum_r, cum_c

    def chain(forms, hl, rev, state):
        ct, m = state
        c, g_r, cum_r, cum_c = forms
        r0 = pl.multiple_of(c * lc, lc)
        base = hl * 4 + (2 if rev else 0)
        bf_r = cum_r[base + 1:base + 2, :]
        bf_c = cum_c[:, base + 1:base + 2]
        x_r = g_r[base:base + 1, :] - bf_r
        g = jnp.sum(g_r[base + 1:base + 2, :], axis=1, keepdims=True)
        tri = _tri(lc, rev)
        pm_c = jnp.max(jnp.where(tri, x_r, -jnp.inf), axis=1, keepdims=True)
        xmax = jnp.max(x_r, axis=1, keepdims=True)
        q = jnp.where(lane_head == hl, q_ref[pl.ds(r0, lc), :], jnp.zeros((), BF16))
        kt = kt_ref[:, pl.ds(r0, lc)]
        v1 = jnp.concatenate([v_ref[pl.ds(r0, lc), hl * ML_DV:(hl + 1) * ML_DV], ones_v], axis=1)
        mx = jnp.maximum(m, pm_c)
        e = jnp.exp(jnp.where(tri, x_r - mx, NEG_INF))
        sc = jnp.dot(q, kt, preferred_element_type=F32) * e
        e_int = jnp.exp(m - mx)
        intra = jnp.dot(sc.astype(BF16), v1, preferred_element_type=F32)
        inter = jnp.dot(q, ct.astype(BF16), preferred_element_type=F32)
        num = intra[:, :ML_DV] + e_int * inter[:, :ML_DV]
        den = intra[:, ML_DV:ML_DV + 1] + e_int * inter[:, ML_DV:ML_DV + 1]
        hout = num / jnp.maximum(jnp.abs(den), jnp.exp(-mx - bf_c))
        acc_ref[pl.ds(r0, lc), hl * ML_DV:(hl + 1) * ML_DV] += hout
        m_loc = g + xmax
        w_r = jnp.exp(x_r - xmax)
        ktw = jnp.where(row_head == hl, kt.astype(F32) * w_r, 0.0).astype(BF16)
        c_loc = jnp.dot(ktw, v1, preferred_element_type=F32)
        m_new = jnp.maximum(g + m, m_loc)
        dec = jnp.exp(g + m - m_new)
        inc = jnp.exp(m_loc - m_new)
        return dec * ct + inc * c_loc, m_new

    acc_ref[...] = jnp.zeros_like(acc_ref)
    chains = [(hl, rev) for hl in range(hps) for rev in (False, True)]
    init = tuple((jnp.zeros((hps * ML_DK, 2 * ML_DV), F32), jnp.zeros((1, 1), F32)) for _ in chains)

    def step_fn(step, states):
        forms = {rev: gate_forms(step, rev) for rev in (False, True)}
        return tuple(chain(forms[rev], hl, rev, st) for (hl, rev), st in zip(chains, states))

    lax.fori_loop(0, n_all, step_fn, init)

    for hl in range(hps):
        cols = slice(hl * ML_DV, (hl + 1) * ML_DV)
        z = acc_ref[:, cols]
        z = z * lax.rsqrt(jnp.mean(z * z, axis=-1, keepdims=True) + RMS_EPS) * ng_ref[:, cols]
        y_ref[:, cols] = (z.astype(BF16) * og_ref[:, cols])


def _ml_mixer(qkv, og, gates, norm_g):
    nb, l, _ = qkv.shape
    lc = ML_SCAN
    n_all, n_ctx = l // lc, CTX_LEN // lc
    hps = 2
    ngrp = ML_HEADS // hps
    nq = ML_HEADS * ML_DK
    g = gates[:, :, :4 * ML_HEADS].transpose(0, 2, 1)
    kt = qkv[:, :, nq:2 * nq].transpose(0, 2, 1)
    vb = hps * ML_DV
    return pl.pallas_call(
        functools.partial(_ml_kernel, n_ctx=n_ctx, n_all=n_all),
        grid=(nb, ngrp),
        in_specs=[pl.BlockSpec((None, l, hps * ML_DK), lambda b, h: (b, 0, h)),
                  pl.BlockSpec((None, hps * ML_DK, l), lambda b, h: (b, h, 0)),
                  pl.BlockSpec((None, l, vb), lambda b, h: (b, 0, 2 * nq // vb + h)),
                  pl.BlockSpec((None, l, vb), lambda b, h: (b, 0, h)),
                  pl.BlockSpec((None, hps * 4, l), lambda b, h: (b, h, 0)),
                  pl.BlockSpec((None, l, LANES), lambda b, h: (b, 0, 0)),
                  pl.BlockSpec((1, vb), lambda b, h: (0, h))],
        out_specs=pl.BlockSpec((None, l, vb), lambda b, h: (b, 0, h)),
        out_shape=jax.ShapeDtypeStruct((nb, l, ML_HEADS * ML_DV), BF16),
        scratch_shapes=[pltpu.VMEM((l, vb), F32)],
        compiler_params=_cparams("arbitrary", "arbitrary"),
        name="mlstm_mixer",
    )(qkv, kt, qkv, og, g, gates, norm_g.reshape(1, -1))


def _gl_kernel(q_ref, k_ref, v_ref, gate_ref, la0_ref, la1_ref, ng_ref, y_ref, acc_ref, st_ref, *, n_ctx, n_all):
    lc = GL_SCAN
    hps = q_ref.shape[1] // GL_DK
    t0 = (((0,), (0,)), ((), ()))
    nt = (((1,), (1,)), ((), ()))
    scale = GL_DK ** -0.5

    def chain(step, hl, rev):
        la_ref = la1_ref if rev else la0_ref
        sidx = 2 * hl + (1 if rev else 0)
        kc = slice(hl * GL_DK, (hl + 1) * GL_DK)
        vc = slice(hl * GL_DV, (hl + 1) * GL_DV)
        c = _chunk_index(step, rev, n_ctx, n_all)
        r0 = pl.multiple_of(c * lc, lc)
        tri = _tri(lc, rev)
        la = la_ref[pl.ds(r0, lc), kc]
        bc3 = jnp.dot(tri.astype(BF16), jnp.concatenate(_split3(la), axis=1), preferred_element_type=F32)
        bc = bc3[:, :GL_DK] + bc3[:, GL_DK:2 * GL_DK] + bc3[:, 2 * GL_DK:]
        g = jnp.sum(la, axis=0, keepdims=True)
        q = q_ref[pl.ds(r0, lc), kc].astype(F32) * scale
        k = k_ref[pl.ds(r0, lc), kc].astype(F32)
        v = v_ref[pl.ds(r0, lc), vc]
        st = st_ref[sidx]
        eg = jnp.exp(g)
        k_dec = k * jnp.exp(-bc)
        q_t = (q * jnp.exp(bc)).astype(BF16)
        k_t = k_dec.astype(BF16)
        att = jnp.where(tri, lax.dot_general(q_t, k_t, nt, preferred_element_type=F32), 0.0)
        o = (jnp.dot(att.astype(BF16), v, preferred_element_type=F32)
             + lax.dot_general(q_t, st.astype(BF16), nt, preferred_element_type=F32))
        acc_ref[pl.ds(r0, lc), vc] += o
        kd = (k_dec * eg).astype(BF16)
        s_loc = lax.dot_general(v, kd, t0, preferred_element_type=F32)
        st_ref[sidx] = st * eg + s_loc

    acc_ref[...] = jnp.zeros_like(acc_ref)
    st_ref[...] = jnp.zeros_like(st_ref)

    def step_fn(step, carry):
        for hl in range(hps):
            chain(step, hl, False)
            chain(step, hl, True)
        return carry

    lax.fori_loop(0, n_all, step_fn, 0)
    for hl in range(hps):
        vc = slice(hl * GL_DV, (hl + 1) * GL_DV)
        z = acc_ref[:, vc]
        z = z * lax.rsqrt(jnp.mean(z * z, axis=-1, keepdims=True) + RMS_EPS) * ng_ref[:, vc]
        y_ref[:, vc] = z.astype(BF16) * gate_ref[:, vc]


def _gl_mixer(qkv, gate, la, norm_g):
    nb, l, _ = qkv.shape
    lc = GL_SCAN
    n_all, n_ctx = l // lc, CTX_LEN // lc
    hps = 2
    ngrp = GL_HEADS // hps
    kb, vb = hps * GL_DK, hps * GL_DV
    nk = GL_HEADS * GL_DK
    return pl.pallas_call(
        functools.partial(_gl_kernel, n_ctx=n_ctx, n_all=n_all),
        grid=(nb, ngrp),
        in_specs=[pl.BlockSpec((None, l, kb), lambda b, h: (b, 0, h)),
                  pl.BlockSpec((None, l, kb), lambda b, h: (b, 0, nk // kb + h)),
                  pl.BlockSpec((None, l, vb), lambda b, h: (b, 0, 2 * nk // vb + h)),
                  pl.BlockSpec((None, l, vb), lambda b, h: (b, 0, h)),
                  pl.BlockSpec((None, l, kb), lambda b, h: (b, 0, h)),
                  pl.BlockSpec((None, l, kb), lambda b, h: (b, 0, nk // kb + h)),
                  pl.BlockSpec((1, vb), lambda b, h: (0, h))],
        out_specs=pl.BlockSpec((None, l, vb), lambda b, h: (b, 0, h)),
        out_shape=jax.ShapeDtypeStruct((nb, l, GL_HEADS * GL_DV), BF16),
        scratch_shapes=[pltpu.VMEM((l, vb), F32), pltpu.VMEM((2 * hps, GL_DV, GL_DK), F32)],
        compiler_params=_cparams("arbitrary", "arbitrary"),
        name="gla_mixer",
    )(qkv, qkv, qkv, gate, la, la, norm_g.reshape(1, -1))


def _store_token_tiles(ref, val):
    rows = val.shape[0]
    for s in range(val.shape[1] // LANES):
        ref[pl.ds(s, rows, stride=SUBLANES), :] = val[:, s * LANES:(s + 1) * LANES]


def _load_token_tiles(ref, rows):
    return jnp.concatenate([ref[pl.ds(s, rows, stride=SUBLANES), :] for s in range(SUBLANES)], axis=1)


def _post_kernel(y_ref, x_ref, mod_ref, wo_ref, g_ref, wr_ref, br_ref, xo_ref, f_ref, rt_ref, cnt_ref, run_ref):
    first = (pl.program_id(0) == 0) & (pl.program_id(1) == 0)

    @pl.when(first)
    def _():
        run_ref[...] = jnp.zeros_like(run_ref)

    for sub in range(PAIR):
        _post_tile(y_ref.at[sub], x_ref.at[sub], mod_ref.at[sub], wo_ref, g_ref, wr_ref, br_ref,
                   xo_ref.at[sub], f_ref.at[sub], rt_ref.at[sub], cnt_ref, run_ref)


def _post_tile(y_ref, x_ref, mod_ref, wo_ref, g_ref, wr_ref, br_ref, xo_ref, f_ref, rt_ref, cnt_ref, run_ref):
    o = jnp.dot(y_ref[...], wo_ref[...], preferred_element_type=F32)
    xn = x_ref[...] + mod_ref[2:3, :] * o
    xo_ref[...] = xn
    f = _norm_mod(xn, g_ref[...], mod_ref, 3)
    _store_token_tiles(f_ref, f)
    f_hi = f.astype(BF16)
    f_lo = (f - f_hi.astype(F32)).astype(BF16)
    hh = jnp.dot(f_hi, wr_ref[...], preferred_element_type=F32)
    lh = jnp.dot(f_lo, wr_ref[:, :LANES], preferred_element_type=F32)
    lg = hh[:, :LANES] + (hh[:, LANES:] + lh) + br_ref[...]
    lane = lax.broadcasted_iota(jnp.int32, lg.shape, 1).astype(F32)

    def top(mask):
        v = jnp.max(jnp.where(mask, lg, -jnp.inf), axis=-1, keepdims=True)
        return v, jnp.min(jnp.where(mask & (lg == v), lane, float(LANES)), axis=-1, keepdims=True)

    gm = lane < MOE_GROUPS
    mg, g_idx = top(gm)
    g_w = 1.0 / jnp.sum(jnp.where(gm, jnp.exp(lg - mg), 0.0), axis=-1, keepdims=True)
    lo = MOE_GROUPS + MOE_PER_GROUP * g_idx
    em = (lane >= lo) & (lane < lo + MOE_PER_GROUP)
    v0, i0 = top(em)
    v1, i1 = top(em & (lane != i0))
    e0, e1 = i0 - MOE_GROUPS, i1 - MOE_GROUPS
    t = jnp.exp(v1 - v0)
    w0 = g_w / (1.0 + t)
    w1 = g_w * t / (1.0 + t)
    oh = ((lane == e0) | (lane == e1)).astype(F32)
    rows = lg.shape[0]
    below = lax.broadcasted_iota(jnp.int32, (rows, rows), 0) > lax.broadcasted_iota(jnp.int32, (rows, rows), 1)
    tot = run_ref[...] + jnp.dot(below.astype(BF16), oh.astype(BF16), preferred_element_type=F32)
    r0 = jnp.sum(jnp.where(lane == e0, tot, 0.0), axis=-1, keepdims=True)
    r1 = jnp.sum(jnp.where(lane == e1, tot, 0.0), axis=-1, keepdims=True)
    run_ref[...] = run_ref[...] + jnp.sum(oh, axis=0, keepdims=True)
    cnt_ref[...] = jnp.broadcast_to(run_ref[...], cnt_ref.shape)
    out = jnp.zeros_like(lg)
    for i, val in enumerate((e0, e1, w0, w1, r0, r1)):
        out = jnp.where(lane == i, val, out)
    rt_ref[...] = out


def _post_call(y, x, mods, w_o, g_ffn, w_r, b_r):
    nb, l, d = x.shape
    nt = l // TM
    tile = lambda b, t: (b, t, 0)
    x_new, f_tiles, rt, cnt = pl.pallas_call(
        _post_kernel,
        grid=(nb // PAIR, nt),
        in_specs=[pl.BlockSpec((PAIR, TM, d), tile),
                  pl.BlockSpec((PAIR, TM, d), tile),
                  _mod_spec(nb),
                  pl.BlockSpec((d, d), lambda b, t: (0, 0)),
                  pl.BlockSpec((1, d), lambda b, t: (0, 0)),
                  pl.BlockSpec((d, 2 * LANES), lambda b, t: (0, 0)),
                  pl.BlockSpec((1, LANES), lambda b, t: (0, 0))],
        out_specs=[pl.BlockSpec((PAIR, TM, d), tile),
                   pl.BlockSpec((PAIR, TM * SUBLANES, LANES), tile),
                   pl.BlockSpec((PAIR, TM, LANES), tile),
                   pl.BlockSpec((SUBLANES, LANES), lambda b, t: (0, 0))],
        out_shape=[jax.ShapeDtypeStruct((nb, l, d), F32),
                   jax.ShapeDtypeStruct((nb, l * SUBLANES, LANES), F32),
                   jax.ShapeDtypeStruct((nb, l, LANES), F32),
                   jax.ShapeDtypeStruct((SUBLANES, LANES), F32)],
        scratch_shapes=[pltpu.VMEM((1, LANES), F32)],
        compiler_params=_cparams("arbitrary", "arbitrary"),
        name="outproj_router",
    )(y, x, mods, w_o, g_ffn.reshape(1, d), w_r, b_r)
    return x_new, f_tiles.reshape(nb * l * SUBLANES, LANES), rt.reshape(nb * l, LANES), cnt


def _dispatch_kernel(dest_ref, pend_ref, f_ref, xs_out, zbuf, sem):
    tile = f_ref.shape[0] // SUBLANES
    base = pl.program_id(0) * tile

    @pl.when(pl.program_id(0) == 0)
    def _():
        zbuf[...] = jnp.zeros_like(zbuf)

        def zero_block(blk):
            start = pl.multiple_of(blk * (MOE_BLOCK * SUBLANES), MOE_BLOCK * SUBLANES)
            return pltpu.make_async_copy(zbuf, xs_out.at[pl.ds(start, MOE_BLOCK * SUBLANES), :], sem)

        def has_rows(e):
            return pend_ref[e] > (0 if e == 0 else pend_ref[e - 1])

        for e in range(MOE_EXPERTS):
            pl.when(has_rows(e))(lambda e=e: zero_block(pend_ref[e] // MOE_BLOCK - 1).start())
        for e in range(MOE_EXPERTS):
            pl.when(has_rows(e))(lambda e=e: zero_block(pend_ref[e] // MOE_BLOCK - 1).wait())
        n_act = pend_ref[MOE_EXPERTS - 1] // MOE_BLOCK
        n_blk = xs_out.shape[0] // (MOE_BLOCK * SUBLANES)
        lax.fori_loop(n_act, n_blk, lambda b, c: (zero_block(b).start(), c)[1], 0)
        lax.fori_loop(n_act, n_blk, lambda b, c: (zero_block(b).wait(), c)[1], 0)

    def copy(t, k):
        d = dest_ref[2 * (base + t) + k]
        return pltpu.make_async_copy(f_ref.at[pl.ds(t * SUBLANES, SUBLANES), :],
                                     xs_out.at[pl.ds(d * SUBLANES, SUBLANES), :], sem)

    def issue(t, c):
        copy(t, 0).start()
        copy(t, 1).start(priority=1)
        return c

    lax.fori_loop(0, tile, issue, 0, unroll=DMA_UNROLL)
    for _ in range(2):
        pltpu.make_async_copy(f_ref, xs_out.at[pl.ds(0, tile * SUBLANES), :], sem).wait()


def _dispatch(dest, p_end, f_tiles, n_slots):
    n_tok = f_tiles.shape[0] // SUBLANES
    tile = DISPATCH_TILE
    assert n_tok % tile == 0
    return pl.pallas_call(
        _dispatch_kernel,
        grid_spec=pltpu.PrefetchScalarGridSpec(
            num_scalar_prefetch=2,
            grid=(n_tok // tile,),
            in_specs=[pl.BlockSpec((tile * SUBLANES, LANES), lambda i, dr, pe: (i, 0))],
            out_specs=pl.BlockSpec(memory_space=pl.ANY),
            scratch_shapes=[pltpu.VMEM((MOE_BLOCK * SUBLANES, LANES), F32), pltpu.SemaphoreType.DMA(())]),
        out_shape=jax.ShapeDtypeStruct((n_slots * SUBLANES, LANES), F32),
        compiler_params=_cparams("arbitrary"),
        name="moe_dispatch",
    )(dest, p_end, f_tiles)


def _expert_kernel(blk_e_ref, nact_ref, nxt_ref, slot_ref, xs_ref, wg_hbm, wu_hbm, wd_hbm, ys_ref,
                   wg_f, wu_f, wd_f, wg_s, wu_s, wd_s, sem, *, layer):
    i = pl.program_id(0)
    active = i < nact_ref[0]
    e = blk_e_ref[i]
    changed = (i == 0) | (e != blk_e_ref[jnp.maximum(i - 1, 0)])
    slot = slot_ref[i]

    def fetch(expert, s):
        return [pltpu.make_async_copy(w.at[layer, expert], buf.at[s], sem.at[s, k])
                for k, (w, buf) in enumerate(((wg_hbm, wg_f), (wu_hbm, wu_f), (wd_hbm, wd_f)))]

    @pl.when(i == 0)
    def _():
        for c in fetch(e, slot):
            c.start()

    @pl.when(active & changed)
    def _():
        nxt = nxt_ref[i]

        @pl.when(nxt >= 0)
        def _():
            for c in fetch(nxt, 1 - slot):
                c.start()

        for c in fetch(e, slot):
            c.wait()
        wg_s[...] = wg_f[slot].astype(BF16)
        wu_s[...] = wu_f[slot].astype(BF16)
        wd_s[...] = wd_f[slot].astype(BF16)

    @pl.when(active)
    def _():
        x = _load_token_tiles(xs_ref, MOE_BLOCK).astype(BF16)
        a = jnp.dot(x, wg_s[...], preferred_element_type=F32)
        u = jnp.dot(x, wu_s[...], preferred_element_type=F32)
        y = jnp.dot((_silu(a) * u).astype(BF16), wd_s[...], preferred_element_type=F32)
        _store_token_tiles(ys_ref, y)

    @pl.when(jnp.logical_not(active))
    def _():
        ys_ref[...] = jnp.zeros_like(ys_ref)


def _experts(blk_e, nact, nxt_e, slot, xs, w_gate, w_up, w_down, layer):
    n_blk = blk_e.shape[0]
    _, _, d, ff = w_gate.shape
    rows = MOE_BLOCK * SUBLANES
    last = lambda i, na: jnp.minimum(i, jnp.maximum(na[0] - 1, 0))
    hbm = pl.BlockSpec(memory_space=pl.ANY)
    return pl.pallas_call(
        functools.partial(_expert_kernel, layer=layer),
        grid_spec=pltpu.PrefetchScalarGridSpec(
            num_scalar_prefetch=4,
            grid=(n_blk,),
            in_specs=[pl.BlockSpec((rows, LANES), lambda i, be, na, nx, sl: (last(i, na), 0)), hbm, hbm, hbm],
            out_specs=pl.BlockSpec((rows, LANES), lambda i, be, na, nx, sl: (i, 0)),
            scratch_shapes=[pltpu.VMEM((2, d, ff), F32), pltpu.VMEM((2, d, ff), F32), pltpu.VMEM((2, ff, d), F32),
                            pltpu.VMEM((d, ff), BF16), pltpu.VMEM((d, ff), BF16), pltpu.VMEM((ff, d), BF16),
                            pltpu.SemaphoreType.DMA((2, 3))]),
        out_shape=jax.ShapeDtypeStruct(xs.shape, F32),
        compiler_params=_cparams("arbitrary"),
        name="moe_experts",
    )(blk_e, nact, nxt_e, slot, xs, w_gate, w_up, w_down)


def _combine_kernel(dest_ref, ys_hbm, x_ref, rt_ref, mod_ref, o_ref, ybuf, sem, *, nt, t0):
    i = pl.program_id(0)
    n = pl.num_programs(0)
    nte = nt - t0

    def gather(step, slot, start):
        base = ((step // nte) * nt + step % nte + t0) * TM

        def copy(t, k):
            d = dest_ref[2 * (base + t) + k]
            return pltpu.make_async_copy(ys_hbm.at[pl.ds(d * SUBLANES, SUBLANES), :],
                                         ybuf.at[slot, k, pl.ds(t * SUBLANES, SUBLANES), :], sem.at[slot])

        def issue(t, c):
            copy(t, 0).start()
            copy(t, 1).start(priority=1)
            return c

        if start:
            lax.fori_loop(0, TM, issue, 0, unroll=DMA_UNROLL)
        else:
            for k in range(2):
                pltpu.make_async_copy(ys_hbm.at[pl.ds(0, TM * SUBLANES), :], ybuf.at[slot, k], sem.at[slot]).wait()

    slot = i % 2
    pl.when(i == 0)(lambda: gather(i, slot, True))
    pl.when(i + 1 < n)(lambda: gather(i + 1, 1 - slot, True))
    gather(i, slot, False)
    y0 = _load_token_tiles(ybuf.at[slot, 0], TM)
    y1 = _load_token_tiles(ybuf.at[slot, 1], TM)
    rt = rt_ref[...]
    y = rt[:, 2:3] * y0 + rt[:, 3:4] * y1
    o_ref[...] = x_ref[...] + mod_ref[5:6, :] * y


def _combine(dest, ys, x, rt, mods, skip_ctx):
    nb, l, d = x.shape
    nt = l // TM
    t0 = 1 if skip_ctx else 0
    nte = nt - t0
    bt = lambda i: (i // nte, i % nte + t0)
    mod_row = (lambda b, t: b) if skip_ctx else (lambda b, t: jnp.where(t == 0, nb, b))
    return pl.pallas_call(
        functools.partial(_combine_kernel, nt=nt, t0=t0),
        grid_spec=pltpu.PrefetchScalarGridSpec(
            num_scalar_prefetch=1,
            grid=(nb * nte,),
            in_specs=[pl.BlockSpec(memory_space=pl.ANY),
                      pl.BlockSpec((None, TM, d), lambda i, dr: (*bt(i), 0)),
                      pl.BlockSpec((TM, LANES), lambda i, dr: (bt(i)[0] * nt + bt(i)[1], 0)),
                      pl.BlockSpec((None, 6, d), lambda i, dr: (mod_row(*bt(i)), 0, 0))],
            out_specs=pl.BlockSpec((None, TM, d), lambda i, dr: (i // nte, i % nte, 0)),
            scratch_shapes=[pltpu.VMEM((2, 2, TM * SUBLANES, LANES), F32), pltpu.SemaphoreType.DMA((2,))]),
        out_shape=jax.ShapeDtypeStruct((nb, nte * TM, d), F32),
        compiler_params=_cparams("arbitrary"),
        name="moe_combine",
    )(dest, ys, x, rt, mods)


def _moe(y_mix, x, mods, w_o, g_ffn, w_grp, b_grp, w_exp, b_exp, w_gate, w_up, w_down, layer, skip_ctx):
    nb, l, d = x.shape
    n_tok = nb * l
    nr = MOE_GROUPS + MOE_EXPERTS
    w_r = jnp.pad(jnp.concatenate([w_grp, w_exp], axis=1), ((0, 0), (0, LANES - nr)))
    b_r = jnp.pad(jnp.concatenate([b_grp, b_exp]), (0, LANES - nr)).reshape(1, LANES)
    w_hi = w_r.astype(BF16)
    w_r = jnp.concatenate([w_hi, (w_r - w_hi.astype(F32)).astype(BF16)], axis=1)
    x_new, f_tiles, rt, cnt = _post_call(y_mix, x, mods, w_o, g_ffn, w_r, b_r)
    counts = cnt[0, :MOE_EXPERTS].astype(jnp.int32)
    padded = (counts + MOE_BLOCK - 1) // MOE_BLOCK * MOE_BLOCK
    p_end = jnp.cumsum(padded)
    p_start = p_end - padded
    n_pairs = 2 * n_tok
    n_blk = -(-(n_pairs + MOE_EXPERTS * (MOE_BLOCK - 1)) // MOE_BLOCK)
    eid = rt[:, 0:2].astype(jnp.int32)
    first = jnp.sum(jnp.where(eid[:, :, None] == jnp.arange(MOE_EXPERTS, dtype=jnp.int32), p_start, 0), axis=-1)
    dest = (first + rt[:, 4:6].astype(jnp.int32)).reshape(-1)
    blk_row = jnp.arange(n_blk, dtype=jnp.int32) * MOE_BLOCK
    blk_e = jnp.minimum(jnp.sum((p_end[None, :] <= blk_row[:, None]).astype(jnp.int32), axis=1), MOE_EXPERTS - 1)
    nact = (p_end[-1:] // MOE_BLOCK).astype(jnp.int32)
    xs = _dispatch(dest, p_end.astype(jnp.int32), f_tiles, n_blk * MOE_BLOCK)
    has = padded > 0
    ids = jnp.arange(MOE_EXPERTS, dtype=jnp.int32)
    rank = jnp.cumsum(has.astype(jnp.int32)) - has.astype(jnp.int32)
    later = jnp.where((ids[None, :] > ids[:, None]) & has[None, :], ids[None, :], MOE_EXPERTS)
    nxt = jnp.min(later, axis=1)
    nxt = jnp.where(nxt == MOE_EXPERTS, -1, nxt)
    sel = blk_e[:, None] == ids[None, :]
    slot = jnp.sum(jnp.where(sel, rank[None, :] % 2, 0), axis=1).astype(jnp.int32)
    nxt_e = jnp.sum(jnp.where(sel, nxt[None, :], 0), axis=1).astype(jnp.int32)
    ys = _experts(blk_e, nact, nxt_e, slot, xs, w_gate, w_up, w_down, layer)
    return _combine(dest, ys, x_new, rt, mods, skip_ctx)


def _rope_tables(seq):
    nf = SW_HD // 4
    inv = ROPE_BASE ** (-jnp.arange(nf, dtype=F32) / nf)
    pos = jnp.arange(seq, dtype=jnp.int32)
    rows, cols = (pos // GRID_W).astype(F32), (pos % GRID_W).astype(F32)
    lane = jnp.arange(LANES)
    p = jnp.where((lane % SW_HD < SW_HD // 2)[None, :], rows[:, None], cols[:, None])
    ang = p * inv[lane % nf][None, :]
    sign = jnp.where((lane % (2 * nf)) < nf, -1.0, 1.0)[None, :]
    cs = jnp.concatenate([jnp.ones((CTX_LEN, LANES), F32), jnp.cos(ang)], axis=0)
    sn = jnp.concatenate([jnp.zeros((CTX_LEN, LANES), F32), jnp.sin(ang) * sign], axis=0)
    return cs, sn


def _pad_cols(w, n):
    return jnp.pad(w, ((0, 0), (0, n - w.shape[1])))


def kernel(x, c, ctx, c_ctx, ada_w, ada_b, norm_mix_g, norm_ffn_g, na_w_qkv, na_qk_g, na_rpb, na_w_o, ml_w_in, ml_b_gates, ml_norm_g, ml_w_o, sw_w_qkv, sw_qk_g, sw_sink, sw_w_o, gl_w_in, gl_w_a2, gl_b_a, gl_norm_g, gl_w_o, moe_w_grp, moe_b_grp, moe_w_exp, moe_b_exp, moe_w_gate, moe_w_up, moe_w_down):
    nb, seq, d = x.shape
    depth = ada_w.shape[0]
    assert d == D_MODEL and ctx.shape[1] == CTX_LEN == TM and seq % TM == 0 and nb % PAIR == 0
    rows = -(-(nb + PAIR) // SUBLANES) * SUBLANES
    cond = jnp.pad(jnp.concatenate([c] + [c_ctx[None]] * PAIR, axis=0), ((0, rows - nb - PAIR), (0, 0)))
    mods_all = _ada_mods(cond, ada_w, ada_b)
    xa = jnp.concatenate([ctx, x], axis=1)
    for i in range(depth):
        j, kind = divmod(i, 4)
        mods = mods_all[i]
        g_mix = norm_mix_g[i]
        if kind == 0:
            qkg = na_qk_g[j]
            (qkv,) = _proj_call(_proj_na_kernel, xa, mods, g_mix, na_w_qkv[j].astype(BF16), [(qkg, False)],
                                [3 * NA_HEADS * NA_HD], [BF16], "proj_na")
            y = _na_attention(qkv, _na_bias_table(na_rpb[j], seq // GRID_W))
            w_o = na_w_o[j]
        elif kind == 1:
            n_in = -(-ml_w_in.shape[2] // LANES) * LANES
            ng = 4 * ML_HEADS
            w_in = ml_w_in[j]
            w_g = w_in[:, -ng:].reshape(d, 2, 2, ML_HEADS).transpose(0, 3, 1, 2).reshape(d, ng)
            w_in = jnp.concatenate([w_in[:, :-ng], w_g], axis=1)
            bg = ml_b_gates[j].reshape(2, 2, ML_HEADS).transpose(2, 0, 1).reshape(ng)
            bg = jnp.pad(bg, (0, LANES - ng)).reshape(1, LANES)
            qkv, og, gates = _proj_call(_proj_ml_kernel, xa, mods, g_mix, _pad_cols(w_in, n_in).astype(BF16),
                                        [(bg, False)], [2 * ML_HEADS * ML_DK + ML_HEADS * ML_DV, ML_HEADS * ML_DV, LANES],
                                        [BF16, BF16, F32], "proj_ml")
            y = _ml_mixer(qkv, og, gates, ml_norm_g[j])
            w_o = ml_w_o[j]
        elif kind == 2:
            qkg = jnp.concatenate([sw_qk_g[j], sw_qk_g[j]], axis=1)
            cs, sn = _rope_tables(seq)
            nqc = SW_HEADS * SW_HD
            w_in = sw_w_qkv[j]
            w_q = w_in[:, :nqc].reshape(d, 2, 2, 4, SW_HD).transpose(0, 1, 3, 2, 4).reshape(d, nqc)
            w_in = jnp.concatenate([w_q, w_in[:, nqc:]], axis=1)
            q, kv = _proj_call(_proj_sw_kernel, xa, mods, g_mix, w_in.astype(BF16),
                               [(qkg, False), (cs, True), (sn, True)],
                               [nqc, 2 * SW_KV * SW_HD], [BF16, BF16], "proj_sw")
            y = _sw_attention(q, kv, sw_sink[j])
            w_o = sw_w_o[j].reshape(2, 2, 4, SW_HD, d).transpose(0, 2, 1, 3, 4).reshape(nqc, d)
        else:
            n_in = -(-gl_w_in.shape[2] // LANES) * LANES
            nk = GL_HEADS * GL_DK
            wa = jnp.zeros((LANES, 2 * nk), F32)
            wa = wa.at[:GL_RANK, :nk].set(gl_w_a2[j, 0]).at[GL_RANK:2 * GL_RANK, nk:].set(gl_w_a2[j, 1])
            ba = gl_b_a[j].reshape(1, 2 * nk)
            qkv, gate, la = _proj_call(_proj_gl_kernel, xa, mods, g_mix, _pad_cols(gl_w_in[j], n_in).astype(BF16),
                                       [(wa.astype(BF16), False), (ba, False)],
                                       [2 * nk + GL_HEADS * GL_DV, GL_HEADS * GL_DV, 2 * nk], [BF16, BF16, F32], "proj_gl")
            y = _gl_mixer(qkv, gate, la, gl_norm_g[j])
            w_o = gl_w_o[j]
        xa = _moe(y, xa, mods, w_o.astype(BF16), norm_ffn_g[i], moe_w_grp[i], moe_b_grp[i], moe_w_exp[i], moe_b_exp[i],
                  moe_w_gate, moe_w_up, moe_w_down, i, i == depth - 1)
    return xa
```

```python
import functools

import jax
import jax.numpy as jnp
import numpy as np
from jax import lax
from jax.experimental import pallas as pl
from jax.experimental.pallas import tpu as pltpu

F32 = jnp.float32
BF16 = jnp.bfloat16
HIGHEST = lax.Precision.HIGHEST

D_MODEL = 1024
CTX_LEN = 256
GRID_W = 64
RMS_EPS = 1e-6
NEG_INF = -1e30
ROPE_BASE = 10000.0

NA_HEADS, NA_HD, NA_WIN_R, NA_WIN_C = 8, 128, 8, 16
NA_QROWS = 4
NA_KROWS = NA_QROWS + NA_WIN_R - 1
NA_UNROLL = 4
ML_HEADS, ML_DK, ML_DV, ML_CAP = 8, 64, 128, 15.0
ML_SCAN = 256
SW_HEADS, SW_KV, SW_HD, SW_WINDOW = 16, 4, 64, 128
GL_HEADS, GL_DK, GL_DV, GL_RANK, GL_TAU = 4, 128, 256, 16, 16.0
GL_SCAN = 128
MOE_GROUPS, MOE_PER_GROUP, MOE_EXPERTS, MOE_FF = 4, 8, 32, 512
MOE_BLOCK = 512

LANES = 128
SUBLANES = 8
TM = 256
PAIR = 2
VMEM_LIMIT = 48 * 1024 * 1024
DMA_UNROLL = 16
DISPATCH_TILE = 3 * TM


def _cparams(*sem):
    return pltpu.CompilerParams(dimension_semantics=sem, vmem_limit_bytes=VMEM_LIMIT)


def _norm_mod(x, g, mod_ref, k):
    y = x * lax.rsqrt(jnp.mean(x * x, axis=-1, keepdims=True) + RMS_EPS) * g
    return y * (1.0 + mod_ref[k + 1:k + 2, :]) + mod_ref[k:k + 1, :]


def _log_sigmoid(x):
    return jnp.minimum(x, 0.0) - jnp.log(1.0 + jnp.exp(-jnp.abs(x)))


def _silu(x):
    return x * jax.nn.sigmoid(x)


def _mod_kernel(c_ref, w_ref, b_ref, o_ref):
    s = _silu(c_ref[...])
    o_ref[...] = jnp.dot(s.astype(BF16), w_ref[...].astype(BF16), preferred_element_type=F32) + b_ref[...]


def _ada_mods(cond, ada_w, ada_b):
    depth, d, _ = ada_w.shape
    rows = cond.shape[0]
    out = pl.pallas_call(
        _mod_kernel,
        grid=(depth, 6),
        in_specs=[pl.BlockSpec((rows, d), lambda i, n: (0, 0)),
                  pl.BlockSpec((None, d, d), lambda i, n: (i, 0, n)),
                  pl.BlockSpec((None, 1, d), lambda i, n: (i, 0, n))],
        out_specs=pl.BlockSpec((None, rows, d), lambda i, n: (i, 0, n)),
        out_shape=jax.ShapeDtypeStruct((depth, rows, 6 * d), F32),
        compiler_params=_cparams("arbitrary", "arbitrary"),
        name="ada_mods",
    )(cond, ada_w, ada_b.reshape(depth, 1, 6 * d))
    return out.reshape(depth, rows, 6, d)


def _mod_spec(nb):
    return pl.BlockSpec((PAIR, 6, D_MODEL), lambda b, t: (jnp.where(t == 0, nb // PAIR, b), 0, 0))


def _paired(body, paired, *refs):
    for sub in range(PAIR):
        body(*[r.at[sub] if p else r for r, p in zip(refs, paired)])


def _proj_call(kernel, x, mods, g, w, extras, out_cols, out_dtypes, name):
    nb, l, d = x.shape
    nt = l // TM
    n = w.shape[1]
    in_specs = [pl.BlockSpec((PAIR, TM, d), lambda b, t: (b, t, 0)),
                _mod_spec(nb),
                pl.BlockSpec((1, d), lambda b, t: (0, 0)),
                pl.BlockSpec((d, n), lambda b, t: (0, 0))]
    args = [x, mods, g.reshape(1, d), w]
    for e, per_tile in extras:
        if per_tile:
            in_specs.append(pl.BlockSpec((TM, e.shape[1]), lambda b, t: (t, 0)))
        else:
            in_specs.append(pl.BlockSpec(e.shape, lambda b, t: (0, 0)))
        args.append(e)
    paired = (True, True) + (False,) * (len(in_specs) - 2) + (True,) * len(out_cols)
    return pl.pallas_call(
        functools.partial(_paired, kernel, paired),
        grid=(nb // PAIR, nt),
        in_specs=in_specs,
        out_specs=[pl.BlockSpec((PAIR, TM, c), lambda b, t: (b, t, 0)) for c in out_cols],
        out_shape=[jax.ShapeDtypeStruct((nb, l, c), dt) for c, dt in zip(out_cols, out_dtypes)],
        compiler_params=_cparams("arbitrary", "arbitrary"),
        name=name,
    )(*args)


def _proj_na_kernel(x_ref, mod_ref, g_ref, w_ref, qkg_ref, o_ref):
    h = _norm_mod(x_ref[...], g_ref[...], mod_ref, 0)
    u = jnp.dot(h.astype(BF16), w_ref[...], preferred_element_type=F32)
    nq = NA_HEADS * NA_HD
    for part in range(2):
        gain = qkg_ref[part:part + 1, :]
        if part == 0:
            gain = gain * (NA_HD ** -0.5)
        for hh in range(NA_HEADS):
            lo = part * nq + hh * NA_HD
            z = u[:, lo:lo + NA_HD]
            z = z * lax.rsqrt(jnp.mean(z * z, axis=-1, keepdims=True) + RMS_EPS) * gain
            o_ref[:, lo:lo + NA_HD] = z.astype(BF16)
    o_ref[:, 2 * nq:] = u[:, 2 * nq:].astype(BF16)


def _head64_rms(z, gain):
    lane = lax.broadcasted_iota(jnp.int32, z.shape, 1)
    lo = lane < SW_HD
    zz = z * z
    s_lo = jnp.sum(jnp.where(lo, zz, 0.0), axis=-1, keepdims=True)
    s_hi = jnp.sum(jnp.where(lo, 0.0, zz), axis=-1, keepdims=True)
    ms = jnp.where(lo, s_lo, s_hi) * (1.0 / SW_HD)
    return z * lax.rsqrt(ms + RMS_EPS) * gain


def _rope_slab(z, cs, sn):
    lane = lax.broadcasted_iota(jnp.int32, z.shape, 1)
    first = (lane % 32) < 16
    partner = jnp.where(first, pltpu.roll(z, LANES - 16, 1), pltpu.roll(z, 16, 1))
    return z * cs + partner * sn


def _proj_sw_kernel(x_ref, mod_ref, g_ref, w_ref, qkg_ref, cs_ref, sn_ref, q_ref, kv_ref):
    h = _norm_mod(x_ref[...], g_ref[...], mod_ref, 0)
    u = jnp.dot(h.astype(BF16), w_ref[...], preferred_element_type=F32)
    cs, sn = cs_ref[...], sn_ref[...]
    nq, nk = SW_HEADS * SW_HD, SW_KV * SW_HD
    scale = SW_HD ** -0.5
    for s in range(nq // LANES):
        z = _head64_rms(u[:, s * LANES:(s + 1) * LANES], qkg_ref[0:1, :])
        q_ref[:, s * LANES:(s + 1) * LANES] = (_rope_slab(z, cs, sn) * scale).astype(BF16)
    for s in range(nk // LANES):
        z = _head64_rms(u[:, nq + s * LANES:nq + (s + 1) * LANES], qkg_ref[1:2, :])
        kv_ref[:, s * LANES:(s + 1) * LANES] = _rope_slab(z, cs, sn).astype(BF16)
    kv_ref[:, nk:] = u[:, nq + nk:].astype(BF16)


def _proj_ml_kernel(x_ref, mod_ref, g_ref, w_ref, bg_ref, qkv_ref, og_ref, gt_ref):
    h = _norm_mod(x_ref[...], g_ref[...], mod_ref, 0)
    u = jnp.dot(h.astype(BF16), w_ref[...], preferred_element_type=F32)
    nq, nv = ML_HEADS * ML_DK, ML_HEADS * ML_DV
    qkv_ref[:, :nq] = (u[:, :nq] * (ML_DK ** -0.5)).astype(BF16)
    qkv_ref[:, nq:] = u[:, nq:2 * nq + nv].astype(BF16)
    og_ref[...] = jax.nn.sigmoid(u[:, 2 * nq + nv:2 * nq + 2 * nv]).astype(BF16)
    pre = u[:, 2 * nq + 2 * nv:] + bg_ref[...]
    pre = ML_CAP * jnp.tanh(pre / ML_CAP)
    lane = lax.broadcasted_iota(jnp.int32, pre.shape, 1)
    gt_ref[...] = jnp.where((lane % 2) == 1, _log_sigmoid(pre), pre)


def _proj_gl_kernel(x_ref, mod_ref, g_ref, w_ref, wa_ref, ba_ref, qkv_ref, gate_ref, la_ref):
    h = _norm_mod(x_ref[...], g_ref[...], mod_ref, 0)
    u = jnp.dot(h.astype(BF16), w_ref[...], preferred_element_type=F32)
    nq, nv = GL_HEADS * GL_DK, GL_HEADS * GL_DV
    qkv_ref[...] = u[:, :2 * nq + nv].astype(BF16)
    gate_ref[...] = _silu(u[:, 2 * nq + nv:2 * nq + 2 * nv]).astype(BF16)
    z = u[:, 2 * nq + 2 * nv:]
    a = jnp.dot(z.astype(BF16), wa_ref[...], preferred_element_type=F32) + ba_ref[...]
    la_ref[...] = _log_sigmoid(a) * (1.0 / GL_TAU)


def _na_bias_table(rpb, n_rows):
    cq = np.arange(GRID_W)[:, None]
    ck = np.arange(GRID_W)[None, :]
    c0 = np.clip(cq - NA_WIN_C // 2, 0, GRID_W - NA_WIN_C)
    col_ok = (ck >= c0) & (ck < c0 + NA_WIN_C)
    dc = np.clip(ck - cq + NA_WIN_C - 1, 0, 2 * NA_WIN_C - 2)
    pick = ((dc[None] == np.arange(2 * NA_WIN_C - 1)[:, None, None]) & col_ok[None]).astype(np.float32)
    m = jnp.einsum('hrd,dqk->hqrk', rpb.astype(F32), jnp.asarray(pick), precision=HIGHEST)
    m = jnp.where(jnp.asarray(col_ok)[None, :, None, :], m, NEG_INF)
    blocks = []
    for jj in range(n_rows // NA_QROWS):
        ws = min(max(NA_QROWS * jj - NA_WIN_R // 2, 0), n_rows - NA_KROWS)
        per_row = []
        for ri in range(NA_QROWS):
            r = NA_QROWS * jj + ri
            r0 = min(max(r - NA_WIN_R // 2, 0), n_rows - NA_WIN_R)
            lead = r0 - ws
            d0 = r0 - r + NA_WIN_R - 1
            piece = jnp.pad(m[:, :, d0:d0 + NA_WIN_R, :],
                            ((0, 0), (0, 0), (lead, NA_KROWS - NA_WIN_R - lead), (0, 0)), constant_values=NEG_INF)
            per_row.append(piece.reshape(rpb.shape[0], GRID_W, NA_KROWS * GRID_W))
        blocks.append(jnp.concatenate(per_row, axis=1))
    return jnp.stack(blocks, axis=1)


def _na_kernel(q_ref, k_ref, v_ref, bias_ref, o_ref, *, n_rows):
    j = pl.program_id(1)
    nb = q_ref.shape[0]
    nt = (((1,), (1,)), ((), ()))
    nkw = NA_KROWS * GRID_W

    @pl.when(j == 0)
    def _ctx():
        def body(b, c):
            q = q_ref[b]
            s = lax.dot_general(q, k_ref[b, 0:CTX_LEN, :], nt, preferred_element_type=F32)
            p = jnp.exp(s - jnp.max(s, axis=-1, keepdims=True))
            l = jnp.sum(p, axis=-1, keepdims=True)
            o = jnp.dot(p.astype(BF16), v_ref[b, 0:CTX_LEN, :], preferred_element_type=F32)
            o_ref[b] = (o / l).astype(BF16)
            return c
        lax.fori_loop(0, nb, body, 0, unroll=NA_UNROLL)

    @pl.when(j > 0)
    def _lat():
        ws = jnp.clip(NA_QROWS * (j - 1) - NA_WIN_R // 2, 0, n_rows - NA_KROWS)
        start = pl.multiple_of(CTX_LEN + ws * GRID_W, GRID_W)
        bias = bias_ref[...]

        def body(b, c):
            q = q_ref[b]
            s_n = lax.dot_general(q, k_ref[b, pl.ds(start, nkw), :], nt, preferred_element_type=F32) + bias
            s_c = lax.dot_general(q, k_ref[b, 0:CTX_LEN, :], nt, preferred_element_type=F32)
            m = jnp.maximum(jnp.max(s_n, axis=-1, keepdims=True), jnp.max(s_c, axis=-1, keepdims=True))
            p_n = jnp.exp(s_n - m)
            p_c = jnp.exp(s_c - m)
            l = jnp.sum(p_n, axis=-1, keepdims=True) + jnp.sum(p_c, axis=-1, keepdims=True)
            o = (jnp.dot(p_n.astype(BF16), v_ref[b, pl.ds(start, nkw), :], preferred_element_type=F32)
                 + jnp.dot(p_c.astype(BF16), v_ref[b, 0:CTX_LEN, :], preferred_element_type=F32))
            o_ref[b] = (o / l).astype(BF16)
            return c
        lax.fori_loop(0, nb, body, 0, unroll=NA_UNROLL)


def _na_attention(qkv, bias_tab):
    nb, l, _ = qkv.shape
    n_rows = (l - CTX_LEN) // GRID_W
    nt = l // TM
    hq = NA_HEADS
    return pl.pallas_call(
        functools.partial(_na_kernel, n_rows=n_rows),
        grid=(hq, nt),
        in_specs=[pl.BlockSpec((nb, TM, NA_HD), lambda h, j: (0, j, h)),
                  pl.BlockSpec((nb, l, NA_HD), lambda h, j: (0, 0, hq + h)),
                  pl.BlockSpec((nb, l, NA_HD), lambda h, j: (0, 0, 2 * hq + h)),
                  pl.BlockSpec((None, None, TM, NA_KROWS * GRID_W),
                               lambda h, j: (h, jnp.maximum(j - 1, 0), 0, 0))],
        out_specs=pl.BlockSpec((nb, TM, NA_HD), lambda h, j: (0, j, h)),
        out_shape=jax.ShapeDtypeStruct((nb, l, hq * NA_HD), BF16),
        compiler_params=_cparams("arbitrary", "arbitrary"),
        name="na_attention",
    )(qkv, qkv, qkv, bias_tab)


def _sw_kernel(sink_ref, q_ref, kv_ref, o_ref, *, seq):
    j = pl.program_id(1)
    nt = (((1,), (1,)), ((), ()))
    nkv = SW_KV * SW_HD
    kwin = TM + 2 * SW_WINDOW
    low = lax.broadcasted_iota(jnp.int32, (1, LANES), 1) < SW_HD

    def attend(segs):
        for pair in range(SW_KV // 2):
            kc = slice(pair * LANES, (pair + 1) * LANES)
            vc = slice(nkv + pair * LANES, nkv + (pair + 1) * LANES)
            ks = [kv_ref[rows, kc] for rows, _ in segs]
            vs = [kv_ref[rows, vc] for rows, _ in segs]
            one = jnp.ones((), BF16)
            v_half = [[jnp.where(low, v, one) for v in vs], [jnp.where(low, one, v) for v in vs]]
            for i in range(SW_HEADS // SW_KV):
                cols = slice((4 * pair + i) * LANES, (4 * pair + i + 1) * LANES)
                qs = q_ref[:, cols]
                outs = []
                for half in range(2):
                    sink = sink_ref[8 * pair + 4 * half + i]
                    qm = jnp.where(low if half == 0 else jnp.logical_not(low), qs, jnp.zeros((), BF16))
                    sc = []
                    for k, (_, mask) in zip(ks, segs):
                        s = lax.dot_general(qm, k, nt, preferred_element_type=F32)
                        sc.append(s if mask is None else jnp.where(mask, s, NEG_INF))
                    m = sink
                    for s in sc:
                        m = jnp.maximum(m, jnp.max(s, axis=-1, keepdims=True))
                    acc = None
                    for s, v in zip(sc, v_half[half]):
                        pv = jnp.dot(jnp.exp((s - m).astype(BF16)), v, preferred_element_type=F32)
                        acc = pv if acc is None else acc + pv
                    denom = pltpu.roll(acc, SW_HD, 1) + jnp.exp(sink - m)
                    outs.append(acc / denom)
                o_ref[:, cols] = jnp.where(low, outs[0], outs[1]).astype(BF16)

    @pl.when(j == 0)
    def _ctx():
        attend([(slice(0, CTX_LEN), None)])

    @pl.when(j > 0)
    def _lat():
        q0 = (j - 1) * TM
        ws = jnp.clip(q0 - SW_WINDOW, 0, seq - kwin)
        start = pl.multiple_of(CTX_LEN + ws, SW_WINDOW)
        qpos = q0 + lax.broadcasted_iota(jnp.int32, (TM, kwin), 0)
        kpos = ws + lax.broadcasted_iota(jnp.int32, (TM, kwin), 1)
        ok = jnp.abs(qpos - kpos) <= SW_WINDOW
        attend([(pl.ds(start, kwin), ok), (slice(0, CTX_LEN), None)])


def _sw_attention(q, kv, sink):
    nb, l, nq = q.shape
    nt = l // TM
    return pl.pallas_call(
        functools.partial(_sw_kernel, seq=l - CTX_LEN),
        grid=(nb, nt),
        in_specs=[pl.BlockSpec(memory_space=pltpu.SMEM),
                  pl.BlockSpec((None, TM, nq), lambda b, j: (b, j, 0)),
                  pl.BlockSpec((None, l, kv.shape[2]), lambda b, j: (b, 0, 0))],
        out_specs=pl.BlockSpec((None, TM, nq), lambda b, j: (b, j, 0)),
        out_shape=jax.ShapeDtypeStruct((nb, l, nq), BF16),
        compiler_params=_cparams("arbitrary", "arbitrary"),
        name="sw_attention",
    )(sink, q, kv)


def _chunk_index(step, rev, n_ctx, n_all):
    if not rev:
        return step
    return jnp.where(step < n_ctx, n_ctx - 1 - step, n_all + n_ctx - 1 - step)


def _tri(n, rev):
    r = lax.broadcasted_iota(jnp.int32, (n, n), 0)
    c = lax.broadcasted_iota(jnp.int32, (n, n), 1)
    return (c >= r) if rev else (c <= r)


def _split3(x):
    hi = x.astype(BF16)
    r = x - hi.astype(F32)
    mid = r.astype(BF16)
    return hi, mid, (r - mid.astype(F32)).astype(BF16)


def _ml_kernel(q_ref, kt_ref, v_ref, og_ref, gr_ref, gc_ref, ng_ref, y_ref, acc_ref, *, n_ctx, n_all):
    lc = ML_SCAN
    hps = q_ref.shape[1] // ML_DK
    ones_v = jnp.ones((lc, ML_DV), BF16)
    lane_head = lax.broadcasted_iota(jnp.int32, (1, hps * ML_DK), 1) // ML_DK
    row_head = lax.broadcasted_iota(jnp.int32, (hps * ML_DK, 1), 0) // ML_DK

    def gate_forms(step, rev):
        c = _chunk_index(step, rev, n_ctx, n_all)
        r0 = pl.multiple_of(c * lc, lc)
        g_r = gr_ref[:, pl.ds(r0, lc)]
        g_c = pltpu.roll(gc_ref[pl.ds(r0, lc), :], (LANES - pl.program_id(1) * SUBLANES) % LANES, 1)
        cum3 = jnp.dot(jnp.concatenate(_split3(g_r), axis=0), _tri(lc, not rev).astype(BF16),
                       preferred_element_type=F32)
        cum_r = cum3[0:SUBLANES] + cum3[SUBLANES:2 * SUBLANES] + cum3[2 * SUBLANES:]
        tri_c = _tri(lc, rev).astype(BF16)
        cum_c = sum(jnp.dot(tri_c, part, preferred_element_type=F32) for part in _split3(g_c))
        return c, g_r, cum_r, cum_c

    def chain(forms, hl, rev, state):
        ct, m = state
        c, g_r, cum_r, cum_c = forms
        r0 = pl.multiple_of(c * lc, lc)
        base = hl * 4 + (2 if rev else 0)
        bf_r = cum_r[base + 1:base + 2, :]
        bf_c = cum_c[:, base + 1:base + 2]
        x_r = g_r[base:base + 1, :] - bf_r
        g = jnp.sum(g_r[base + 1:base + 2, :], axis=1, keepdims=True)
        tri = _tri(lc, rev)
        pm_c = jnp.max(jnp.where(tri, x_r, -jnp.inf), axis=1, keepdims=True)
        xmax = jnp.max(x_r, axis=1, keepdims=True)
        q = jnp.where(lane_head == hl, q_ref[pl.ds(r0, lc), :], jnp.zeros((), BF16))
        kt = kt_ref[:, pl.ds(r0, lc)]
        v1 = jnp.concatenate([v_ref[pl.ds(r0, lc), hl * ML_DV:(hl + 1) * ML_DV], ones_v], axis=1)
        mx = jnp.maximum(m, pm_c)
        e = jnp.exp(jnp.where(tri, x_r - mx, NEG_INF))
        sc = jnp.dot(q, kt, preferred_element_type=F32) * e
        e_int = jnp.exp(m - mx)
        intra = jnp.dot(sc.astype(BF16), v1, preferred_element_type=F32)
        inter = jnp.dot(q, ct.astype(BF16), preferred_element_type=F32)
        num = intra[:, :ML_DV] + e_int * inter[:, :ML_DV]
        den = intra[:, ML_DV:ML_DV + 1] + e_int * inter[:, ML_DV:ML_DV + 1]
        hout = num / jnp.maximum(jnp.abs(den), jnp.exp(-mx - bf_c))
        acc_ref[pl.ds(r0, lc), hl * ML_DV:(hl + 1) * ML_DV] += hout
        m_loc = g + xmax
        w_r = jnp.exp(x_r - xmax)
        ktw = jnp.where(row_head == hl, kt.astype(F32) * w_r, 0.0).astype(BF16)
        c_loc = jnp.dot(ktw, v1, preferred_element_type=F32)
        m_new = jnp.maximum(g + m, m_loc)
        dec = jnp.exp(g + m - m_new)
        inc = jnp.exp(m_loc - m_new)
        return dec * ct + inc * c_loc, m_new

    acc_ref[...] = jnp.zeros_like(acc_ref)
    chains = [(hl, rev) for hl in range(hps) for rev in (False, True)]
    init = tuple((jnp.zeros((hps * ML_DK, 2 * ML_DV), F32), jnp.zeros((1, 1), F32)) for _ in chains)

    def step_fn(step, states):
        forms = {rev: gate_forms(step, rev) for rev in (False, True)}
        return tuple(chain(forms[rev], hl, rev, st) for (hl, rev), st in zip(chains, states))

    lax.fori_loop(0, n_all, step_fn, init)

    for hl in range(hps):
        cols = slice(hl * ML_DV, (hl + 1) * ML_DV)
        z = acc_ref[:, cols]
        z = z * lax.rsqrt(jnp.mean(z * z, axis=-1, keepdims=True) + RMS_EPS) * ng_ref[:, cols]
        y_ref[:, cols] = (z.astype(BF16) * og_ref[:, cols])


def _ml_mixer(qkv, og, gates, norm_g):
    nb, l, _ = qkv.shape
    lc = ML_SCAN
    n_all, n_ctx = l // lc, CTX_LEN // lc
    hps = 2
    ngrp = ML_HEADS // hps
    nq = ML_HEADS * ML_DK
    g = gates[:, :, :4 * ML_HEADS].transpose(0, 2, 1)
    kt = qkv[:, :, nq:2 * nq].transpose(0, 2, 1)
    vb = hps * ML_DV
    return pl.pallas_call(
        functools.partial(_ml_kernel, n_ctx=n_ctx, n_all=n_all),
        grid=(nb, ngrp),
        in_specs=[pl.BlockSpec((None, l, hps * ML_DK), lambda b, h: (b, 0, h)),
                  pl.BlockSpec((None, hps * ML_DK, l), lambda b, h: (b, h, 0)),
                  pl.BlockSpec((None, l, vb), lambda b, h: (b, 0, 2 * nq // vb + h)),
                  pl.BlockSpec((None, l, vb), lambda b, h: (b, 0, h)),
                  pl.BlockSpec((None, hps * 4, l), lambda b, h: (b, h, 0)),
                  pl.BlockSpec((None, l, LANES), lambda b, h: (b, 0, 0)),
                  pl.BlockSpec((1, vb), lambda b, h: (0, h))],
        out_specs=pl.BlockSpec((None, l, vb), lambda b, h: (b, 0, h)),
        out_shape=jax.ShapeDtypeStruct((nb, l, ML_HEADS * ML_DV), BF16),
        scratch_shapes=[pltpu.VMEM((l, vb), F32)],
        compiler_params=_cparams("arbitrary", "arbitrary"),
        name="mlstm_mixer",
    )(qkv, kt, qkv, og, g, gates, norm_g.reshape(1, -1))


def _gl_kernel(q_ref, k_ref, v_ref, gate_ref, la0_ref, la1_ref, ng_ref, y_ref, acc_ref, st_ref, *, n_ctx, n_all):
    lc = GL_SCAN
    hps = q_ref.shape[1] // GL_DK
    t0 = (((0,), (0,)), ((), ()))
    nt = (((1,), (1,)), ((), ()))
    scale = GL_DK ** -0.5

    def chain(step, hl, rev):
        la_ref = la1_ref if rev else la0_ref
        sidx = 2 * hl + (1 if rev else 0)
        kc = slice(hl * GL_DK, (hl + 1) * GL_DK)
        vc = slice(hl * GL_DV, (hl + 1) * GL_DV)
        c = _chunk_index(step, rev, n_ctx, n_all)
        r0 = pl.multiple_of(c * lc, lc)
        tri = _tri(lc, rev)
        la = la_ref[pl.ds(r0, lc), kc]
        bc3 = jnp.dot(tri.astype(BF16), jnp.concatenate(_split3(la), axis=1), preferred_element_type=F32)
        bc = bc3[:, :GL_DK] + bc3[:, GL_DK:2 * GL_DK] + bc3[:, 2 * GL_DK:]
        g = jnp.sum(la, axis=0, keepdims=True)
        q = q_ref[pl.ds(r0, lc), kc].astype(F32) * scale
        k = k_ref[pl.ds(r0, lc), kc].astype(F32)
        v = v_ref[pl.ds(r0, lc), vc]
        st = st_ref[sidx]
        eg = jnp.exp(g)
        k_dec = k * jnp.exp(-bc)
        q_t = (q * jnp.exp(bc)).astype(BF16)
        k_t = k_dec.astype(BF16)
        att = jnp.where(tri, lax.dot_general(q_t, k_t, nt, preferred_element_type=F32), 0.0)
        o = (jnp.dot(att.astype(BF16), v, preferred_element_type=F32)
             + lax.dot_general(q_t, st.astype(BF16), nt, preferred_element_type=F32))
        acc_ref[pl.ds(r0, lc), vc] += o
        kd = (k_dec * eg).astype(BF16)
        s_loc = lax.dot_general(v, kd, t0, preferred_element_type=F32)
        st_ref[sidx] = st * eg + s_loc

    acc_ref[...] = jnp.zeros_like(acc_ref)
    st_ref[...] = jnp.zeros_like(st_ref)

    def step_fn(step, carry):
        for hl in range(hps):
            chain(step, hl, False)
            chain(step, hl, True)
        return carry

    lax.fori_loop(0, n_all, step_fn, 0)
    for hl in range(hps):
        vc = slice(hl * GL_DV, (hl + 1) * GL_DV)
        z = acc_ref[:, vc]
        z = z * lax.rsqrt(jnp.mean(z * z, axis=-1, keepdims=True) + RMS_EPS) * ng_ref[:, vc]
        y_ref[:, vc] = z.astype(BF16) * gate_ref[:, vc]


def _gl_mixer(qkv, gate, la, norm_g):
    nb, l, _ = qkv.shape
    lc = GL_SCAN
    n_all, n_ctx = l // lc, CTX_LEN // lc
    hps = 2
    ngrp = GL_HEADS // hps
    kb, vb = hps * GL_DK, hps * GL_DV
    nk = GL_HEADS * GL_DK
    return pl.pallas_call(
        functools.partial(_gl_kernel, n_ctx=n_ctx, n_all=n_all),
        grid=(nb, ngrp),
        in_specs=[pl.BlockSpec((None, l, kb), lambda b, h: (b, 0, h)),
                  pl.BlockSpec((None, l, kb), lambda b, h: (b, 0, nk // kb + h)),
                  pl.BlockSpec((None, l, vb), lambda b, h: (b, 0, 2 * nk // vb + h)),
                  pl.BlockSpec((None, l, vb), lambda b, h: (b, 0, h)),
                  pl.BlockSpec((None, l, kb), lambda b, h: (b, 0, h)),
                  pl.BlockSpec((None, l, kb), lambda b, h: (b, 0, nk // kb + h)),
                  pl.BlockSpec((1, vb), lambda b, h: (0, h))],
        out_specs=pl.BlockSpec((None, l, vb), lambda b, h: (b, 0, h)),
        out_shape=jax.ShapeDtypeStruct((nb, l, GL_HEADS * GL_DV), BF16),
        scratch_shapes=[pltpu.VMEM((l, vb), F32), pltpu.VMEM((2 * hps, GL_DV, GL_DK), F32)],
        compiler_params=_cparams("arbitrary", "arbitrary"),
        name="gla_mixer",
    )(qkv, qkv, qkv, gate, la, la, norm_g.reshape(1, -1))


def _store_token_tiles(ref, val):
    rows = val.shape[0]
    for s in range(val.shape[1] // LANES):
        ref[pl.ds(s, rows, stride=SUBLANES), :] = val[:, s * LANES:(s + 1) * LANES]


def _load_token_tiles(ref, rows):
    return jnp.concatenate([ref[pl.ds(s, rows, stride=SUBLANES), :] for s in range(SUBLANES)], axis=1)


def _post_kernel(y_ref, x_ref, mod_ref, wo_ref, g_ref, wr_ref, br_ref, xo_ref, f_ref, rt_ref, cnt_ref, run_ref):
    first = (pl.program_id(0) == 0) & (pl.program_id(1) == 0)

    @pl.when(first)
    def _():
        run_ref[...] = jnp.zeros_like(run_ref)

    for sub in range(PAIR):
        _post_tile(y_ref.at[sub], x_ref.at[sub], mod_ref.at[sub], wo_ref, g_ref, wr_ref, br_ref,
                   xo_ref.at[sub], f_ref.at[sub], rt_ref.at[sub], cnt_ref, run_ref)


def _post_tile(y_ref, x_ref, mod_ref, wo_ref, g_ref, wr_ref, br_ref, xo_ref, f_ref, rt_ref, cnt_ref, run_ref):
    o = jnp.dot(y_ref[...], wo_ref[...], preferred_element_type=F32)
    xn = x_ref[...] + mod_ref[2:3, :] * o
    xo_ref[...] = xn
    f = _norm_mod(xn, g_ref[...], mod_ref, 3)
    _store_token_tiles(f_ref, f)
    f_hi = f.astype(BF16)
    f_lo = (f - f_hi.astype(F32)).astype(BF16)
    hh = jnp.dot(f_hi, wr_ref[...], preferred_element_type=F32)
    lh = jnp.dot(f_lo, wr_ref[:, :LANES], preferred_element_type=F32)
    lg = hh[:, :LANES] + (hh[:, LANES:] + lh) + br_ref[...]
    lane = lax.broadcasted_iota(jnp.int32, lg.shape, 1).astype(F32)

    def top(mask):
        v = jnp.max(jnp.where(mask, lg, -jnp.inf), axis=-1, keepdims=True)
        return v, jnp.min(jnp.where(mask & (lg == v), lane, float(LANES)), axis=-1, keepdims=True)

    gm = lane < MOE_GROUPS
    mg, g_idx = top(gm)
    g_w = 1.0 / jnp.sum(jnp.where(gm, jnp.exp(lg - mg), 0.0), axis=-1, keepdims=True)
    lo = MOE_GROUPS + MOE_PER_GROUP * g_idx
    em = (lane >= lo) & (lane < lo + MOE_PER_GROUP)
    v0, i0 = top(em)
    v1, i1 = top(em & (lane != i0))
    e0, e1 = i0 - MOE_GROUPS, i1 - MOE_GROUPS
    t = jnp.exp(v1 - v0)
    w0 = g_w / (1.0 + t)
    w1 = g_w * t / (1.0 + t)
    oh = ((lane == e0) | (lane == e1)).astype(F32)
    rows = lg.shape[0]
    below = lax.broadcasted_iota(jnp.int32, (rows, rows), 0) > lax.broadcasted_iota(jnp.int32, (rows, rows), 1)
    tot = run_ref[...] + jnp.dot(below.astype(BF16), oh.astype(BF16), preferred_element_type=F32)
    r0 = jnp.sum(jnp.where(lane == e0, tot, 0.0), axis=-1, keepdims=True)
    r1 = jnp.sum(jnp.where(lane == e1, tot, 0.0), axis=-1, keepdims=True)
    run_ref[...] = run_ref[...] + jnp.sum(oh, axis=0, keepdims=True)
    cnt_ref[...] = jnp.broadcast_to(run_ref[...], cnt_ref.shape)
    out = jnp.zeros_like(lg)
    for i, val in enumerate((e0, e1, w0, w1, r0, r1)):
        out = jnp.where(lane == i, val, out)
    rt_ref[...] = out


def _post_call(y, x, mods, w_o, g_ffn, w_r, b_r):
    nb, l, d = x.shape
    nt = l // TM
    tile = lambda b, t: (b, t, 0)
    x_new, f_tiles, rt, cnt = pl.pallas_call(
        _post_kernel,
        grid=(nb // PAIR, nt),
        in_specs=[pl.BlockSpec((PAIR, TM, d), tile),
                  pl.BlockSpec((PAIR, TM, d), tile),
                  _mod_spec(nb),
                  pl.BlockSpec((d, d), lambda b, t: (0, 0)),
                  pl.BlockSpec((1, d), lambda b, t: (0, 0)),
                  pl.BlockSpec((d, 2 * LANES), lambda b, t: (0, 0)),
                  pl.BlockSpec((1, LANES), lambda b, t: (0, 0))],
        out_specs=[pl.BlockSpec((PAIR, TM, d), tile),
                   pl.BlockSpec((PAIR, TM * SUBLANES, LANES), tile),
                   pl.BlockSpec((PAIR, TM, LANES), tile),
                   pl.BlockSpec((SUBLANES, LANES), lambda b, t: (0, 0))],
        out_shape=[jax.ShapeDtypeStruct((nb, l, d), F32),
                   jax.ShapeDtypeStruct((nb, l * SUBLANES, LANES), F32),
                   jax.ShapeDtypeStruct((nb, l, LANES), F32),
                   jax.ShapeDtypeStruct((SUBLANES, LANES), F32)],
        scratch_shapes=[pltpu.VMEM((1, LANES), F32)],
        compiler_params=_cparams("arbitrary", "arbitrary"),
        name="outproj_router",
    )(y, x, mods, w_o, g_ffn.reshape(1, d), w_r, b_r)
    return x_new, f_tiles.reshape(nb * l * SUBLANES, LANES), rt.reshape(nb * l, LANES), cnt


def _dispatch_kernel(dest_ref, pend_ref, f_ref, xs_out, zbuf, sem):
    tile = f_ref.shape[0] // SUBLANES
    base = pl.program_id(0) * tile

    @pl.when(pl.program_id(0) == 0)
    def _():
        zbuf[...] = jnp.zeros_like(zbuf)

        def zero_block(blk):
            start = pl.multiple_of(blk * (MOE_BLOCK * SUBLANES), MOE_BLOCK * SUBLANES)
            return pltpu.make_async_copy(zbuf, xs_out.at[pl.ds(start, MOE_BLOCK * SUBLANES), :], sem)

        def has_rows(e):
            return pend_ref[e] > (0 if e == 0 else pend_ref[e - 1])

        for e in range(MOE_EXPERTS):
            pl.when(has_rows(e))(lambda e=e: zero_block(pend_ref[e] // MOE_BLOCK - 1).start())
        for e in range(MOE_EXPERTS):
            pl.when(has_rows(e))(lambda e=e: zero_block(pend_ref[e] // MOE_BLOCK - 1).wait())
        n_act = pend_ref[MOE_EXPERTS - 1] // MOE_BLOCK
        n_blk = xs_out.shape[0] // (MOE_BLOCK * SUBLANES)
        lax.fori_loop(n_act, n_blk, lambda b, c: (zero_block(b).start(), c)[1], 0)
        lax.fori_loop(n_act, n_blk, lambda b, c: (zero_block(b).wait(), c)[1], 0)

    def copy(t, k):
        d = dest_ref[2 * (base + t) + k]
        return pltpu.make_async_copy(f_ref.at[pl.ds(t * SUBLANES, SUBLANES), :],
                                     xs_out.at[pl.ds(d * SUBLANES, SUBLANES), :], sem)

    def issue(t, c):
        copy(t, 0).start()
        copy(t, 1).start(priority=1)
        return c

    lax.fori_loop(0, tile, issue, 0, unroll=DMA_UNROLL)
    for _ in range(2):
        pltpu.make_async_copy(f_ref, xs_out.at[pl.ds(0, tile * SUBLANES), :], sem).wait()


def _dispatch(dest, p_end, f_tiles, n_slots):
    n_tok = f_tiles.shape[0] // SUBLANES
    tile = DISPATCH_TILE
    assert n_tok % tile == 0
    return pl.pallas_call(
        _dispatch_kernel,
        grid_spec=pltpu.PrefetchScalarGridSpec(
            num_scalar_prefetch=2,
            grid=(n_tok // tile,),
            in_specs=[pl.BlockSpec((tile * SUBLANES, LANES), lambda i, dr, pe: (i, 0))],
            out_specs=pl.BlockSpec(memory_space=pl.ANY),
            scratch_shapes=[pltpu.VMEM((MOE_BLOCK * SUBLANES, LANES), F32), pltpu.SemaphoreType.DMA(())]),
        out_shape=jax.ShapeDtypeStruct((n_slots * SUBLANES, LANES), F32),
        compiler_params=_cparams("arbitrary"),
        name="moe_dispatch",
    )(dest, p_end, f_tiles)


def _expert_kernel(blk_e_ref, nact_ref, nxt_ref, slot_ref, xs_ref, wg_hbm, wu_hbm, wd_hbm, ys_ref,
                   wg_f, wu_f, wd_f, wg_s, wu_s, wd_s, sem, *, layer):
    i = pl.program_id(0)
    active = i < nact_ref[0]
    e = blk_e_ref[i]
    changed = (i == 0) | (e != blk_e_ref[jnp.maximum(i - 1, 0)])
    slot = slot_ref[i]

    def fetch(expert, s):
        return [pltpu.make_async_copy(w.at[layer, expert], buf.at[s], sem.at[s, k])
                for k, (w, buf) in enumerate(((wg_hbm, wg_f), (wu_hbm, wu_f), (wd_hbm, wd_f)))]

    @pl.when(i == 0)
    def _():
        for c in fetch(e, slot):
            c.start()

    @pl.when(active & changed)
    def _():
        nxt = nxt_ref[i]

        @pl.when(nxt >= 0)
        def _():
            for c in fetch(nxt, 1 - slot):
                c.start()

        for c in fetch(e, slot):
            c.wait()
        wg_s[...] = wg_f[slot].astype(BF16)
        wu_s[...] = wu_f[slot].astype(BF16)
        wd_s[...] = wd_f[slot].astype(BF16)

    @pl.when(active)
    def _():
        x = _load_token_tiles(xs_ref, MOE_BLOCK).astype(BF16)
        a = jnp.dot(x, wg_s[...], preferred_element_type=F32)
        u = jnp.dot(x, wu_s[...], preferred_element_type=F32)
        y = jnp.dot((_silu(a) * u).astype(BF16), wd_s[...], preferred_element_type=F32)
        _store_token_tiles(ys_ref, y)

    @pl.when(jnp.logical_not(active))
    def _():
        ys_ref[...] = jnp.zeros_like(ys_ref)


def _experts(blk_e, nact, nxt_e, slot, xs, w_gate, w_up, w_down, layer):
    n_blk = blk_e.shape[0]
    _, _, d, ff = w_gate.shape
    rows = MOE_BLOCK * SUBLANES
    last = lambda i, na: jnp.minimum(i, jnp.maximum(na[0] - 1, 0))
    hbm = pl.BlockSpec(memory_space=pl.ANY)
    return pl.pallas_call(
        functools.partial(_expert_kernel, layer=layer),
        grid_spec=pltpu.PrefetchScalarGridSpec(
            num_scalar_prefetch=4,
            grid=(n_blk,),
            in_specs=[pl.BlockSpec((rows, LANES), lambda i, be, na, nx, sl: (last(i, na), 0)), hbm, hbm, hbm],
            out_specs=pl.BlockSpec((rows, LANES), lambda i, be, na, nx, sl: (i, 0)),
            scratch_shapes=[pltpu.VMEM((2, d, ff), F32), pltpu.VMEM((2, d, ff), F32), pltpu.VMEM((2, ff, d), F32),
                            pltpu.VMEM((d, ff), BF16), pltpu.VMEM((d, ff), BF16), pltpu.VMEM((ff, d), BF16),
                            pltpu.SemaphoreType.DMA((2, 3))]),
        out_shape=jax.ShapeDtypeStruct(xs.shape, F32),
        compiler_params=_cparams("arbitrary"),
        name="moe_experts",
    )(blk_e, nact, nxt_e, slot, xs, w_gate, w_up, w_down)


def _combine_kernel(dest_ref, ys_hbm, x_ref, rt_ref, mod_ref, o_ref, ybuf, sem, *, nt, t0):
    i = pl.program_id(0)
    n = pl.num_programs(0)
    nte = nt - t0

    def gather(step, slot, start):
        base = ((step // nte) * nt + step % nte + t0) * TM

        def copy(t, k):
            d = dest_ref[2 * (base + t) + k]
            return pltpu.make_async_copy(ys_hbm.at[pl.ds(d * SUBLANES, SUBLANES), :],
                                         ybuf.at[slot, k, pl.ds(t * SUBLANES, SUBLANES), :], sem.at[slot])

        def issue(t, c):
            copy(t, 0).start()
            copy(t, 1).start(priority=1)
            return c

        if start:
            lax.fori_loop(0, TM, issue, 0, unroll=DMA_UNROLL)
        else:
            for k in range(2):
                pltpu.make_async_copy(ys_hbm.at[pl.ds(0, TM * SUBLANES), :], ybuf.at[slot, k], sem.at[slot]).wait()

    slot = i % 2
    pl.when(i == 0)(lambda: gather(i, slot, True))
    pl.when(i + 1 < n)(lambda: gather(i + 1, 1 - slot, True))
    gather(i, slot, False)
    y0 = _load_token_tiles(ybuf.at[slot, 0], TM)
    y1 = _load_token_tiles(ybuf.at[slot, 1], TM)
    rt = rt_ref[...]
    y = rt[:, 2:3] * y0 + rt[:, 3:4] * y1
    o_ref[...] = x_ref[...] + mod_ref[5:6, :] * y


def _combine(dest, ys, x, rt, mods, skip_ctx):
    nb, l, d = x.shape
    nt = l // TM
    t0 = 1 if skip_ctx else 0
    nte = nt - t0
    bt = lambda i: (i // nte, i % nte + t0)
    mod_row = (lambda b, t: b) if skip_ctx else (lambda b, t: jnp.where(t == 0, nb, b))
    return pl.pallas_call(
        functools.partial(_combine_kernel, nt=nt, t0=t0),
        grid_spec=pltpu.PrefetchScalarGridSpec(
            num_scalar_prefetch=1,
            grid=(nb * nte,),
            in_specs=[pl.BlockSpec(memory_space=pl.ANY),
                      pl.BlockSpec((None, TM, d), lambda i, dr: (*bt(i), 0)),
                      pl.BlockSpec((TM, LANES), lambda i, dr: (bt(i)[0] * nt + bt(i)[1], 0)),
                      pl.BlockSpec((None, 6, d), lambda i, dr: (mod_row(*bt(i)), 0, 0))],
            out_specs=pl.BlockSpec((None, TM, d), lambda i, dr: (i // nte, i % nte, 0)),
            scratch_shapes=[pltpu.VMEM((2, 2, TM * SUBLANES, LANES), F32), pltpu.SemaphoreType.DMA((2,))]),
        out_shape=jax.ShapeDtypeStruct((nb, nte * TM, d), F32),
        compiler_params=_cparams("arbitrary"),
        name="moe_combine",
    )(dest, ys, x, rt, mods)


def _moe(y_mix, x, mods, w_o, g_ffn, w_grp, b_grp, w_exp, b_exp, w_gate, w_up, w_down, layer, skip_ctx):
    nb, l, d = x.shape
    n_tok = nb * l
    nr = MOE_GROUPS + MOE_EXPERTS
    w_r = jnp.pad(jnp.concatenate([w_grp, w_exp], axis=1), ((0, 0), (0, LANES - nr)))
    b_r = jnp.pad(jnp.concatenate([b_grp, b_exp]), (0, LANES - nr)).reshape(1, LANES)
    w_hi = w_r.astype(BF16)
    w_r = jnp.concatenate([w_hi, (w_r - w_hi.astype(F32)).astype(BF16)], axis=1)
    x_new, f_tiles, rt, cnt = _post_call(y_mix, x, mods, w_o, g_ffn, w_r, b_r)
    counts = cnt[0, :MOE_EXPERTS].astype(jnp.int32)
    padded = (counts + MOE_BLOCK - 1) // MOE_BLOCK * MOE_BLOCK
    p_end = jnp.cumsum(padded)
    p_start = p_end - padded
    n_pairs = 2 * n_tok
    n_blk = -(-(n_pairs + MOE_EXPERTS * (MOE_BLOCK - 1)) // MOE_BLOCK)
    eid = rt[:, 0:2].astype(jnp.int32)
    first = jnp.sum(jnp.where(eid[:, :, None] == jnp.arange(MOE_EXPERTS, dtype=jnp.int32), p_start, 0), axis=-1)
    dest = (first + rt[:, 4:6].astype(jnp.int32)).reshape(-1)
    blk_row = jnp.arange(n_blk, dtype=jnp.int32) * MOE_BLOCK
    blk_e = jnp.minimum(jnp.sum((p_end[None, :] <= blk_row[:, None]).astype(jnp.int32), axis=1), MOE_EXPERTS - 1)
    nact = (p_end[-1:] // MOE_BLOCK).astype(jnp.int32)
    xs = _dispatch(dest, p_end.astype(jnp.int32), f_tiles, n_blk * MOE_BLOCK)
    has = padded > 0
    ids = jnp.arange(MOE_EXPERTS, dtype=jnp.int32)
    rank = jnp.cumsum(has.astype(jnp.int32)) - has.astype(jnp.int32)
    later = jnp.where((ids[None, :] > ids[:, None]) & has[None, :], ids[None, :], MOE_EXPERTS)
    nxt = jnp.min(later, axis=1)
    nxt = jnp.where(nxt == MOE_EXPERTS, -1, nxt)
    sel = blk_e[:, None] == ids[None, :]
    slot = jnp.sum(jnp.where(sel, rank[None, :] % 2, 0), axis=1).astype(jnp.int32)
    nxt_e = jnp.sum(jnp.where(sel, nxt[None, :], 0), axis=1).astype(jnp.int32)
    ys = _experts(blk_e, nact, nxt_e, slot, xs, w_gate, w_up, w_down, layer)
    return _combine(dest, ys, x_new, rt, mods, skip_ctx)


def _rope_tables(seq):
    nf = SW_HD // 4
    inv = ROPE_BASE ** (-jnp.arange(nf, dtype=F32) / nf)
    pos = jnp.arange(seq, dtype=jnp.int32)
    rows, cols = (pos // GRID_W).astype(F32), (pos % GRID_W).astype(F32)
    lane = jnp.arange(LANES)
    p = jnp.where((lane % SW_HD < SW_HD // 2)[None, :], rows[:, None], cols[:, None])
    ang = p * inv[lane % nf][None, :]
    sign = jnp.where((lane % (2 * nf)) < nf, -1.0, 1.0)[None, :]
    cs = jnp.concatenate([jnp.ones((CTX_LEN, LANES), F32), jnp.cos(ang)], axis=0)
    sn = jnp.concatenate([jnp.zeros((CTX_LEN, LANES), F32), jnp.sin(ang) * sign], axis=0)
    return cs, sn


def _pad_cols(w, n):
    return jnp.pad(w, ((0, 0), (0, n - w.shape[1])))


def kernel(x, c, ctx, c_ctx, ada_w, ada_b, norm_mix_g, norm_ffn_g, na_w_qkv, na_qk_g, na_rpb, na_w_o, ml_w_in, ml_b_gates, ml_norm_g, ml_w_o, sw_w_qkv, sw_qk_g, sw_sink, sw_w_o, gl_w_in, gl_w_a2, gl_b_a, gl_norm_g, gl_w_o, moe_w_grp, moe_b_grp, moe_w_exp, moe_b_exp, moe_w_gate, moe_w_up, moe_w_down):
    nb, seq, d = x.shape
    depth = ada_w.shape[0]
    assert d == D_MODEL and ctx.shape[1] == CTX_LEN == TM and seq % TM == 0 and nb % PAIR == 0
    rows = -(-(nb + PAIR) // SUBLANES) * SUBLANES
    cond = jnp.pad(jnp.concatenate([c] + [c_ctx[None]] * PAIR, axis=0), ((0, rows - nb - PAIR), (0, 0)))
    mods_all = _ada_mods(cond, ada_w, ada_b)
    xa = jnp.concatenate([ctx, x], axis=1)
    for i in range(depth):
        j, kind = divmod(i, 4)
        mods = mods_all[i]
        g_mix = norm_mix_g[i]
        if kind == 0:
            qkg = na_qk_g[j]
            (qkv,) = _proj_call(_proj_na_kernel, xa, mods, g_mix, na_w_qkv[j].astype(BF16), [(qkg, False)],
                                [3 * NA_HEADS * NA_HD], [BF16], "proj_na")
            y = _na_attention(qkv, _na_bias_table(na_rpb[j], seq // GRID_W))
            w_o = na_w_o[j]
        elif kind == 1:
            n_in = -(-ml_w_in.shape[2] // LANES) * LANES
            ng = 4 * ML_HEADS
            w_in = ml_w_in[j]
            w_g = w_in[:, -ng:].reshape(d, 2, 2, ML_HEADS).transpose(0, 3, 1, 2).reshape(d, ng)
            w_in = jnp.concatenate([w_in[:, :-ng], w_g], axis=1)
            bg = ml_b_gates[j].reshape(2, 2, ML_HEADS).transpose(2, 0, 1).reshape(ng)
            bg = jnp.pad(bg, (0, LANES - ng)).reshape(1, LANES)
            qkv, og, gates = _proj_call(_proj_ml_kernel, xa, mods, g_mix, _pad_cols(w_in, n_in).astype(BF16),
                                        [(bg, False)], [2 * ML_HEADS * ML_DK + ML_HEADS * ML_DV, ML_HEADS * ML_DV, LANES],
                                        [BF16, BF16, F32], "proj_ml")
            y = _ml_mixer(qkv, og, gates, ml_norm_g[j])
            w_o = ml_w_o[j]
        elif kind == 2:
            qkg = jnp.concatenate([sw_qk_g[j], sw_qk_g[j]], axis=1)
            cs, sn = _rope_tables(seq)
            nqc = SW_HEADS * SW_HD
            w_in = sw_w_qkv[j]
            w_q = w_in[:, :nqc].reshape(d, 2, 2, 4, SW_HD).transpose(0, 1, 3, 2, 4).reshape(d, nqc)
            w_in = jnp.concatenate([w_q, w_in[:, nqc:]], axis=1)
            q, kv = _proj_call(_proj_sw_kernel, xa, mods, g_mix, w_in.astype(BF16),
                               [(qkg, False), (cs, True), (sn, True)],
                               [nqc, 2 * SW_KV * SW_HD], [BF16, BF16], "proj_sw")
            y = _sw_attention(q, kv, sw_sink[j])
            w_o = sw_w_o[j].reshape(2, 2, 4, SW_HD, d).transpose(0, 2, 1, 3, 4).reshape(nqc, d)
        else:
            n_in = -(-gl_w_in.shape[2] // LANES) * LANES
            nk = GL_HEADS * GL_DK
            wa = jnp.zeros((LANES, 2 * nk), F32)
            wa = wa.at[:GL_RANK, :nk].set(gl_w_a2[j, 0]).at[GL_RANK:2 * GL_RANK, nk:].set(gl_w_a2[j, 1])
            ba = gl_b_a[j].reshape(1, 2 * nk)
            qkv, gate, la = _proj_call(_proj_gl_kernel, xa, mods, g_mix, _pad_cols(gl_w_in[j], n_in).astype(BF16),
                                       [(wa.astype(BF16), False), (ba, False)],
                                       [2 * nk + GL_HEADS * GL_DV, GL_HEADS * GL_DV, 2 * nk], [BF16, BF16, F32], "proj_gl")
            y = _gl_mixer(qkv, gate, la, gl_norm_g[j])
            w_o = gl_w_o[j]
        xa = _moe(y, xa, mods, w_o.astype(BF16), norm_ffn_g[i], moe_w_grp[i], moe_b_grp[i], moe_w_exp[i], moe_b_exp[i],
                  moe_w_gate, moe_w_up, moe_w_down, i, i == depth - 1)
    return xa
```

```python
import functools

import jax
import jax.numpy as jnp
import numpy as np
from jax import lax
from jax.experimental import pallas as pl
from jax.experimental.pallas import tpu as pltpu

F32 = jnp.float32
BF16 = jnp.bfloat16
HIGHEST = lax.Precision.HIGHEST

D_MODEL = 1024
CTX_LEN = 256
GRID_W = 64
RMS_EPS = 1e-6
NEG_INF = -1e30
ROPE_BASE = 10000.0

NA_HEADS, NA_HD, NA_WIN_R, NA_WIN_C = 8, 128, 8, 16
NA_QROWS = 4
NA_KROWS = NA_QROWS + NA_WIN_R - 1
NA_UNROLL = 4
ML_HEADS, ML_DK, ML_DV, ML_CAP = 8, 64, 128, 15.0
ML_SCAN = 256
SW_HEADS, SW_KV, SW_HD, SW_WINDOW = 16, 4, 64, 128
GL_HEADS, GL_DK, GL_DV, GL_RANK, GL_TAU = 4, 128, 256, 16, 16.0
GL_SCAN = 128
MOE_GROUPS, MOE_PER_GROUP, MOE_EXPERTS, MOE_FF = 4, 8, 32, 512
MOE_BLOCK = 512

LANES = 128
SUBLANES = 8
TM = 256
PAIR = 2
VMEM_LIMIT = 48 * 1024 * 1024
DMA_UNROLL = 16
DISPATCH_TILE = 2 * TM


def _cparams(*sem):
    return pltpu.CompilerParams(dimension_semantics=sem, vmem_limit_bytes=VMEM_LIMIT)


def _norm_mod(x, g, mod_ref, k):
    y = x * lax.rsqrt(jnp.mean(x * x, axis=-1, keepdims=True) + RMS_EPS) * g
    return y * (1.0 + mod_ref[k + 1:k + 2, :]) + mod_ref[k:k + 1, :]


def _log_sigmoid(x):
    return jnp.minimum(x, 0.0) - jnp.log(1.0 + jnp.exp(-jnp.abs(x)))


def _silu(x):
    return x * jax.nn.sigmoid(x)


def _mod_kernel(c_ref, w_ref, b_ref, o_ref):
    s = _silu(c_ref[...])
    o_ref[...] = jnp.dot(s.astype(BF16), w_ref[...].astype(BF16), preferred_element_type=F32) + b_ref[...]


def _ada_mods(cond, ada_w, ada_b):
    depth, d, _ = ada_w.shape
    rows = cond.shape[0]
    out = pl.pallas_call(
        _mod_kernel,
        grid=(depth, 6),
        in_specs=[pl.BlockSpec((rows, d), lambda i, n: (0, 0)),
                  pl.BlockSpec((None, d, d), lambda i, n: (i, 0, n)),
                  pl.BlockSpec((None, 1, d), lambda i, n: (i, 0, n))],
        out_specs=pl.BlockSpec((None, rows, d), lambda i, n: (i, 0, n)),
        out_shape=jax.ShapeDtypeStruct((depth, rows, 6 * d), F32),
        compiler_params=_cparams("arbitrary", "arbitrary"),
        name="ada_mods",
    )(cond, ada_w, ada_b.reshape(depth, 1, 6 * d))
    return out.reshape(depth, rows, 6, d)


def _mod_spec(nb):
    return pl.BlockSpec((PAIR, 6, D_MODEL), lambda b, t: (jnp.where(t == 0, nb // PAIR, b), 0, 0))


def _paired(body, paired, *refs):
    for sub in range(PAIR):
        body(*[r.at[sub] if p else r for r, p in zip(refs, paired)])


def _proj_call(kernel, x, mods, g, w, extras, out_cols, out_dtypes, name):
    nb, l, d = x.shape
    nt = l // TM
    n = w.shape[1]
    in_specs = [pl.BlockSpec((PAIR, TM, d), lambda b, t: (b, t, 0)),
                _mod_spec(nb),
                pl.BlockSpec((1, d), lambda b, t: (0, 0)),
                pl.BlockSpec((d, n), lambda b, t: (0, 0))]
    args = [x, mods, g.reshape(1, d), w]
    for e, per_tile in extras:
        if per_tile:
            in_specs.append(pl.BlockSpec((TM, e.shape[1]), lambda b, t: (t, 0)))
        else:
            in_specs.append(pl.BlockSpec(e.shape, lambda b, t: (0, 0)))
        args.append(e)
    paired = (True, True) + (False,) * (len(in_specs) - 2) + (True,) * len(out_cols)
    return pl.pallas_call(
        functools.partial(_paired, kernel, paired),
        grid=(nb // PAIR, nt),
        in_specs=in_specs,
        out_specs=[pl.BlockSpec((PAIR, TM, c), lambda b, t: (b, t, 0)) for c in out_cols],
        out_shape=[jax.ShapeDtypeStruct((nb, l, c), dt) for c, dt in zip(out_cols, out_dtypes)],
        compiler_params=_cparams("arbitrary", "arbitrary"),
        name=name,
    )(*args)


def _proj_na_kernel(x_ref, mod_ref, g_ref, w_ref, qkg_ref, o_ref):
    h = _norm_mod(x_ref[...], g_ref[...], mod_ref, 0)
    u = jnp.dot(h.astype(BF16), w_ref[...], preferred_element_type=F32)
    nq = NA_HEADS * NA_HD
    for part in range(2):
        gain = qkg_ref[part:part + 1, :]
        if part == 0:
            gain = gain * (NA_HD ** -0.5)
        for hh in range(NA_HEADS):
            lo = part * nq + hh * NA_HD
            z = u[:, lo:lo + NA_HD]
            z = z * lax.rsqrt(jnp.mean(z * z, axis=-1, keepdims=True) + RMS_EPS) * gain
            o_ref[:, lo:lo + NA_HD] = z.astype(BF16)
    o_ref[:, 2 * nq:] = u[:, 2 * nq:].astype(BF16)


def _head64_rms(z, gain):
    lane = lax.broadcasted_iota(jnp.int32, z.shape, 1)
    lo = lane < SW_HD
    zz = z * z
    s_lo = jnp.sum(jnp.where(lo, zz, 0.0), axis=-1, keepdims=True)
    s_hi = jnp.sum(jnp.where(lo, 0.0, zz), axis=-1, keepdims=True)
    ms = jnp.where(lo, s_lo, s_hi) * (1.0 / SW_HD)
    return z * lax.rsqrt(ms + RMS_EPS) * gain


def _rope_slab(z, cs, sn):
    lane = lax.broadcasted_iota(jnp.int32, z.shape, 1)
    first = (lane % 32) < 16
    partner = jnp.where(first, pltpu.roll(z, LANES - 16, 1), pltpu.roll(z, 16, 1))
    return z * cs + partner * sn


def _proj_sw_kernel(x_ref, mod_ref, g_ref, w_ref, qkg_ref, cs_ref, sn_ref, q_ref, kv_ref):
    h = _norm_mod(x_ref[...], g_ref[...], mod_ref, 0)
    u = jnp.dot(h.astype(BF16), w_ref[...], preferred_element_type=F32)
    cs, sn = cs_ref[...], sn_ref[...]
    nq, nk = SW_HEADS * SW_HD, SW_KV * SW_HD
    scale = SW_HD ** -0.5
    for s in range(nq // LANES):
        z = _head64_rms(u[:, s * LANES:(s + 1) * LANES], qkg_ref[0:1, :])
        q_ref[:, s * LANES:(s + 1) * LANES] = (_rope_slab(z, cs, sn) * scale).astype(BF16)
    for s in range(nk // LANES):
        z = _head64_rms(u[:, nq + s * LANES:nq + (s + 1) * LANES], qkg_ref[1:2, :])
        kv_ref[:, s * LANES:(s + 1) * LANES] = _rope_slab(z, cs, sn).astype(BF16)
    kv_ref[:, nk:] = u[:, nq + nk:].astype(BF16)


def _proj_ml_kernel(x_ref, mod_ref, g_ref, w_ref, bg_ref, qkv_ref, og_ref, gt_ref):
    h = _norm_mod(x_ref[...], g_ref[...], mod_ref, 0)
    u = jnp.dot(h.astype(BF16), w_ref[...], preferred_element_type=F32)
    nq, nv = ML_HEADS * ML_DK, ML_HEADS * ML_DV
    qkv_ref[:, :nq] = (u[:, :nq] * (ML_DK ** -0.5)).astype(BF16)
    qkv_ref[:, nq:] = u[:, nq:2 * nq + nv].astype(BF16)
    og_ref[...] = jax.nn.sigmoid(u[:, 2 * nq + nv:2 * nq + 2 * nv]).astype(BF16)
    pre = u[:, 2 * nq + 2 * nv:] + bg_ref[...]
    pre = ML_CAP * jnp.tanh(pre / ML_CAP)
    lane = lax.broadcasted_iota(jnp.int32, pre.shape, 1)
    gt_ref[...] = jnp.where((lane % 2) == 1, _log_sigmoid(pre), pre)


def _proj_gl_kernel(x_ref, mod_ref, g_ref, w_ref, wa_ref, ba_ref, qkv_ref, gate_ref, la_ref):
    h = _norm_mod(x_ref[...], g_ref[...], mod_ref, 0)
    u = jnp.dot(h.astype(BF16), w_ref[...], preferred_element_type=F32)
    nq, nv = GL_HEADS * GL_DK, GL_HEADS * GL_DV
    qkv_ref[...] = u[:, :2 * nq + nv].astype(BF16)
    gate_ref[...] = _silu(u[:, 2 * nq + nv:2 * nq + 2 * nv]).astype(BF16)
    z = u[:, 2 * nq + 2 * nv:]
    a = jnp.dot(z.astype(BF16), wa_ref[...], preferred_element_type=F32) + ba_ref[...]
    la_ref[...] = _log_sigmoid(a) * (1.0 / GL_TAU)


def _na_bias_table(rpb, n_rows):
    cq = np.arange(GRID_W)[:, None]
    ck = np.arange(GRID_W)[None, :]
    c0 = np.clip(cq - NA_WIN_C // 2, 0, GRID_W - NA_WIN_C)
    col_ok = (ck >= c0) & (ck < c0 + NA_WIN_C)
    dc = np.clip(ck - cq + NA_WIN_C - 1, 0, 2 * NA_WIN_C - 2)
    pick = ((dc[None] == np.arange(2 * NA_WIN_C - 1)[:, None, None]) & col_ok[None]).astype(np.float32)
    m = jnp.einsum('hrd,dqk->hqrk', rpb.astype(F32), jnp.asarray(pick), precision=HIGHEST)
    m = jnp.where(jnp.asarray(col_ok)[None, :, None, :], m, NEG_INF)
    blocks = []
    for jj in range(n_rows // NA_QROWS):
        ws = min(max(NA_QROWS * jj - NA_WIN_R // 2, 0), n_rows - NA_KROWS)
        per_row = []
        for ri in range(NA_QROWS):
            r = NA_QROWS * jj + ri
            r0 = min(max(r - NA_WIN_R // 2, 0), n_rows - NA_WIN_R)
            lead = r0 - ws
            d0 = r0 - r + NA_WIN_R - 1
            piece = jnp.pad(m[:, :, d0:d0 + NA_WIN_R, :],
                            ((0, 0), (0, 0), (lead, NA_KROWS - NA_WIN_R - lead), (0, 0)), constant_values=NEG_INF)
            per_row.append(piece.reshape(rpb.shape[0], GRID_W, NA_KROWS * GRID_W))
        blocks.append(jnp.concatenate(per_row, axis=1))
    return jnp.stack(blocks, axis=1)


def _na_kernel(q_ref, k_ref, v_ref, bias_ref, o_ref, *, n_rows):
    j = pl.program_id(1)
    nb = q_ref.shape[0]
    nt = (((1,), (1,)), ((), ()))
    nkw = NA_KROWS * GRID_W

    @pl.when(j == 0)
    def _ctx():
        def body(b, c):
            q = q_ref[b]
            s = lax.dot_general(q, k_ref[b, 0:CTX_LEN, :], nt, preferred_element_type=F32)
            p = jnp.exp(s - jnp.max(s, axis=-1, keepdims=True))
            l = jnp.sum(p, axis=-1, keepdims=True)
            o = jnp.dot(p.astype(BF16), v_ref[b, 0:CTX_LEN, :], preferred_element_type=F32)
            o_ref[b] = (o / l).astype(BF16)
            return c
        lax.fori_loop(0, nb, body, 0, unroll=NA_UNROLL)

    @pl.when(j > 0)
    def _lat():
        ws = jnp.clip(NA_QROWS * (j - 1) - NA_WIN_R // 2, 0, n_rows - NA_KROWS)
        start = pl.multiple_of(CTX_LEN + ws * GRID_W, GRID_W)
        bias = bias_ref[...]

        def body(b, c):
            q = q_ref[b]
            s_n = lax.dot_general(q, k_ref[b, pl.ds(start, nkw), :], nt, preferred_element_type=F32) + bias
            s_c = lax.dot_general(q, k_ref[b, 0:CTX_LEN, :], nt, preferred_element_type=F32)
            m = jnp.maximum(jnp.max(s_n, axis=-1, keepdims=True), jnp.max(s_c, axis=-1, keepdims=True))
            p_n = jnp.exp(s_n - m)
            p_c = jnp.exp(s_c - m)
            l = jnp.sum(p_n, axis=-1, keepdims=True) + jnp.sum(p_c, axis=-1, keepdims=True)
            o = (jnp.dot(p_n.astype(BF16), v_ref[b, pl.ds(start, nkw), :], preferred_element_type=F32)
                 + jnp.dot(p_c.astype(BF16), v_ref[b, 0:CTX_LEN, :], preferred_element_type=F32))
            o_ref[b] = (o / l).astype(BF16)
            return c
        lax.fori_loop(0, nb, body, 0, unroll=NA_UNROLL)


def _na_attention(qkv, bias_tab):
    nb, l, _ = qkv.shape
    n_rows = (l - CTX_LEN) // GRID_W
    nt = l // TM
    hq = NA_HEADS
    return pl.pallas_call(
        functools.partial(_na_kernel, n_rows=n_rows),
        grid=(hq, nt),
        in_specs=[pl.BlockSpec((nb, TM, NA_HD), lambda h, j: (0, j, h)),
                  pl.BlockSpec((nb, l, NA_HD), lambda h, j: (0, 0, hq + h)),
                  pl.BlockSpec((nb, l, NA_HD), lambda h, j: (0, 0, 2 * hq + h)),
                  pl.BlockSpec((None, None, TM, NA_KROWS * GRID_W),
                               lambda h, j: (h, jnp.maximum(j - 1, 0), 0, 0))],
        out_specs=pl.BlockSpec((nb, TM, NA_HD), lambda h, j: (0, j, h)),
        out_shape=jax.ShapeDtypeStruct((nb, l, hq * NA_HD), BF16),
        compiler_params=_cparams("arbitrary", "arbitrary"),
        name="na_attention",
    )(qkv, qkv, qkv, bias_tab)


def _sw_kernel(sink_ref, q_ref, kv_ref, o_ref, *, seq):
    j = pl.program_id(1)
    nt = (((1,), (1,)), ((), ()))
    nkv = SW_KV * SW_HD
    kwin = TM + 2 * SW_WINDOW
    low = lax.broadcasted_iota(jnp.int32, (1, LANES), 1) < SW_HD

    def attend(segs):
        for pair in range(SW_KV // 2):
            kc = slice(pair * LANES, (pair + 1) * LANES)
            vc = slice(nkv + pair * LANES, nkv + (pair + 1) * LANES)
            ks = [kv_ref[rows, kc] for rows, _ in segs]
            vs = [kv_ref[rows, vc] for rows, _ in segs]
            one = jnp.ones((), BF16)
            v_half = [[jnp.where(low, v, one) for v in vs], [jnp.where(low, one, v) for v in vs]]
            for i in range(SW_HEADS // SW_KV):
                cols = slice((4 * pair + i) * LANES, (4 * pair + i + 1) * LANES)
                qs = q_ref[:, cols]
                outs = []
                for half in range(2):
                    sink = sink_ref[8 * pair + 4 * half + i]
                    qm = jnp.where(low if half == 0 else jnp.logical_not(low), qs, jnp.zeros((), BF16))
                    sc = []
                    for k, (_, mask) in zip(ks, segs):
                        s = lax.dot_general(qm, k, nt, preferred_element_type=F32)
                        sc.append(s if mask is None else jnp.where(mask, s, NEG_INF))
                    m = sink
                    for s in sc:
                        m = jnp.maximum(m, jnp.max(s, axis=-1, keepdims=True))
                    acc = None
                    for s, v in zip(sc, v_half[half]):
                        pv = jnp.dot(jnp.exp((s - m).astype(BF16)), v, preferred_element_type=F32)
                        acc = pv if acc is None else acc + pv
                    denom = pltpu.roll(acc, SW_HD, 1) + jnp.exp(sink - m)
                    outs.append(acc / denom)
                o_ref[:, cols] = jnp.where(low, outs[0], outs[1]).astype(BF16)

    @pl.when(j == 0)
    def _ctx():
        attend([(slice(0, CTX_LEN), None)])

    @pl.when(j > 0)
    def _lat():
        q0 = (j - 1) * TM
        ws = jnp.clip(q0 - SW_WINDOW, 0, seq - kwin)
        start = pl.multiple_of(CTX_LEN + ws, SW_WINDOW)
        qpos = q0 + lax.broadcasted_iota(jnp.int32, (TM, kwin), 0)
        kpos = ws + lax.broadcasted_iota(jnp.int32, (TM, kwin), 1)
        ok = jnp.abs(qpos - kpos) <= SW_WINDOW
        attend([(pl.ds(start, kwin), ok), (slice(0, CTX_LEN), None)])


def _sw_attention(q, kv, sink):
    nb, l, nq = q.shape
    nt = l // TM
    return pl.pallas_call(
        functools.partial(_sw_kernel, seq=l - CTX_LEN),
        grid=(nb, nt),
        in_specs=[pl.BlockSpec(memory_space=pltpu.SMEM),
                  pl.BlockSpec((None, TM, nq), lambda b, j: (b, j, 0)),
                  pl.BlockSpec((None, l, kv.shape[2]), lambda b, j: (b, 0, 0))],
        out_specs=pl.BlockSpec((None, TM, nq), lambda b, j: (b, j, 0)),
        out_shape=jax.ShapeDtypeStruct((nb, l, nq), BF16),
        compiler_params=_cparams("arbitrary", "arbitrary"),
        name="sw_attention",
    )(sink, q, kv)


def _chunk_index(step, rev, n_ctx, n_all):
    if not rev:
        return step
    return jnp.where(step < n_ctx, n_ctx - 1 - step, n_all + n_ctx - 1 - step)


def _tri(n, rev):
    r = lax.broadcasted_iota(jnp.int32, (n, n), 0)
    c = lax.broadcasted_iota(jnp.int32, (n, n), 1)
    return (c >= r) if rev else (c <= r)


def _split3(x):
    hi = x.astype(BF16)
    r = x - hi.astype(F32)
    mid = r.astype(BF16)
    return hi, mid, (r - mid.astype(F32)).astype(BF16)


def _ml_kernel(q_ref, kt_ref, v_ref, og_ref, gr_ref, gc_ref, ng_ref, y_ref, acc_ref, *, n_ctx, n_all):
    lc = ML_SCAN
    hps = q_ref.shape[1] // ML_DK
    ones_v = jnp.ones((lc, ML_DV), BF16)
    lane_head = lax.broadcasted_iota(jnp.int32, (1, hps * ML_DK), 1) // ML_DK
    row_head = lax.broadcasted_iota(jnp.int32, (hps * ML_DK, 1), 0) // ML_DK

    def gate_forms(step, rev):
        c = _chunk_index(step, rev, n_ctx, n_all)
        r0 = pl.multiple_of(c * lc, lc)
        g_r = gr_ref[:, pl.ds(r0, lc)]
        g_c = pltpu.roll(gc_ref[pl.ds(r0, lc), :], (LANES - pl.program_id(1) * SUBLANES) % LANES, 1)
        cum3 = jnp.dot(jnp.concatenate(_split3(g_r), axis=0), _tri(lc, not rev).astype(BF16),
                       preferred_element_type=F32)
        cum_r = cum3[0:SUBLANES] + cum3[SUBLANES:2 * SUBLANES] + cum3[2 * SUBLANES:]
        tri_c = _tri(lc, rev).astype(BF16)
        cum_c = sum(jnp.dot(tri_c, part, preferred_element_type=F32) for part in _split3(g_c))
        return c, g_r, cum_r, cum_c

    def chain(forms, hl, rev, state):
        ct, m = state
        c, g_r, cum_r, cum_c = forms
        r0 = pl.multiple_of(c * lc, lc)
        base = hl * 4 + (2 if rev else 0)
        bf_r = cum_r[base + 1:base + 2, :]
        bf_c = cum_c[:, base + 1:base + 2]
        x_r = g_r[base:base + 1, :] - bf_r
        g = jnp.sum(g_r[base + 1:base + 2, :], axis=1, keepdims=True)
        tri = _tri(lc, rev)
        pm_c = jnp.max(jnp.where(tri, x_r, -jnp.inf), axis=1, keepdims=True)
        xmax = jnp.max(x_r, axis=1, keepdims=True)
        q = jnp.where(lane_head == hl, q_ref[pl.ds(r0, lc), :], jnp.zeros((), BF16))
        kt = kt_ref[:, pl.ds(r0, lc)]
        v1 = jnp.concatenate([v_ref[pl.ds(r0, lc), hl * ML_DV:(hl + 1) * ML_DV], ones_v], axis=1)
        mx = jnp.maximum(m, pm_c)
        e = jnp.exp(jnp.where(tri, x_r - mx, NEG_INF))
        sc = jnp.dot(q, kt, preferred_element_type=F32) * e
        e_int = jnp.exp(m - mx)
        intra = jnp.dot(sc.astype(BF16), v1, preferred_element_type=F32)
        inter = jnp.dot(q, ct.astype(BF16), preferred_element_type=F32)
        num = intra[:, :ML_DV] + e_int * inter[:, :ML_DV]
        den = intra[:, ML_DV:ML_DV + 1] + e_int * inter[:, ML_DV:ML_DV + 1]
        hout = num / jnp.maximum(jnp.abs(den), jnp.exp(-mx - bf_c))
        acc_ref[pl.ds(r0, lc), hl * ML_DV:(hl + 1) * ML_DV] += hout
        m_loc = g + xmax
        w_r = jnp.exp(x_r - xmax)
        ktw = jnp.where(row_head == hl, kt.astype(F32) * w_r, 0.0).astype(BF16)
        c_loc = jnp.dot(ktw, v1, preferred_element_type=F32)
        m_new = jnp.maximum(g + m, m_loc)
        dec = jnp.exp(g + m - m_new)
        inc = jnp.exp(m_loc - m_new)
        return dec * ct + inc * c_loc, m_new

    acc_ref[...] = jnp.zeros_like(acc_ref)
    chains = [(hl, rev) for hl in range(hps) for rev in (False, True)]
    init = tuple((jnp.zeros((hps * ML_DK, 2 * ML_DV), F32), jnp.zeros((1, 1), F32)) for _ in chains)

    def step_fn(step, states):
        forms = {rev: gate_forms(step, rev) for rev in (False, True)}
        return tuple(chain(forms[rev], hl, rev, st) for (hl, rev), st in zip(chains, states))

    lax.fori_loop(0, n_all, step_fn, init)

    for hl in range(hps):
        cols = slice(hl * ML_DV, (hl + 1) * ML_DV)
        z = acc_ref[:, cols]
        z = z * lax.rsqrt(jnp.mean(z * z, axis=-1, keepdims=True) + RMS_EPS) * ng_ref[:, cols]
        y_ref[:, cols] = (z.astype(BF16) * og_ref[:, cols])


def _ml_mixer(qkv, og, gates, norm_g):
    nb, l, _ = qkv.shape
    lc = ML_SCAN
    n_all, n_ctx = l // lc, CTX_LEN // lc
    hps = 2
    ngrp = ML_HEADS // hps
    nq = ML_HEADS * ML_DK
    g = gates[:, :, :4 * ML_HEADS].transpose(0, 2, 1)
    kt = qkv[:, :, nq:2 * nq].transpose(0, 2, 1)
    vb = hps * ML_DV
    return pl.pallas_call(
        functools.partial(_ml_kernel, n_ctx=n_ctx, n_all=n_all),
        grid=(nb, ngrp),
        in_specs=[pl.BlockSpec((None, l, hps * ML_DK), lambda b, h: (b, 0, h)),
                  pl.BlockSpec((None, hps * ML_DK, l), lambda b, h: (b, h, 0)),
                  pl.BlockSpec((None, l, vb), lambda b, h: (b, 0, 2 * nq // vb + h)),
                  pl.BlockSpec((None, l, vb), lambda b, h: (b, 0, h)),
                  pl.BlockSpec((None, hps * 4, l), lambda b, h: (b, h, 0)),
                  pl.BlockSpec((None, l, LANES), lambda b, h: (b, 0, 0)),
                  pl.BlockSpec((1, vb), lambda b, h: (0, h))],
        out_specs=pl.BlockSpec((None, l, vb), lambda b, h: (b, 0, h)),
        out_shape=jax.ShapeDtypeStruct((nb, l, ML_HEADS * ML_DV), BF16),
        scratch_shapes=[pltpu.VMEM((l, vb), F32)],
        compiler_params=_cparams("arbitrary", "arbitrary"),
        name="mlstm_mixer",
    )(qkv, kt, qkv, og, g, gates, norm_g.reshape(1, -1))


def _gl_kernel(q_ref, k_ref, v_ref, gate_ref, la0_ref, la1_ref, ng_ref, y_ref, acc_ref, st_ref, *, n_ctx, n_all):
    lc = GL_SCAN
    hps = q_ref.shape[1] // GL_DK
    t0 = (((0,), (0,)), ((), ()))
    nt = (((1,), (1,)), ((), ()))
    scale = GL_DK ** -0.5

    def chain(step, hl, rev):
        la_ref = la1_ref if rev else la0_ref
        sidx = 2 * hl + (1 if rev else 0)
        kc = slice(hl * GL_DK, (hl + 1) * GL_DK)
        vc = slice(hl * GL_DV, (hl + 1) * GL_DV)
        c = _chunk_index(step, rev, n_ctx, n_all)
        r0 = pl.multiple_of(c * lc, lc)
        tri = _tri(lc, rev)
        la = la_ref[pl.ds(r0, lc), kc]
        bc3 = jnp.dot(tri.astype(BF16), jnp.concatenate(_split3(la), axis=1), preferred_element_type=F32)
        bc = bc3[:, :GL_DK] + bc3[:, GL_DK:2 * GL_DK] + bc3[:, 2 * GL_DK:]
        g = jnp.sum(la, axis=0, keepdims=True)
        q = q_ref[pl.ds(r0, lc), kc].astype(F32) * scale
        k = k_ref[pl.ds(r0, lc), kc].astype(F32)
        v = v_ref[pl.ds(r0, lc), vc]
        st = st_ref[sidx]
        eg = jnp.exp(g)
        k_dec = k * jnp.exp(-bc)
        q_t = (q * jnp.exp(bc)).astype(BF16)
        k_t = k_dec.astype(BF16)
        att = jnp.where(tri, lax.dot_general(q_t, k_t, nt, preferred_element_type=F32), 0.0)
        o = (jnp.dot(att.astype(BF16), v, preferred_element_type=F32)
             + lax.dot_general(q_t, st.astype(BF16), nt, preferred_element_type=F32))
        acc_ref[pl.ds(r0, lc), vc] += o
        kd = (k_dec * eg).astype(BF16)
        s_loc = lax.dot_general(v, kd, t0, preferred_element_type=F32)
        st_ref[sidx] = st * eg + s_loc

    acc_ref[...] = jnp.zeros_like(acc_ref)
    st_ref[...] = jnp.zeros_like(st_ref)

    def step_fn(step, carry):
        for hl in range(hps):
            chain(step, hl, False)
            chain(step, hl, True)
        return carry

    lax.fori_loop(0, n_all, step_fn, 0)
    for hl in range(hps):
        vc = slice(hl * GL_DV, (hl + 1) * GL_DV)
        z = acc_ref[:, vc]
        z = z * lax.rsqrt(jnp.mean(z * z, axis=-1, keepdims=True) + RMS_EPS) * ng_ref[:, vc]
        y_ref[:, vc] = z.astype(BF16) * gate_ref[:, vc]


def _gl_mixer(qkv, gate, la, norm_g):
    nb, l, _ = qkv.shape
    lc = GL_SCAN
    n_all, n_ctx = l // lc, CTX_LEN // lc
    hps = 2
    ngrp = GL_HEADS // hps
    kb, vb = hps * GL_DK, hps * GL_DV
    nk = GL_HEADS * GL_DK
    return pl.pallas_call(
        functools.partial(_gl_kernel, n_ctx=n_ctx, n_all=n_all),
        grid=(nb, ngrp),
        in_specs=[pl.BlockSpec((None, l, kb), lambda b, h: (b, 0, h)),
                  pl.BlockSpec((None, l, kb), lambda b, h: (b, 0, nk // kb + h)),
                  pl.BlockSpec((None, l, vb), lambda b, h: (b, 0, 2 * nk // vb + h)),
                  pl.BlockSpec((None, l, vb), lambda b, h: (b, 0, h)),
                  pl.BlockSpec((None, l, kb), lambda b, h: (b, 0, h)),
                  pl.BlockSpec((None, l, kb), lambda b, h: (b, 0, nk // kb + h)),
                  pl.BlockSpec((1, vb), lambda b, h: (0, h))],
        out_specs=pl.BlockSpec((None, l, vb), lambda b, h: (b, 0, h)),
        out_shape=jax.ShapeDtypeStruct((nb, l, GL_HEADS * GL_DV), BF16),
        scratch_shapes=[pltpu.VMEM((l, vb), F32), pltpu.VMEM((2 * hps, GL_DV, GL_DK), F32)],
        compiler_params=_cparams("arbitrary", "arbitrary"),
        name="gla_mixer",
    )(qkv, qkv, qkv, gate, la, la, norm_g.reshape(1, -1))


def _store_token_tiles(ref, val):
    rows = val.shape[0]
    for s in range(val.shape[1] // LANES):
        ref[pl.ds(s, rows, stride=SUBLANES), :] = val[:, s * LANES:(s + 1) * LANES]


def _load_token_tiles(ref, rows):
    return jnp.concatenate([ref[pl.ds(s, rows, stride=SUBLANES), :] for s in range(SUBLANES)], axis=1)


def _post_kernel(y_ref, x_ref, mod_ref, wo_ref, g_ref, wr_ref, br_ref, xo_ref, f_ref, rt_ref, cnt_ref, run_ref):
    first = (pl.program_id(0) == 0) & (pl.program_id(1) == 0)

    @pl.when(first)
    def _():
        run_ref[...] = jnp.zeros_like(run_ref)

    for sub in range(PAIR):
        _post_tile(y_ref.at[sub], x_ref.at[sub], mod_ref.at[sub], wo_ref, g_ref, wr_ref, br_ref,
                   xo_ref.at[sub], f_ref.at[sub], rt_ref.at[sub], cnt_ref, run_ref)


def _post_tile(y_ref, x_ref, mod_ref, wo_ref, g_ref, wr_ref, br_ref, xo_ref, f_ref, rt_ref, cnt_ref, run_ref):
    o = jnp.dot(y_ref[...], wo_ref[...], preferred_element_type=F32)
    xn = x_ref[...] + mod_ref[2:3, :] * o
    xo_ref[...] = xn
    f = _norm_mod(xn, g_ref[...], mod_ref, 3)
    _store_token_tiles(f_ref, f)
    f_hi = f.astype(BF16)
    f_lo = (f - f_hi.astype(F32)).astype(BF16)
    hh = jnp.dot(f_hi, wr_ref[...], preferred_element_type=F32)
    lh = jnp.dot(f_lo, wr_ref[:, :LANES], preferred_element_type=F32)
    lg = hh[:, :LANES] + (hh[:, LANES:] + lh) + br_ref[...]
    lane = lax.broadcasted_iota(jnp.int32, lg.shape, 1).astype(F32)

    def top(mask):
        v = jnp.max(jnp.where(mask, lg, -jnp.inf), axis=-1, keepdims=True)
        return v, jnp.min(jnp.where(mask & (lg == v), lane, float(LANES)), axis=-1, keepdims=True)

    gm = lane < MOE_GROUPS
    mg, g_idx = top(gm)
    g_w = 1.0 / jnp.sum(jnp.where(gm, jnp.exp(lg - mg), 0.0), axis=-1, keepdims=True)
    lo = MOE_GROUPS + MOE_PER_GROUP * g_idx
    em = (lane >= lo) & (lane < lo + MOE_PER_GROUP)
    v0, i0 = top(em)
    v1, i1 = top(em & (lane != i0))
    e0, e1 = i0 - MOE_GROUPS, i1 - MOE_GROUPS
    t = jnp.exp(v1 - v0)
    w0 = g_w / (1.0 + t)
    w1 = g_w * t / (1.0 + t)
    oh = ((lane == e0) | (lane == e1)).astype(F32)
    rows = lg.shape[0]
    below = lax.broadcasted_iota(jnp.int32, (rows, rows), 0) > lax.broadcasted_iota(jnp.int32, (rows, rows), 1)
    tot = run_ref[...] + jnp.dot(below.astype(BF16), oh.astype(BF16), preferred_element_type=F32)
    r0 = jnp.sum(jnp.where(lane == e0, tot, 0.0), axis=-1, keepdims=True)
    r1 = jnp.sum(jnp.where(lane == e1, tot, 0.0), axis=-1, keepdims=True)
    run_ref[...] = run_ref[...] + jnp.sum(oh, axis=0, keepdims=True)
    cnt_ref[...] = jnp.broadcast_to(run_ref[...], cnt_ref.shape)
    out = jnp.zeros_like(lg)
    for i, val in enumerate((e0, e1, w0, w1, r0, r1)):
        out = jnp.where(lane == i, val, out)
    rt_ref[...] = out


def _post_call(y, x, mods, w_o, g_ffn, w_r, b_r, skip_ctx):
    nb, l, d = x.shape
    t0 = 1 if skip_ctx else 0
    nt = l // TM - t0
    l = nt * TM
    src = lambda b, t: (b, t + t0, 0)
    tile = lambda b, t: (b, t, 0)
    mod_spec = pl.BlockSpec((PAIR, 6, d), lambda b, t: (b, 0, 0)) if skip_ctx else _mod_spec(nb)
    x_new, f_tiles, rt, cnt = pl.pallas_call(
        _post_kernel,
        grid=(nb // PAIR, nt),
        in_specs=[pl.BlockSpec((PAIR, TM, d), src),
                  pl.BlockSpec((PAIR, TM, d), src),
                  mod_spec,
                  pl.BlockSpec((d, d), lambda b, t: (0, 0)),
                  pl.BlockSpec((1, d), lambda b, t: (0, 0)),
                  pl.BlockSpec((d, 2 * LANES), lambda b, t: (0, 0)),
                  pl.BlockSpec((1, LANES), lambda b, t: (0, 0))],
        out_specs=[pl.BlockSpec((PAIR, TM, d), tile),
                   pl.BlockSpec((PAIR, TM * SUBLANES, LANES), tile),
                   pl.BlockSpec((PAIR, TM, LANES), tile),
                   pl.BlockSpec((SUBLANES, LANES), lambda b, t: (0, 0))],
        out_shape=[jax.ShapeDtypeStruct((nb, l, d), F32),
                   jax.ShapeDtypeStruct((nb, l * SUBLANES, LANES), F32),
                   jax.ShapeDtypeStruct((nb, l, LANES), F32),
                   jax.ShapeDtypeStruct((SUBLANES, LANES), F32)],
        scratch_shapes=[pltpu.VMEM((1, LANES), F32)],
        compiler_params=_cparams("arbitrary", "arbitrary"),
        name="outproj_router",
    )(y, x, mods, w_o, g_ffn.reshape(1, d), w_r, b_r)
    return x_new, f_tiles.reshape(nb * l * SUBLANES, LANES), rt.reshape(nb * l, LANES), cnt


def _dispatch_kernel(dest_ref, pend_ref, f_ref, xs_out, zbuf, sem):
    tile = f_ref.shape[0] // SUBLANES
    base = pl.program_id(0) * tile

    @pl.when(pl.program_id(0) == 0)
    def _():
        zbuf[...] = jnp.zeros_like(zbuf)

        def zero_block(blk):
            start = pl.multiple_of(blk * (MOE_BLOCK * SUBLANES), MOE_BLOCK * SUBLANES)
            return pltpu.make_async_copy(zbuf, xs_out.at[pl.ds(start, MOE_BLOCK * SUBLANES), :], sem)

        def has_rows(e):
            return pend_ref[e] > (0 if e == 0 else pend_ref[e - 1])

        for e in range(MOE_EXPERTS):
            pl.when(has_rows(e))(lambda e=e: zero_block(pend_ref[e] // MOE_BLOCK - 1).start())
        for e in range(MOE_EXPERTS):
            pl.when(has_rows(e))(lambda e=e: zero_block(pend_ref[e] // MOE_BLOCK - 1).wait())
        n_act = pend_ref[MOE_EXPERTS - 1] // MOE_BLOCK
        n_blk = xs_out.shape[0] // (MOE_BLOCK * SUBLANES)
        lax.fori_loop(n_act, n_blk, lambda b, c: (zero_block(b).start(), c)[1], 0)
        lax.fori_loop(n_act, n_blk, lambda b, c: (zero_block(b).wait(), c)[1], 0)

    def copy(t, k):
        d = dest_ref[2 * (base + t) + k]
        return pltpu.make_async_copy(f_ref.at[pl.ds(t * SUBLANES, SUBLANES), :],
                                     xs_out.at[pl.ds(d * SUBLANES, SUBLANES), :], sem)

    def issue(t, c):
        copy(t, 0).start()
        copy(t, 1).start(priority=1)
        return c

    lax.fori_loop(0, tile, issue, 0, unroll=DMA_UNROLL)
    for _ in range(2):
        pltpu.make_async_copy(f_ref, xs_out.at[pl.ds(0, tile * SUBLANES), :], sem).wait()


def _dispatch(dest, p_end, f_tiles, n_slots):
    n_tok = f_tiles.shape[0] // SUBLANES
    tile = DISPATCH_TILE
    assert n_tok % tile == 0
    return pl.pallas_call(
        _dispatch_kernel,
        grid_spec=pltpu.PrefetchScalarGridSpec(
            num_scalar_prefetch=2,
            grid=(n_tok // tile,),
            in_specs=[pl.BlockSpec((tile * SUBLANES, LANES), lambda i, dr, pe: (i, 0))],
            out_specs=pl.BlockSpec(memory_space=pl.ANY),
            scratch_shapes=[pltpu.VMEM((MOE_BLOCK * SUBLANES, LANES), F32), pltpu.SemaphoreType.DMA(())]),
        out_shape=jax.ShapeDtypeStruct((n_slots * SUBLANES, LANES), F32),
        compiler_params=_cparams("arbitrary"),
        name="moe_dispatch",
    )(dest, p_end, f_tiles)


def _expert_kernel(blk_e_ref, nact_ref, nxt_ref, slot_ref, xs_ref, wg_hbm, wu_hbm, wd_hbm, ys_ref,
                   wg_f, wu_f, wd_f, wg_s, wu_s, wd_s, sem, *, layer):
    i = pl.program_id(0)
    active = i < nact_ref[0]
    e = blk_e_ref[i]
    changed = (i == 0) | (e != blk_e_ref[jnp.maximum(i - 1, 0)])
    slot = slot_ref[i]

    def fetch(expert, s):
        return [pltpu.make_async_copy(w.at[layer, expert], buf.at[s], sem.at[s, k])
                for k, (w, buf) in enumerate(((wg_hbm, wg_f), (wu_hbm, wu_f), (wd_hbm, wd_f)))]

    @pl.when(i == 0)
    def _():
        for c in fetch(e, slot):
            c.start()

    @pl.when(active & changed)
    def _():
        nxt = nxt_ref[i]

        @pl.when(nxt >= 0)
        def _():
            for c in fetch(nxt, 1 - slot):
                c.start()

        for c in fetch(e, slot):
            c.wait()
        wg_s[...] = wg_f[slot].astype(BF16)
        wu_s[...] = wu_f[slot].astype(BF16)
        wd_s[...] = wd_f[slot].astype(BF16)

    @pl.when(active)
    def _():
        x = _load_token_tiles(xs_ref, MOE_BLOCK).astype(BF16)
        a = jnp.dot(x, wg_s[...], preferred_element_type=F32)
        u = jnp.dot(x, wu_s[...], preferred_element_type=F32)
        y = jnp.dot((_silu(a) * u).astype(BF16), wd_s[...], preferred_element_type=F32)
        _store_token_tiles(ys_ref, y)

    @pl.when(jnp.logical_not(active))
    def _():
        ys_ref[...] = jnp.zeros_like(ys_ref)


def _experts(blk_e, nact, nxt_e, slot, xs, w_gate, w_up, w_down, layer):
    n_blk = blk_e.shape[0]
    _, _, d, ff = w_gate.shape
    rows = MOE_BLOCK * SUBLANES
    last = lambda i, na: jnp.minimum(i, jnp.maximum(na[0] - 1, 0))
    hbm = pl.BlockSpec(memory_space=pl.ANY)
    return pl.pallas_call(
        functools.partial(_expert_kernel, layer=layer),
        grid_spec=pltpu.PrefetchScalarGridSpec(
            num_scalar_prefetch=4,
            grid=(n_blk,),
            in_specs=[pl.BlockSpec((rows, LANES), lambda i, be, na, nx, sl: (last(i, na), 0)), hbm, hbm, hbm],
            out_specs=pl.BlockSpec((rows, LANES), lambda i, be, na, nx, sl: (i, 0)),
            scratch_shapes=[pltpu.VMEM((2, d, ff), F32), pltpu.VMEM((2, d, ff), F32), pltpu.VMEM((2, ff, d), F32),
                            pltpu.VMEM((d, ff), BF16), pltpu.VMEM((d, ff), BF16), pltpu.VMEM((ff, d), BF16),
                            pltpu.SemaphoreType.DMA((2, 3))]),
        out_shape=jax.ShapeDtypeStruct(xs.shape, F32),
        compiler_params=_cparams("arbitrary"),
        name="moe_experts",
    )(blk_e, nact, nxt_e, slot, xs, w_gate, w_up, w_down)


def _combine_kernel(dest_ref, ys_hbm, x_ref, rt_ref, mod_ref, o_ref, ybuf, sem):
    i = pl.program_id(0)
    n = pl.num_programs(0)

    def gather(step, slot, start):
        base = step * TM

        def copy(t, k):
            d = dest_ref[2 * (base + t) + k]
            return pltpu.make_async_copy(ys_hbm.at[pl.ds(d * SUBLANES, SUBLANES), :],
                                         ybuf.at[slot, k, pl.ds(t * SUBLANES, SUBLANES), :], sem.at[slot])

        def issue(t, c):
            copy(t, 0).start()
            copy(t, 1).start(priority=1)
            return c

        if start:
            lax.fori_loop(0, TM, issue, 0, unroll=DMA_UNROLL)
        else:
            for k in range(2):
                pltpu.make_async_copy(ys_hbm.at[pl.ds(0, TM * SUBLANES), :], ybuf.at[slot, k], sem.at[slot]).wait()

    slot = i % 2
    pl.when(i == 0)(lambda: gather(i, slot, True))
    pl.when(i + 1 < n)(lambda: gather(i + 1, 1 - slot, True))
    gather(i, slot, False)
    y0 = _load_token_tiles(ybuf.at[slot, 0], TM)
    y1 = _load_token_tiles(ybuf.at[slot, 1], TM)
    rt = rt_ref[...]
    y = rt[:, 2:3] * y0 + rt[:, 3:4] * y1
    o_ref[...] = x_ref[...] + mod_ref[5:6, :] * y


def _combine(dest, ys, x, rt, mods, has_ctx):
    nb, l, d = x.shape
    nt = l // TM
    mod_row = (lambda i: jnp.where(i % nt == 0, nb, i // nt)) if has_ctx else (lambda i: i // nt)
    return pl.pallas_call(
        _combine_kernel,
        grid_spec=pltpu.PrefetchScalarGridSpec(
            num_scalar_prefetch=1,
            grid=(nb * nt,),
            in_specs=[pl.BlockSpec(memory_space=pl.ANY),
                      pl.BlockSpec((None, TM, d), lambda i, dr: (i // nt, i % nt, 0)),
                      pl.BlockSpec((TM, LANES), lambda i, dr: (i, 0)),
                      pl.BlockSpec((None, 6, d), lambda i, dr: (mod_row(i), 0, 0))],
            out_specs=pl.BlockSpec((None, TM, d), lambda i, dr: (i // nt, i % nt, 0)),
            scratch_shapes=[pltpu.VMEM((2, 2, TM * SUBLANES, LANES), F32), pltpu.SemaphoreType.DMA((2,))]),
        out_shape=jax.ShapeDtypeStruct((nb, l, d), F32),
        compiler_params=_cparams("arbitrary"),
        name="moe_combine",
    )(dest, ys, x, rt, mods)


def _moe(y_mix, x, mods, w_o, g_ffn, w_grp, b_grp, w_exp, b_exp, w_gate, w_up, w_down, layer, skip_ctx):
    nr = MOE_GROUPS + MOE_EXPERTS
    w_r = jnp.pad(jnp.concatenate([w_grp, w_exp], axis=1), ((0, 0), (0, LANES - nr)))
    b_r = jnp.pad(jnp.concatenate([b_grp, b_exp]), (0, LANES - nr)).reshape(1, LANES)
    w_hi = w_r.astype(BF16)
    w_r = jnp.concatenate([w_hi, (w_r - w_hi.astype(F32)).astype(BF16)], axis=1)
    x_new, f_tiles, rt, cnt = _post_call(y_mix, x, mods, w_o, g_ffn, w_r, b_r, skip_ctx)
    n_tok = rt.shape[0]
    counts = cnt[0, :MOE_EXPERTS].astype(jnp.int32)
    padded = (counts + MOE_BLOCK - 1) // MOE_BLOCK * MOE_BLOCK
    p_end = jnp.cumsum(padded)
    p_start = p_end - padded
    n_pairs = 2 * n_tok
    n_blk = -(-(n_pairs + MOE_EXPERTS * (MOE_BLOCK - 1)) // MOE_BLOCK)
    eid = rt[:, 0:2].astype(jnp.int32)
    first = jnp.sum(jnp.where(eid[:, :, None] == jnp.arange(MOE_EXPERTS, dtype=jnp.int32), p_start, 0), axis=-1)
    dest = (first + rt[:, 4:6].astype(jnp.int32)).reshape(-1)
    blk_row = jnp.arange(n_blk, dtype=jnp.int32) * MOE_BLOCK
    blk_e = jnp.minimum(jnp.sum((p_end[None, :] <= blk_row[:, None]).astype(jnp.int32), axis=1), MOE_EXPERTS - 1)
    nact = (p_end[-1:] // MOE_BLOCK).astype(jnp.int32)
    xs = _dispatch(dest, p_end.astype(jnp.int32), f_tiles, n_blk * MOE_BLOCK)
    has = padded > 0
    ids = jnp.arange(MOE_EXPERTS, dtype=jnp.int32)
    rank = jnp.cumsum(has.astype(jnp.int32)) - has.astype(jnp.int32)
    later = jnp.where((ids[None, :] > ids[:, None]) & has[None, :], ids[None, :], MOE_EXPERTS)
    nxt = jnp.min(later, axis=1)
    nxt = jnp.where(nxt == MOE_EXPERTS, -1, nxt)
    sel = blk_e[:, None] == ids[None, :]
    slot = jnp.sum(jnp.where(sel, rank[None, :] % 2, 0), axis=1).astype(jnp.int32)
    nxt_e = jnp.sum(jnp.where(sel, nxt[None, :], 0), axis=1).astype(jnp.int32)
    ys = _experts(blk_e, nact, nxt_e, slot, xs, w_gate, w_up, w_down, layer)
    return _combine(dest, ys, x_new, rt, mods, not skip_ctx)


def _rope_tables(seq):
    nf = SW_HD // 4
    inv = ROPE_BASE ** (-jnp.arange(nf, dtype=F32) / nf)
    pos = jnp.arange(seq, dtype=jnp.int32)
    rows, cols = (pos // GRID_W).astype(F32), (pos % GRID_W).astype(F32)
    lane = jnp.arange(LANES)
    p = jnp.where((lane % SW_HD < SW_HD // 2)[None, :], rows[:, None], cols[:, None])
    ang = p * inv[lane % nf][None, :]
    sign = jnp.where((lane % (2 * nf)) < nf, -1.0, 1.0)[None, :]
    cs = jnp.concatenate([jnp.ones((CTX_LEN, LANES), F32), jnp.cos(ang)], axis=0)
    sn = jnp.concatenate([jnp.zeros((CTX_LEN, LANES), F32), jnp.sin(ang) * sign], axis=0)
    return cs, sn


def _pad_cols(w, n):
    return jnp.pad(w, ((0, 0), (0, n - w.shape[1])))


def kernel(x, c, ctx, c_ctx, ada_w, ada_b, norm_mix_g, norm_ffn_g, na_w_qkv, na_qk_g, na_rpb, na_w_o, ml_w_in, ml_b_gates, ml_norm_g, ml_w_o, sw_w_qkv, sw_qk_g, sw_sink, sw_w_o, gl_w_in, gl_w_a2, gl_b_a, gl_norm_g, gl_w_o, moe_w_grp, moe_b_grp, moe_w_exp, moe_b_exp, moe_w_gate, moe_w_up, moe_w_down):
    nb, seq, d = x.shape
    depth = ada_w.shape[0]
    assert d == D_MODEL and ctx.shape[1] == CTX_LEN == TM and seq % TM == 0 and nb % PAIR == 0
    rows = -(-(nb + PAIR) // SUBLANES) * SUBLANES
    cond = jnp.pad(jnp.concatenate([c] + [c_ctx[None]] * PAIR, axis=0), ((0, rows - nb - PAIR), (0, 0)))
    mods_all = _ada_mods(cond, ada_w, ada_b)
    xa = jnp.concatenate([ctx, x], axis=1)
    for i in range(depth):
        j, kind = divmod(i, 4)
        mods = mods_all[i]
        g_mix = norm_mix_g[i]
        if kind == 0:
            qkg = na_qk_g[j]
            (qkv,) = _proj_call(_proj_na_kernel, xa, mods, g_mix, na_w_qkv[j].astype(BF16), [(qkg, False)],
                                [3 * NA_HEADS * NA_HD], [BF16], "proj_na")
            y = _na_attention(qkv, _na_bias_table(na_rpb[j], seq // GRID_W))
            w_o = na_w_o[j]
        elif kind == 1:
            n_in = -(-ml_w_in.shape[2] // LANES) * LANES
            ng = 4 * ML_HEADS
            w_in = ml_w_in[j]
            w_g = w_in[:, -ng:].reshape(d, 2, 2, ML_HEADS).transpose(0, 3, 1, 2).reshape(d, ng)
            w_in = jnp.concatenate([w_in[:, :-ng], w_g], axis=1)
            bg = ml_b_gates[j].reshape(2, 2, ML_HEADS).transpose(2, 0, 1).reshape(ng)
            bg = jnp.pad(bg, (0, LANES - ng)).reshape(1, LANES)
            qkv, og, gates = _proj_call(_proj_ml_kernel, xa, mods, g_mix, _pad_cols(w_in, n_in).astype(BF16),
                                        [(bg, False)], [2 * ML_HEADS * ML_DK + ML_HEADS * ML_DV, ML_HEADS * ML_DV, LANES],
                                        [BF16, BF16, F32], "proj_ml")
            y = _ml_mixer(qkv, og, gates, ml_norm_g[j])
            w_o = ml_w_o[j]
        elif kind == 2:
            qkg = jnp.concatenate([sw_qk_g[j], sw_qk_g[j]], axis=1)
            cs, sn = _rope_tables(seq)
            nqc = SW_HEADS * SW_HD
            w_in = sw_w_qkv[j]
            w_q = w_in[:, :nqc].reshape(d, 2, 2, 4, SW_HD).transpose(0, 1, 3, 2, 4).reshape(d, nqc)
            w_in = jnp.concatenate([w_q, w_in[:, nqc:]], axis=1)
            q, kv = _proj_call(_proj_sw_kernel, xa, mods, g_mix, w_in.astype(BF16),
                               [(qkg, False), (cs, True), (sn, True)],
                               [nqc, 2 * SW_KV * SW_HD], [BF16, BF16], "proj_sw")
            y = _sw_attention(q, kv, sw_sink[j])
            w_o = sw_w_o[j].reshape(2, 2, 4, SW_HD, d).transpose(0, 2, 1, 3, 4).reshape(nqc, d)
        else:
            n_in = -(-gl_w_in.shape[2] // LANES) * LANES
            nk = GL_HEADS * GL_DK
            wa = jnp.zeros((LANES, 2 * nk), F32)
            wa = wa.at[:GL_RANK, :nk].set(gl_w_a2[j, 0]).at[GL_RANK:2 * GL_RANK, nk:].set(gl_w_a2[j, 1])
            ba = gl_b_a[j].reshape(1, 2 * nk)
            qkv, gate, la = _proj_call(_proj_gl_kernel, xa, mods, g_mix, _pad_cols(gl_w_in[j], n_in).astype(BF16),
                                       [(wa.astype(BF16), False), (ba, False)],
                                       [2 * nk + GL_HEADS * GL_DV, GL_HEADS * GL_DV, 2 * nk], [BF16, BF16, F32], "proj_gl")
            y = _gl_mixer(qkv, gate, la, gl_norm_g[j])
            w_o = gl_w_o[j]
        xa = _moe(y, xa, mods, w_o.astype(BF16), norm_ffn_g[i], moe_w_grp[i], moe_b_grp[i], moe_w_exp[i], moe_b_exp[i],
                  moe_w_gate, moe_w_up, moe_w_down, i, i == depth - 1)
    return xa
```

```python
import functools

import jax
import jax.numpy as jnp
import numpy as np
from jax import lax
from jax.experimental import pallas as pl
from jax.experimental.pallas import tpu as pltpu

F32 = jnp.float32
BF16 = jnp.bfloat16
HIGHEST = lax.Precision.HIGHEST

D_MODEL = 1024
CTX_LEN = 256
GRID_W = 64
RMS_EPS = 1e-6
NEG_INF = -1e30
ROPE_BASE = 10000.0

NA_HEADS, NA_HD, NA_WIN_R, NA_WIN_C = 8, 128, 8, 16
NA_QROWS = 4
NA_KROWS = NA_QROWS + NA_WIN_R - 1
NA_UNROLL = 4
ML_HEADS, ML_DK, ML_DV, ML_CAP = 8, 64, 128, 15.0
ML_SCAN = 256
SW_HEADS, SW_KV, SW_HD, SW_WINDOW = 16, 4, 64, 128
GL_HEADS, GL_DK, GL_DV, GL_RANK, GL_TAU = 4, 128, 256, 16, 16.0
GL_SCAN = 128
MOE_GROUPS, MOE_PER_GROUP, MOE_EXPERTS, MOE_FF = 4, 8, 32, 512
MOE_BLOCK = 512

LANES = 128
SUBLANES = 8
TM = 256
PAIR = 2
VMEM_LIMIT = 48 * 1024 * 1024
DMA_UNROLL = 16
DISPATCH_TILE = 4 * TM


def _cparams(*sem):
    return pltpu.CompilerParams(dimension_semantics=sem, vmem_limit_bytes=VMEM_LIMIT)


def _norm_mod(x, g, mod_ref, k):
    y = x * lax.rsqrt(jnp.mean(x * x, axis=-1, keepdims=True) + RMS_EPS) * g
    return y * (1.0 + mod_ref[k + 1:k + 2, :]) + mod_ref[k:k + 1, :]


def _log_sigmoid(x):
    return jnp.minimum(x, 0.0) - jnp.log(1.0 + jnp.exp(-jnp.abs(x)))


def _silu(x):
    return x * jax.nn.sigmoid(x)


def _mod_kernel(c_ref, w_ref, b_ref, o_ref):
    s = _silu(c_ref[...])
    o_ref[...] = jnp.dot(s.astype(BF16), w_ref[...].astype(BF16), preferred_element_type=F32) + b_ref[...]


def _ada_mods(cond, ada_w, ada_b):
    depth, d, _ = ada_w.shape
    rows = cond.shape[0]
    out = pl.pallas_call(
        _mod_kernel,
        grid=(depth, 6),
        in_specs=[pl.BlockSpec((rows, d), lambda i, n: (0, 0)),
                  pl.BlockSpec((None, d, d), lambda i, n: (i, 0, n)),
                  pl.BlockSpec((None, 1, d), lambda i, n: (i, 0, n))],
        out_specs=pl.BlockSpec((None, rows, d), lambda i, n: (i, 0, n)),
        out_shape=jax.ShapeDtypeStruct((depth, rows, 6 * d), F32),
        compiler_params=_cparams("arbitrary", "arbitrary"),
        name="ada_mods",
    )(cond, ada_w, ada_b.reshape(depth, 1, 6 * d))
    return out.reshape(depth, rows, 6, d)


def _mod_spec(nb):
    return pl.BlockSpec((PAIR, 6, D_MODEL), lambda b, t: (jnp.where(t == 0, nb // PAIR, b), 0, 0))


def _paired(body, paired, *refs):
    for sub in range(PAIR):
        body(*[r.at[sub] if p else r for r, p in zip(refs, paired)])


def _proj_call(kernel, x, mods, g, w, extras, out_cols, out_dtypes, name):
    nb, l, d = x.shape
    nt = l // TM
    n = w.shape[1]
    in_specs = [pl.BlockSpec((PAIR, TM, d), lambda b, t: (b, t, 0)),
                _mod_spec(nb),
                pl.BlockSpec((1, d), lambda b, t: (0, 0)),
                pl.BlockSpec((d, n), lambda b, t: (0, 0))]
    args = [x, mods, g.reshape(1, d), w]
    for e, per_tile in extras:
        if per_tile:
            in_specs.append(pl.BlockSpec((TM, e.shape[1]), lambda b, t: (t, 0)))
        else:
            in_specs.append(pl.BlockSpec(e.shape, lambda b, t: (0, 0)))
        args.append(e)
    paired = (True, True) + (False,) * (len(in_specs) - 2) + (True,) * len(out_cols)
    return pl.pallas_call(
        functools.partial(_paired, kernel, paired),
        grid=(nb // PAIR, nt),
        in_specs=in_specs,
        out_specs=[pl.BlockSpec((PAIR, TM, c), lambda b, t: (b, t, 0)) for c in out_cols],
        out_shape=[jax.ShapeDtypeStruct((nb, l, c), dt) for c, dt in zip(out_cols, out_dtypes)],
        compiler_params=_cparams("arbitrary", "arbitrary"),
        name=name,
    )(*args)


def _proj_na_kernel(x_ref, mod_ref, g_ref, w_ref, qkg_ref, o_ref):
    h = _norm_mod(x_ref[...], g_ref[...], mod_ref, 0)
    u = jnp.dot(h.astype(BF16), w_ref[...], preferred_element_type=F32)
    nq = NA_HEADS * NA_HD
    for part in range(2):
        gain = qkg_ref[part:part + 1, :]
        if part == 0:
            gain = gain * (NA_HD ** -0.5)
        for hh in range(NA_HEADS):
            lo = part * nq + hh * NA_HD
            z = u[:, lo:lo + NA_HD]
            z = z * lax.rsqrt(jnp.mean(z * z, axis=-1, keepdims=True) + RMS_EPS) * gain
            o_ref[:, lo:lo + NA_HD] = z.astype(BF16)
    o_ref[:, 2 * nq:] = u[:, 2 * nq:].astype(BF16)


def _head64_rms(z, gain):
    lane = lax.broadcasted_iota(jnp.int32, z.shape, 1)
    lo = lane < SW_HD
    zz = z * z
    s_lo = jnp.sum(jnp.where(lo, zz, 0.0), axis=-1, keepdims=True)
    s_hi = jnp.sum(jnp.where(lo, 0.0, zz), axis=-1, keepdims=True)
    ms = jnp.where(lo, s_lo, s_hi) * (1.0 / SW_HD)
    return z * lax.rsqrt(ms + RMS_EPS) * gain


def _rope_slab(z, cs, sn):
    lane = lax.broadcasted_iota(jnp.int32, z.shape, 1)
    first = (lane % 32) < 16
    partner = jnp.where(first, pltpu.roll(z, LANES - 16, 1), pltpu.roll(z, 16, 1))
    return z * cs + partner * sn


def _proj_sw_kernel(x_ref, mod_ref, g_ref, w_ref, qkg_ref, cs_ref, sn_ref, q_ref, kv_ref):
    h = _norm_mod(x_ref[...], g_ref[...], mod_ref, 0)
    u = jnp.dot(h.astype(BF16), w_ref[...], preferred_element_type=F32)
    cs, sn = cs_ref[...], sn_ref[...]
    nq, nk = SW_HEADS * SW_HD, SW_KV * SW_HD
    scale = SW_HD ** -0.5
    for s in range(nq // LANES):
        z = _head64_rms(u[:, s * LANES:(s + 1) * LANES], qkg_ref[0:1, :])
        q_ref[:, s * LANES:(s + 1) * LANES] = (_rope_slab(z, cs, sn) * scale).astype(BF16)
    for s in range(nk // LANES):
        z = _head64_rms(u[:, nq + s * LANES:nq + (s + 1) * LANES], qkg_ref[1:2, :])
        kv_ref[:, s * LANES:(s + 1) * LANES] = _rope_slab(z, cs, sn).astype(BF16)
    kv_ref[:, nk:] = u[:, nq + nk:].astype(BF16)


def _proj_ml_kernel(x_ref, mod_ref, g_ref, w_ref, bg_ref, qkv_ref, og_ref, gt_ref):
    h = _norm_mod(x_ref[...], g_ref[...], mod_ref, 0)
    u = jnp.dot(h.astype(BF16), w_ref[...], preferred_element_type=F32)
    nq, nv = ML_HEADS * ML_DK, ML_HEADS * ML_DV
    qkv_ref[:, :nq] = (u[:, :nq] * (ML_DK ** -0.5)).astype(BF16)
    qkv_ref[:, nq:] = u[:, nq:2 * nq + nv].astype(BF16)
    og_ref[...] = jax.nn.sigmoid(u[:, 2 * nq + nv:2 * nq + 2 * nv]).astype(BF16)
    pre = u[:, 2 * nq + 2 * nv:] + bg_ref[...]
    pre = ML_CAP * jnp.tanh(pre / ML_CAP)
    lane = lax.broadcasted_iota(jnp.int32, pre.shape, 1)
    gt_ref[...] = jnp.where((lane % 2) == 1, _log_sigmoid(pre), pre)


def _proj_gl_kernel(x_ref, mod_ref, g_ref, w_ref, wa_ref, ba_ref, qkv_ref, gate_ref, la_ref):
    h = _norm_mod(x_ref[...], g_ref[...], mod_ref, 0)
    u = jnp.dot(h.astype(BF16), w_ref[...], preferred_element_type=F32)
    nq, nv = GL_HEADS * GL_DK, GL_HEADS * GL_DV
    qkv_ref[...] = u[:, :2 * nq + nv].astype(BF16)
    gate_ref[...] = _silu(u[:, 2 * nq + nv:2 * nq + 2 * nv]).astype(BF16)
    z = u[:, 2 * nq + 2 * nv:]
    a = jnp.dot(z.astype(BF16), wa_ref[...], preferred_element_type=F32) + ba_ref[...]
    la_ref[...] = _log_sigmoid(a) * (1.0 / GL_TAU)


def _na_bias_table(rpb, n_rows):
    cq = np.arange(GRID_W)[:, None]
    ck = np.arange(GRID_W)[None, :]
    c0 = np.clip(cq - NA_WIN_C // 2, 0, GRID_W - NA_WIN_C)
    col_ok = (ck >= c0) & (ck < c0 + NA_WIN_C)
    dc = np.clip(ck - cq + NA_WIN_C - 1, 0, 2 * NA_WIN_C - 2)
    pick = ((dc[None] == np.arange(2 * NA_WIN_C - 1)[:, None, None]) & col_ok[None]).astype(np.float32)
    m = jnp.einsum('hrd,dqk->hqrk', rpb.astype(F32), jnp.asarray(pick), precision=HIGHEST)
    m = jnp.where(jnp.asarray(col_ok)[None, :, None, :], m, NEG_INF)
    blocks = []
    for jj in range(n_rows // NA_QROWS):
        ws = min(max(NA_QROWS * jj - NA_WIN_R // 2, 0), n_rows - NA_KROWS)
        per_row = []
        for ri in range(NA_QROWS):
            r = NA_QROWS * jj + ri
            r0 = min(max(r - NA_WIN_R // 2, 0), n_rows - NA_WIN_R)
            lead = r0 - ws
            d0 = r0 - r + NA_WIN_R - 1
            piece = jnp.pad(m[:, :, d0:d0 + NA_WIN_R, :],
                            ((0, 0), (0, 0), (lead, NA_KROWS - NA_WIN_R - lead), (0, 0)), constant_values=NEG_INF)
            per_row.append(piece.reshape(rpb.shape[0], GRID_W, NA_KROWS * GRID_W))
        blocks.append(jnp.concatenate(per_row, axis=1))
    return jnp.stack(blocks, axis=1)


def _na_kernel(q_ref, k_ref, v_ref, bias_ref, o_ref, *, n_rows):
    j = pl.program_id(1)
    nb = q_ref.shape[0]
    nt = (((1,), (1,)), ((), ()))
    nkw = NA_KROWS * GRID_W

    @pl.when(j == 0)
    def _ctx():
        def body(b, c):
            q = q_ref[b]
            s = lax.dot_general(q, k_ref[b, 0:CTX_LEN, :], nt, preferred_element_type=F32)
            p = jnp.exp(s - jnp.max(s, axis=-1, keepdims=True))
            l = jnp.sum(p, axis=-1, keepdims=True)
            o = jnp.dot(p.astype(BF16), v_ref[b, 0:CTX_LEN, :], preferred_element_type=F32)
            o_ref[b] = (o / l).astype(BF16)
            return c
        lax.fori_loop(0, nb, body, 0, unroll=NA_UNROLL)

    @pl.when(j > 0)
    def _lat():
        ws = jnp.clip(NA_QROWS * (j - 1) - NA_WIN_R // 2, 0, n_rows - NA_KROWS)
        start = pl.multiple_of(CTX_LEN + ws * GRID_W, GRID_W)
        bias = bias_ref[...]

        def body(b, c):
            q = q_ref[b]
            s_n = lax.dot_general(q, k_ref[b, pl.ds(start, nkw), :], nt, preferred_element_type=F32) + bias
            s_c = lax.dot_general(q, k_ref[b, 0:CTX_LEN, :], nt, preferred_element_type=F32)
            m = jnp.maximum(jnp.max(s_n, axis=-1, keepdims=True), jnp.max(s_c, axis=-1, keepdims=True))
            p_n = jnp.exp(s_n - m)
            p_c = jnp.exp(s_c - m)
            l = jnp.sum(p_n, axis=-1, keepdims=True) + jnp.sum(p_c, axis=-1, keepdims=True)
            o = (jnp.dot(p_n.astype(BF16), v_ref[b, pl.ds(start, nkw), :], preferred_element_type=F32)
                 + jnp.dot(p_c.astype(BF16), v_ref[b, 0:CTX_LEN, :], preferred_element_type=F32))
            o_ref[b] = (o / l).astype(BF16)
            return c
        lax.fori_loop(0, nb, body, 0, unroll=NA_UNROLL)


def _na_attention(qkv, bias_tab):
    nb, l, _ = qkv.shape
    n_rows = (l - CTX_LEN) // GRID_W
    nt = l // TM
    hq = NA_HEADS
    return pl.pallas_call(
        functools.partial(_na_kernel, n_rows=n_rows),
        grid=(hq, nt),
        in_specs=[pl.BlockSpec((nb, TM, NA_HD), lambda h, j: (0, j, h)),
                  pl.BlockSpec((nb, l, NA_HD), lambda h, j: (0, 0, hq + h)),
                  pl.BlockSpec((nb, l, NA_HD), lambda h, j: (0, 0, 2 * hq + h)),
                  pl.BlockSpec((None, None, TM, NA_KROWS * GRID_W),
                               lambda h, j: (h, jnp.maximum(j - 1, 0), 0, 0))],
        out_specs=pl.BlockSpec((nb, TM, NA_HD), lambda h, j: (0, j, h)),
        out_shape=jax.ShapeDtypeStruct((nb, l, hq * NA_HD), BF16),
        compiler_params=_cparams("arbitrary", "arbitrary"),
        name="na_attention",
    )(qkv, qkv, qkv, bias_tab)


def _sw_kernel(sink_ref, q_ref, kv_ref, o_ref, *, seq):
    j = pl.program_id(1)
    nt = (((1,), (1,)), ((), ()))
    nkv = SW_KV * SW_HD
    kwin = TM + 2 * SW_WINDOW
    low = lax.broadcasted_iota(jnp.int32, (1, LANES), 1) < SW_HD

    def attend(segs):
        for pair in range(SW_KV // 2):
            kc = slice(pair * LANES, (pair + 1) * LANES)
            vc = slice(nkv + pair * LANES, nkv + (pair + 1) * LANES)
            ks = [kv_ref[rows, kc] for rows, _ in segs]
            vs = [kv_ref[rows, vc] for rows, _ in segs]
            one = jnp.ones((), BF16)
            v_half = [[jnp.where(low, v, one) for v in vs], [jnp.where(low, one, v) for v in vs]]
            for i in range(SW_HEADS // SW_KV):
                cols = slice((4 * pair + i) * LANES, (4 * pair + i + 1) * LANES)
                qs = q_ref[:, cols]
                outs = []
                for half in range(2):
                    sink = sink_ref[8 * pair + 4 * half + i]
                    qm = jnp.where(low if half == 0 else jnp.logical_not(low), qs, jnp.zeros((), BF16))
                    sc = []
                    for k, (_, mask) in zip(ks, segs):
                        s = lax.dot_general(qm, k, nt, preferred_element_type=F32)
                        sc.append(s if mask is None else jnp.where(mask, s, NEG_INF))
                    m = sink
                    for s in sc:
                        m = jnp.maximum(m, jnp.max(s, axis=-1, keepdims=True))
                    acc = None
                    for s, v in zip(sc, v_half[half]):
                        pv = jnp.dot(jnp.exp((s - m).astype(BF16)), v, preferred_element_type=F32)
                        acc = pv if acc is None else acc + pv
                    denom = pltpu.roll(acc, SW_HD, 1) + jnp.exp(sink - m)
                    outs.append(acc / denom)
                o_ref[:, cols] = jnp.where(low, outs[0], outs[1]).astype(BF16)

    @pl.when(j == 0)
    def _ctx():
        attend([(slice(0, CTX_LEN), None)])

    @pl.when(j > 0)
    def _lat():
        q0 = (j - 1) * TM
        ws = jnp.clip(q0 - SW_WINDOW, 0, seq - kwin)
        start = pl.multiple_of(CTX_LEN + ws, SW_WINDOW)
        qpos = q0 + lax.broadcasted_iota(jnp.int32, (TM, kwin), 0)
        kpos = ws + lax.broadcasted_iota(jnp.int32, (TM, kwin), 1)
        ok = jnp.abs(qpos - kpos) <= SW_WINDOW
        attend([(pl.ds(start, kwin), ok), (slice(0, CTX_LEN), None)])


def _sw_attention(q, kv, sink):
    nb, l, nq = q.shape
    nt = l // TM
    return pl.pallas_call(
        functools.partial(_sw_kernel, seq=l - CTX_LEN),
        grid=(nb, nt),
        in_specs=[pl.BlockSpec(memory_space=pltpu.SMEM),
                  pl.BlockSpec((None, TM, nq), lambda b, j: (b, j, 0)),
                  pl.BlockSpec((None, l, kv.shape[2]), lambda b, j: (b, 0, 0))],
        out_specs=pl.BlockSpec((None, TM, nq), lambda b, j: (b, j, 0)),
        out_shape=jax.ShapeDtypeStruct((nb, l, nq), BF16),
        compiler_params=_cparams("arbitrary", "arbitrary"),
        name="sw_attention",
    )(sink, q, kv)


def _chunk_index(step, rev, n_ctx, n_all):
    if not rev:
        return step
    return jnp.where(step < n_ctx, n_ctx - 1 - step, n_all + n_ctx - 1 - step)


def _tri(n, rev):
    r = lax.broadcasted_iota(jnp.int32, (n, n), 0)
    c = lax.broadcasted_iota(jnp.int32, (n, n), 1)
    return (c >= r) if rev else (c <= r)


def _split3(x):
    hi = x.astype(BF16)
    r = x - hi.astype(F32)
    mid = r.astype(BF16)
    return hi, mid, (r - mid.astype(F32)).astype(BF16)


def _ml_kernel(q_ref, kt_ref, v_ref, og_ref, gr_ref, gc_ref, ng_ref, y_ref, acc_ref, *, n_ctx, n_all):
    lc = ML_SCAN
    hps = q_ref.shape[1] // ML_DK
    ones_v = jnp.ones((lc, ML_DV), BF16)
    lane_head = lax.broadcasted_iota(jnp.int32, (1, hps * ML_DK), 1) // ML_DK
    row_head = lax.broadcasted_iota(jnp.int32, (hps * ML_DK, 1), 0) // ML_DK

    def gate_forms(step, rev):
        c = _chunk_index(step, rev, n_ctx, n_all)
        r0 = pl.multiple_of(c * lc, lc)
        g_r = gr_ref[:, pl.ds(r0, lc)]
        g_c = pltpu.roll(gc_ref[pl.ds(r0, lc), :], (LANES - pl.program_id(1) * SUBLANES) % LANES, 1)
        cum3 = jnp.dot(jnp.concatenate(_split3(g_r), axis=0), _tri(lc, not rev).astype(BF16),
                       preferred_element_type=F32)
        cum_r = cum3[0:SUBLANES] + cum3[SUBLANES:2 * SUBLANES] + cum3[2 * SUBLANES:]
        tri_c = _tri(lc, rev).astype(BF16)
        cum_c = sum(jnp.dot(tri_c, part, preferred_element_type=F32) for part in _split3(g_c))
        return c, g_r, cum_r, cum_c

    def chain(forms, hl, rev, state):
        ct, m = state
        c, g_r, cum_r, cum_c = forms
        r0 = pl.multiple_of(c * lc, lc)
        base = hl * 4 + (2 if rev else 0)
        bf_r = cum_r[base + 1:base + 2, :]
        bf_c = cum_c[:, base + 1:base + 2]
        x_r = g_r[base:base + 1, :] - bf_r
        g = jnp.sum(g_r[base + 1:base + 2, :], axis=1, keepdims=True)
        tri = _tri(lc, rev)
        pm_c = jnp.max(jnp.where(tri, x_r, -jnp.inf), axis=1, keepdims=True)
        xmax = jnp.max(x_r, axis=1, keepdims=True)
        q = jnp.where(lane_head == hl, q_ref[pl.ds(r0, lc), :], jnp.zeros((), BF16))
        kt = kt_ref[:, pl.ds(r0, lc)]
        v1 = jnp.concatenate([v_ref[pl.ds(r0, lc), hl * ML_DV:(hl + 1) * ML_DV], ones_v], axis=1)
        mx = jnp.maximum(m, pm_c)
        e = jnp.exp(jnp.where(tri, x_r - mx, NEG_INF))
        sc = jnp.dot(q, kt, preferred_element_type=F32) * e
        e_int = jnp.exp(m - mx)
        intra = jnp.dot(sc.astype(BF16), v1, preferred_element_type=F32)
        inter = jnp.dot(q, ct.astype(BF16), preferred_element_type=F32)
        num = intra[:, :ML_DV] + e_int * inter[:, :ML_DV]
        den = intra[:, ML_DV:ML_DV + 1] + e_int * inter[:, ML_DV:ML_DV + 1]
        hout = num / jnp.maximum(jnp.abs(den), jnp.exp(-mx - bf_c))
        acc_ref[pl.ds(r0, lc), hl * ML_DV:(hl + 1) * ML_DV] += hout
        m_loc = g + xmax
        w_r = jnp.exp(x_r - xmax)
        ktw = jnp.where(row_head == hl, kt.astype(F32) * w_r, 0.0).astype(BF16)
        c_loc = jnp.dot(ktw, v1, preferred_element_type=F32)
        m_new = jnp.maximum(g + m, m_loc)
        dec = jnp.exp(g + m - m_new)
        inc = jnp.exp(m_loc - m_new)
        return dec * ct + inc * c_loc, m_new

    acc_ref[...] = jnp.zeros_like(acc_ref)
    chains = [(hl, rev) for hl in range(hps) for rev in (False, True)]
    init = tuple((jnp.zeros((hps * ML_DK, 2 * ML_DV), F32), jnp.zeros((1, 1), F32)) for _ in chains)

    def step_fn(step, states):
        forms = {rev: gate_forms(step, rev) for rev in (False, True)}
        return tuple(chain(forms[rev], hl, rev, st) for (hl, rev), st in zip(chains, states))

    lax.fori_loop(0, n_all, step_fn, init)

    for hl in range(hps):
        cols = slice(hl * ML_DV, (hl + 1) * ML_DV)
        z = acc_ref[:, cols]
        z = z * lax.rsqrt(jnp.mean(z * z, axis=-1, keepdims=True) + RMS_EPS) * ng_ref[:, cols]
        y_ref[:, cols] = (z.astype(BF16) * og_ref[:, cols])


def _ml_mixer(qkv, og, gates, norm_g):
    nb, l, _ = qkv.shape
    lc = ML_SCAN
    n_all, n_ctx = l // lc, CTX_LEN // lc
    hps = 2
    ngrp = ML_HEADS // hps
    nq = ML_HEADS * ML_DK
    g = gates[:, :, :4 * ML_HEADS].transpose(0, 2, 1)
    kt = qkv[:, :, nq:2 * nq].transpose(0, 2, 1)
    vb = hps * ML_DV
    return pl.pallas_call(
        functools.partial(_ml_kernel, n_ctx=n_ctx, n_all=n_all),
        grid=(nb, ngrp),
        in_specs=[pl.BlockSpec((None, l, hps * ML_DK), lambda b, h: (b, 0, h)),
                  pl.BlockSpec((None, hps * ML_DK, l), lambda b, h: (b, h, 0)),
                  pl.BlockSpec((None, l, vb), lambda b, h: (b, 0, 2 * nq // vb + h)),
                  pl.BlockSpec((None, l, vb), lambda b, h: (b, 0, h)),
                  pl.BlockSpec((None, hps * 4, l), lambda b, h: (b, h, 0)),
                  pl.BlockSpec((None, l, LANES), lambda b, h: (b, 0, 0)),
                  pl.BlockSpec((1, vb), lambda b, h: (0, h))],
        out_specs=pl.BlockSpec((None, l, vb), lambda b, h: (b, 0, h)),
        out_shape=jax.ShapeDtypeStruct((nb, l, ML_HEADS * ML_DV), BF16),
        scratch_shapes=[pltpu.VMEM((l, vb), F32)],
        compiler_params=_cparams("arbitrary", "arbitrary"),
        name="mlstm_mixer",
    )(qkv, kt, qkv, og, g, gates, norm_g.reshape(1, -1))


def _gl_kernel(q_ref, k_ref, v_ref, gate_ref, la0_ref, la1_ref, ng_ref, y_ref, acc_ref, st_ref, *, n_ctx, n_all):
    lc = GL_SCAN
    hps = q_ref.shape[1] // GL_DK
    t0 = (((0,), (0,)), ((), ()))
    nt = (((1,), (1,)), ((), ()))
    scale = GL_DK ** -0.5

    def chain(step, hl, rev):
        la_ref = la1_ref if rev else la0_ref
        sidx = 2 * hl + (1 if rev else 0)
        kc = slice(hl * GL_DK, (hl + 1) * GL_DK)
        vc = slice(hl * GL_DV, (hl + 1) * GL_DV)
        c = _chunk_index(step, rev, n_ctx, n_all)
        r0 = pl.multiple_of(c * lc, lc)
        tri = _tri(lc, rev)
        la = la_ref[pl.ds(r0, lc), kc]
        bc3 = jnp.dot(tri.astype(BF16), jnp.concatenate(_split3(la), axis=1), preferred_element_type=F32)
        bc = bc3[:, :GL_DK] + bc3[:, GL_DK:2 * GL_DK] + bc3[:, 2 * GL_DK:]
        g = jnp.sum(la, axis=0, keepdims=True)
        q = q_ref[pl.ds(r0, lc), kc].astype(F32) * scale
        k = k_ref[pl.ds(r0, lc), kc].astype(F32)
        v = v_ref[pl.ds(r0, lc), vc]
        st = st_ref[sidx]
        eg = jnp.exp(g)
        k_dec = k * jnp.exp(-bc)
        q_t = (q * jnp.exp(bc)).astype(BF16)
        k_t = k_dec.astype(BF16)
        att = jnp.where(tri, lax.dot_general(q_t, k_t, nt, preferred_element_type=F32), 0.0)
        o = (jnp.dot(att.astype(BF16), v, preferred_element_type=F32)
             + lax.dot_general(q_t, st.astype(BF16), nt, preferred_element_type=F32))
        acc_ref[pl.ds(r0, lc), vc] += o
        kd = (k_dec * eg).astype(BF16)
        s_loc = lax.dot_general(v, kd, t0, preferred_element_type=F32)
        st_ref[sidx] = st * eg + s_loc

    acc_ref[...] = jnp.zeros_like(acc_ref)
    st_ref[...] = jnp.zeros_like(st_ref)

    def step_fn(step, carry):
        for hl in range(hps):
            chain(step, hl, False)
            chain(step, hl, True)
        return carry

    lax.fori_loop(0, n_all, step_fn, 0)
    for hl in range(hps):
        vc = slice(hl * GL_DV, (hl + 1) * GL_DV)
        z = acc_ref[:, vc]
        z = z * lax.rsqrt(jnp.mean(z * z, axis=-1, keepdims=True) + RMS_EPS) * ng_ref[:, vc]
        y_ref[:, vc] = z.astype(BF16) * gate_ref[:, vc]


def _gl_mixer(qkv, gate, la, norm_g):
    nb, l, _ = qkv.shape
    lc = GL_SCAN
    n_all, n_ctx = l // lc, CTX_LEN // lc
    hps = 2
    ngrp = GL_HEADS // hps
    kb, vb = hps * GL_DK, hps * GL_DV
    nk = GL_HEADS * GL_DK
    return pl.pallas_call(
        functools.partial(_gl_kernel, n_ctx=n_ctx, n_all=n_all),
        grid=(nb, ngrp),
        in_specs=[pl.BlockSpec((None, l, kb), lambda b, h: (b, 0, h)),
                  pl.BlockSpec((None, l, kb), lambda b, h: (b, 0, nk // kb + h)),
                  pl.BlockSpec((None, l, vb), lambda b, h: (b, 0, 2 * nk // vb + h)),
                  pl.BlockSpec((None, l, vb), lambda b, h: (b, 0, h)),
                  pl.BlockSpec((None, l, kb), lambda b, h: (b, 0, h)),
                  pl.BlockSpec((None, l, kb), lambda b, h: (b, 0, nk // kb + h)),
                  pl.BlockSpec((1, vb), lambda b, h: (0, h))],
        out_specs=pl.BlockSpec((None, l, vb), lambda b, h: (b, 0, h)),
        out_shape=jax.ShapeDtypeStruct((nb, l, GL_HEADS * GL_DV), BF16),
        scratch_shapes=[pltpu.VMEM((l, vb), F32), pltpu.VMEM((2 * hps, GL_DV, GL_DK), F32)],
        compiler_params=_cparams("arbitrary", "arbitrary"),
        name="gla_mixer",
    )(qkv, qkv, qkv, gate, la, la, norm_g.reshape(1, -1))


def _store_token_tiles(ref, val):
    rows = val.shape[0]
    for s in range(val.shape[1] // LANES):
        ref[pl.ds(s, rows, stride=SUBLANES), :] = val[:, s * LANES:(s + 1) * LANES]


def _load_token_tiles(ref, rows):
    return jnp.concatenate([ref[pl.ds(s, rows, stride=SUBLANES), :] for s in range(SUBLANES)], axis=1)


def _post_kernel(y_ref, x_ref, mod_ref, wo_ref, g_ref, wr_ref, br_ref, xo_ref, f_ref, rt_ref, cnt_ref, run_ref):
    first = (pl.program_id(0) == 0) & (pl.program_id(1) == 0)

    @pl.when(first)
    def _():
        run_ref[...] = jnp.zeros_like(run_ref)

    for sub in range(PAIR):
        _post_tile(y_ref.at[sub], x_ref.at[sub], mod_ref.at[sub], wo_ref, g_ref, wr_ref, br_ref,
                   xo_ref.at[sub], f_ref.at[sub], rt_ref.at[sub], cnt_ref, run_ref)


def _post_tile(y_ref, x_ref, mod_ref, wo_ref, g_ref, wr_ref, br_ref, xo_ref, f_ref, rt_ref, cnt_ref, run_ref):
    o = jnp.dot(y_ref[...], wo_ref[...], preferred_element_type=F32)
    xn = x_ref[...] + mod_ref[2:3, :] * o
    xo_ref[...] = xn
    f = _norm_mod(xn, g_ref[...], mod_ref, 3)
    _store_token_tiles(f_ref, f)
    f_hi = f.astype(BF16)
    f_lo = (f - f_hi.astype(F32)).astype(BF16)
    hh = jnp.dot(f_hi, wr_ref[...], preferred_element_type=F32)
    lh = jnp.dot(f_lo, wr_ref[:, :LANES], preferred_element_type=F32)
    lg = hh[:, :LANES] + (hh[:, LANES:] + lh) + br_ref[...]
    lane = lax.broadcasted_iota(jnp.int32, lg.shape, 1).astype(F32)

    def top(mask):
        v = jnp.max(jnp.where(mask, lg, -jnp.inf), axis=-1, keepdims=True)
        return v, jnp.min(jnp.where(mask & (lg == v), lane, float(LANES)), axis=-1, keepdims=True)

    gm = lane < MOE_GROUPS
    mg, g_idx = top(gm)
    g_w = 1.0 / jnp.sum(jnp.where(gm, jnp.exp(lg - mg), 0.0), axis=-1, keepdims=True)
    lo = MOE_GROUPS + MOE_PER_GROUP * g_idx
    em = (lane >= lo) & (lane < lo + MOE_PER_GROUP)
    v0, i0 = top(em)
    v1, i1 = top(em & (lane != i0))
    e0, e1 = i0 - MOE_GROUPS, i1 - MOE_GROUPS
    t = jnp.exp(v1 - v0)
    w0 = g_w / (1.0 + t)
    w1 = g_w * t / (1.0 + t)
    oh = ((lane == e0) | (lane == e1)).astype(F32)
    rows = lg.shape[0]
    below = lax.broadcasted_iota(jnp.int32, (rows, rows), 0) > lax.broadcasted_iota(jnp.int32, (rows, rows), 1)
    tot = run_ref[...] + jnp.dot(below.astype(BF16), oh.astype(BF16), preferred_element_type=F32)
    r0 = jnp.sum(jnp.where(lane == e0, tot, 0.0), axis=-1, keepdims=True)
    r1 = jnp.sum(jnp.where(lane == e1, tot, 0.0), axis=-1, keepdims=True)
    run_ref[...] = run_ref[...] + jnp.sum(oh, axis=0, keepdims=True)
    cnt_ref[...] = jnp.broadcast_to(run_ref[...], cnt_ref.shape)
    out = jnp.zeros_like(lg)
    for i, val in enumerate((e0, e1, w0, w1, r0, r1)):
        out = jnp.where(lane == i, val, out)
    rt_ref[...] = out


def _post_call(y, x, mods, w_o, g_ffn, w_r, b_r, skip_ctx):
    nb, l, d = x.shape
    t0 = 1 if skip_ctx else 0
    nt = l // TM - t0
    l = nt * TM
    src = lambda b, t: (b, t + t0, 0)
    tile = lambda b, t: (b, t, 0)
    mod_spec = pl.BlockSpec((PAIR, 6, d), lambda b, t: (b, 0, 0)) if skip_ctx else _mod_spec(nb)
    x_new, f_tiles, rt, cnt = pl.pallas_call(
        _post_kernel,
        grid=(nb // PAIR, nt),
        in_specs=[pl.BlockSpec((PAIR, TM, d), src),
                  pl.BlockSpec((PAIR, TM, d), src),
                  mod_spec,
                  pl.BlockSpec((d, d), lambda b, t: (0, 0)),
                  pl.BlockSpec((1, d), lambda b, t: (0, 0)),
                  pl.BlockSpec((d, 2 * LANES), lambda b, t: (0, 0)),
                  pl.BlockSpec((1, LANES), lambda b, t: (0, 0))],
        out_specs=[pl.BlockSpec((PAIR, TM, d), tile),
                   pl.BlockSpec((PAIR, TM * SUBLANES, LANES), tile),
                   pl.BlockSpec((PAIR, TM, LANES), tile),
                   pl.BlockSpec((SUBLANES, LANES), lambda b, t: (0, 0))],
        out_shape=[jax.ShapeDtypeStruct((nb, l, d), F32),
                   jax.ShapeDtypeStruct((nb, l * SUBLANES, LANES), F32),
                   jax.ShapeDtypeStruct((nb, l, LANES), F32),
                   jax.ShapeDtypeStruct((SUBLANES, LANES), F32)],
        scratch_shapes=[pltpu.VMEM((1, LANES), F32)],
        compiler_params=_cparams("arbitrary", "arbitrary"),
        name="outproj_router",
    )(y, x, mods, w_o, g_ffn.reshape(1, d), w_r, b_r)
    return x_new, f_tiles.reshape(nb * l * SUBLANES, LANES), rt.reshape(nb * l, LANES), cnt


def _dispatch_kernel(dest_ref, pend_ref, f_ref, xs_out, zbuf, sem):
    tile = f_ref.shape[0] // SUBLANES
    base = pl.program_id(0) * tile

    @pl.when(pl.program_id(0) == 0)
    def _():
        zbuf[...] = jnp.zeros_like(zbuf)

        def zero_block(blk):
            start = pl.multiple_of(blk * (MOE_BLOCK * SUBLANES), MOE_BLOCK * SUBLANES)
            return pltpu.make_async_copy(zbuf, xs_out.at[pl.ds(start, MOE_BLOCK * SUBLANES), :], sem)

        def has_rows(e):
            return pend_ref[e] > (0 if e == 0 else pend_ref[e - 1])

        for e in range(MOE_EXPERTS):
            pl.when(has_rows(e))(lambda e=e: zero_block(pend_ref[e] // MOE_BLOCK - 1).start())
        for e in range(MOE_EXPERTS):
            pl.when(has_rows(e))(lambda e=e: zero_block(pend_ref[e] // MOE_BLOCK - 1).wait())
        n_act = pend_ref[MOE_EXPERTS - 1] // MOE_BLOCK
        n_blk = xs_out.shape[0] // (MOE_BLOCK * SUBLANES)
        lax.fori_loop(n_act, n_blk, lambda b, c: (zero_block(b).start(), c)[1], 0)
        lax.fori_loop(n_act, n_blk, lambda b, c: (zero_block(b).wait(), c)[1], 0)

    def copy(t, k):
        d = dest_ref[2 * (base + t) + k]
        return pltpu.make_async_copy(f_ref.at[pl.ds(t * SUBLANES, SUBLANES), :],
                                     xs_out.at[pl.ds(d * SUBLANES, SUBLANES), :], sem)

    def issue(t, c):
        copy(t, 0).start()
        copy(t, 1).start(priority=1)
        return c

    lax.fori_loop(0, tile, issue, 0, unroll=DMA_UNROLL)
    for _ in range(2):
        pltpu.make_async_copy(f_ref, xs_out.at[pl.ds(0, tile * SUBLANES), :], sem).wait()


def _dispatch(dest, p_end, f_tiles, n_slots):
    n_tok = f_tiles.shape[0] // SUBLANES
    tile = DISPATCH_TILE
    assert n_tok % tile == 0
    return pl.pallas_call(
        _dispatch_kernel,
        grid_spec=pltpu.PrefetchScalarGridSpec(
            num_scalar_prefetch=2,
            grid=(n_tok // tile,),
            in_specs=[pl.BlockSpec((tile * SUBLANES, LANES), lambda i, dr, pe: (i, 0))],
            out_specs=pl.BlockSpec(memory_space=pl.ANY),
            scratch_shapes=[pltpu.VMEM((MOE_BLOCK * SUBLANES, LANES), F32), pltpu.SemaphoreType.DMA(())]),
        out_shape=jax.ShapeDtypeStruct((n_slots * SUBLANES, LANES), F32),
        compiler_params=_cparams("arbitrary"),
        name="moe_dispatch",
    )(dest, p_end, f_tiles)


def _expert_kernel(blk_e_ref, nact_ref, nxt_ref, slot_ref, xs_ref, wg_hbm, wu_hbm, wd_hbm, ys_ref,
                   wg_f, wu_f, wd_f, wg_s, wu_s, wd_s, sem, *, layer):
    i = pl.program_id(0)
    active = i < nact_ref[0]
    e = blk_e_ref[i]
    changed = (i == 0) | (e != blk_e_ref[jnp.maximum(i - 1, 0)])
    slot = slot_ref[i]

    def fetch(expert, s):
        return [pltpu.make_async_copy(w.at[layer, expert], buf.at[s], sem.at[s, k])
                for k, (w, buf) in enumerate(((wg_hbm, wg_f), (wu_hbm, wu_f), (wd_hbm, wd_f)))]

    @pl.when(i == 0)
    def _():
        for c in fetch(e, slot):
            c.start()

    @pl.when(active & changed)
    def _():
        nxt = nxt_ref[i]

        @pl.when(nxt >= 0)
        def _():
            for c in fetch(nxt, 1 - slot):
                c.start()

        for c in fetch(e, slot):
            c.wait()
        wg_s[...] = wg_f[slot].astype(BF16)
        wu_s[...] = wu_f[slot].astype(BF16)
        wd_s[...] = wd_f[slot].astype(BF16)

    @pl.when(active)
    def _():
        x = _load_token_tiles(xs_ref, MOE_BLOCK).astype(BF16)
        a = jnp.dot(x, wg_s[...], preferred_element_type=F32)
        u = jnp.dot(x, wu_s[...], preferred_element_type=F32)
        y = jnp.dot((_silu(a) * u).astype(BF16), wd_s[...], preferred_element_type=F32)
        _store_token_tiles(ys_ref, y)

    @pl.when(jnp.logical_not(active))
    def _():
        ys_ref[...] = jnp.zeros_like(ys_ref)


def _experts(blk_e, nact, nxt_e, slot, xs, w_gate, w_up, w_down, layer):
    n_blk = blk_e.shape[0]
    _, _, d, ff = w_gate.shape
    rows = MOE_BLOCK * SUBLANES
    last = lambda i, na: jnp.minimum(i, jnp.maximum(na[0] - 1, 0))
    hbm = pl.BlockSpec(memory_space=pl.ANY)
    return pl.pallas_call(
        functools.partial(_expert_kernel, layer=layer),
        grid_spec=pltpu.PrefetchScalarGridSpec(
            num_scalar_prefetch=4,
            grid=(n_blk,),
            in_specs=[pl.BlockSpec((rows, LANES), lambda i, be, na, nx, sl: (last(i, na), 0)), hbm, hbm, hbm],
            out_specs=pl.BlockSpec((rows, LANES), lambda i, be, na, nx, sl: (i, 0)),
            scratch_shapes=[pltpu.VMEM((2, d, ff), F32), pltpu.VMEM((2, d, ff), F32), pltpu.VMEM((2, ff, d), F32),
                            pltpu.VMEM((d, ff), BF16), pltpu.VMEM((d, ff), BF16), pltpu.VMEM((ff, d), BF16),
                            pltpu.SemaphoreType.DMA((2, 3))]),
        out_shape=jax.ShapeDtypeStruct(xs.shape, F32),
        compiler_params=_cparams("arbitrary"),
        name="moe_experts",
    )(blk_e, nact, nxt_e, slot, xs, w_gate, w_up, w_down)


def _combine_kernel(dest_ref, ys_hbm, x_ref, rt_ref, mod_ref, o_ref, ybuf, sem):
    i = pl.program_id(0)
    n = pl.num_programs(0)

    def gather(step, slot, start):
        base = step * TM

        def copy(t, k):
            d = dest_ref[2 * (base + t) + k]
            return pltpu.make_async_copy(ys_hbm.at[pl.ds(d * SUBLANES, SUBLANES), :],
                                         ybuf.at[slot, k, pl.ds(t * SUBLANES, SUBLANES), :], sem.at[slot])

        def issue(t, c):
            copy(t, 0).start()
            copy(t, 1).start(priority=1)
            return c

        if start:
            lax.fori_loop(0, TM, issue, 0, unroll=DMA_UNROLL)
        else:
            for k in range(2):
                pltpu.make_async_copy(ys_hbm.at[pl.ds(0, TM * SUBLANES), :], ybuf.at[slot, k], sem.at[slot]).wait()

    slot = i % 2
    pl.when(i == 0)(lambda: gather(i, slot, True))
    pl.when(i + 1 < n)(lambda: gather(i + 1, 1 - slot, True))
    gather(i, slot, False)
    y0 = _load_token_tiles(ybuf.at[slot, 0], TM)
    y1 = _load_token_tiles(ybuf.at[slot, 1], TM)
    rt = rt_ref[...]
    y = rt[:, 2:3] * y0 + rt[:, 3:4] * y1
    o_ref[...] = x_ref[...] + mod_ref[5:6, :] * y


def _combine(dest, ys, x, rt, mods, has_ctx):
    nb, l, d = x.shape
    nt = l // TM
    mod_row = (lambda i: jnp.where(i % nt == 0, nb, i // nt)) if has_ctx else (lambda i: i // nt)
    return pl.pallas_call(
        _combine_kernel,
        grid_spec=pltpu.PrefetchScalarGridSpec(
            num_scalar_prefetch=1,
            grid=(nb * nt,),
            in_specs=[pl.BlockSpec(memory_space=pl.ANY),
                      pl.BlockSpec((None, TM, d), lambda i, dr: (i // nt, i % nt, 0)),
                      pl.BlockSpec((TM, LANES), lambda i, dr: (i, 0)),
                      pl.BlockSpec((None, 6, d), lambda i, dr: (mod_row(i), 0, 0))],
            out_specs=pl.BlockSpec((None, TM, d), lambda i, dr: (i // nt, i % nt, 0)),
            scratch_shapes=[pltpu.VMEM((2, 2, TM * SUBLANES, LANES), F32), pltpu.SemaphoreType.DMA((2,))]),
        out_shape=jax.ShapeDtypeStruct((nb, l, d), F32),
        compiler_params=_cparams("arbitrary"),
        name="moe_combine",
    )(dest, ys, x, rt, mods)


def _moe(y_mix, x, mods, w_o, g_ffn, w_grp, b_grp, w_exp, b_exp, w_gate, w_up, w_down, layer, skip_ctx):
    nr = MOE_GROUPS + MOE_EXPERTS
    w_r = jnp.pad(jnp.concatenate([w_grp, w_exp], axis=1), ((0, 0), (0, LANES - nr)))
    b_r = jnp.pad(jnp.concatenate([b_grp, b_exp]), (0, LANES - nr)).reshape(1, LANES)
    w_hi = w_r.astype(BF16)
    w_r = jnp.concatenate([w_hi, (w_r - w_hi.astype(F32)).astype(BF16)], axis=1)
    x_new, f_tiles, rt, cnt = _post_call(y_mix, x, mods, w_o, g_ffn, w_r, b_r, skip_ctx)
    n_tok = rt.shape[0]
    counts = cnt[0, :MOE_EXPERTS].astype(jnp.int32)
    padded = (counts + MOE_BLOCK - 1) // MOE_BLOCK * MOE_BLOCK
    p_end = jnp.cumsum(padded)
    p_start = p_end - padded
    n_pairs = 2 * n_tok
    n_blk = -(-(n_pairs + MOE_EXPERTS * (MOE_BLOCK - 1)) // MOE_BLOCK)
    eid = rt[:, 0:2].astype(jnp.int32)
    first = jnp.sum(jnp.where(eid[:, :, None] == jnp.arange(MOE_EXPERTS, dtype=jnp.int32), p_start, 0), axis=-1)
    dest = (first + rt[:, 4:6].astype(jnp.int32)).reshape(-1)
    blk_row = jnp.arange(n_blk, dtype=jnp.int32) * MOE_BLOCK
    blk_e = jnp.minimum(jnp.sum((p_end[None, :] <= blk_row[:, None]).astype(jnp.int32), axis=1), MOE_EXPERTS - 1)
    nact = (p_end[-1:] // MOE_BLOCK).astype(jnp.int32)
    xs = _dispatch(dest, p_end.astype(jnp.int32), f_tiles, n_blk * MOE_BLOCK)
    has = padded > 0
    ids = jnp.arange(MOE_EXPERTS, dtype=jnp.int32)
    rank = jnp.cumsum(has.astype(jnp.int32)) - has.astype(jnp.int32)
    later = jnp.where((ids[None, :] > ids[:, None]) & has[None, :], ids[None, :], MOE_EXPERTS)
    nxt = jnp.min(later, axis=1)
    nxt = jnp.where(nxt == MOE_EXPERTS, -1, nxt)
    sel = blk_e[:, None] == ids[None, :]
    slot = jnp.sum(jnp.where(sel, rank[None, :] % 2, 0), axis=1).astype(jnp.int32)
    nxt_e = jnp.sum(jnp.where(sel, nxt[None, :], 0), axis=1).astype(jnp.int32)
    ys = _experts(blk_e, nact, nxt_e, slot, xs, w_gate, w_up, w_down, layer)
    return _combine(dest, ys, x_new, rt, mods, not skip_ctx)


def _rope_tables(seq):
    nf = SW_HD // 4
    inv = ROPE_BASE ** (-jnp.arange(nf, dtype=F32) / nf)
    pos = jnp.arange(seq, dtype=jnp.int32)
    rows, cols = (pos // GRID_W).astype(F32), (pos % GRID_W).astype(F32)
    lane = jnp.arange(LANES)
    p = jnp.where((lane % SW_HD < SW_HD // 2)[None, :], rows[:, None], cols[:, None])
    ang = p * inv[lane % nf][None, :]
    sign = jnp.where((lane % (2 * nf)) < nf, -1.0, 1.0)[None, :]
    cs = jnp.concatenate([jnp.ones((CTX_LEN, LANES), F32), jnp.cos(ang)], axis=0)
    sn = jnp.concatenate([jnp.zeros((CTX_LEN, LANES), F32), jnp.sin(ang) * sign], axis=0)
    return cs, sn


def _pad_cols(w, n):
    return jnp.pad(w, ((0, 0), (0, n - w.shape[1])))


def kernel(x, c, ctx, c_ctx, ada_w, ada_b, norm_mix_g, norm_ffn_g, na_w_qkv, na_qk_g, na_rpb, na_w_o, ml_w_in, ml_b_gates, ml_norm_g, ml_w_o, sw_w_qkv, sw_qk_g, sw_sink, sw_w_o, gl_w_in, gl_w_a2, gl_b_a, gl_norm_g, gl_w_o, moe_w_grp, moe_b_grp, moe_w_exp, moe_b_exp, moe_w_gate, moe_w_up, moe_w_down):
    nb, seq, d = x.shape
    depth = ada_w.shape[0]
    assert d == D_MODEL and ctx.shape[1] == CTX_LEN == TM and seq % TM == 0 and nb % PAIR == 0
    rows = -(-(nb + PAIR) // SUBLANES) * SUBLANES
    cond = jnp.pad(jnp.concatenate([c] + [c_ctx[None]] * PAIR, axis=0), ((0, rows - nb - PAIR), (0, 0)))
    mods_all = _ada_mods(cond, ada_w, ada_b)
    xa = jnp.concatenate([ctx, x], axis=1)
    for i in range(depth):
        j, kind = divmod(i, 4)
        mods = mods_all[i]
        g_mix = norm_mix_g[i]
        if kind == 0:
            qkg = na_qk_g[j]
            (qkv,) = _proj_call(_proj_na_kernel, xa, mods, g_mix, na_w_qkv[j].astype(BF16), [(qkg, False)],
                                [3 * NA_HEADS * NA_HD], [BF16], "proj_na")
            y = _na_attention(qkv, _na_bias_table(na_rpb[j], seq // GRID_W))
            w_o = na_w_o[j]
        elif kind == 1:
            n_in = -(-ml_w_in.shape[2] // LANES) * LANES
            ng = 4 * ML_HEADS
            w_in = ml_w_in[j]
            w_g = w_in[:, -ng:].reshape(d, 2, 2, ML_HEADS).transpose(0, 3, 1, 2).reshape(d, ng)
            w_in = jnp.concatenate([w_in[:, :-ng], w_g], axis=1)
            bg = ml_b_gates[j].reshape(2, 2, ML_HEADS).transpose(2, 0, 1).reshape(ng)
            bg = jnp.pad(bg, (0, LANES - ng)).reshape(1, LANES)
            qkv, og, gates = _proj_call(_proj_ml_kernel, xa, mods, g_mix, _pad_cols(w_in, n_in).astype(BF16),
                                        [(bg, False)], [2 * ML_HEADS * ML_DK + ML_HEADS * ML_DV, ML_HEADS * ML_DV, LANES],
                                        [BF16, BF16, F32], "proj_ml")
            y = _ml_mixer(qkv, og, gates, ml_norm_g[j])
            w_o = ml_w_o[j]
        elif kind == 2:
            qkg = jnp.concatenate([sw_qk_g[j], sw_qk_g[j]], axis=1)
            cs, sn = _rope_tables(seq)
            nqc = SW_HEADS * SW_HD
            w_in = sw_w_qkv[j]
            w_q = w_in[:, :nqc].reshape(d, 2, 2, 4, SW_HD).transpose(0, 1, 3, 2, 4).reshape(d, nqc)
            w_in = jnp.concatenate([w_q, w_in[:, nqc:]], axis=1)
            q, kv = _proj_call(_proj_sw_kernel, xa, mods, g_mix, w_in.astype(BF16),
                               [(qkg, False), (cs, True), (sn, True)],
                               [nqc, 2 * SW_KV * SW_HD], [BF16, BF16], "proj_sw")
            y = _sw_attention(q, kv, sw_sink[j])
            w_o = sw_w_o[j].reshape(2, 2, 4, SW_HD, d).transpose(0, 2, 1, 3, 4).reshape(nqc, d)
        else:
            n_in = -(-gl_w_in.shape[2] // LANES) * LANES
            nk = GL_HEADS * GL_DK
            wa = jnp.zeros((LANES, 2 * nk), F32)
            wa = wa.at[:GL_RANK, :nk].set(gl_w_a2[j, 0]).at[GL_RANK:2 * GL_RANK, nk:].set(gl_w_a2[j, 1])
            ba = gl_b_a[j].reshape(1, 2 * nk)
            qkv, gate, la = _proj_call(_proj_gl_kernel, xa, mods, g_mix, _pad_cols(gl_w_in[j], n_in).astype(BF16),
                                       [(wa.astype(BF16), False), (ba, False)],
                                       [2 * nk + GL_HEADS * GL_DV, GL_HEADS * GL_DV, 2 * nk], [BF16, BF16, F32], "proj_gl")
            y = _gl_mixer(qkv, gate, la, gl_norm_g[j])
            w_o = gl_w_o[j]
        xa = _moe(y, xa, mods, w_o.astype(BF16), norm_ffn_g[i], moe_w_grp[i], moe_b_grp[i], moe_w_exp[i], moe_b_exp[i],
                  moe_w_gate, moe_w_up, moe_w_down, i, i == depth - 1)
    return xa
```

```python
import functools

import jax
import jax.numpy as jnp
import numpy as np
from jax import lax
from jax.experimental import pallas as pl
from jax.experimental.pallas import tpu as pltpu

F32 = jnp.float32
BF16 = jnp.bfloat16
HIGHEST = lax.Precision.HIGHEST

D_MODEL = 1024
CTX_LEN = 256
GRID_W = 64
RMS_EPS = 1e-6
NEG_INF = -1e30
ROPE_BASE = 10000.0

NA_HEADS, NA_HD, NA_WIN_R, NA_WIN_C = 8, 128, 8, 16
NA_QROWS = 4
NA_KROWS = NA_QROWS + NA_WIN_R - 1
NA_UNROLL = 8
ML_HEADS, ML_DK, ML_DV, ML_CAP = 8, 64, 128, 15.0
ML_SCAN = 256
SW_HEADS, SW_KV, SW_HD, SW_WINDOW = 16, 4, 64, 128
GL_HEADS, GL_DK, GL_DV, GL_RANK, GL_TAU = 4, 128, 256, 16, 16.0
GL_SCAN = 128
MOE_GROUPS, MOE_PER_GROUP, MOE_EXPERTS, MOE_FF = 4, 8, 32, 512
MOE_BLOCK = 512

LANES = 128
SUBLANES = 8
TM = 256
PAIR = 2
VMEM_LIMIT = 48 * 1024 * 1024
DMA_UNROLL = 16
DISPATCH_TILE = 8 * TM


def _cparams(*sem):
    return pltpu.CompilerParams(dimension_semantics=sem, vmem_limit_bytes=VMEM_LIMIT)


def _norm_mod(x, g, mod_ref, k):
    y = x * lax.rsqrt(jnp.mean(x * x, axis=-1, keepdims=True) + RMS_EPS) * g
    return y * (1.0 + mod_ref[k + 1:k + 2, :]) + mod_ref[k:k + 1, :]


def _log_sigmoid(x):
    return jnp.minimum(x, 0.0) - jnp.log(1.0 + jnp.exp(-jnp.abs(x)))


def _silu(x):
    return x * jax.nn.sigmoid(x)


def _mod_kernel(c_ref, w_ref, b_ref, o_ref):
    s = _silu(c_ref[...])
    o_ref[...] = jnp.dot(s.astype(BF16), w_ref[...].astype(BF16), preferred_element_type=F32) + b_ref[...]


def _ada_mods(cond, ada_w, ada_b):
    depth, d, _ = ada_w.shape
    rows = cond.shape[0]
    out = pl.pallas_call(
        _mod_kernel,
        grid=(depth, 6),
        in_specs=[pl.BlockSpec((rows, d), lambda i, n: (0, 0)),
                  pl.BlockSpec((None, d, d), lambda i, n: (i, 0, n)),
                  pl.BlockSpec((None, 1, d), lambda i, n: (i, 0, n))],
        out_specs=pl.BlockSpec((None, rows, d), lambda i, n: (i, 0, n)),
        out_shape=jax.ShapeDtypeStruct((depth, rows, 6 * d), F32),
        compiler_params=_cparams("arbitrary", "arbitrary"),
        name="ada_mods",
    )(cond, ada_w, ada_b.reshape(depth, 1, 6 * d))
    return out.reshape(depth, rows, 6, d)


def _mod_spec(nb):
    return pl.BlockSpec((PAIR, 6, D_MODEL), lambda b, t: (jnp.where(t == 0, nb // PAIR, b), 0, 0))


def _paired(body, paired, *refs):
    for sub in range(PAIR):
        body(*[r.at[sub] if p else r for r, p in zip(refs, paired)])


def _proj_call(kernel, x, mods, g, w, extras, out_cols, out_dtypes, name):
    nb, l, d = x.shape
    nt = l // TM
    n = w.shape[1]
    in_specs = [pl.BlockSpec((PAIR, TM, d), lambda b, t: (b, t, 0)),
                _mod_spec(nb),
                pl.BlockSpec((1, d), lambda b, t: (0, 0)),
                pl.BlockSpec((d, n), lambda b, t: (0, 0))]
    args = [x, mods, g.reshape(1, d), w]
    for e, per_tile in extras:
        if per_tile:
            in_specs.append(pl.BlockSpec((TM, e.shape[1]), lambda b, t: (t, 0)))
        else:
            in_specs.append(pl.BlockSpec(e.shape, lambda b, t: (0, 0)))
        args.append(e)
    paired = (True, True) + (False,) * (len(in_specs) - 2) + (True,) * len(out_cols)
    return pl.pallas_call(
        functools.partial(_paired, kernel, paired),
        grid=(nb // PAIR, nt),
        in_specs=in_specs,
        out_specs=[pl.BlockSpec((PAIR, TM, c), lambda b, t: (b, t, 0)) for c in out_cols],
        out_shape=[jax.ShapeDtypeStruct((nb, l, c), dt) for c, dt in zip(out_cols, out_dtypes)],
        compiler_params=_cparams("arbitrary", "arbitrary"),
        name=name,
    )(*args)


def _proj_na_kernel(x_ref, mod_ref, g_ref, w_ref, qkg_ref, o_ref):
    h = _norm_mod(x_ref[...], g_ref[...], mod_ref, 0)
    u = jnp.dot(h.astype(BF16), w_ref[...], preferred_element_type=F32)
    nq = NA_HEADS * NA_HD
    for part in range(2):
        gain = qkg_ref[part:part + 1, :]
        if part == 0:
            gain = gain * (NA_HD ** -0.5)
        for hh in range(NA_HEADS):
            lo = part * nq + hh * NA_HD
            z = u[:, lo:lo + NA_HD]
            z = z * lax.rsqrt(jnp.mean(z * z, axis=-1, keepdims=True) + RMS_EPS) * gain
            o_ref[:, lo:lo + NA_HD] = z.astype(BF16)
    o_ref[:, 2 * nq:] = u[:, 2 * nq:].astype(BF16)


def _head64_rms(z, gain):
    lane = lax.broadcasted_iota(jnp.int32, z.shape, 1)
    lo = lane < SW_HD
    zz = z * z
    s_lo = jnp.sum(jnp.where(lo, zz, 0.0), axis=-1, keepdims=True)
    s_hi = jnp.sum(jnp.where(lo, 0.0, zz), axis=-1, keepdims=True)
    ms = jnp.where(lo, s_lo, s_hi) * (1.0 / SW_HD)
    return z * lax.rsqrt(ms + RMS_EPS) * gain


def _rope_slab(z, cs, sn):
    lane = lax.broadcasted_iota(jnp.int32, z.shape, 1)
    first = (lane % 32) < 16
    partner = jnp.where(first, pltpu.roll(z, LANES - 16, 1), pltpu.roll(z, 16, 1))
    return z * cs + partner * sn


def _proj_sw_kernel(x_ref, mod_ref, g_ref, w_ref, qkg_ref, cs_ref, sn_ref, q_ref, kv_ref):
    h = _norm_mod(x_ref[...], g_ref[...], mod_ref, 0)
    u = jnp.dot(h.astype(BF16), w_ref[...], preferred_element_type=F32)
    cs, sn = cs_ref[...], sn_ref[...]
    nq, nk = SW_HEADS * SW_HD, SW_KV * SW_HD
    scale = SW_HD ** -0.5
    for s in range(nq // LANES):
        z = _head64_rms(u[:, s * LANES:(s + 1) * LANES], qkg_ref[0:1, :])
        q_ref[:, s * LANES:(s + 1) * LANES] = (_rope_slab(z, cs, sn) * scale).astype(BF16)
    for s in range(nk // LANES):
        z = _head64_rms(u[:, nq + s * LANES:nq + (s + 1) * LANES], qkg_ref[1:2, :])
        kv_ref[:, s * LANES:(s + 1) * LANES] = _rope_slab(z, cs, sn).astype(BF16)
    kv_ref[:, nk:] = u[:, nq + nk:].astype(BF16)


def _proj_ml_kernel(x_ref, mod_ref, g_ref, w_ref, bg_ref, qkv_ref, og_ref, gt_ref):
    h = _norm_mod(x_ref[...], g_ref[...], mod_ref, 0)
    u = jnp.dot(h.astype(BF16), w_ref[...], preferred_element_type=F32)
    nq, nv = ML_HEADS * ML_DK, ML_HEADS * ML_DV
    qkv_ref[:, :nq] = (u[:, :nq] * (ML_DK ** -0.5)).astype(BF16)
    qkv_ref[:, nq:] = u[:, nq:2 * nq + nv].astype(BF16)
    og_ref[...] = jax.nn.sigmoid(u[:, 2 * nq + nv:2 * nq + 2 * nv]).astype(BF16)
    pre = u[:, 2 * nq + 2 * nv:] + bg_ref[...]
    pre = ML_CAP * jnp.tanh(pre / ML_CAP)
    lane = lax.broadcasted_iota(jnp.int32, pre.shape, 1)
    gt_ref[...] = jnp.where((lane % 2) == 1, _log_sigmoid(pre), pre)


def _proj_gl_kernel(x_ref, mod_ref, g_ref, w_ref, wa_ref, ba_ref, qkv_ref, gate_ref, la_ref):
    h = _norm_mod(x_ref[...], g_ref[...], mod_ref, 0)
    u = jnp.dot(h.astype(BF16), w_ref[...], preferred_element_type=F32)
    nq, nv = GL_HEADS * GL_DK, GL_HEADS * GL_DV
    qkv_ref[...] = u[:, :2 * nq + nv].astype(BF16)
    gate_ref[...] = _silu(u[:, 2 * nq + nv:2 * nq + 2 * nv]).astype(BF16)
    z = u[:, 2 * nq + 2 * nv:]
    a = jnp.dot(z.astype(BF16), wa_ref[...], preferred_element_type=F32) + ba_ref[...]
    la_ref[...] = _log_sigmoid(a) * (1.0 / GL_TAU)


def _na_bias_table(rpb, n_rows):
    cq = np.arange(GRID_W)[:, None]
    ck = np.arange(GRID_W)[None, :]
    c0 = np.clip(cq - NA_WIN_C // 2, 0, GRID_W - NA_WIN_C)
    col_ok = (ck >= c0) & (ck < c0 + NA_WIN_C)
    dc = np.clip(ck - cq + NA_WIN_C - 1, 0, 2 * NA_WIN_C - 2)
    pick = ((dc[None] == np.arange(2 * NA_WIN_C - 1)[:, None, None]) & col_ok[None]).astype(np.float32)
    m = jnp.einsum('hrd,dqk->hqrk', rpb.astype(F32), jnp.asarray(pick), precision=HIGHEST)
    m = jnp.where(jnp.asarray(col_ok)[None, :, None, :], m, NEG_INF)
    blocks = []
    for jj in range(n_rows // NA_QROWS):
        ws = min(max(NA_QROWS * jj - NA_WIN_R // 2, 0), n_rows - NA_KROWS)
        per_row = []
        for ri in range(NA_QROWS):
            r = NA_QROWS * jj + ri
            r0 = min(max(r - NA_WIN_R // 2, 0), n_rows - NA_WIN_R)
            lead = r0 - ws
            d0 = r0 - r + NA_WIN_R - 1
            piece = jnp.pad(m[:, :, d0:d0 + NA_WIN_R, :],
                            ((0, 0), (0, 0), (lead, NA_KROWS - NA_WIN_R - lead), (0, 0)), constant_values=NEG_INF)
            per_row.append(piece.reshape(rpb.shape[0], GRID_W, NA_KROWS * GRID_W))
        blocks.append(jnp.concatenate(per_row, axis=1))
    return jnp.stack(blocks, axis=1)


def _na_kernel(q_ref, k_ref, v_ref, bias_ref, o_ref, *, n_rows):
    j = pl.program_id(1)
    nb = q_ref.shape[0]
    nt = (((1,), (1,)), ((), ()))
    nkw = NA_KROWS * GRID_W

    @pl.when(j == 0)
    def _ctx():
        def body(b, c):
            q = q_ref[b]
            s = lax.dot_general(q, k_ref[b, 0:CTX_LEN, :], nt, preferred_element_type=F32)
            p = jnp.exp(s - jnp.max(s, axis=-1, keepdims=True))
            l = jnp.sum(p, axis=-1, keepdims=True)
            o = jnp.dot(p.astype(BF16), v_ref[b, 0:CTX_LEN, :], preferred_element_type=F32)
            o_ref[b] = (o / l).astype(BF16)
            return c
        lax.fori_loop(0, nb, body, 0, unroll=NA_UNROLL)

    @pl.when(j > 0)
    def _lat():
        ws = jnp.clip(NA_QROWS * (j - 1) - NA_WIN_R // 2, 0, n_rows - NA_KROWS)
        start = pl.multiple_of(CTX_LEN + ws * GRID_W, GRID_W)
        bias = bias_ref[...]

        def body(b, c):
            q = q_ref[b]
            s_n = lax.dot_general(q, k_ref[b, pl.ds(start, nkw), :], nt, preferred_element_type=F32) + bias
            s_c = lax.dot_general(q, k_ref[b, 0:CTX_LEN, :], nt, preferred_element_type=F32)
            m = jnp.maximum(jnp.max(s_n, axis=-1, keepdims=True), jnp.max(s_c, axis=-1, keepdims=True))
            p_n = jnp.exp(s_n - m)
            p_c = jnp.exp(s_c - m)
            l = jnp.sum(p_n, axis=-1, keepdims=True) + jnp.sum(p_c, axis=-1, keepdims=True)
            o = (jnp.dot(p_n.astype(BF16), v_ref[b, pl.ds(start, nkw), :], preferred_element_type=F32)
                 + jnp.dot(p_c.astype(BF16), v_ref[b, 0:CTX_LEN, :], preferred_element_type=F32))
            o_ref[b] = (o / l).astype(BF16)
            return c
        lax.fori_loop(0, nb, body, 0, unroll=NA_UNROLL)


def _na_attention(qkv, bias_tab):
    nb, l, _ = qkv.shape
    n_rows = (l - CTX_LEN) // GRID_W
    nt = l // TM
    hq = NA_HEADS
    return pl.pallas_call(
        functools.partial(_na_kernel, n_rows=n_rows),
        grid=(hq, nt),
        in_specs=[pl.BlockSpec((nb, TM, NA_HD), lambda h, j: (0, j, h)),
                  pl.BlockSpec((nb, l, NA_HD), lambda h, j: (0, 0, hq + h)),
                  pl.BlockSpec((nb, l, NA_HD), lambda h, j: (0, 0, 2 * hq + h)),
                  pl.BlockSpec((None, None, TM, NA_KROWS * GRID_W),
                               lambda h, j: (h, jnp.maximum(j - 1, 0), 0, 0))],
        out_specs=pl.BlockSpec((nb, TM, NA_HD), lambda h, j: (0, j, h)),
        out_shape=jax.ShapeDtypeStruct((nb, l, hq * NA_HD), BF16),
        compiler_params=_cparams("arbitrary", "arbitrary"),
        name="na_attention",
    )(qkv, qkv, qkv, bias_tab)


def _sw_kernel(sink_ref, q_ref, kv_ref, o_ref, *, seq):
    j = pl.program_id(1)
    nt = (((1,), (1,)), ((), ()))
    nkv = SW_KV * SW_HD
    kwin = TM + 2 * SW_WINDOW
    low = lax.broadcasted_iota(jnp.int32, (1, LANES), 1) < SW_HD

    def attend(segs):
        for pair in range(SW_KV // 2):
            kc = slice(pair * LANES, (pair + 1) * LANES)
            vc = slice(nkv + pair * LANES, nkv + (pair + 1) * LANES)
            ks = [kv_ref[rows, kc] for rows, _ in segs]
            vs = [kv_ref[rows, vc] for rows, _ in segs]
            one = jnp.ones((), BF16)
            v_half = [[jnp.where(low, v, one) for v in vs], [jnp.where(low, one, v) for v in vs]]
            for i in range(SW_HEADS // SW_KV):
                cols = slice((4 * pair + i) * LANES, (4 * pair + i + 1) * LANES)
                qs = q_ref[:, cols]
                outs = []
                for half in range(2):
                    sink = sink_ref[8 * pair + 4 * half + i]
                    qm = jnp.where(low if half == 0 else jnp.logical_not(low), qs, jnp.zeros((), BF16))
                    sc = []
                    for k, (_, mask) in zip(ks, segs):
                        s = lax.dot_general(qm, k, nt, preferred_element_type=F32)
                        sc.append(s if mask is None else jnp.where(mask, s, NEG_INF))
                    m = sink
                    for s in sc:
                        m = jnp.maximum(m, jnp.max(s, axis=-1, keepdims=True))
                    acc = None
                    for s, v in zip(sc, v_half[half]):
                        pv = jnp.dot(jnp.exp((s - m).astype(BF16)), v, preferred_element_type=F32)
                        acc = pv if acc is None else acc + pv
                    denom = pltpu.roll(acc, SW_HD, 1) + jnp.exp(sink - m)
                    outs.append(acc / denom)
                o_ref[:, cols] = jnp.where(low, outs[0], outs[1]).astype(BF16)

    @pl.when(j == 0)
    def _ctx():
        attend([(slice(0, CTX_LEN), None)])

    @pl.when(j > 0)
    def _lat():
        q0 = (j - 1) * TM
        ws = jnp.clip(q0 - SW_WINDOW, 0, seq - kwin)
        start = pl.multiple_of(CTX_LEN + ws, SW_WINDOW)
        qpos = q0 + lax.broadcasted_iota(jnp.int32, (TM, kwin), 0)
        kpos = ws + lax.broadcasted_iota(jnp.int32, (TM, kwin), 1)
        ok = jnp.abs(qpos - kpos) <= SW_WINDOW
        attend([(pl.ds(start, kwin), ok), (slice(0, CTX_LEN), None)])


def _sw_attention(q, kv, sink):
    nb, l, nq = q.shape
    nt = l // TM
    return pl.pallas_call(
        functools.partial(_sw_kernel, seq=l - CTX_LEN),
        grid=(nb, nt),
        in_specs=[pl.BlockSpec(memory_space=pltpu.SMEM),
                  pl.BlockSpec((None, TM, nq), lambda b, j: (b, j, 0)),
                  pl.BlockSpec((None, l, kv.shape[2]), lambda b, j: (b, 0, 0))],
        out_specs=pl.BlockSpec((None, TM, nq), lambda b, j: (b, j, 0)),
        out_shape=jax.ShapeDtypeStruct((nb, l, nq), BF16),
        compiler_params=_cparams("arbitrary", "arbitrary"),
        name="sw_attention",
    )(sink, q, kv)


def _chunk_index(step, rev, n_ctx, n_all):
    if not rev:
        return step
    return jnp.where(step < n_ctx, n_ctx - 1 - step, n_all + n_ctx - 1 - step)


def _tri(n, rev):
    r = lax.broadcasted_iota(jnp.int32, (n, n), 0)
    c = lax.broadcasted_iota(jnp.int32, (n, n), 1)
    return (c >= r) if rev else (c <= r)


def _split3(x):
    hi = x.astype(BF16)
    r = x - hi.astype(F32)
    mid = r.astype(BF16)
    return hi, mid, (r - mid.astype(F32)).astype(BF16)


def _ml_kernel(q_ref, kt_ref, v_ref, og_ref, gr_ref, gc_ref, ng_ref, y_ref, acc_ref, *, n_ctx, n_all):
    lc = ML_SCAN
    hps = q_ref.shape[1] // ML_DK
    ones_v = jnp.ones((lc, ML_DV), BF16)
    lane_head = lax.broadcasted_iota(jnp.int32, (1, hps * ML_DK), 1) // ML_DK
    row_head = lax.broadcasted_iota(jnp.int32, (hps * ML_DK, 1), 0) // ML_DK

    def gate_forms(step, rev):
        c = _chunk_index(step, rev, n_ctx, n_all)
        r0 = pl.multiple_of(c * lc, lc)
        g_r = gr_ref[:, pl.ds(r0, lc)]
        g_c = pltpu.roll(gc_ref[pl.ds(r0, lc), :], (LANES - pl.program_id(1) * SUBLANES) % LANES, 1)
        cum3 = jnp.dot(jnp.concatenate(_split3(g_r), axis=0), _tri(lc, not rev).astype(BF16),
                       preferred_element_type=F32)
        cum_r = cum3[0:SUBLANES] + cum3[SUBLANES:2 * SUBLANES] + cum3[2 * SUBLANES:]
        tri_c = _tri(lc, rev).astype(BF16)
        cum_c = sum(jnp.dot(tri_c, part, preferred_element_type=F32) for part in _split3(g_c))
        return c, g_r, cum_r, cum_c

    def chain(forms, hl, rev, state):
        ct, m = state
        c, g_r, cum_r, cum_c = forms
        r0 = pl.multiple_of(c * lc, lc)
        base = hl * 4 + (2 if rev else 0)
        bf_r = cum_r[base + 1:base + 2, :]
        bf_c = cum_c[:, base + 1:base + 2]
        x_r = g_r[base:base + 1, :] - bf_r
        g = jnp.sum(g_r[base + 1:base + 2, :], axis=1, keepdims=True)
        tri = _tri(lc, rev)
        pm_c = jnp.max(jnp.where(tri, x_r, -jnp.inf), axis=1, keepdims=True)
        xmax = jnp.max(x_r, axis=1, keepdims=True)
        q = jnp.where(lane_head == hl, q_ref[pl.ds(r0, lc), :], jnp.zeros((), BF16))
        kt = kt_ref[:, pl.ds(r0, lc)]
        v1 = jnp.concatenate([v_ref[pl.ds(r0, lc), hl * ML_DV:(hl + 1) * ML_DV], ones_v], axis=1)
        mx = jnp.maximum(m, pm_c)
        e = jnp.exp(jnp.where(tri, x_r - mx, NEG_INF))
        sc = jnp.dot(q, kt, preferred_element_type=F32) * e
        e_int = jnp.exp(m - mx)
        intra = jnp.dot(sc.astype(BF16), v1, preferred_element_type=F32)
        inter = jnp.dot(q, ct.astype(BF16), preferred_element_type=F32)
        num = intra[:, :ML_DV] + e_int * inter[:, :ML_DV]
        den = intra[:, ML_DV:ML_DV + 1] + e_int * inter[:, ML_DV:ML_DV + 1]
        hout = num / jnp.maximum(jnp.abs(den), jnp.exp(-mx - bf_c))
        acc_ref[pl.ds(r0, lc), hl * ML_DV:(hl + 1) * ML_DV] += hout
        m_loc = g + xmax
        w_r = jnp.exp(x_r - xmax)
        ktw = jnp.where(row_head == hl, kt.astype(F32) * w_r, 0.0).astype(BF16)
        c_loc = jnp.dot(ktw, v1, preferred_element_type=F32)
        m_new = jnp.maximum(g + m, m_loc)
        dec = jnp.exp(g + m - m_new)
        inc = jnp.exp(m_loc - m_new)
        return dec * ct + inc * c_loc, m_new

    acc_ref[...] = jnp.zeros_like(acc_ref)
    chains = [(hl, rev) for hl in range(hps) for rev in (False, True)]
    init = tuple((jnp.zeros((hps * ML_DK, 2 * ML_DV), F32), jnp.zeros((1, 1), F32)) for _ in chains)

    def step_fn(step, states):
        forms = {rev: gate_forms(step, rev) for rev in (False, True)}
        return tuple(chain(forms[rev], hl, rev, st) for (hl, rev), st in zip(chains, states))

    lax.fori_loop(0, n_all, step_fn, init)

    for hl in range(hps):
        cols = slice(hl * ML_DV, (hl + 1) * ML_DV)
        z = acc_ref[:, cols]
        z = z * lax.rsqrt(jnp.mean(z * z, axis=-1, keepdims=True) + RMS_EPS) * ng_ref[:, cols]
        y_ref[:, cols] = (z.astype(BF16) * og_ref[:, cols])


def _ml_mixer(qkv, og, gates, norm_g):
    nb, l, _ = qkv.shape
    lc = ML_SCAN
    n_all, n_ctx = l // lc, CTX_LEN // lc
    hps = 2
    ngrp = ML_HEADS // hps
    nq = ML_HEADS * ML_DK
    g = gates[:, :, :4 * ML_HEADS].transpose(0, 2, 1)
    kt = qkv[:, :, nq:2 * nq].transpose(0, 2, 1)
    vb = hps * ML_DV
    return pl.pallas_call(
        functools.partial(_ml_kernel, n_ctx=n_ctx, n_all=n_all),
        grid=(nb, ngrp),
        in_specs=[pl.BlockSpec((None, l, hps * ML_DK), lambda b, h: (b, 0, h)),
                  pl.BlockSpec((None, hps * ML_DK, l), lambda b, h: (b, h, 0)),
                  pl.BlockSpec((None, l, vb), lambda b, h: (b, 0, 2 * nq // vb + h)),
                  pl.BlockSpec((None, l, vb), lambda b, h: (b, 0, h)),
                  pl.BlockSpec((None, hps * 4, l), lambda b, h: (b, h, 0)),
                  pl.BlockSpec((None, l, LANES), lambda b, h: (b, 0, 0)),
                  pl.BlockSpec((1, vb), lambda b, h: (0, h))],
        out_specs=pl.BlockSpec((None, l, vb), lambda b, h: (b, 0, h)),
        out_shape=jax.ShapeDtypeStruct((nb, l, ML_HEADS * ML_DV), BF16),
        scratch_shapes=[pltpu.VMEM((l, vb), F32)],
        compiler_params=_cparams("arbitrary", "arbitrary"),
        name="mlstm_mixer",
    )(qkv, kt, qkv, og, g, gates, norm_g.reshape(1, -1))


def _gl_kernel(q_ref, k_ref, v_ref, gate_ref, la0_ref, la1_ref, ng_ref, y_ref, acc_ref, st_ref, *, n_ctx, n_all):
    lc = GL_SCAN
    hps = q_ref.shape[1] // GL_DK
    t0 = (((0,), (0,)), ((), ()))
    nt = (((1,), (1,)), ((), ()))
    scale = GL_DK ** -0.5

    def chain(step, hl, rev):
        la_ref = la1_ref if rev else la0_ref
        sidx = 2 * hl + (1 if rev else 0)
        kc = slice(hl * GL_DK, (hl + 1) * GL_DK)
        vc = slice(hl * GL_DV, (hl + 1) * GL_DV)
        c = _chunk_index(step, rev, n_ctx, n_all)
        r0 = pl.multiple_of(c * lc, lc)
        tri = _tri(lc, rev)
        la = la_ref[pl.ds(r0, lc), kc]
        bc3 = jnp.dot(tri.astype(BF16), jnp.concatenate(_split3(la), axis=1), preferred_element_type=F32)
        bc = bc3[:, :GL_DK] + bc3[:, GL_DK:2 * GL_DK] + bc3[:, 2 * GL_DK:]
        g = jnp.sum(la, axis=0, keepdims=True)
        q = q_ref[pl.ds(r0, lc), kc].astype(F32) * scale
        k = k_ref[pl.ds(r0, lc), kc].astype(F32)
        v = v_ref[pl.ds(r0, lc), vc]
        st = st_ref[sidx]
        eg = jnp.exp(g)
        k_dec = k * jnp.exp(-bc)
        q_t = (q * jnp.exp(bc)).astype(BF16)
        k_t = k_dec.astype(BF16)
        att = jnp.where(tri, lax.dot_general(q_t, k_t, nt, preferred_element_type=F32), 0.0)
        o = (jnp.dot(att.astype(BF16), v, preferred_element_type=F32)
             + lax.dot_general(q_t, st.astype(BF16), nt, preferred_element_type=F32))
        acc_ref[pl.ds(r0, lc), vc] += o
        kd = (k_dec * eg).astype(BF16)
        s_loc = lax.dot_general(v, kd, t0, preferred_element_type=F32)
        st_ref[sidx] = st * eg + s_loc

    acc_ref[...] = jnp.zeros_like(acc_ref)
    st_ref[...] = jnp.zeros_like(st_ref)

    def step_fn(step, carry):
        for hl in range(hps):
            chain(step, hl, False)
            chain(step, hl, True)
        return carry

    lax.fori_loop(0, n_all, step_fn, 0)
    for hl in range(hps):
        vc = slice(hl * GL_DV, (hl + 1) * GL_DV)
        z = acc_ref[:, vc]
        z = z * lax.rsqrt(jnp.mean(z * z, axis=-1, keepdims=True) + RMS_EPS) * ng_ref[:, vc]
        y_ref[:, vc] = z.astype(BF16) * gate_ref[:, vc]


def _gl_mixer(qkv, gate, la, norm_g):
    nb, l, _ = qkv.shape
    lc = GL_SCAN
    n_all, n_ctx = l // lc, CTX_LEN // lc
    hps = 2
    ngrp = GL_HEADS // hps
    kb, vb = hps * GL_DK, hps * GL_DV
    nk = GL_HEADS * GL_DK
    return pl.pallas_call(
        functools.partial(_gl_kernel, n_ctx=n_ctx, n_all=n_all),
        grid=(nb, ngrp),
        in_specs=[pl.BlockSpec((None, l, kb), lambda b, h: (b, 0, h)),
                  pl.BlockSpec((None, l, kb), lambda b, h: (b, 0, nk // kb + h)),
                  pl.BlockSpec((None, l, vb), lambda b, h: (b, 0, 2 * nk // vb + h)),
                  pl.BlockSpec((None, l, vb), lambda b, h: (b, 0, h)),
                  pl.BlockSpec((None, l, kb), lambda b, h: (b, 0, h)),
                  pl.BlockSpec((None, l, kb), lambda b, h: (b, 0, nk // kb + h)),
                  pl.BlockSpec((1, vb), lambda b, h: (0, h))],
        out_specs=pl.BlockSpec((None, l, vb), lambda b, h: (b, 0, h)),
        out_shape=jax.ShapeDtypeStruct((nb, l, GL_HEADS * GL_DV), BF16),
        scratch_shapes=[pltpu.VMEM((l, vb), F32), pltpu.VMEM((2 * hps, GL_DV, GL_DK), F32)],
        compiler_params=_cparams("arbitrary", "arbitrary"),
        name="gla_mixer",
    )(qkv, qkv, qkv, gate, la, la, norm_g.reshape(1, -1))


def _store_token_tiles(ref, val):
    rows = val.shape[0]
    for s in range(val.shape[1] // LANES):
        ref[pl.ds(s, rows, stride=SUBLANES), :] = val[:, s * LANES:(s + 1) * LANES]


def _load_token_tiles(ref, rows):
    return jnp.concatenate([ref[pl.ds(s, rows, stride=SUBLANES), :] for s in range(SUBLANES)], axis=1)


def _post_kernel(y_ref, x_ref, mod_ref, wo_ref, g_ref, wr_ref, br_ref, xo_ref, f_ref, rt_ref, cnt_ref, run_ref):
    first = (pl.program_id(0) == 0) & (pl.program_id(1) == 0)

    @pl.when(first)
    def _():
        run_ref[...] = jnp.zeros_like(run_ref)

    for sub in range(PAIR):
        _post_tile(y_ref.at[sub], x_ref.at[sub], mod_ref.at[sub], wo_ref, g_ref, wr_ref, br_ref,
                   xo_ref.at[sub], f_ref.at[sub], rt_ref.at[sub], cnt_ref, run_ref)


def _post_tile(y_ref, x_ref, mod_ref, wo_ref, g_ref, wr_ref, br_ref, xo_ref, f_ref, rt_ref, cnt_ref, run_ref):
    o = jnp.dot(y_ref[...], wo_ref[...], preferred_element_type=F32)
    xn = x_ref[...] + mod_ref[2:3, :] * o
    xo_ref[...] = xn
    f = _norm_mod(xn, g_ref[...], mod_ref, 3)
    _store_token_tiles(f_ref, f)
    f_hi = f.astype(BF16)
    f_lo = (f - f_hi.astype(F32)).astype(BF16)
    hh = jnp.dot(f_hi, wr_ref[...], preferred_element_type=F32)
    lh = jnp.dot(f_lo, wr_ref[:, :LANES], preferred_element_type=F32)
    lg = hh[:, :LANES] + (hh[:, LANES:] + lh) + br_ref[...]
    lane = lax.broadcasted_iota(jnp.int32, lg.shape, 1).astype(F32)

    def top(mask):
        v = jnp.max(jnp.where(mask, lg, -jnp.inf), axis=-1, keepdims=True)
        return v, jnp.min(jnp.where(mask & (lg == v), lane, float(LANES)), axis=-1, keepdims=True)

    gm = lane < MOE_GROUPS
    mg, g_idx = top(gm)
    g_w = 1.0 / jnp.sum(jnp.where(gm, jnp.exp(lg - mg), 0.0), axis=-1, keepdims=True)
    lo = MOE_GROUPS + MOE_PER_GROUP * g_idx
    em = (lane >= lo) & (lane < lo + MOE_PER_GROUP)
    v0, i0 = top(em)
    v1, i1 = top(em & (lane != i0))
    e0, e1 = i0 - MOE_GROUPS, i1 - MOE_GROUPS
    t = jnp.exp(v1 - v0)
    w0 = g_w / (1.0 + t)
    w1 = g_w * t / (1.0 + t)
    oh = ((lane == e0) | (lane == e1)).astype(F32)
    rows = lg.shape[0]
    below = lax.broadcasted_iota(jnp.int32, (rows, rows), 0) > lax.broadcasted_iota(jnp.int32, (rows, rows), 1)
    tot = run_ref[...] + jnp.dot(below.astype(BF16), oh.astype(BF16), preferred_element_type=F32)
    r0 = jnp.sum(jnp.where(lane == e0, tot, 0.0), axis=-1, keepdims=True)
    r1 = jnp.sum(jnp.where(lane == e1, tot, 0.0), axis=-1, keepdims=True)
    run_ref[...] = run_ref[...] + jnp.sum(oh, axis=0, keepdims=True)
    cnt_ref[...] = jnp.broadcast_to(run_ref[...], cnt_ref.shape)
    out = jnp.zeros_like(lg)
    for i, val in enumerate((e0, e1, w0, w1, r0, r1)):
        out = jnp.where(lane == i, val, out)
    rt_ref[...] = out


def _post_call(y, x, mods, w_o, g_ffn, w_r, b_r, skip_ctx):
    nb, l, d = x.shape
    t0 = 1 if skip_ctx else 0
    nt = l // TM - t0
    l = nt * TM
    src = lambda b, t: (b, t + t0, 0)
    tile = lambda b, t: (b, t, 0)
    mod_spec = pl.BlockSpec((PAIR, 6, d), lambda b, t: (b, 0, 0)) if skip_ctx else _mod_spec(nb)
    x_new, f_tiles, rt, cnt = pl.pallas_call(
        _post_kernel,
        grid=(nb // PAIR, nt),
        in_specs=[pl.BlockSpec((PAIR, TM, d), src),
                  pl.BlockSpec((PAIR, TM, d), src),
                  mod_spec,
                  pl.BlockSpec((d, d), lambda b, t: (0, 0)),
                  pl.BlockSpec((1, d), lambda b, t: (0, 0)),
                  pl.BlockSpec((d, 2 * LANES), lambda b, t: (0, 0)),
                  pl.BlockSpec((1, LANES), lambda b, t: (0, 0))],
        out_specs=[pl.BlockSpec((PAIR, TM, d), tile),
                   pl.BlockSpec((PAIR, TM * SUBLANES, LANES), tile),
                   pl.BlockSpec((PAIR, TM, LANES), tile),
                   pl.BlockSpec((SUBLANES, LANES), lambda b, t: (0, 0))],
        out_shape=[jax.ShapeDtypeStruct((nb, l, d), F32),
                   jax.ShapeDtypeStruct((nb, l * SUBLANES, LANES), F32),
                   jax.ShapeDtypeStruct((nb, l, LANES), F32),
                   jax.ShapeDtypeStruct((SUBLANES, LANES), F32)],
        scratch_shapes=[pltpu.VMEM((1, LANES), F32)],
        compiler_params=_cparams("arbitrary", "arbitrary"),
        name="outproj_router",
    )(y, x, mods, w_o, g_ffn.reshape(1, d), w_r, b_r)
    return x_new, f_tiles.reshape(nb * l * SUBLANES, LANES), rt.reshape(nb * l, LANES), cnt


def _dispatch_kernel(dest_ref, pend_ref, f_ref, xs_out, zbuf, sem):
    tile = f_ref.shape[0] // SUBLANES
    base = pl.program_id(0) * tile

    @pl.when(pl.program_id(0) == 0)
    def _():
        zbuf[...] = jnp.zeros_like(zbuf)

        def zero_block(blk):
            start = pl.multiple_of(blk * (MOE_BLOCK * SUBLANES), MOE_BLOCK * SUBLANES)
            return pltpu.make_async_copy(zbuf, xs_out.at[pl.ds(start, MOE_BLOCK * SUBLANES), :], sem)

        def has_rows(e):
            return pend_ref[e] > (0 if e == 0 else pend_ref[e - 1])

        for e in range(MOE_EXPERTS):
            pl.when(has_rows(e))(lambda e=e: zero_block(pend_ref[e] // MOE_BLOCK - 1).start())
        for e in range(MOE_EXPERTS):
            pl.when(has_rows(e))(lambda e=e: zero_block(pend_ref[e] // MOE_BLOCK - 1).wait())
        n_act = pend_ref[MOE_EXPERTS - 1] // MOE_BLOCK
        n_blk = xs_out.shape[0] // (MOE_BLOCK * SUBLANES)
        lax.fori_loop(n_act, n_blk, lambda b, c: (zero_block(b).start(), c)[1], 0)
        lax.fori_loop(n_act, n_blk, lambda b, c: (zero_block(b).wait(), c)[1], 0)

    def copy(t, k):
        d = dest_ref[2 * (base + t) + k]
        return pltpu.make_async_copy(f_ref.at[pl.ds(t * SUBLANES, SUBLANES), :],
                                     xs_out.at[pl.ds(d * SUBLANES, SUBLANES), :], sem)

    def issue(t, c):
        copy(t, 0).start()
        copy(t, 1).start(priority=1)
        return c

    lax.fori_loop(0, tile, issue, 0, unroll=DMA_UNROLL)
    for _ in range(2):
        pltpu.make_async_copy(f_ref, xs_out.at[pl.ds(0, tile * SUBLANES), :], sem).wait()


def _dispatch(dest, p_end, f_tiles, n_slots):
    n_tok = f_tiles.shape[0] // SUBLANES
    tile = DISPATCH_TILE
    assert n_tok % tile == 0
    return pl.pallas_call(
        _dispatch_kernel,
        grid_spec=pltpu.PrefetchScalarGridSpec(
            num_scalar_prefetch=2,
            grid=(n_tok // tile,),
            in_specs=[pl.BlockSpec((tile * SUBLANES, LANES), lambda i, dr, pe: (i, 0))],
            out_specs=pl.BlockSpec(memory_space=pl.ANY),
            scratch_shapes=[pltpu.VMEM((MOE_BLOCK * SUBLANES, LANES), F32), pltpu.SemaphoreType.DMA(())]),
        out_shape=jax.ShapeDtypeStruct((n_slots * SUBLANES, LANES), F32),
        compiler_params=_cparams("arbitrary"),
        name="moe_dispatch",
    )(dest, p_end, f_tiles)


def _expert_kernel(blk_e_ref, nact_ref, nxt_ref, slot_ref, xs_ref, wg_hbm, wu_hbm, wd_hbm, ys_ref,
                   wg_f, wu_f, wd_f, wg_s, wu_s, wd_s, sem, *, layer):
    i = pl.program_id(0)
    active = i < nact_ref[0]
    e = blk_e_ref[i]
    changed = (i == 0) | (e != blk_e_ref[jnp.maximum(i - 1, 0)])
    slot = slot_ref[i]

    def fetch(expert, s):
        return [pltpu.make_async_copy(w.at[layer, expert], buf.at[s], sem.at[s, k])
                for k, (w, buf) in enumerate(((wg_hbm, wg_f), (wu_hbm, wu_f), (wd_hbm, wd_f)))]

    @pl.when(i == 0)
    def _():
        for c in fetch(e, slot):
            c.start()

    @pl.when(active & changed)
    def _():
        nxt = nxt_ref[i]

        @pl.when(nxt >= 0)
        def _():
            for c in fetch(nxt, 1 - slot):
                c.start()

        for c in fetch(e, slot):
            c.wait()
        wg_s[...] = wg_f[slot].astype(BF16)
        wu_s[...] = wu_f[slot].astype(BF16)
        wd_s[...] = wd_f[slot].astype(BF16)

    @pl.when(active)
    def _():
        x = _load_token_tiles(xs_ref, MOE_BLOCK).astype(BF16)
        a = jnp.dot(x, wg_s[...], preferred_element_type=F32)
        u = jnp.dot(x, wu_s[...], preferred_element_type=F32)
        y = jnp.dot((_silu(a) * u).astype(BF16), wd_s[...], preferred_element_type=F32)
        _store_token_tiles(ys_ref, y)

    @pl.when(jnp.logical_not(active))
    def _():
        ys_ref[...] = jnp.zeros_like(ys_ref)


def _experts(blk_e, nact, nxt_e, slot, xs, w_gate, w_up, w_down, layer):
    n_blk = blk_e.shape[0]
    _, _, d, ff = w_gate.shape
    rows = MOE_BLOCK * SUBLANES
    last = lambda i, na: jnp.minimum(i, jnp.maximum(na[0] - 1, 0))
    hbm = pl.BlockSpec(memory_space=pl.ANY)
    return pl.pallas_call(
        functools.partial(_expert_kernel, layer=layer),
        grid_spec=pltpu.PrefetchScalarGridSpec(
            num_scalar_prefetch=4,
            grid=(n_blk,),
            in_specs=[pl.BlockSpec((rows, LANES), lambda i, be, na, nx, sl: (last(i, na), 0)), hbm, hbm, hbm],
            out_specs=pl.BlockSpec((rows, LANES), lambda i, be, na, nx, sl: (i, 0)),
            scratch_shapes=[pltpu.VMEM((2, d, ff), F32), pltpu.VMEM((2, d, ff), F32), pltpu.VMEM((2, ff, d), F32),
                            pltpu.VMEM((d, ff), BF16), pltpu.VMEM((d, ff), BF16), pltpu.VMEM((ff, d), BF16),
                            pltpu.SemaphoreType.DMA((2, 3))]),
        out_shape=jax.ShapeDtypeStruct(xs.shape, F32),
        compiler_params=_cparams("arbitrary"),
        name="moe_experts",
    )(blk_e, nact, nxt_e, slot, xs, w_gate, w_up, w_down)


def _combine_kernel(dest_ref, ys_hbm, x_ref, rt_ref, mod_ref, o_ref, ybuf, sem):
    i = pl.program_id(0)
    n = pl.num_programs(0)

    def gather(step, slot, start):
        base = step * TM

        def copy(t, k):
            d = dest_ref[2 * (base + t) + k]
            return pltpu.make_async_copy(ys_hbm.at[pl.ds(d * SUBLANES, SUBLANES), :],
                                         ybuf.at[slot, k, pl.ds(t * SUBLANES, SUBLANES), :], sem.at[slot])

        def issue(t, c):
            copy(t, 0).start()
            copy(t, 1).start(priority=1)
            return c

        if start:
            lax.fori_loop(0, TM, issue, 0, unroll=DMA_UNROLL)
        else:
            for k in range(2):
                pltpu.make_async_copy(ys_hbm.at[pl.ds(0, TM * SUBLANES), :], ybuf.at[slot, k], sem.at[slot]).wait()

    slot = i % 2
    pl.when(i == 0)(lambda: gather(i, slot, True))
    pl.when(i + 1 < n)(lambda: gather(i + 1, 1 - slot, True))
    gather(i, slot, False)
    y0 = _load_token_tiles(ybuf.at[slot, 0], TM)
    y1 = _load_token_tiles(ybuf.at[slot, 1], TM)
    rt = rt_ref[...]
    y = rt[:, 2:3] * y0 + rt[:, 3:4] * y1
    o_ref[...] = x_ref[...] + mod_ref[5:6, :] * y


def _combine(dest, ys, x, rt, mods, has_ctx):
    nb, l, d = x.shape
    nt = l // TM
    mod_row = (lambda i: jnp.where(i % nt == 0, nb, i // nt)) if has_ctx else (lambda i: i // nt)
    return pl.pallas_call(
        _combine_kernel,
        grid_spec=pltpu.PrefetchScalarGridSpec(
            num_scalar_prefetch=1,
            grid=(nb * nt,),
            in_specs=[pl.BlockSpec(memory_space=pl.ANY),
                      pl.BlockSpec((None, TM, d), lambda i, dr: (i // nt, i % nt, 0)),
                      pl.BlockSpec((TM, LANES), lambda i, dr: (i, 0)),
                      pl.BlockSpec((None, 6, d), lambda i, dr: (mod_row(i), 0, 0))],
            out_specs=pl.BlockSpec((None, TM, d), lambda i, dr: (i // nt, i % nt, 0)),
            scratch_shapes=[pltpu.VMEM((2, 2, TM * SUBLANES, LANES), F32), pltpu.SemaphoreType.DMA((2,))]),
        out_shape=jax.ShapeDtypeStruct((nb, l, d), F32),
        compiler_params=_cparams("arbitrary"),
        name="moe_combine",
    )(dest, ys, x, rt, mods)


def _moe(y_mix, x, mods, w_o, g_ffn, w_grp, b_grp, w_exp, b_exp, w_gate, w_up, w_down, layer, skip_ctx):
    nr = MOE_GROUPS + MOE_EXPERTS
    w_r = jnp.pad(jnp.concatenate([w_grp, w_exp], axis=1), ((0, 0), (0, LANES - nr)))
    b_r = jnp.pad(jnp.concatenate([b_grp, b_exp]), (0, LANES - nr)).reshape(1, LANES)
    w_hi = w_r.astype(BF16)
    w_r = jnp.concatenate([w_hi, (w_r - w_hi.astype(F32)).astype(BF16)], axis=1)
    x_new, f_tiles, rt, cnt = _post_call(y_mix, x, mods, w_o, g_ffn, w_r, b_r, skip_ctx)
    n_tok = rt.shape[0]
    counts = cnt[0, :MOE_EXPERTS].astype(jnp.int32)
    padded = (counts + MOE_BLOCK - 1) // MOE_BLOCK * MOE_BLOCK
    p_end = jnp.cumsum(padded)
    p_start = p_end - padded
    n_pairs = 2 * n_tok
    n_blk = -(-(n_pairs + MOE_EXPERTS * (MOE_BLOCK - 1)) // MOE_BLOCK)
    eid = rt[:, 0:2].astype(jnp.int32)
    first = jnp.sum(jnp.where(eid[:, :, None] == jnp.arange(MOE_EXPERTS, dtype=jnp.int32), p_start, 0), axis=-1)
    dest = (first + rt[:, 4:6].astype(jnp.int32)).reshape(-1)
    blk_row = jnp.arange(n_blk, dtype=jnp.int32) * MOE_BLOCK
    blk_e = jnp.minimum(jnp.sum((p_end[None, :] <= blk_row[:, None]).astype(jnp.int32), axis=1), MOE_EXPERTS - 1)
    nact = (p_end[-1:] // MOE_BLOCK).astype(jnp.int32)
    xs = _dispatch(dest, p_end.astype(jnp.int32), f_tiles, n_blk * MOE_BLOCK)
    has = padded > 0
    ids = jnp.arange(MOE_EXPERTS, dtype=jnp.int32)
    rank = jnp.cumsum(has.astype(jnp.int32)) - has.astype(jnp.int32)
    later = jnp.where((ids[None, :] > ids[:, None]) & has[None, :], ids[None, :], MOE_EXPERTS)
    nxt = jnp.min(later, axis=1)
    nxt = jnp.where(nxt == MOE_EXPERTS, -1, nxt)
    sel = blk_e[:, None] == ids[None, :]
    slot = jnp.sum(jnp.where(sel, rank[None, :] % 2, 0), axis=1).astype(jnp.int32)
    nxt_e = jnp.sum(jnp.where(sel, nxt[None, :], 0), axis=1).astype(jnp.int32)
    ys = _experts(blk_e, nact, nxt_e, slot, xs, w_gate, w_up, w_down, layer)
    return _combine(dest, ys, x_new, rt, mods, not skip_ctx)


def _rope_tables(seq):
    nf = SW_HD // 4
    inv = ROPE_BASE ** (-jnp.arange(nf, dtype=F32) / nf)
    pos = jnp.arange(seq, dtype=jnp.int32)
    rows, cols = (pos // GRID_W).astype(F32), (pos % GRID_W).astype(F32)
    lane = jnp.arange(LANES)
    p = jnp.where((lane % SW_HD < SW_HD // 2)[None, :], rows[:, None], cols[:, None])
    ang = p * inv[lane % nf][None, :]
    sign = jnp.where((lane % (2 * nf)) < nf, -1.0, 1.0)[None, :]
    cs = jnp.concatenate([jnp.ones((CTX_LEN, LANES), F32), jnp.cos(ang)], axis=0)
    sn = jnp.concatenate([jnp.zeros((CTX_LEN, LANES), F32), jnp.sin(ang) * sign], axis=0)
    return cs, sn


def _pad_cols(w, n):
    return jnp.pad(w, ((0, 0), (0, n - w.shape[1])))


def kernel(x, c, ctx, c_ctx, ada_w, ada_b, norm_mix_g, norm_ffn_g, na_w_qkv, na_qk_g, na_rpb, na_w_o, ml_w_in, ml_b_gates, ml_norm_g, ml_w_o, sw_w_qkv, sw_qk_g, sw_sink, sw_w_o, gl_w_in, gl_w_a2, gl_b_a, gl_norm_g, gl_w_o, moe_w_grp, moe_b_grp, moe_w_exp, moe_b_exp, moe_w_gate, moe_w_up, moe_w_down):
    nb, seq, d = x.shape
    depth = ada_w.shape[0]
    assert d == D_MODEL and ctx.shape[1] == CTX_LEN == TM and seq % TM == 0 and nb % PAIR == 0
    rows = -(-(nb + PAIR) // SUBLANES) * SUBLANES
    cond = jnp.pad(jnp.concatenate([c] + [c_ctx[None]] * PAIR, axis=0), ((0, rows - nb - PAIR), (0, 0)))
    mods_all = _ada_mods(cond, ada_w, ada_b)
    xa = jnp.concatenate([ctx, x], axis=1)
    for i in range(depth):
        j, kind = divmod(i, 4)
        mods = mods_all[i]
        g_mix = norm_mix_g[i]
        if kind == 0:
            qkg = na_qk_g[j]
            (qkv,) = _proj_call(_proj_na_kernel, xa, mods, g_mix, na_w_qkv[j].astype(BF16), [(qkg, False)],
                                [3 * NA_HEADS * NA_HD], [BF16], "proj_na")
            y = _na_attention(qkv, _na_bias_table(na_rpb[j], seq // GRID_W))
            w_o = na_w_o[j]
        elif kind == 1:
            n_in = -(-ml_w_in.shape[2] // LANES) * LANES
            ng = 4 * ML_HEADS
            w_in = ml_w_in[j]
            w_g = w_in[:, -ng:].reshape(d, 2, 2, ML_HEADS).transpose(0, 3, 1, 2).reshape(d, ng)
            w_in = jnp.concatenate([w_in[:, :-ng], w_g], axis=1)
            bg = ml_b_gates[j].reshape(2, 2, ML_HEADS).transpose(2, 0, 1).reshape(ng)
            bg = jnp.pad(bg, (0, LANES - ng)).reshape(1, LANES)
            qkv, og, gates = _proj_call(_proj_ml_kernel, xa, mods, g_mix, _pad_cols(w_in, n_in).astype(BF16),
                                        [(bg, False)], [2 * ML_HEADS * ML_DK + ML_HEADS * ML_DV, ML_HEADS * ML_DV, LANES],
                                        [BF16, BF16, F32], "proj_ml")
            y = _ml_mixer(qkv, og, gates, ml_norm_g[j])
            w_o = ml_w_o[j]
        elif kind == 2:
            qkg = jnp.concatenate([sw_qk_g[j], sw_qk_g[j]], axis=1)
            cs, sn = _rope_tables(seq)
            nqc = SW_HEADS * SW_HD
            w_in = sw_w_qkv[j]
            w_q = w_in[:, :nqc].reshape(d, 2, 2, 4, SW_HD).transpose(0, 1, 3, 2, 4).reshape(d, nqc)
            w_in = jnp.concatenate([w_q, w_in[:, nqc:]], axis=1)
            q, kv = _proj_call(_proj_sw_kernel, xa, mods, g_mix, w_in.astype(BF16),
                               [(qkg, False), (cs, True), (sn, True)],
                               [nqc, 2 * SW_KV * SW_HD], [BF16, BF16], "proj_sw")
            y = _sw_attention(q, kv, sw_sink[j])
            w_o = sw_w_o[j].reshape(2, 2, 4, SW_HD, d).transpose(0, 2, 1, 3, 4).reshape(nqc, d)
        else:
            n_in = -(-gl_w_in.shape[2] // LANES) * LANES
            nk = GL_HEADS * GL_DK
            wa = jnp.zeros((LANES, 2 * nk), F32)
            wa = wa.at[:GL_RANK, :nk].set(gl_w_a2[j, 0]).at[GL_RANK:2 * GL_RANK, nk:].set(gl_w_a2[j, 1])
            ba = gl_b_a[j].reshape(1, 2 * nk)
            qkv, gate, la = _proj_call(_proj_gl_kernel, xa, mods, g_mix, _pad_cols(gl_w_in[j], n_in).astype(BF16),
                                       [(wa.astype(BF16), False), (ba, False)],
                                       [2 * nk + GL_HEADS * GL_DV, GL_HEADS * GL_DV, 2 * nk], [BF16, BF16, F32], "proj_gl")
            y = _gl_mixer(qkv, gate, la, gl_norm_g[j])
            w_o = gl_w_o[j]
        xa = _moe(y, xa, mods, w_o.astype(BF16), norm_ffn_g[i], moe_w_grp[i], moe_b_grp[i], moe_w_exp[i], moe_b_exp[i],
                  moe_w_gate, moe_w_up, moe_w_down, i, i == depth - 1)
    return xa
```
